```python
import jax, jax.numpy as jnp
from jax import lax
import numpy as np


D_MODEL = 1024
BATCH = 1
SEQ = 16384
DEPTH = 2
DEC_BATCH = 16
DEC_SEQ = 32
PAST_LEN = 4096

CHUNK = 64
HEAD_DIM = 64
ATTN_SCALE = HEAD_DIM ** -0.5
POOL_WINDOWS = (2, 4, 8, 16)
N_POOL_GROUPS = 4
C_POOL = D_MODEL // 2
POOL_GROUP = C_POOL // N_POOL_GROUPS
POOL_PAD = 15
SWA_HEADS = (D_MODEL // 2) // HEAD_DIM
SWA_KV_HEADS = 2
SWA_REP = SWA_HEADS // SWA_KV_HEADS
WINDOW = 128
SWA_NB = WINDOW // CHUNK
MIX_AB = C_POOL + SWA_HEADS * HEAD_DIM
AB_IN = MIX_AB + 2 * SWA_KV_HEADS * HEAD_DIM + MIX_AB
FOX_HEADS = D_MODEL // HEAD_DIM
MIX_C = FOX_HEADS * HEAD_DIM
C_IN = 4 * MIX_C + FOX_HEADS
QBLK = 128
FORGET_BIAS_INIT = 3.0
NORM_EPS = 1e-6
NEG_INF = -1e30
F32 = jnp.float32

kernel_name = 'chunk_causal_pool_swa_fox_hybrid_step'


def rms_norm(x, g):
    xf = x.astype(F32)
    y = xf * lax.rsqrt(jnp.mean(xf * xf, axis=-1, keepdims=True) + NORM_EPS)
    return (y * g.astype(F32)).astype(x.dtype)


def alibi_slopes():
    s = 2.0 ** (-(8.0 / SWA_HEADS) * np.arange(1, SWA_HEADS + 1))
    return jnp.asarray(s.astype(np.float32)).reshape(SWA_KV_HEADS, SWA_REP, 1, 1)


def multiscale_pool(u, prefix, pos, w_pool, pool_scale):
    n, t, _ = u.shape
    ext = jnp.concatenate([prefix.astype(u.dtype), u], axis=1)
    cs = jnp.pad(jnp.cumsum(ext.astype(F32), axis=1), ((0, 0), (1, 0), (0, 0)))
    outs = []
    for g, w in enumerate(POOL_WINDOWS):
        sl = slice(g * POOL_GROUP, (g + 1) * POOL_GROUP)
        hi = cs[:, POOL_PAD + 1:POOL_PAD + 1 + t, sl]
        lo = cs[:, POOL_PAD + 1 - w:POOL_PAD + 1 - w + t, sl]
        cnt = jnp.minimum(pos + 1, w).astype(F32)[None, :, None]
        outs.append((hi - lo) / cnt)
    pooled = jnp.stack(outs, axis=2)
    diff = (pooled - u.reshape(n, t, N_POOL_GROUPS, POOL_GROUP).astype(F32)).astype(u.dtype)
    mixed = jnp.einsum('ntgc,gcd->ntgd', diff, w_pool).reshape(n, t, C_POOL)
    return mixed * pool_scale, ext[:, -POOL_PAD:]


def sink_attention(q, k, v, bias, sinks):
    s = jnp.einsum('...qgrd,...kgd->...grqk', q, k).astype(F32) * ATTN_SCALE + bias
    sink = sinks.astype(F32).reshape(SWA_KV_HEADS, SWA_REP, 1, 1)
    m = jnp.maximum(jnp.max(s, axis=-1, keepdims=True), sink)
    p = jnp.exp(s - m)
    w = p / (jnp.sum(p, axis=-1, keepdims=True) + jnp.exp(sink - m))
    return jnp.einsum('...grqk,...kgd->...qgrd', w.astype(v.dtype), v)


def swa_prompt(q, k, v, sinks):
    n, t = q.shape[:2]
    nc = t // CHUNK
    qc = q.reshape(n, nc, CHUNK, SWA_KV_HEADS, SWA_REP, HEAD_DIM)

    def band(a):
        ac = a.reshape(n, nc, CHUNK, SWA_KV_HEADS, HEAD_DIM)
        ap = jnp.pad(ac, ((0, 0), (SWA_NB, 0), (0, 0), (0, 0), (0, 0)))
        return jnp.concatenate([ap[:, j:j + nc] for j in range(SWA_NB + 1)], axis=2)

    kb, vb = band(k), band(v)
    kj = jnp.arange((SWA_NB + 1) * CHUNK)
    rel = jnp.arange(CHUNK)[:, None] + SWA_NB * CHUNK - kj[None, :]
    alibi = -alibi_slopes() * jnp.abs(rel).astype(F32)
    kpos = (jnp.arange(nc)[:, None] - SWA_NB) * CHUNK + kj[None, :]
    mask = jnp.where(kpos >= 0, 0.0, NEG_INF).astype(F32)[:, None, None, None, :]
    out = sink_attention(qc, kb, vb, alibi + mask, sinks)
    return out.reshape(n, t, SWA_HEADS * HEAD_DIM)


def swa_sample(q, k, v, k_prefix, v_prefix, sinks):
    n, t = q.shape[:2]
    kk = jnp.concatenate([k_prefix.astype(k.dtype), k], axis=1)
    vv = jnp.concatenate([v_prefix.astype(v.dtype), v], axis=1)
    rel = jnp.arange(t)[:, None] + WINDOW - jnp.arange(WINDOW + t)[None, :]
    bias = -alibi_slopes() * jnp.abs(rel).astype(F32)
    out = sink_attention(q, kk, vv, bias, sinks)
    return out.reshape(n, t, SWA_HEADS * HEAD_DIM), kk[:, -WINDOW:], vv[:, -WINDOW:]


def ab_layer(x, pos, pool_prefix, swa_k_prefix, swa_v_prefix, norm_g, w_in, w_pool, pool_scale, qn_g, kn_g, sinks, w_out):
    n, t, _ = x.shape
    h = rms_norm(x, norm_g)
    proj = h @ w_in
    i1 = C_POOL
    i2 = i1 + SWA_HEADS * HEAD_DIM
    i3 = i2 + SWA_KV_HEADS * HEAD_DIM
    i4 = i3 + SWA_KV_HEADS * HEAD_DIM
    u, q, k, v, gate = jnp.split(proj, [i1, i2, i3, i4], axis=-1)
    pool_out, pool_state = multiscale_pool(u, pool_prefix, pos, w_pool, pool_scale)
    q = rms_norm(q.reshape(n, t, SWA_HEADS, HEAD_DIM), qn_g).reshape(n, t, SWA_KV_HEADS, SWA_REP, HEAD_DIM)
    k = rms_norm(k.reshape(n, t, SWA_KV_HEADS, HEAD_DIM), kn_g)
    v = v.reshape(n, t, SWA_KV_HEADS, HEAD_DIM)
    if swa_k_prefix is None:
        attn = swa_prompt(q, k, v, sinks)
        k_state, v_state = k[:, -WINDOW:], v[:, -WINDOW:]
    else:
        attn, k_state, v_state = swa_sample(q, k, v, swa_k_prefix, swa_v_prefix, sinks)
    mixed = jnp.concatenate([pool_out, attn], axis=-1) * jax.nn.silu(gate)
    return x + mixed @ w_out, pool_state, k_state, v_state


def forgetting_core(q, k, v, bias, mask):
    s = jnp.einsum('nqhd,nkhd->nhqk', q, k).astype(F32) * ATTN_SCALE + bias
    p = jax.nn.softmax(jnp.where(mask, s, -jnp.inf), axis=-1)
    return jnp.einsum('nhqk,nkhd->nqhd', p.astype(v.dtype), v)


def fox_prompt(q, k, v, logf):
    n, t = q.shape[:2]
    nb = t // QBLK
    cum_t = jnp.cumsum(logf, axis=1).transpose(0, 2, 1)
    kpos = jnp.arange(t)
    qb = q.reshape(n, nb, QBLK, FOX_HEADS, HEAD_DIM).swapaxes(0, 1)
    cb = cum_t.reshape(n, FOX_HEADS, nb, QBLK).transpose(2, 0, 1, 3)
    pb = kpos.reshape(nb, QBLK)

    def block(args):
        qi, ci, pi = args
        bias = ci[..., None] - cum_t[:, :, None, :]
        return forgetting_core(qi, k, v, bias, pi[:, None] >= kpos[None, :])

    out = lax.map(block, (qb, cb, pb))
    return out.swapaxes(0, 1).reshape(n, t, MIX_C)


def fox_sample(q, k, v, logf, k_prefix, v_prefix, logf_prefix):
    n, t = q.shape[:2]
    p_len = k_prefix.shape[1]
    kk = jnp.concatenate([k_prefix.astype(k.dtype), k], axis=1)
    vv = jnp.concatenate([v_prefix.astype(v.dtype), v], axis=1)
    cum_t = jnp.cumsum(jnp.concatenate([logf_prefix.astype(F32), logf], axis=1), axis=1).transpose(0, 2, 1)
    bias = cum_t[:, :, -t:, None] - cum_t[:, :, None, :]
    mask = (p_len + jnp.arange(t))[:, None] >= jnp.arange(p_len + t)[None, :]
    return forgetting_core(q, kk, vv, bias, mask).reshape(n, t, MIX_C)


def fox_layer(x, k_prefix, v_prefix, logf_prefix, norm_g, w_in, b_forget, qn_g, kn_g, w_out):
    n, t, _ = x.shape
    h = rms_norm(x, norm_g)
    proj = h @ w_in
    q, k, v, gate, fpre = jnp.split(proj, [MIX_C, 2 * MIX_C, 3 * MIX_C, 4 * MIX_C], axis=-1)
    q = rms_norm(q.reshape(n, t, FOX_HEADS, HEAD_DIM), qn_g)
    k = rms_norm(k.reshape(n, t, FOX_HEADS, HEAD_DIM), kn_g)
    v = v.reshape(n, t, FOX_HEADS, HEAD_DIM)
    logf = jax.nn.log_sigmoid(fpre.astype(F32) + b_forget.astype(F32))
    if k_prefix is None:
        attn = fox_prompt(q, k, v, logf)
    else:
        attn = fox_sample(q, k, v, logf, k_prefix, v_prefix, logf_prefix)
    return x + (attn * jax.nn.silu(gate)) @ w_out, k, v, logf


def setup_inputs(seed: int = 0) -> dict:
    key = jax.random.key(seed)
    ks = jax.random.split(key, 24)
    nrm = lambda i, shape: jax.random.normal(ks[i], shape, F32)
    return {
        'x_prompt': nrm(0, (BATCH, SEQ, D_MODEL)),
        'x_sample': nrm(1, (DEC_BATCH, DEC_SEQ, D_MODEL)),
        'state_pool': nrm(2, (DEC_BATCH, POOL_PAD, C_POOL)),
        'cache_swa_k': nrm(3, (DEC_BATCH, WINDOW, SWA_KV_HEADS, HEAD_DIM)),
        'cache_swa_v': nrm(4, (DEC_BATCH, WINDOW, SWA_KV_HEADS, HEAD_DIM)),
        'cache_fox_k': nrm(5, (DEC_BATCH, PAST_LEN, FOX_HEADS, HEAD_DIM)),
        'cache_fox_v': nrm(6, (DEC_BATCH, PAST_LEN, FOX_HEADS, HEAD_DIM)),
        'cache_fox_logf': jax.nn.log_sigmoid(FORGET_BIAS_INIT + nrm(7, (DEC_BATCH, PAST_LEN, FOX_HEADS))),
        'norm0_g': 1.0 + 0.02 * nrm(8, (D_MODEL,)),
        'w_in0': nrm(9, (D_MODEL, AB_IN)) * D_MODEL ** -0.5,
        'w_pool': nrm(10, (N_POOL_GROUPS, POOL_GROUP, POOL_GROUP)) * POOL_GROUP ** -0.5,
        'pool_scale': 1.0 + 0.1 * nrm(11, (C_POOL,)),
        'swa_qn_g': 1.0 + 0.02 * nrm(12, (HEAD_DIM,)),
        'swa_kn_g': 1.0 + 0.02 * nrm(13, (HEAD_DIM,)),
        'swa_sinks': 0.5 * nrm(14, (SWA_HEADS,)),
        'w_out0': nrm(15, (MIX_AB, D_MODEL)) * MIX_AB ** -0.5,
        'norm1_g': 1.0 + 0.02 * nrm(16, (D_MODEL,)),
        'w_in1': nrm(17, (D_MODEL, C_IN)) * D_MODEL ** -0.5,
        'b_forget': FORGET_BIAS_INIT + 0.5 * nrm(18, (FOX_HEADS,)),
        'fox_qn_g': 1.0 + 0.02 * nrm(19, (HEAD_DIM,)),
        'fox_kn_g': 1.0 + 0.02 * nrm(20, (HEAD_DIM,)),
        'w_out1': nrm(21, (MIX_C, D_MODEL)) * MIX_C ** -0.5,
    }


def reference(x_prompt, x_sample, state_pool, cache_swa_k, cache_swa_v, cache_fox_k, cache_fox_v, cache_fox_logf,
              norm0_g, w_in0, w_pool, pool_scale, swa_qn_g, swa_kn_g, swa_sinks, w_out0,
              norm1_g, w_in1, b_forget, fox_qn_g, fox_kn_g, w_out1):
    yp, ys = x_prompt, x_sample
    pos_p = jnp.arange(yp.shape[1])
    pos_s = PAST_LEN + jnp.arange(ys.shape[1])
    for layer in range(DEPTH):
        if layer % 2 == 0:
            zeros_prefix = jnp.zeros((yp.shape[0], POOL_PAD, C_POOL), yp.dtype)
            yp, pool_p, swk_p, swv_p = ab_layer(yp, pos_p, zeros_prefix, None, None, norm0_g, w_in0, w_pool,
                                                pool_scale, swa_qn_g, swa_kn_g, swa_sinks, w_out0)
            ys, pool_s, swk_s, swv_s = ab_layer(ys, pos_s, state_pool, cache_swa_k, cache_swa_v, norm0_g, w_in0,
                                                w_pool, pool_scale, swa_qn_g, swa_kn_g, swa_sinks, w_out0)
        else:
            yp, fk_p, fv_p, fl_p = fox_layer(yp, None, None, None, norm1_g, w_in1, b_forget, fox_qn_g, fox_kn_g, w_out1)
            ys, fk_s, fv_s, fl_s = fox_layer(ys, cache_fox_k, cache_fox_v, cache_fox_logf, norm1_g, w_in1, b_forget,
                                             fox_qn_g, fox_kn_g, w_out1)
    return (yp, ys, pool_p, pool_s, swk_p, swv_p, swk_s, swv_s, fk_p, fv_p, fl_p, fk_s, fv_s, fl_s)
```

```python
import functools

import numpy as np
import jax
import jax.numpy as jnp
from jax import lax
from jax.experimental import pallas as pl
from jax.experimental.pallas import tpu as pltpu

F32 = jnp.float32
BF16 = jnp.bfloat16

D_MODEL = 1024
HEAD_DIM = 64
ATTN_SCALE = HEAD_DIM ** -0.5
POOL_WINDOWS = (2, 4, 8, 16)
C_POOL = 512
POOL_PAD = 15
SWA_HEADS = 8
SWA_KV_HEADS = 2
SWA_REP = 4
WINDOW = 128
FOX_HEADS = 16
NORM_EPS = 1e-6
NEG_INF = -1e30

LANES = 128
SWA_KEYS = 256
VMEM_LIMIT = 56 * 1024 * 1024


def _cparams(sem):
    return pltpu.CompilerParams(dimension_semantics=sem, vmem_limit_bytes=VMEM_LIMIT)


def _rms(x, g):
    ms = jnp.mean(x * x, axis=-1, keepdims=True)
    return x * lax.rsqrt(ms + NORM_EPS) * g


def _split2(x):
    hi = x.astype(BF16)
    lo = (x - hi.astype(F32)).astype(BF16)
    return hi, lo


def _split3(x):
    hi = x.astype(BF16)
    r = x - hi.astype(F32)
    mid = r.astype(BF16)
    lo = (r - mid.astype(F32)).astype(BF16)
    return hi, mid, lo


def _head_rms_tile(x, seg, g):
    hi, lo = _split2(x * x)
    ss = jnp.dot(hi, seg, preferred_element_type=F32) + jnp.dot(lo, seg, preferred_element_type=F32)
    return x * lax.rsqrt(ss * (1.0 / HEAD_DIM) + NORM_EPS) * g


def _silu(g):
    return g / (1.0 + jnp.exp(-g))


def _l0_kernel(*refs, ns, ts, chunk, has_cache, base_pos, nblk):
    if has_cache:
        (x_ref, pp_ref, kp_ref, vp_ref, g0_ref, win_ref, wpool_ref, pscale_ref, qg_ref, kg_ref, seg_ref,
         slope_ref, sink_ref, wout_ref, y_ref, ps_ref, ks_ref, vs_ref, uext, kext, vext, mix) = refs
    else:
        (x_ref, g0_ref, win_ref, wpool_ref, pscale_ref, qg_ref, kg_ref, seg_ref,
         slope_ref, sink_ref, wout_ref, y_ref, ps_ref, ks_ref, vs_ref, uext, kext, vext, mix) = refs
    padk = SWA_KEYS - WINDOW - chunk
    hist = padk + WINDOW
    i = pl.program_id(0)

    x = x_ref[...]
    h = _rms(x, g0_ref[...]).astype(BF16)
    proj = jnp.dot(h, win_ref[...], preferred_element_type=F32)
    u = proj[:, 0:C_POOL]
    gate = proj[:, 1280:2304]
    seg = seg_ref[...]
    qn = [_head_rms_tile(proj[:, 512 + LANES * j:512 + LANES * (j + 1)], seg,
                         qg_ref[:, LANES * j:LANES * (j + 1)]) * ATTN_SCALE for j in range(4)]
    kn = _head_rms_tile(proj[:, 1024:1152], seg, kg_ref[...])
    v = proj[:, 1152:1280]

    if has_cache:
        for s in range(ns):
            uext[s, 0:16, :] = pp_ref[s]
            kext[s, 0:padk, :] = jnp.zeros((padk, LANES), F32)
            vext[s, 0:padk, :] = jnp.zeros((padk, LANES), F32)
            kext[s, padk:hist, :] = kp_ref[s]
            vext[s, padk:hist, :] = vp_ref[s]
    else:
        @pl.when(i == 0)
        def _():
            uext[0, 0:16, :] = jnp.zeros((16, C_POOL), F32)
            kext[0, 0:hist, :] = jnp.zeros((hist, LANES), F32)
            vext[0, 0:hist, :] = jnp.zeros((hist, LANES), F32)

        @pl.when(i > 0)
        def _():
            uext[0, 0:16, :] = uext[0, ts:ts + 16, :]
            kext[0, 0:hist, :] = kext[0, ts:ts + hist, :]
            vext[0, 0:hist, :] = vext[0, ts:ts + hist, :]

    for s in range(ns):
        uext[s, 16:16 + ts, :] = u[s * ts:(s + 1) * ts]
        kext[s, hist:hist + ts, :] = kn[s * ts:(s + 1) * ts]
        vext[s, hist:hist + ts, :] = v[s * ts:(s + 1) * ts]

    pos = base_pos + i * ts + lax.broadcasted_iota(jnp.int32, (ts, LANES), 0)
    for s in range(ns):
        for g, w in enumerate(POOL_WINDOWS):
            cols = slice(LANES * g, LANES * (g + 1))
            acc = uext[s, 16:16 + ts, cols]
            cur = acc
            for j in range(1, w):
                acc = acc + uext[s, 16 - j:16 - j + ts, cols]
            cnt = jnp.minimum(pos + 1, w).astype(F32)
            mix[s * ts:(s + 1) * ts, cols] = acc / cnt - cur
    for g in range(4):
        cols = slice(LANES * g, LANES * (g + 1))
        d = mix[:, cols].astype(BF16)
        mix[:, cols] = jnp.dot(d, wpool_ref[g], preferred_element_type=F32) * pscale_ref[:, cols]

    rows4 = SWA_REP * chunk
    lo_c = lax.broadcasted_iota(jnp.int32, (chunk, LANES), 1) < HEAD_DIM
    lo_k = lax.broadcasted_iota(jnp.int32, (SWA_KEYS, LANES), 1) < HEAD_DIM
    kj = lax.broadcasted_iota(jnp.int32, (rows4, SWA_KEYS), 1)
    qi = lax.broadcasted_iota(jnp.int32, (rows4, SWA_KEYS), 0) % chunk
    absrel = jnp.abs(qi + WINDOW - (kj - padk)).astype(F32)
    bias = []
    sinkc = []
    for g in range(SWA_KV_HEADS):
        sl = slope_ref[g]
        bias.append(-jnp.concatenate([sl, sl], axis=1) * absrel)
        sinkc.append(sink_ref[g][:, 0:1])
    nch = ts // chunk
    for s in range(ns):
        for c in range(nch):
            r0 = c * chunk
            rows = slice(s * ts + r0, s * ts + r0 + chunk)
            if has_cache:
                thresh = padk
            else:
                thresh = padk + jnp.maximum(0, WINDOW - (i * nch + c) * chunk)
            kwin = kext[s, r0:r0 + SWA_KEYS, :]
            vwin = vext[s, r0:r0 + SWA_KEYS, :]
            krl = pltpu.roll(kwin, HEAD_DIM, 1)
            vrl = pltpu.roll(vwin, HEAD_DIM, 1)
            for g in range(SWA_KV_HEADS):
                if g == 0:
                    kd = jnp.where(lo_k, kwin, krl).astype(BF16)
                    vd = jnp.where(lo_k, vwin, vrl).astype(BF16)
                else:
                    kd = jnp.where(lo_k, krl, kwin).astype(BF16)
                    vd = jnp.where(lo_k, vrl, vwin).astype(BF16)
                q0 = qn[2 * g][rows]
                q1 = qn[2 * g + 1][rows]
                qs = jnp.concatenate([jnp.where(lo_c, q0, 0.0), jnp.where(lo_c, 0.0, q0),
                                      jnp.where(lo_c, q1, 0.0), jnp.where(lo_c, 0.0, q1)], axis=0).astype(BF16)
                sc = lax.dot_general(qs, kd, (((1,), (1,)), ((), ())), preferred_element_type=F32)
                sc = jnp.where(kj >= thresh, sc + bias[g], NEG_INF)
                m = jnp.maximum(jnp.max(sc, axis=-1, keepdims=True), sinkc[g])
                p = jnp.exp(sc - m)
                den = jnp.sum(p, axis=-1, keepdims=True) + jnp.exp(sinkc[g] - m)
                wgt = (p / den).astype(BF16)
                o = jnp.dot(wgt, vd, preferred_element_type=F32)
                for jj in range(2):
                    t = 2 * g + jj
                    mix[rows, C_POOL + LANES * t:C_POOL + LANES * (t + 1)] = jnp.where(
                        lo_c, o[(2 * jj) * chunk:(2 * jj + 1) * chunk], o[(2 * jj + 1) * chunk:(2 * jj + 2) * chunk])

    mixed = (mix[...] * _silu(gate)).astype(BF16)
    y_ref[...] = x + jnp.dot(mixed, wout_ref[...], preferred_element_type=F32)

    @pl.when(i == nblk - 1)
    def _():
        for s in range(ns):
            ps_ref[s] = uext[s, ts:ts + 16, :]
            ks_ref[s] = kext[s, padk + ts:padk + ts + WINDOW, :]
            vs_ref[s] = vext[s, padk + ts:padk + ts + WINDOW, :]


def _const_spec(shape):
    nd = len(shape)
    return pl.BlockSpec(shape, lambda i, _nd=nd: (0,) * _nd)


def _layer0(x2d, prefix, wts, *, ns, ts, chunk, base_pos):
    n = x2d.shape[0]
    tb = ns * ts
    nblk = n // tb
    has_cache = prefix is not None
    padk = SWA_KEYS - WINDOW - chunk
    rows4 = SWA_REP * chunk
    slope = np.repeat(2.0 ** (-(np.arange(SWA_HEADS) + 1.0)), chunk).reshape(SWA_KV_HEADS, rows4, 1)
    slope = jnp.asarray(np.broadcast_to(slope, (SWA_KV_HEADS, rows4, LANES)).astype(np.float32))
    sink = jnp.broadcast_to(jnp.repeat(wts["sinks"].astype(F32), chunk).reshape(SWA_KV_HEADS, rows4, 1),
                            (SWA_KV_HEADS, rows4, LANES))
    consts = [wts["g0"], wts["w_in0"], wts["w_pool"], wts["pool_scale"], wts["qg0"], wts["kg0"], wts["seg"],
              slope, sink, wts["w_out0"]]
    in_specs = [pl.BlockSpec((tb, D_MODEL), lambda i: (i, 0))]
    args = [x2d]
    if has_cache:
        for a in prefix:
            in_specs.append(_const_spec(a.shape))
            args.append(a)
    for a in consts:
        in_specs.append(_const_spec(a.shape))
        args.append(a)
    out_shape = [jax.ShapeDtypeStruct((n, D_MODEL), F32),
                 jax.ShapeDtypeStruct((ns, 16, C_POOL), F32),
                 jax.ShapeDtypeStruct((ns, WINDOW, LANES), F32),
                 jax.ShapeDtypeStruct((ns, WINDOW, LANES), F32)]
    out_specs = [pl.BlockSpec((tb, D_MODEL), lambda i: (i, 0)),
                 _const_spec((ns, 16, C_POOL)), _const_spec((ns, WINDOW, LANES)), _const_spec((ns, WINDOW, LANES))]
    scratch = [pltpu.VMEM((ns, 16 + ts, C_POOL), F32),
               pltpu.VMEM((ns, padk + WINDOW + ts, LANES), F32),
               pltpu.VMEM((ns, padk + WINDOW + ts, LANES), F32),
               pltpu.VMEM((tb, D_MODEL), F32)]
    kern = functools.partial(_l0_kernel, ns=ns, ts=ts, chunk=chunk, has_cache=has_cache,
                             base_pos=base_pos, nblk=nblk)
    return pl.pallas_call(
        kern, grid=(nblk,), in_specs=in_specs, out_specs=out_specs, out_shape=out_shape,
        scratch_shapes=scratch, compiler_params=_cparams(("arbitrary",)),
        name="layer0_prompt" if not has_cache else "layer0_sample")(*args)


def _fox_proj_kernel(*refs, tb, prompt):
    if prompt:
        (x_ref, g1_ref, w_ref, wf_ref, bf_ref, qg_ref, kg_ref, seg_ref, tri_ref, e_ref,
         k_ref, v_ref, lf_ref, sg_ref, qa_ref, ka_ref, va_ref, carry) = refs
    else:
        (x_ref, g1_ref, w_ref, wf_ref, bf_ref, qg_ref, kg_ref, seg_ref,
         k_ref, v_ref, lf_ref, sg_ref, q_ref) = refs
    i = pl.program_id(0)
    x = x_ref[...]
    h = _rms(x, g1_ref[...]).astype(BF16)
    z = jnp.dot(h, wf_ref[...], preferred_element_type=F32) + bf_ref[...]
    logf = jnp.minimum(z, 0.0) - jnp.log1p(jnp.exp(-jnp.abs(z)))
    lf_ref[...] = logf[:, 0:FOX_HEADS]
    seg = seg_ref[...]
    mc = D_MODEL

    q = jnp.dot(h, w_ref[:, 0:mc], preferred_element_type=F32)
    k = jnp.dot(h, w_ref[:, mc:2 * mc], preferred_element_type=F32)
    v = jnp.dot(h, w_ref[:, 2 * mc:3 * mc], preferred_element_type=F32)
    gate = jnp.dot(h, w_ref[:, 3 * mc:4 * mc], preferred_element_type=F32)
    v_ref[...] = v
    sg_ref[...] = _silu(gate)

    if prompt:
        @pl.when(i == 0)
        def _():
            carry[...] = jnp.zeros(carry.shape, F32)

        tri = tri_ref[...]
        hi, mid, lo = _split3(logf)
        cum = (jnp.dot(tri, hi, preferred_element_type=F32) + jnp.dot(tri, mid, preferred_element_type=F32)
               + jnp.dot(tri, lo, preferred_element_type=F32)) + carry[0:1, :]
        carry[0:1, :] = cum[tb - 1:tb, :]
        nh, nm, nl = _split3(-cum)
        biasall = jnp.dot(jnp.concatenate([nh, nm, nl], axis=1), e_ref[...], preferred_element_type=F32)
        lane = lax.broadcasted_iota(jnp.int32, (tb, LANES), 1)
        lo_l = lane < HEAD_DIM
        q_aug = jnp.where((lane >= HEAD_DIM) & (lane < HEAD_DIM + 3), 1.0, 0.0)
        v_aug = jnp.where(lane == HEAD_DIM, 1.0, 0.0)

    for j in range(FOX_HEADS // 2):
        cols = slice(LANES * j, LANES * (j + 1))
        qt = _head_rms_tile(q[:, cols], seg, qg_ref[:, cols]) * ATTN_SCALE
        kt = _head_rms_tile(k[:, cols], seg, kg_ref[:, cols])
        k_ref[:, cols] = kt
        if prompt:
            vt = v[:, cols]
            for half in range(2):
                hd = 2 * j + half
                if half:
                    qh, kh, vh = (pltpu.roll(a, HEAD_DIM, 1) for a in (qt, kt, vt))
                else:
                    qh, kh, vh = qt, kt, vt
                qa_ref[hd] = jnp.where(lo_l, qh, q_aug).astype(BF16)
                ka_ref[hd] = jnp.where(lo_l, kh, biasall[:, LANES * hd:LANES * (hd + 1)]).astype(BF16)
                va_ref[hd] = jnp.where(lo_l, vh, v_aug).astype(BF16)
        else:
            q_ref[:, cols] = qt


def _fox_proj(x2d, wts, *, tb, prompt):
    n = x2d.shape[0]
    nblk = n // tb
    consts = [wts["g1"], wts["w_in1"], wts["w_f"], wts["b_f"], wts["qg1"], wts["kg1"], wts["seg"]]
    if prompt:
        tri = jnp.asarray(np.tril(np.ones((tb, tb), np.float32)), BF16)
        e = np.zeros((3 * LANES, FOX_HEADS * LANES), np.float32)
        for part in range(3):
            for hd in range(FOX_HEADS):
                e[part * LANES + hd, hd * LANES + HEAD_DIM + part] = 1.0
        consts += [tri, jnp.asarray(e, BF16)]
    in_specs = [pl.BlockSpec((tb, D_MODEL), lambda i: (i, 0))] + [_const_spec(a.shape) for a in consts]
    row_spec = pl.BlockSpec((tb, D_MODEL), lambda i: (i, 0))
    out_shape = [jax.ShapeDtypeStruct((n, D_MODEL), F32), jax.ShapeDtypeStruct((n, D_MODEL), F32),
                 jax.ShapeDtypeStruct((n, FOX_HEADS), F32), jax.ShapeDtypeStruct((n, D_MODEL), F32)]
    out_specs = [row_spec, row_spec, pl.BlockSpec((tb, FOX_HEADS), lambda i: (i, 0)), row_spec]
    scratch = []
    if prompt:
        aug = jax.ShapeDtypeStruct((FOX_HEADS, n, LANES), BF16)
        aug_spec = pl.BlockSpec((FOX_HEADS, tb, LANES), lambda i: (0, i, 0))
        out_shape += [aug, aug, aug]
        out_specs += [aug_spec, aug_spec, aug_spec]
        scratch = [pltpu.VMEM((8, LANES), F32)]
    else:
        out_shape.append(jax.ShapeDtypeStruct((n, D_MODEL), F32))
        out_specs.append(row_spec)
    kern = functools.partial(_fox_proj_kernel, tb=tb, prompt=prompt)
    return pl.pallas_call(
        kern, grid=(nblk,), in_specs=in_specs, out_specs=out_specs, out_shape=out_shape,
        scratch_shapes=scratch, compiler_params=_cparams(("arbitrary",)),
        name="fox_proj_prompt" if prompt else "fox_proj_sample")(x2d, *consts)


def _fox_attn_kernel(qi_tab, ki_tab, q_ref, k_ref, v_ref, o_ref, m_scr, acc_scr, *, tq):
    j = pl.program_id(1)
    qi = qi_tab[j]
    ki = ki_tab[j]

    @pl.when(ki == 0)
    def _():
        m_scr[...] = jnp.full(m_scr.shape, NEG_INF, F32)
        acc_scr[...] = jnp.zeros(acc_scr.shape, F32)

    def step(masked):
        for hh in range(2):
            s = lax.dot_general(q_ref[hh], k_ref[hh], (((1,), (1,)), ((), ())), preferred_element_type=F32)
            if masked:
                row = lax.broadcasted_iota(jnp.int32, (tq, tq), 0)
                col = lax.broadcasted_iota(jnp.int32, (tq, tq), 1)
                s = jnp.where(col <= row, s, NEG_INF)
            m_old = m_scr[hh]
            m_new = jnp.maximum(m_old, jnp.max(s, axis=-1, keepdims=True))
            alpha = jnp.exp(m_old - m_new)
            p = jnp.exp(s - jnp.concatenate([m_new] * (tq // LANES), axis=1))
            pv = jnp.dot(p.astype(BF16), v_ref[hh], preferred_element_type=F32)
            acc_scr[hh] = alpha * acc_scr[hh] + pv
            m_scr[hh] = m_new

    @pl.when(ki < qi)
    def _():
        step(False)

    @pl.when(ki == qi)
    def _():
        step(True)
        lane = lax.broadcasted_iota(jnp.int32, (tq, LANES), 1)
        a0 = acc_scr[0]
        a1 = acc_scr[1]
        o0 = a0 / a0[:, HEAD_DIM:HEAD_DIM + 1]
        o1 = a1 / a1[:, HEAD_DIM:HEAD_DIM + 1]
        o_ref[...] = jnp.where(lane < HEAD_DIM, o0, pltpu.roll(o1, HEAD_DIM, 1))


def _fox_attn_prompt(qa, ka, va, *, tq):
    n = qa.shape[1]
    nq = n // tq
    pairs = [(a, b) for a in range(nq) for b in range(a + 1)]
    qi_tab = jnp.asarray(np.array([p[0] for p in pairs], np.int32))
    ki_tab = jnp.asarray(np.array([p[1] for p in pairs], np.int32))
    grid_spec = pltpu.PrefetchScalarGridSpec(
        num_scalar_prefetch=2,
        grid=(FOX_HEADS // 2, len(pairs)),
        in_specs=[pl.BlockSpec((2, tq, LANES), lambda p, j, qt, kt: (p, qt[j], 0)),
                  pl.BlockSpec((2, tq, LANES), lambda p, j, qt, kt: (p, kt[j], 0)),
                  pl.BlockSpec((2, tq, LANES), lambda p, j, qt, kt: (p, kt[j], 0))],
        out_specs=pl.BlockSpec((tq, LANES), lambda p, j, qt, kt: (qt[j], p)),
        scratch_shapes=[pltpu.VMEM((2, tq, LANES), F32), pltpu.VMEM((2, tq, LANES), F32)])
    return pl.pallas_call(
        functools.partial(_fox_attn_kernel, tq=tq), grid_spec=grid_spec,
        out_shape=jax.ShapeDtypeStruct((n, D_MODEL), F32),
        compiler_params=_cparams(("arbitrary", "arbitrary")), name="fox_attn_prompt")(qi_tab, ki_tab, qa, ka, va)


CUM_BLK = 512
NEW_PAD = 128


def _fox_sample_kernel(q_ref, kc_ref, vc_ref, lfc_ref, kn_ref, vn_ref, lfn_ref, tri_ref, o_ref, negc, negn,
                       *, t_new, p_len):
    qd = pl.program_id(1)
    rows = 4 * t_new

    @pl.when(qd == 0)
    def _():
        tri = tri_ref[...]
        carry = jnp.zeros((FOX_HEADS, 1), F32)
        for b in range(p_len // CUM_BLK):
            hi, mid, lo = _split3(lfc_ref[0, :, b * CUM_BLK:(b + 1) * CUM_BLK])
            c = (jnp.dot(hi, tri, preferred_element_type=F32) + jnp.dot(mid, tri, preferred_element_type=F32)
                 + jnp.dot(lo, tri, preferred_element_type=F32)) + carry
            negc[:, b * CUM_BLK:(b + 1) * CUM_BLK] = -c
            carry = c[:, CUM_BLK - 1:CUM_BLK]
        hi, mid, lo = _split3(lfn_ref[0])
        tn = tri[0:NEW_PAD, 0:NEW_PAD]
        c = (jnp.dot(hi, tn, preferred_element_type=F32) + jnp.dot(mid, tn, preferred_element_type=F32)
             + jnp.dot(lo, tn, preferred_element_type=F32)) + carry
        negn[...] = -c

    qq = q_ref[0]
    rhead = lax.broadcasted_iota(jnp.int32, (rows, 2 * LANES), 0) // t_new
    lhead = lax.broadcasted_iota(jnp.int32, (rows, 2 * LANES), 1) // HEAD_DIM
    qbd = jnp.where(rhead == lhead, jnp.concatenate([qq] * 4, axis=0), 0.0).astype(BF16)

    kc = kc_ref[0].astype(BF16)
    s_c = lax.dot_general(qbd, kc, (((1,), (1,)), ((), ())), preferred_element_type=F32)
    s_n = lax.dot_general(qbd, kn_ref[0].astype(BF16), (((1,), (1,)), ((), ())), preferred_element_type=F32)
    bc = jnp.concatenate([jnp.broadcast_to(negc[pl.ds(4 * qd + jh, 1), :], (t_new, p_len)) for jh in range(4)], axis=0)
    bn = jnp.concatenate([jnp.broadcast_to(negn[pl.ds(4 * qd + jh, 1), :], (t_new, NEW_PAD)) for jh in range(4)], axis=0)
    s_c = s_c + bc
    qrow = lax.broadcasted_iota(jnp.int32, (rows, NEW_PAD), 0) % t_new
    kcol = lax.broadcasted_iota(jnp.int32, (rows, NEW_PAD), 1)
    s_n = jnp.where(kcol <= qrow, s_n + bn, NEG_INF)
    m = jnp.maximum(jnp.max(s_c, axis=-1, keepdims=True), jnp.max(s_n, axis=-1, keepdims=True))
    p_c = jnp.exp(s_c - m)
    p_n = jnp.exp(s_n - m)
    den = jnp.sum(p_c, axis=-1, keepdims=True) + jnp.sum(p_n, axis=-1, keepdims=True)
    o = (jnp.dot(p_c.astype(BF16), vc_ref[0].astype(BF16), preferred_element_type=F32)
         + jnp.dot(p_n.astype(BF16), vn_ref[0].astype(BF16), preferred_element_type=F32)) / den
    lh = lax.broadcasted_iota(jnp.int32, (t_new, 2 * LANES), 1) // HEAD_DIM
    out = jnp.zeros((t_new, 2 * LANES), F32)
    for jh in range(4):
        out = jnp.where(lh == jh, o[jh * t_new:(jh + 1) * t_new], out)
    o_ref[0] = out


def _fox_attn_sample(q, cache_k, cache_v, lfc_t, k_new, v_new, lfn_t):
    b, t_new, _ = q.shape
    p_len = cache_k.shape[1]
    tri = jnp.asarray(np.triu(np.ones((CUM_BLK, CUM_BLK), np.float32)), BF16)
    quad = 2 * LANES
    in_specs = [pl.BlockSpec((1, t_new, quad), lambda r, d: (r, 0, d)),
                pl.BlockSpec((1, p_len, quad), lambda r, d: (r, 0, d)),
                pl.BlockSpec((1, p_len, quad), lambda r, d: (r, 0, d)),
                pl.BlockSpec((1, FOX_HEADS, p_len), lambda r, d: (r, 0, 0)),
                pl.BlockSpec((1, NEW_PAD, quad), lambda r, d: (r, 0, d)),
                pl.BlockSpec((1, NEW_PAD, quad), lambda r, d: (r, 0, d)),
                pl.BlockSpec((1, FOX_HEADS, NEW_PAD), lambda r, d: (r, 0, 0)),
                pl.BlockSpec((CUM_BLK, CUM_BLK), lambda r, d: (0, 0))]
    return pl.pallas_call(
        functools.partial(_fox_sample_kernel, t_new=t_new, p_len=p_len),
        grid=(b, D_MODEL // quad), in_specs=in_specs,
        out_specs=pl.BlockSpec((1, t_new, quad), lambda r, d: (r, 0, d)),
        out_shape=jax.ShapeDtypeStruct((b, t_new, D_MODEL), F32),
        scratch_shapes=[pltpu.VMEM((FOX_HEADS, p_len), F32), pltpu.VMEM((FOX_HEADS, NEW_PAD), F32)],
        compiler_params=_cparams(("arbitrary", "arbitrary")), name="fox_attn_sample")(
            q, cache_k, cache_v, lfc_t, k_new, v_new, lfn_t, tri)


def _out_proj_kernel(r_ref, a_ref, sg_ref, w_ref, y_ref):
    mixed = (a_ref[...] * sg_ref[...]).astype(BF16)
    y_ref[...] = r_ref[...] + jnp.dot(mixed, w_ref[...], preferred_element_type=F32)


def _out_proj(resid, attn, sg, w, *, tb):
    n = resid.shape[0]
    row = pl.BlockSpec((tb, D_MODEL), lambda i: (i, 0))
    return pl.pallas_call(
        _out_proj_kernel, grid=(n // tb,), in_specs=[row, row, row, _const_spec(w.shape)], out_specs=row,
        out_shape=jax.ShapeDtypeStruct((n, D_MODEL), F32),
        compiler_params=_cparams(("arbitrary",)), name="out_proj")(resid, attn, sg, w)


L0_TB = 512
FOX_TB = 256
ATTN_TQ = 512
SWA_CHUNK = 64


def kernel(x_prompt, x_sample, state_pool, cache_swa_k, cache_swa_v, cache_fox_k, cache_fox_v, cache_fox_logf,
           norm0_g, w_in0, w_pool, pool_scale, swa_qn_g, swa_kn_g, swa_sinks, w_out0,
           norm1_g, w_in1, b_forget, fox_qn_g, fox_kn_g, w_out1):
    nb, seq, _ = x_prompt.shape
    db, dseq, _ = x_sample.shape
    past_len = cache_fox_k.shape[1]
    assert nb == 1 and seq % L0_TB == 0 and seq % ATTN_TQ == 0 and dseq % 8 == 0

    seg = np.kron(np.eye(2, dtype=np.float32), np.ones((HEAD_DIM, HEAD_DIM), np.float32))
    mc = FOX_HEADS * HEAD_DIM
    wts = {
        "g0": norm0_g.reshape(1, D_MODEL), "w_in0": w_in0.astype(BF16), "w_pool": w_pool.astype(BF16),
        "pool_scale": pool_scale.reshape(1, C_POOL),
        "qg0": jnp.tile(swa_qn_g, SWA_HEADS).reshape(1, SWA_HEADS * HEAD_DIM),
        "kg0": jnp.tile(swa_kn_g, SWA_KV_HEADS).reshape(1, LANES),
        "seg": jnp.asarray(seg, BF16), "sinks": swa_sinks, "w_out0": w_out0.astype(BF16),
        "g1": norm1_g.reshape(1, D_MODEL), "w_in1": w_in1[:, :4 * mc].astype(BF16),
        "w_f": jnp.pad(w_in1[:, 4 * mc:], ((0, 0), (0, LANES - FOX_HEADS))).astype(BF16),
        "b_f": jnp.pad(b_forget.astype(F32), (0, LANES - FOX_HEADS)).reshape(1, LANES),
        "qg1": jnp.tile(fox_qn_g, FOX_HEADS).reshape(1, mc), "kg1": jnp.tile(fox_kn_g, FOX_HEADS).reshape(1, mc),
    }
    w_out1 = w_out1.astype(BF16)

    xp = x_prompt.reshape(seq, D_MODEL)
    y0p, pool_p, swk_p, swv_p = _layer0(xp, None, wts, ns=1, ts=L0_TB, chunk=SWA_CHUNK, base_pos=0)
    fk_p, fv_p, fl_p, sg_p, qa, ka, va = _fox_proj(y0p, wts, tb=FOX_TB, prompt=True)
    attn_p = _fox_attn_prompt(qa, ka, va, tq=ATTN_TQ)
    yp = _out_proj(y0p, attn_p, sg_p, w_out1, tb=L0_TB)

    xs = x_sample.reshape(db * dseq, D_MODEL)
    prefix = (jnp.pad(state_pool, ((0, 0), (1, 0), (0, 0))),
              cache_swa_k.reshape(db, WINDOW, LANES), cache_swa_v.reshape(db, WINDOW, LANES))
    y0s, pool_s, swk_s, swv_s = _layer0(xs, prefix, wts, ns=db, ts=dseq, chunk=dseq, base_pos=past_len)
    fk_s, fv_s, fl_s, sg_s, q_s = _fox_proj(y0s, wts, tb=FOX_TB, prompt=False)
    pad_rows = ((0, 0), (0, NEW_PAD - dseq), (0, 0))
    attn_s = _fox_attn_sample(
        q_s.reshape(db, dseq, D_MODEL),
        cache_fox_k.reshape(db, past_len, D_MODEL), cache_fox_v.reshape(db, past_len, D_MODEL),
        jnp.transpose(cache_fox_logf, (0, 2, 1)),
        jnp.pad(fk_s.reshape(db, dseq, D_MODEL), pad_rows), jnp.pad(fv_s.reshape(db, dseq, D_MODEL), pad_rows),
        jnp.pad(jnp.transpose(fl_s.reshape(db, dseq, FOX_HEADS), (0, 2, 1)), ((0, 0), (0, 0), (0, NEW_PAD - dseq))))
    ys = _out_proj(y0s, attn_s.reshape(db * dseq, D_MODEL), sg_s, w_out1, tb=db * dseq)

    return (yp.reshape(1, seq, D_MODEL), ys.reshape(db, dseq, D_MODEL),
            pool_p[:, 1:], pool_s[:, 1:],
            swk_p.reshape(1, WINDOW, SWA_KV_HEADS, HEAD_DIM), swv_p.reshape(1, WINDOW, SWA_KV_HEADS, HEAD_DIM),
            swk_s.reshape(db, WINDOW, SWA_KV_HEADS, HEAD_DIM), swv_s.reshape(db, WINDOW, SWA_KV_HEADS, HEAD_DIM),
            fk_p.reshape(1, seq, FOX_HEADS, HEAD_DIM), fv_p.reshape(1, seq, FOX_HEADS, HEAD_DIM),
            fl_p.reshape(1, seq, FOX_HEADS),
            fk_s.reshape(db, dseq, FOX_HEADS, HEAD_DIM), fv_s.reshape(db, dseq, FOX_HEADS, HEAD_DIM),
            fl_s.reshape(db, dseq, FOX_HEADS))
```

```python
import functools

import numpy as np
import jax
import jax.numpy as jnp
from jax import lax
from jax.experimental import pallas as pl
from jax.experimental.pallas import tpu as pltpu

F32 = jnp.float32
BF16 = jnp.bfloat16

D_MODEL = 1024
HEAD_DIM = 64
ATTN_SCALE = HEAD_DIM ** -0.5
POOL_WINDOWS = (2, 4, 8, 16)
C_POOL = 512
POOL_PAD = 15
SWA_HEADS = 8
SWA_KV_HEADS = 2
SWA_REP = 4
WINDOW = 128
FOX_HEADS = 16
NORM_EPS = 1e-6
NEG_INF = -1e30
LOG2E = 1.4426950408889634

LANES = 128
SWA_KEYS = 256
VMEM_LIMIT = 56 * 1024 * 1024


def _cparams(sem):
    return pltpu.CompilerParams(dimension_semantics=sem, vmem_limit_bytes=VMEM_LIMIT)


def _rms(x, g):
    ms = jnp.mean(x * x, axis=-1, keepdims=True)
    return x * lax.rsqrt(ms + NORM_EPS) * g


def _split2(x):
    hi = x.astype(BF16)
    lo = (x - hi.astype(F32)).astype(BF16)
    return hi, lo


def _split3(x):
    hi = x.astype(BF16)
    r = x - hi.astype(F32)
    mid = r.astype(BF16)
    lo = (r - mid.astype(F32)).astype(BF16)
    return hi, mid, lo


def _head_rms_tile(x, seg, g):
    hi, lo = _split2(x * x)
    ss = jnp.dot(hi, seg, preferred_element_type=F32) + jnp.dot(lo, seg, preferred_element_type=F32)
    return x * lax.rsqrt(ss * (1.0 / HEAD_DIM) + NORM_EPS) * g


def _silu(g):
    return g / (1.0 + jnp.exp(-g))


def _l0_kernel(*refs, ns, ts, chunk, has_cache, base_pos, nblk):
    if has_cache:
        (x_ref, pp_ref, kp_ref, vp_ref, g0_ref, win_ref, wpool_ref, pscale_ref, qg_ref, kg_ref, seg_ref,
         slope_ref, sink_ref, wout_ref, y_ref, ps_ref, ks_ref, vs_ref, uext, kext, vext, mix) = refs
    else:
        (x_ref, g0_ref, win_ref, wpool_ref, pscale_ref, qg_ref, kg_ref, seg_ref,
         slope_ref, sink_ref, wout_ref, y_ref, ps_ref, ks_ref, vs_ref, uext, kext, vext, mix) = refs
    padk = SWA_KEYS - WINDOW - chunk
    hist = padk + WINDOW
    i = pl.program_id(0)

    x = x_ref[...]
    h = _rms(x, g0_ref[...]).astype(BF16)
    proj = jnp.dot(h, win_ref[...], preferred_element_type=F32)
    u = proj[:, 0:C_POOL]
    gate = proj[:, 1280:2304]
    seg = seg_ref[...]
    qn = [_head_rms_tile(proj[:, 512 + LANES * j:512 + LANES * (j + 1)], seg,
                         qg_ref[:, LANES * j:LANES * (j + 1)]) * ATTN_SCALE for j in range(4)]
    kn = _head_rms_tile(proj[:, 1024:1152], seg, kg_ref[...])
    v = proj[:, 1152:1280]

    if has_cache:
        for s in range(ns):
            uext[s, 0:16, :] = pp_ref[s]
            kext[s, 0:padk, :] = jnp.zeros((padk, LANES), F32)
            vext[s, 0:padk, :] = jnp.zeros((padk, LANES), F32)
            kext[s, padk:hist, :] = kp_ref[s]
            vext[s, padk:hist, :] = vp_ref[s]
    else:
        @pl.when(i == 0)
        def _():
            uext[0, 0:16, :] = jnp.zeros((16, C_POOL), F32)
            kext[0, 0:hist, :] = jnp.zeros((hist, LANES), F32)
            vext[0, 0:hist, :] = jnp.zeros((hist, LANES), F32)

        @pl.when(i > 0)
        def _():
            uext[0, 0:16, :] = uext[0, ts:ts + 16, :]
            kext[0, 0:hist, :] = kext[0, ts:ts + hist, :]
            vext[0, 0:hist, :] = vext[0, ts:ts + hist, :]

    for s in range(ns):
        uext[s, 16:16 + ts, :] = u[s * ts:(s + 1) * ts]
        kext[s, hist:hist + ts, :] = kn[s * ts:(s + 1) * ts]
        vext[s, hist:hist + ts, :] = v[s * ts:(s + 1) * ts]

    pos = base_pos + i * ts + lax.broadcasted_iota(jnp.int32, (ts, LANES), 0)
    for s in range(ns):
        for g, w in enumerate(POOL_WINDOWS):
            cols = slice(LANES * g, LANES * (g + 1))
            acc = uext[s, 16:16 + ts, cols]
            cur = acc
            for j in range(1, w):
                acc = acc + uext[s, 16 - j:16 - j + ts, cols]
            cnt = jnp.minimum(pos + 1, w).astype(F32)
            mix[s * ts:(s + 1) * ts, cols] = acc / cnt - cur
    for g in range(4):
        cols = slice(LANES * g, LANES * (g + 1))
        d = mix[:, cols].astype(BF16)
        mix[:, cols] = jnp.dot(d, wpool_ref[g], preferred_element_type=F32) * pscale_ref[:, cols]

    rows4 = SWA_REP * chunk
    lo_c = lax.broadcasted_iota(jnp.int32, (chunk, LANES), 1) < HEAD_DIM
    lo_k = lax.broadcasted_iota(jnp.int32, (SWA_KEYS, LANES), 1) < HEAD_DIM
    kj = lax.broadcasted_iota(jnp.int32, (rows4, SWA_KEYS), 1)
    qi = lax.broadcasted_iota(jnp.int32, (rows4, SWA_KEYS), 0) % chunk
    absrel = jnp.abs(qi + WINDOW - (kj - padk)).astype(F32)
    bias = []
    sinkc = []
    for g in range(SWA_KV_HEADS):
        sl = slope_ref[g]
        bias.append(-jnp.concatenate([sl, sl], axis=1) * absrel)
        sinkc.append(sink_ref[g][:, 0:1])
    nch = ts // chunk
    for s in range(ns):
        for c in range(nch):
            r0 = c * chunk
            rows = slice(s * ts + r0, s * ts + r0 + chunk)
            if has_cache:
                thresh = padk
            else:
                thresh = padk + jnp.maximum(0, WINDOW - (i * nch + c) * chunk)
            kwin = kext[s, r0:r0 + SWA_KEYS, :]
            vwin = vext[s, r0:r0 + SWA_KEYS, :]
            krl = pltpu.roll(kwin, HEAD_DIM, 1)
            vrl = pltpu.roll(vwin, HEAD_DIM, 1)
            for g in range(SWA_KV_HEADS):
                if g == 0:
                    kd = jnp.where(lo_k, kwin, krl).astype(BF16)
                    vd = jnp.where(lo_k, vwin, vrl).astype(BF16)
                else:
                    kd = jnp.where(lo_k, krl, kwin).astype(BF16)
                    vd = jnp.where(lo_k, vrl, vwin).astype(BF16)
                q0 = qn[2 * g][rows]
                q1 = qn[2 * g + 1][rows]
                qs = jnp.concatenate([jnp.where(lo_c, q0, 0.0), jnp.where(lo_c, 0.0, q0),
                                      jnp.where(lo_c, q1, 0.0), jnp.where(lo_c, 0.0, q1)], axis=0).astype(BF16)
                sc = lax.dot_general(qs, kd, (((1,), (1,)), ((), ())), preferred_element_type=F32)
                sc = jnp.where(kj >= thresh, sc + bias[g], NEG_INF)
                m = jnp.maximum(jnp.max(sc, axis=-1, keepdims=True), sinkc[g])
                p = jnp.exp(sc - m)
                den = jnp.sum(p, axis=-1, keepdims=True) + jnp.exp(sinkc[g] - m)
                wgt = (p / den).astype(BF16)
                o = jnp.dot(wgt, vd, preferred_element_type=F32)
                for jj in range(2):
                    t = 2 * g + jj
                    mix[rows, C_POOL + LANES * t:C_POOL + LANES * (t + 1)] = jnp.where(
                        lo_c, o[(2 * jj) * chunk:(2 * jj + 1) * chunk], o[(2 * jj + 1) * chunk:(2 * jj + 2) * chunk])

    mixed = (mix[...] * _silu(gate)).astype(BF16)
    y_ref[...] = x + jnp.dot(mixed, wout_ref[...], preferred_element_type=F32)

    @pl.when(i == nblk - 1)
    def _():
        for s in range(ns):
            ps_ref[s] = uext[s, ts:ts + 16, :]
            ks_ref[s] = kext[s, padk + ts:padk + ts + WINDOW, :]
            vs_ref[s] = vext[s, padk + ts:padk + ts + WINDOW, :]


def _const_spec(shape):
    nd = len(shape)
    return pl.BlockSpec(shape, lambda i, _nd=nd: (0,) * _nd)


def _layer0(x2d, prefix, wts, *, ns, ts, chunk, base_pos):
    n = x2d.shape[0]
    tb = ns * ts
    nblk = n // tb
    has_cache = prefix is not None
    padk = SWA_KEYS - WINDOW - chunk
    rows4 = SWA_REP * chunk
    slope = np.repeat(2.0 ** (-(np.arange(SWA_HEADS) + 1.0)), chunk).reshape(SWA_KV_HEADS, rows4, 1)
    slope = jnp.asarray(np.broadcast_to(slope, (SWA_KV_HEADS, rows4, LANES)).astype(np.float32))
    sink = jnp.broadcast_to(jnp.repeat(wts["sinks"].astype(F32), chunk).reshape(SWA_KV_HEADS, rows4, 1),
                            (SWA_KV_HEADS, rows4, LANES))
    consts = [wts["g0"], wts["w_in0"], wts["w_pool"], wts["pool_scale"], wts["qg0"], wts["kg0"], wts["seg"],
              slope, sink, wts["w_out0"]]
    in_specs = [pl.BlockSpec((tb, D_MODEL), lambda i: (i, 0))]
    args = [x2d]
    if has_cache:
        for a in prefix:
            in_specs.append(_const_spec(a.shape))
            args.append(a)
    for a in consts:
        in_specs.append(_const_spec(a.shape))
        args.append(a)
    out_shape = [jax.ShapeDtypeStruct((n, D_MODEL), F32),
                 jax.ShapeDtypeStruct((ns, 16, C_POOL), F32),
                 jax.ShapeDtypeStruct((ns, WINDOW, LANES), F32),
                 jax.ShapeDtypeStruct((ns, WINDOW, LANES), F32)]
    out_specs = [pl.BlockSpec((tb, D_MODEL), lambda i: (i, 0)),
                 _const_spec((ns, 16, C_POOL)), _const_spec((ns, WINDOW, LANES)), _const_spec((ns, WINDOW, LANES))]
    scratch = [pltpu.VMEM((ns, 16 + ts, C_POOL), F32),
               pltpu.VMEM((ns, padk + WINDOW + ts, LANES), F32),
               pltpu.VMEM((ns, padk + WINDOW + ts, LANES), F32),
               pltpu.VMEM((tb, D_MODEL), F32)]
    kern = functools.partial(_l0_kernel, ns=ns, ts=ts, chunk=chunk, has_cache=has_cache,
                             base_pos=base_pos, nblk=nblk)
    return pl.pallas_call(
        kern, grid=(nblk,), in_specs=in_specs, out_specs=out_specs, out_shape=out_shape,
        scratch_shapes=scratch, compiler_params=_cparams(("arbitrary",)),
        name="layer0_prompt" if not has_cache else "layer0_sample")(*args)


def _fox_proj_kernel(*refs, tb, prompt):
    if prompt:
        (x_ref, g1_ref, w_ref, wf_ref, bf_ref, qg_ref, kg_ref, seg_ref, tri_ref, e_ref,
         k_ref, v_ref, lf_ref, sg_ref, qa_ref, ka_ref, va_ref, carry) = refs
    else:
        (x_ref, g1_ref, w_ref, wf_ref, bf_ref, qg_ref, kg_ref, seg_ref,
         k_ref, v_ref, lf_ref, sg_ref, q_ref) = refs
    i = pl.program_id(0)
    x = x_ref[...]
    h = _rms(x, g1_ref[...]).astype(BF16)
    z = jnp.dot(h, wf_ref[...], preferred_element_type=F32) + bf_ref[...]
    logf = jnp.minimum(z, 0.0) - jnp.log1p(jnp.exp(-jnp.abs(z)))
    lf_ref[...] = logf[:, 0:FOX_HEADS]
    seg = seg_ref[...]
    mc = D_MODEL

    q = jnp.dot(h, w_ref[:, 0:mc], preferred_element_type=F32)
    k = jnp.dot(h, w_ref[:, mc:2 * mc], preferred_element_type=F32)
    v = jnp.dot(h, w_ref[:, 2 * mc:3 * mc], preferred_element_type=F32)
    gate = jnp.dot(h, w_ref[:, 3 * mc:4 * mc], preferred_element_type=F32)
    v_ref[...] = v
    sg_ref[...] = _silu(gate)

    if prompt:
        @pl.when(i == 0)
        def _():
            carry[...] = jnp.zeros(carry.shape, F32)

        tri = tri_ref[...]
        hi, mid, lo = _split3(logf)
        cum = (jnp.dot(tri, hi, preferred_element_type=F32) + jnp.dot(tri, mid, preferred_element_type=F32)
               + jnp.dot(tri, lo, preferred_element_type=F32)) + carry[0:1, :]
        carry[0:1, :] = cum[tb - 1:tb, :]
        nh, nm, nl = _split3(-LOG2E * cum)
        biasall = jnp.dot(jnp.concatenate([nh, nm, nl], axis=1), e_ref[...], preferred_element_type=F32)
        lane = lax.broadcasted_iota(jnp.int32, (tb, LANES), 1)
        lo_l = lane < HEAD_DIM
        q_aug = jnp.where((lane >= HEAD_DIM) & (lane < HEAD_DIM + 3), 1.0, 0.0)
        v_aug = jnp.where(lane == HEAD_DIM, 1.0, 0.0)

    for j in range(FOX_HEADS // 2):
        cols = slice(LANES * j, LANES * (j + 1))
        qt = _head_rms_tile(q[:, cols], seg, qg_ref[:, cols]) * (ATTN_SCALE * LOG2E if prompt else ATTN_SCALE)
        kt = _head_rms_tile(k[:, cols], seg, kg_ref[:, cols])
        k_ref[:, cols] = kt
        if prompt:
            vt = v[:, cols]
            for half in range(2):
                hd = 2 * j + half
                if half:
                    qh, kh, vh = (pltpu.roll(a, HEAD_DIM, 1) for a in (qt, kt, vt))
                else:
                    qh, kh, vh = qt, kt, vt
                qa_ref[hd] = jnp.where(lo_l, qh, q_aug).astype(BF16)
                ka_ref[hd] = jnp.where(lo_l, kh, biasall[:, LANES * hd:LANES * (hd + 1)]).astype(BF16)
                va_ref[hd] = jnp.where(lo_l, vh, v_aug).astype(BF16)
        else:
            q_ref[:, cols] = qt


def _fox_proj(x2d, wts, *, tb, prompt):
    n = x2d.shape[0]
    nblk = n // tb
    consts = [wts["g1"], wts["w_in1"], wts["w_f"], wts["b_f"], wts["qg1"], wts["kg1"], wts["seg"]]
    if prompt:
        tri = jnp.asarray(np.tril(np.ones((tb, tb), np.float32)), BF16)
        e = np.zeros((3 * LANES, FOX_HEADS * LANES), np.float32)
        for part in range(3):
            for hd in range(FOX_HEADS):
                e[part * LANES + hd, hd * LANES + HEAD_DIM + part] = 1.0
        consts += [tri, jnp.asarray(e, BF16)]
    in_specs = [pl.BlockSpec((tb, D_MODEL), lambda i: (i, 0))] + [_const_spec(a.shape) for a in consts]
    row_spec = pl.BlockSpec((tb, D_MODEL), lambda i: (i, 0))
    out_shape = [jax.ShapeDtypeStruct((n, D_MODEL), F32), jax.ShapeDtypeStruct((n, D_MODEL), F32),
                 jax.ShapeDtypeStruct((n, FOX_HEADS), F32), jax.ShapeDtypeStruct((n, D_MODEL), F32)]
    out_specs = [row_spec, row_spec, pl.BlockSpec((tb, FOX_HEADS), lambda i: (i, 0)), row_spec]
    scratch = []
    if prompt:
        aug = jax.ShapeDtypeStruct((FOX_HEADS, n, LANES), BF16)
        aug_spec = pl.BlockSpec((FOX_HEADS, tb, LANES), lambda i: (0, i, 0))
        out_shape += [aug, aug, aug]
        out_specs += [aug_spec, aug_spec, aug_spec]
        scratch = [pltpu.VMEM((8, LANES), F32)]
    else:
        out_shape.append(jax.ShapeDtypeStruct((n, D_MODEL), F32))
        out_specs.append(row_spec)
    kern = functools.partial(_fox_proj_kernel, tb=tb, prompt=prompt)
    return pl.pallas_call(
        kern, grid=(nblk,), in_specs=in_specs, out_specs=out_specs, out_shape=out_shape,
        scratch_shapes=scratch, compiler_params=_cparams(("arbitrary",)),
        name="fox_proj_prompt" if prompt else "fox_proj_sample")(x2d, *consts)


def _fox_attn_kernel(q_ref, k_ref, v_ref, o_ref, m_scr, acc_scr, *, tq):
    qi = pl.program_id(1)
    m_scr[...] = jnp.full(m_scr.shape, NEG_INF, F32)
    acc_scr[...] = jnp.zeros(acc_scr.shape, F32)

    def step(kb, masked):
        off = pl.multiple_of(kb * tq, tq)
        for hh in range(2):
            k = k_ref[hh, pl.ds(off, tq), :]
            v = v_ref[hh, pl.ds(off, tq), :]
            s = lax.dot_general(q_ref[hh], k, (((1,), (1,)), ((), ())), preferred_element_type=F32)
            if masked:
                row = lax.broadcasted_iota(jnp.int32, (tq, tq), 0)
                col = lax.broadcasted_iota(jnp.int32, (tq, tq), 1)
                s = jnp.where(col <= row, s, NEG_INF)
            m_old = m_scr[hh]
            m_new = jnp.maximum(m_old, jnp.max(s, axis=-1, keepdims=True))
            alpha = jnp.exp2(m_old - m_new)
            p = jnp.exp2(s - jnp.concatenate([m_new] * (tq // LANES), axis=1))
            pv = jnp.dot(p.astype(BF16), v, preferred_element_type=F32)
            acc_scr[hh] = alpha * acc_scr[hh] + pv
            m_scr[hh] = m_new

    def body(kb, carry):
        step(kb, False)
        return carry

    lax.fori_loop(0, qi, body, 0)
    step(qi, True)
    lane = lax.broadcasted_iota(jnp.int32, (tq, LANES), 1)
    a0 = acc_scr[0]
    a1 = acc_scr[1]
    o0 = a0 / a0[:, HEAD_DIM:HEAD_DIM + 1]
    o1 = a1 / a1[:, HEAD_DIM:HEAD_DIM + 1]
    o_ref[...] = jnp.where(lane < HEAD_DIM, o0, pltpu.roll(o1, HEAD_DIM, 1))


def _fox_attn_prompt(qa, ka, va, *, tq):
    n = qa.shape[1]
    kv_spec = pl.BlockSpec((2, n, LANES), lambda p, i: (p, 0, 0))
    return pl.pallas_call(
        functools.partial(_fox_attn_kernel, tq=tq), grid=(FOX_HEADS // 2, n // tq),
        in_specs=[pl.BlockSpec((2, tq, LANES), lambda p, i: (p, i, 0)), kv_spec, kv_spec],
        out_specs=pl.BlockSpec((tq, LANES), lambda p, i: (i, p)),
        out_shape=jax.ShapeDtypeStruct((n, D_MODEL), F32),
        scratch_shapes=[pltpu.VMEM((2, tq, LANES), F32), pltpu.VMEM((2, tq, LANES), F32)],
        compiler_params=_cparams(("arbitrary", "arbitrary")), name="fox_attn_prompt")(qa, ka, va)


CUM_BLK = 512
NEW_PAD = 128


def _fox_sample_kernel(q_ref, kc_ref, vc_ref, lfc_ref, kn_ref, vn_ref, lfn_ref, tri_ref, o_ref, negc, negn,
                       *, t_new, p_len):
    qd = pl.program_id(1)
    rows = 4 * t_new

    @pl.when(qd == 0)
    def _():
        tri = tri_ref[...]
        carry = jnp.zeros((FOX_HEADS, 1), F32)
        for b in range(p_len // CUM_BLK):
            hi, mid, lo = _split3(lfc_ref[0, :, b * CUM_BLK:(b + 1) * CUM_BLK])
            c = (jnp.dot(hi, tri, preferred_element_type=F32) + jnp.dot(mid, tri, preferred_element_type=F32)
                 + jnp.dot(lo, tri, preferred_element_type=F32)) + carry
            negc[:, b * CUM_BLK:(b + 1) * CUM_BLK] = -c
            carry = c[:, CUM_BLK - 1:CUM_BLK]
        hi, mid, lo = _split3(lfn_ref[0])
        tn = tri[0:NEW_PAD, 0:NEW_PAD]
        c = (jnp.dot(hi, tn, preferred_element_type=F32) + jnp.dot(mid, tn, preferred_element_type=F32)
             + jnp.dot(lo, tn, preferred_element_type=F32)) + carry
        negn[...] = -c

    qq = q_ref[0]
    rhead = lax.broadcasted_iota(jnp.int32, (rows, 2 * LANES), 0) // t_new
    lhead = lax.broadcasted_iota(jnp.int32, (rows, 2 * LANES), 1) // HEAD_DIM
    qbd = jnp.where(rhead == lhead, jnp.concatenate([qq] * 4, axis=0), 0.0).astype(BF16)

    kc = kc_ref[0].astype(BF16)
    s_c = lax.dot_general(qbd, kc, (((1,), (1,)), ((), ())), preferred_element_type=F32)
    s_n = lax.dot_general(qbd, kn_ref[0].astype(BF16), (((1,), (1,)), ((), ())), preferred_element_type=F32)
    bc = jnp.concatenate([jnp.broadcast_to(negc[pl.ds(4 * qd + jh, 1), :], (t_new, p_len)) for jh in range(4)], axis=0)
    bn = jnp.concatenate([jnp.broadcast_to(negn[pl.ds(4 * qd + jh, 1), :], (t_new, NEW_PAD)) for jh in range(4)], axis=0)
    s_c = s_c + bc
    qrow = lax.broadcasted_iota(jnp.int32, (rows, NEW_PAD), 0) % t_new
    kcol = lax.broadcasted_iota(jnp.int32, (rows, NEW_PAD), 1)
    s_n = jnp.where(kcol <= qrow, s_n + bn, NEG_INF)
    m = jnp.maximum(jnp.max(s_c, axis=-1, keepdims=True), jnp.max(s_n, axis=-1, keepdims=True))
    p_c = jnp.exp(s_c - m)
    p_n = jnp.exp(s_n - m)
    den = jnp.sum(p_c, axis=-1, keepdims=True) + jnp.sum(p_n, axis=-1, keepdims=True)
    o = (jnp.dot(p_c.astype(BF16), vc_ref[0].astype(BF16), preferred_element_type=F32)
         + jnp.dot(p_n.astype(BF16), vn_ref[0].astype(BF16), preferred_element_type=F32)) / den
    lh = lax.broadcasted_iota(jnp.int32, (t_new, 2 * LANES), 1) // HEAD_DIM
    out = jnp.zeros((t_new, 2 * LANES), F32)
    for jh in range(4):
        out = jnp.where(lh == jh, o[jh * t_new:(jh + 1) * t_new], out)
    o_ref[0] = out


def _fox_attn_sample(q, cache_k, cache_v, lfc_t, k_new, v_new, lfn_t):
    b, t_new, _ = q.shape
    p_len = cache_k.shape[1]
    tri = jnp.asarray(np.triu(np.ones((CUM_BLK, CUM_BLK), np.float32)), BF16)
    quad = 2 * LANES
    in_specs = [pl.BlockSpec((1, t_new, quad), lambda r, d: (r, 0, d)),
                pl.BlockSpec((1, p_len, quad), lambda r, d: (r, 0, d)),
                pl.BlockSpec((1, p_len, quad), lambda r, d: (r, 0, d)),
                pl.BlockSpec((1, FOX_HEADS, p_len), lambda r, d: (r, 0, 0)),
                pl.BlockSpec((1, NEW_PAD, quad), lambda r, d: (r, 0, d)),
                pl.BlockSpec((1, NEW_PAD, quad), lambda r, d: (r, 0, d)),
                pl.BlockSpec((1, FOX_HEADS, NEW_PAD), lambda r, d: (r, 0, 0)),
                pl.BlockSpec((CUM_BLK, CUM_BLK), lambda r, d: (0, 0))]
    return pl.pallas_call(
        functools.partial(_fox_sample_kernel, t_new=t_new, p_len=p_len),
        grid=(b, D_MODEL // quad), in_specs=in_specs,
        out_specs=pl.BlockSpec((1, t_new, quad), lambda r, d: (r, 0, d)),
        out_shape=jax.ShapeDtypeStruct((b, t_new, D_MODEL), F32),
        scratch_shapes=[pltpu.VMEM((FOX_HEADS, p_len), F32), pltpu.VMEM((FOX_HEADS, NEW_PAD), F32)],
        compiler_params=_cparams(("arbitrary", "arbitrary")), name="fox_attn_sample")(
            q, cache_k, cache_v, lfc_t, k_new, v_new, lfn_t, tri)


def _out_proj_kernel(r_ref, a_ref, sg_ref, w_ref, y_ref):
    mixed = (a_ref[...] * sg_ref[...]).astype(BF16)
    y_ref[...] = r_ref[...] + jnp.dot(mixed, w_ref[...], preferred_element_type=F32)


def _out_proj(resid, attn, sg, w, *, tb):
    n = resid.shape[0]
    row = pl.BlockSpec((tb, D_MODEL), lambda i: (i, 0))
    return pl.pallas_call(
        _out_proj_kernel, grid=(n // tb,), in_specs=[row, row, row, _const_spec(w.shape)], out_specs=row,
        out_shape=jax.ShapeDtypeStruct((n, D_MODEL), F32),
        compiler_params=_cparams(("arbitrary",)), name="out_proj")(resid, attn, sg, w)


L0_TB = 512
FOX_TB = 256
ATTN_TQ = 512
SWA_CHUNK = 64


def kernel(x_prompt, x_sample, state_pool, cache_swa_k, cache_swa_v, cache_fox_k, cache_fox_v, cache_fox_logf,
           norm0_g, w_in0, w_pool, pool_scale, swa_qn_g, swa_kn_g, swa_sinks, w_out0,
           norm1_g, w_in1, b_forget, fox_qn_g, fox_kn_g, w_out1):
    nb, seq, _ = x_prompt.shape
    db, dseq, _ = x_sample.shape
    past_len = cache_fox_k.shape[1]
    assert nb == 1 and seq % L0_TB == 0 and seq % ATTN_TQ == 0 and dseq % 8 == 0

    seg = np.kron(np.eye(2, dtype=np.float32), np.ones((HEAD_DIM, HEAD_DIM), np.float32))
    mc = FOX_HEADS * HEAD_DIM
    wts = {
        "g0": norm0_g.reshape(1, D_MODEL), "w_in0": w_in0.astype(BF16), "w_pool": w_pool.astype(BF16),
        "pool_scale": pool_scale.reshape(1, C_POOL),
        "qg0": jnp.tile(swa_qn_g, SWA_HEADS).reshape(1, SWA_HEADS * HEAD_DIM),
        "kg0": jnp.tile(swa_kn_g, SWA_KV_HEADS).reshape(1, LANES),
        "seg": jnp.asarray(seg, BF16), "sinks": swa_sinks, "w_out0": w_out0.astype(BF16),
        "g1": norm1_g.reshape(1, D_MODEL), "w_in1": w_in1[:, :4 * mc].astype(BF16),
        "w_f": jnp.pad(w_in1[:, 4 * mc:], ((0, 0), (0, LANES - FOX_HEADS))).astype(BF16),
        "b_f": jnp.pad(b_forget.astype(F32), (0, LANES - FOX_HEADS)).reshape(1, LANES),
        "qg1": jnp.tile(fox_qn_g, FOX_HEADS).reshape(1, mc), "kg1": jnp.tile(fox_kn_g, FOX_HEADS).reshape(1, mc),
    }
    w_out1 = w_out1.astype(BF16)

    xp = x_prompt.reshape(seq, D_MODEL)
    y0p, pool_p, swk_p, swv_p = _layer0(xp, None, wts, ns=1, ts=L0_TB, chunk=SWA_CHUNK, base_pos=0)
    fk_p, fv_p, fl_p, sg_p, qa, ka, va = _fox_proj(y0p, wts, tb=FOX_TB, prompt=True)
    attn_p = _fox_attn_prompt(qa, ka, va, tq=ATTN_TQ)
    yp = _out_proj(y0p, attn_p, sg_p, w_out1, tb=L0_TB)

    xs = x_sample.reshape(db * dseq, D_MODEL)
    prefix = (jnp.pad(state_pool, ((0, 0), (1, 0), (0, 0))),
              cache_swa_k.reshape(db, WINDOW, LANES), cache_swa_v.reshape(db, WINDOW, LANES))
    y0s, pool_s, swk_s, swv_s = _layer0(xs, prefix, wts, ns=db, ts=dseq, chunk=dseq, base_pos=past_len)
    fk_s, fv_s, fl_s, sg_s, q_s = _fox_proj(y0s, wts, tb=FOX_TB, prompt=False)
    pad_rows = ((0, 0), (0, NEW_PAD - dseq), (0, 0))
    attn_s = _fox_attn_sample(
        q_s.reshape(db, dseq, D_MODEL),
        cache_fox_k.reshape(db, past_len, D_MODEL), cache_fox_v.reshape(db, past_len, D_MODEL),
        jnp.transpose(cache_fox_logf, (0, 2, 1)),
        jnp.pad(fk_s.reshape(db, dseq, D_MODEL), pad_rows), jnp.pad(fv_s.reshape(db, dseq, D_MODEL), pad_rows),
        jnp.pad(jnp.transpose(fl_s.reshape(db, dseq, FOX_HEADS), (0, 2, 1)), ((0, 0), (0, 0), (0, NEW_PAD - dseq))))
    ys = _out_proj(y0s, attn_s.reshape(db * dseq, D_MODEL), sg_s, w_out1, tb=db * dseq)

    return (yp.reshape(1, seq, D_MODEL), ys.reshape(db, dseq, D_MODEL),
            pool_p[:, 1:], pool_s[:, 1:],
            swk_p.reshape(1, WINDOW, SWA_KV_HEADS, HEAD_DIM), swv_p.reshape(1, WINDOW, SWA_KV_HEADS, HEAD_DIM),
            swk_s.reshape(db, WINDOW, SWA_KV_HEADS, HEAD_DIM), swv_s.reshape(db, WINDOW, SWA_KV_HEADS, HEAD_DIM),
            fk_p.reshape(1, seq, FOX_HEADS, HEAD_DIM), fv_p.reshape(1, seq, FOX_HEADS, HEAD_DIM),
            fl_p.reshape(1, seq, FOX_HEADS),
            fk_s.reshape(db, dseq, FOX_HEADS, HEAD_DIM), fv_s.reshape(db, dseq, FOX_HEADS, HEAD_DIM),
            fl_s.reshape(db, dseq, FOX_HEADS))
```

```python
import functools

import numpy as np
import jax
import jax.numpy as jnp
from jax import lax
from jax.experimental import pallas as pl
from jax.experimental.pallas import tpu as pltpu

F32 = jnp.float32
BF16 = jnp.bfloat16

D_MODEL = 1024
HEAD_DIM = 64
ATTN_SCALE = HEAD_DIM ** -0.5
POOL_WINDOWS = (2, 4, 8, 16)
C_POOL = 512
POOL_PAD = 15
SWA_HEADS = 8
SWA_KV_HEADS = 2
SWA_REP = 4
WINDOW = 128
FOX_HEADS = 16
NORM_EPS = 1e-6
NEG_INF = -1e30
LOG2E = 1.4426950408889634

LANES = 128
SWA_KEYS = 256
VMEM_LIMIT = 56 * 1024 * 1024


def _cparams(sem):
    return pltpu.CompilerParams(dimension_semantics=sem, vmem_limit_bytes=VMEM_LIMIT)


def _rms(x, g):
    ms = jnp.mean(x * x, axis=-1, keepdims=True)
    return x * lax.rsqrt(ms + NORM_EPS) * g


def _split2(x):
    hi = x.astype(BF16)
    lo = (x - hi.astype(F32)).astype(BF16)
    return hi, lo


def _split3(x):
    hi = x.astype(BF16)
    r = x - hi.astype(F32)
    mid = r.astype(BF16)
    lo = (r - mid.astype(F32)).astype(BF16)
    return hi, mid, lo


def _head_rms_tile(x, seg, g):
    hi, lo = _split2(x * x)
    ss = jnp.dot(hi, seg, preferred_element_type=F32) + jnp.dot(lo, seg, preferred_element_type=F32)
    return x * lax.rsqrt(ss * (1.0 / HEAD_DIM) + NORM_EPS) * g


def _silu(g):
    return g / (1.0 + jnp.exp(-g))


def _l0_kernel(*refs, ns, ts, chunk, has_cache, base_pos, nblk):
    if has_cache:
        (x_ref, pp_ref, kp_ref, vp_ref, g0_ref, win_ref, wpool_ref, pscale_ref, qg_ref, kg_ref, seg_ref,
         slope_ref, sink_ref, wout_ref, y_ref, ps_ref, ks_ref, vs_ref, uext, kext, vext, mix) = refs
    else:
        (x_ref, g0_ref, win_ref, wpool_ref, pscale_ref, qg_ref, kg_ref, seg_ref,
         slope_ref, sink_ref, wout_ref, y_ref, ps_ref, ks_ref, vs_ref, uext, kext, vext, mix) = refs
    padk = SWA_KEYS - WINDOW - chunk
    hist = padk + WINDOW
    i = pl.program_id(0)

    x = x_ref[...]
    h = _rms(x, g0_ref[...]).astype(BF16)
    proj = jnp.dot(h, win_ref[...], preferred_element_type=F32)
    u = proj[:, 0:C_POOL]
    gate = proj[:, 1280:2304]
    seg = seg_ref[...]
    qn = [_head_rms_tile(proj[:, 512 + LANES * j:512 + LANES * (j + 1)], seg,
                         qg_ref[:, LANES * j:LANES * (j + 1)]) * ATTN_SCALE for j in range(4)]
    kn = _head_rms_tile(proj[:, 1024:1152], seg, kg_ref[...])
    v = proj[:, 1152:1280]

    if has_cache:
        for s in range(ns):
            uext[s, 0:16, :] = pp_ref[s]
            kext[s, 0:padk, :] = jnp.zeros((padk, LANES), F32)
            vext[s, 0:padk, :] = jnp.zeros((padk, LANES), F32)
            kext[s, padk:hist, :] = kp_ref[s]
            vext[s, padk:hist, :] = vp_ref[s]
    else:
        @pl.when(i == 0)
        def _():
            uext[0, 0:16, :] = jnp.zeros((16, C_POOL), F32)
            kext[0, 0:hist, :] = jnp.zeros((hist, LANES), F32)
            vext[0, 0:hist, :] = jnp.zeros((hist, LANES), F32)

        @pl.when(i > 0)
        def _():
            uext[0, 0:16, :] = uext[0, ts:ts + 16, :]
            kext[0, 0:hist, :] = kext[0, ts:ts + hist, :]
            vext[0, 0:hist, :] = vext[0, ts:ts + hist, :]

    for s in range(ns):
        uext[s, 16:16 + ts, :] = u[s * ts:(s + 1) * ts]
        kext[s, hist:hist + ts, :] = kn[s * ts:(s + 1) * ts]
        vext[s, hist:hist + ts, :] = v[s * ts:(s + 1) * ts]

    pos = base_pos + i * ts + lax.broadcasted_iota(jnp.int32, (ts, LANES), 0)
    for s in range(ns):
        for g, w in enumerate(POOL_WINDOWS):
            cols = slice(LANES * g, LANES * (g + 1))
            acc = uext[s, 16:16 + ts, cols]
            cur = acc
            for j in range(1, w):
                acc = acc + uext[s, 16 - j:16 - j + ts, cols]
            cnt = jnp.minimum(pos + 1, w).astype(F32)
            mix[s * ts:(s + 1) * ts, cols] = acc / cnt - cur
    for g in range(4):
        cols = slice(LANES * g, LANES * (g + 1))
        d = mix[:, cols].astype(BF16)
        mix[:, cols] = jnp.dot(d, wpool_ref[g], preferred_element_type=F32) * pscale_ref[:, cols]

    rows4 = SWA_REP * chunk
    lo_c = lax.broadcasted_iota(jnp.int32, (chunk, LANES), 1) < HEAD_DIM
    lo_k = lax.broadcasted_iota(jnp.int32, (SWA_KEYS, LANES), 1) < HEAD_DIM
    kj = lax.broadcasted_iota(jnp.int32, (rows4, SWA_KEYS), 1)
    qi = lax.broadcasted_iota(jnp.int32, (rows4, SWA_KEYS), 0) % chunk
    absrel = jnp.abs(qi + WINDOW - (kj - padk)).astype(F32)
    bias = []
    sinkc = []
    for g in range(SWA_KV_HEADS):
        sl = slope_ref[g]
        bias.append(-jnp.concatenate([sl, sl], axis=1) * absrel)
        sinkc.append(sink_ref[g][:, 0:1])
    nch = ts // chunk
    for s in range(ns):
        for c in range(nch):
            r0 = c * chunk
            rows = slice(s * ts + r0, s * ts + r0 + chunk)
            if has_cache:
                thresh = padk
            else:
                thresh = padk + jnp.maximum(0, WINDOW - (i * nch + c) * chunk)
            kwin = kext[s, r0:r0 + SWA_KEYS, :]
            vwin = vext[s, r0:r0 + SWA_KEYS, :]
            krl = pltpu.roll(kwin, HEAD_DIM, 1)
            vrl = pltpu.roll(vwin, HEAD_DIM, 1)
            for g in range(SWA_KV_HEADS):
                if g == 0:
                    kd = jnp.where(lo_k, kwin, krl).astype(BF16)
                    vd = jnp.where(lo_k, vwin, vrl).astype(BF16)
                else:
                    kd = jnp.where(lo_k, krl, kwin).astype(BF16)
                    vd = jnp.where(lo_k, vrl, vwin).astype(BF16)
                q0 = qn[2 * g][rows]
                q1 = qn[2 * g + 1][rows]
                qs = jnp.concatenate([jnp.where(lo_c, q0, 0.0), jnp.where(lo_c, 0.0, q0),
                                      jnp.where(lo_c, q1, 0.0), jnp.where(lo_c, 0.0, q1)], axis=0).astype(BF16)
                sc = lax.dot_general(qs, kd, (((1,), (1,)), ((), ())), preferred_element_type=F32)
                sc = jnp.where(kj >= thresh, sc + bias[g], NEG_INF)
                m = jnp.maximum(jnp.max(sc, axis=-1, keepdims=True), sinkc[g])
                p = jnp.exp(sc - m)
                den = jnp.sum(p, axis=-1, keepdims=True) + jnp.exp(sinkc[g] - m)
                wgt = (p / den).astype(BF16)
                o = jnp.dot(wgt, vd, preferred_element_type=F32)
                for jj in range(2):
                    t = 2 * g + jj
                    mix[rows, C_POOL + LANES * t:C_POOL + LANES * (t + 1)] = jnp.where(
                        lo_c, o[(2 * jj) * chunk:(2 * jj + 1) * chunk], o[(2 * jj + 1) * chunk:(2 * jj + 2) * chunk])

    mixed = (mix[...] * _silu(gate)).astype(BF16)
    y_ref[...] = x + jnp.dot(mixed, wout_ref[...], preferred_element_type=F32)

    @pl.when(i == nblk - 1)
    def _():
        for s in range(ns):
            ps_ref[s] = uext[s, ts:ts + 16, :]
            ks_ref[s] = kext[s, padk + ts:padk + ts + WINDOW, :]
            vs_ref[s] = vext[s, padk + ts:padk + ts + WINDOW, :]


def _const_spec(shape):
    nd = len(shape)
    return pl.BlockSpec(shape, lambda i, _nd=nd: (0,) * _nd)


def _layer0(x2d, prefix, wts, *, ns, ts, chunk, base_pos):
    n = x2d.shape[0]
    tb = ns * ts
    nblk = n // tb
    has_cache = prefix is not None
    padk = SWA_KEYS - WINDOW - chunk
    rows4 = SWA_REP * chunk
    slope = np.repeat(2.0 ** (-(np.arange(SWA_HEADS) + 1.0)), chunk).reshape(SWA_KV_HEADS, rows4, 1)
    slope = jnp.asarray(np.broadcast_to(slope, (SWA_KV_HEADS, rows4, LANES)).astype(np.float32))
    sink = jnp.broadcast_to(jnp.repeat(wts["sinks"].astype(F32), chunk).reshape(SWA_KV_HEADS, rows4, 1),
                            (SWA_KV_HEADS, rows4, LANES))
    consts = [wts["g0"], wts["w_in0"], wts["w_pool"], wts["pool_scale"], wts["qg0"], wts["kg0"], wts["seg"],
              slope, sink, wts["w_out0"]]
    in_specs = [pl.BlockSpec((tb, D_MODEL), lambda i: (i, 0))]
    args = [x2d]
    if has_cache:
        for a in prefix:
            in_specs.append(_const_spec(a.shape))
            args.append(a)
    for a in consts:
        in_specs.append(_const_spec(a.shape))
        args.append(a)
    out_shape = [jax.ShapeDtypeStruct((n, D_MODEL), F32),
                 jax.ShapeDtypeStruct((ns, 16, C_POOL), F32),
                 jax.ShapeDtypeStruct((ns, WINDOW, LANES), F32),
                 jax.ShapeDtypeStruct((ns, WINDOW, LANES), F32)]
    out_specs = [pl.BlockSpec((tb, D_MODEL), lambda i: (i, 0)),
                 _const_spec((ns, 16, C_POOL)), _const_spec((ns, WINDOW, LANES)), _const_spec((ns, WINDOW, LANES))]
    scratch = [pltpu.VMEM((ns, 16 + ts, C_POOL), F32),
               pltpu.VMEM((ns, padk + WINDOW + ts, LANES), F32),
               pltpu.VMEM((ns, padk + WINDOW + ts, LANES), F32),
               pltpu.VMEM((tb, D_MODEL), F32)]
    kern = functools.partial(_l0_kernel, ns=ns, ts=ts, chunk=chunk, has_cache=has_cache,
                             base_pos=base_pos, nblk=nblk)
    return pl.pallas_call(
        kern, grid=(nblk,), in_specs=in_specs, out_specs=out_specs, out_shape=out_shape,
        scratch_shapes=scratch, compiler_params=_cparams(("arbitrary",)),
        name="layer0_prompt" if not has_cache else "layer0_sample")(*args)


def _log_sigmoid(z):
    return jnp.minimum(z, 0.0) - jnp.log1p(jnp.exp(-jnp.abs(z)))


def _fox_proj_sample_kernel(x_ref, g1_ref, w_ref, wf_ref, bf_ref, qg_ref, kg_ref, seg_ref,
                            k_ref, v_ref, lf_ref, sg_ref, q_ref):
    x = x_ref[...]
    h = _rms(x, g1_ref[...]).astype(BF16)
    z = jnp.dot(h, wf_ref[...], preferred_element_type=F32) + bf_ref[...]
    lf_ref[...] = _log_sigmoid(z)[:, 0:FOX_HEADS]
    seg = seg_ref[...]
    mc = D_MODEL
    q = jnp.dot(h, w_ref[:, 0:mc], preferred_element_type=F32)
    k = jnp.dot(h, w_ref[:, mc:2 * mc], preferred_element_type=F32)
    v_ref[...] = jnp.dot(h, w_ref[:, 2 * mc:3 * mc], preferred_element_type=F32)
    sg_ref[...] = _silu(jnp.dot(h, w_ref[:, 3 * mc:4 * mc], preferred_element_type=F32))
    for j in range(FOX_HEADS // 2):
        cols = slice(LANES * j, LANES * (j + 1))
        q_ref[:, cols] = _head_rms_tile(q[:, cols], seg, qg_ref[:, cols]) * ATTN_SCALE
        k_ref[:, cols] = _head_rms_tile(k[:, cols], seg, kg_ref[:, cols])


def _fox_proj_sample(x2d, wts, *, tb):
    n = x2d.shape[0]
    consts = [wts["g1"], wts["w_in1"], wts["w_f"], wts["b_f"], wts["qg1"], wts["kg1"], wts["seg"]]
    row_spec = pl.BlockSpec((tb, D_MODEL), lambda i: (i, 0))
    in_specs = [row_spec] + [_const_spec(a.shape) for a in consts]
    row = jax.ShapeDtypeStruct((n, D_MODEL), F32)
    out_shape = [row, row, jax.ShapeDtypeStruct((n, FOX_HEADS), F32), row, row]
    out_specs = [row_spec, row_spec, pl.BlockSpec((tb, FOX_HEADS), lambda i: (i, 0)), row_spec, row_spec]
    return pl.pallas_call(
        _fox_proj_sample_kernel, grid=(n // tb,), in_specs=in_specs, out_specs=out_specs, out_shape=out_shape,
        compiler_params=_cparams(("arbitrary",)), name="fox_proj_sample")(x2d, *consts)


def _fox_proj_prompt_kernel(x_ref, g1_ref, wk_ref, wg_ref, wf_ref, bf_ref, wqt_ref, wvt_ref, gq_ref, kg_ref,
                            seg_ref, tri_ref, e_ref,
                            k_ref, vt_ref, lf_ref, sg_ref, qa_ref, ka_ref, va_ref, carry, *, tb):
    i = pl.program_id(0)
    x = x_ref[...]
    hf = _rms(x, g1_ref[...])
    h = hf.astype(BF16)
    ht = hf.T.astype(BF16)
    logf = _log_sigmoid(jnp.dot(h, wf_ref[...], preferred_element_type=F32) + bf_ref[...])
    lf_ref[...] = logf[:, 0:FOX_HEADS]
    sg_ref[...] = _silu(jnp.dot(h, wg_ref[...], preferred_element_type=F32))
    k = jnp.dot(h, wk_ref[...], preferred_element_type=F32)

    @pl.when(i == 0)
    def _():
        carry[...] = jnp.zeros(carry.shape, F32)

    tri = tri_ref[...]
    hi, mid, lo = _split3(logf)
    cum = (jnp.dot(tri, hi, preferred_element_type=F32) + jnp.dot(tri, mid, preferred_element_type=F32)
           + jnp.dot(tri, lo, preferred_element_type=F32)) + carry[0:1, :]
    carry[0:1, :] = cum[tb - 1:tb, :]
    nh, nm, nl = _split3(-LOG2E * cum)
    biasall = jnp.dot(jnp.concatenate([nh, nm, nl], axis=1), e_ref[...], preferred_element_type=F32)
    lo_l = lax.broadcasted_iota(jnp.int32, (tb, LANES), 1) < HEAD_DIM
    seg = seg_ref[...]
    for j in range(FOX_HEADS // 2):
        cols = slice(LANES * j, LANES * (j + 1))
        kt = _head_rms_tile(k[:, cols], seg, kg_ref[:, cols])
        k_ref[:, cols] = kt
        for half in range(2):
            hd = 2 * j + half
            kh = pltpu.roll(kt, HEAD_DIM, 1) if half else kt
            ka_ref[hd] = jnp.where(lo_l, kh, biasall[:, LANES * hd:LANES * (hd + 1)]).astype(BF16)

    qt = jnp.dot(wqt_ref[...], ht, preferred_element_type=F32)
    vt = jnp.dot(wvt_ref[...], ht, preferred_element_type=F32)
    vt_ref[...] = vt
    srow = lax.broadcasted_iota(jnp.int32, (HEAD_DIM, tb), 0)
    q_aug = jnp.where(srow < 3, 1.0, 0.0)
    v_aug = jnp.where(srow == 0, 1.0, 0.0)
    gq = jnp.concatenate([gq_ref[...]] * (tb // LANES), axis=1) * (ATTN_SCALE * LOG2E)
    for hd in range(FOX_HEADS):
        rows = slice(HEAD_DIM * hd, HEAD_DIM * (hd + 1))
        qh = qt[rows]
        ss = jnp.sum(qh * qh, axis=0, keepdims=True)
        qn = qh * lax.rsqrt(ss * (1.0 / HEAD_DIM) + NORM_EPS) * gq
        qa_ref[hd] = jnp.concatenate([qn, q_aug], axis=0).astype(BF16)
        va_ref[hd] = jnp.concatenate([vt[rows], v_aug], axis=0).astype(BF16)


def _fox_proj_prompt(x2d, wts, *, tb):
    n = x2d.shape[0]
    tri = jnp.asarray(np.tril(np.ones((tb, tb), np.float32)), BF16)
    e = np.zeros((3 * LANES, FOX_HEADS * LANES), np.float32)
    for part in range(3):
        for hd in range(FOX_HEADS):
            e[part * LANES + hd, hd * LANES + HEAD_DIM + part] = 1.0
    consts = [wts["g1"], wts["w_k1"], wts["w_g1"], wts["w_f"], wts["b_f"], wts["w_q1t"], wts["w_v1t"],
              wts["gq1c"], wts["kg1"], wts["seg"], tri, jnp.asarray(e, BF16)]
    row_spec = pl.BlockSpec((tb, D_MODEL), lambda i: (i, 0))
    in_specs = [row_spec] + [_const_spec(a.shape) for a in consts]
    row = jax.ShapeDtypeStruct((n, D_MODEL), F32)
    out_shape = [row, jax.ShapeDtypeStruct((D_MODEL, n), F32), jax.ShapeDtypeStruct((n, FOX_HEADS), F32), row,
                 jax.ShapeDtypeStruct((FOX_HEADS, LANES, n), BF16), jax.ShapeDtypeStruct((FOX_HEADS, n, LANES), BF16),
                 jax.ShapeDtypeStruct((FOX_HEADS, LANES, n), BF16)]
    tspec = pl.BlockSpec((FOX_HEADS, LANES, tb), lambda i: (0, 0, i))
    out_specs = [row_spec, pl.BlockSpec((D_MODEL, tb), lambda i: (0, i)),
                 pl.BlockSpec((tb, FOX_HEADS), lambda i: (i, 0)), row_spec,
                 tspec, pl.BlockSpec((FOX_HEADS, tb, LANES), lambda i: (0, i, 0)), tspec]
    return pl.pallas_call(
        functools.partial(_fox_proj_prompt_kernel, tb=tb), grid=(n // tb,), in_specs=in_specs,
        out_specs=out_specs, out_shape=out_shape, scratch_shapes=[pltpu.VMEM((8, LANES), F32)],
        compiler_params=_cparams(("arbitrary",)), name="fox_proj_prompt")(x2d, *consts)


def _fox_attn_kernel(q_ref, k_ref, v_ref, o_ref, m_scr, al_scr, acc_scr, s_scr, *, tq):
    qi = pl.program_id(1)
    m_scr[...] = jnp.full(m_scr.shape, NEG_INF, F32)
    acc_scr[...] = jnp.zeros(acc_scr.shape, F32)

    def score_matmuls(kb):
        off = pl.multiple_of(kb * tq, tq)
        return [jnp.dot(k_ref[hh, pl.ds(off, tq), :], q_ref[hh], preferred_element_type=F32)
                for hh in range(2)]

    def score_finish(sts, masked):
        for hh in range(2):
            st = sts[hh]
            if masked:
                krow = lax.broadcasted_iota(jnp.int32, (tq, tq), 0)
                qcol = lax.broadcasted_iota(jnp.int32, (tq, tq), 1)
                st = jnp.where(krow <= qcol, st, NEG_INF)
            m_old = m_scr[hh, 0:1, :]
            m_new = jnp.maximum(m_old, jnp.max(st, axis=0, keepdims=True))
            al_scr[hh, 0:1, :] = jnp.exp2(m_old - m_new)
            m_scr[hh, 0:1, :] = m_new
            s_scr[hh] = st

    def accumulate(kb):
        off = pl.multiple_of(kb * tq, tq)
        for hh in range(2):
            p = jnp.exp2(s_scr[hh] - m_scr[hh, 0:1, :]).astype(BF16)
            pv = jnp.dot(v_ref[hh, :, pl.ds(off, tq)], p, preferred_element_type=F32)
            acc_scr[hh] = al_scr[hh, 0:1, :] * acc_scr[hh] + pv

    @pl.when(qi == 0)
    def _():
        score_finish(score_matmuls(0), True)

    @pl.when(qi > 0)
    def _():
        score_finish(score_matmuls(0), False)

        def advance(kb):
            sts = score_matmuls(kb + 1)
            accumulate(kb)
            score_finish(sts, False)

        def body(t, carry):
            advance(2 * t)
            advance(2 * t + 1)
            return carry

        lax.fori_loop(0, (qi - 1) // 2, body, 0)

        @pl.when((qi - 1) % 2 == 1)
        def _():
            advance(qi - 2)

        sts = score_matmuls(qi)
        accumulate(qi - 1)
        score_finish(sts, True)

    accumulate(qi)
    lane = lax.broadcasted_iota(jnp.int32, (tq, LANES), 1)
    outs = []
    for hh in range(2):
        a = acc_scr[hh]
        outs.append((a / a[HEAD_DIM:HEAD_DIM + 1, :]).T)
    o_ref[...] = jnp.where(lane < HEAD_DIM, outs[0], pltpu.roll(outs[1], HEAD_DIM, 1))


def _fox_attn_prompt(qa, ka, va, *, tq):
    n = ka.shape[1]
    return pl.pallas_call(
        functools.partial(_fox_attn_kernel, tq=tq), grid=(FOX_HEADS // 2, n // tq),
        in_specs=[pl.BlockSpec((2, LANES, tq), lambda p, i: (p, 0, i)),
                  pl.BlockSpec((2, n, LANES), lambda p, i: (p, 0, 0)),
                  pl.BlockSpec((2, LANES, n), lambda p, i: (p, 0, 0))],
        out_specs=pl.BlockSpec((tq, LANES), lambda p, i: (i, p)),
        out_shape=jax.ShapeDtypeStruct((n, D_MODEL), F32),
        scratch_shapes=[pltpu.VMEM((2, 8, tq), F32), pltpu.VMEM((2, 8, tq), F32),
                        pltpu.VMEM((2, LANES, tq), F32), pltpu.VMEM((2, tq, tq), F32)],
        compiler_params=_cparams(("arbitrary", "arbitrary")), name="fox_attn_prompt")(qa, ka, va)


CUM_BLK = 512
NEW_PAD = 128


def _fox_sample_kernel(q_ref, kc_ref, vc_ref, lfc_ref, kn_ref, vn_ref, lfn_ref, tri_ref, o_ref, negc, negn,
                       *, t_new, p_len):
    qd = pl.program_id(1)
    rows = 4 * t_new

    @pl.when(qd == 0)
    def _():
        tri = tri_ref[...]

        def prefix_sums(x, t):
            c3 = jnp.dot(jnp.concatenate(_split3(x), axis=0), t, preferred_element_type=F32)
            return c3[0:FOX_HEADS] + c3[FOX_HEADS:2 * FOX_HEADS] + c3[2 * FOX_HEADS:3 * FOX_HEADS]

        carry = jnp.zeros((FOX_HEADS, 1), F32)
        for b in range(p_len // CUM_BLK):
            c = prefix_sums(lfc_ref[0, :, b * CUM_BLK:(b + 1) * CUM_BLK], tri) + carry
            negc[:, b * CUM_BLK:(b + 1) * CUM_BLK] = -c
            carry = c[:, CUM_BLK - 1:CUM_BLK]
        negn[...] = -(prefix_sums(lfn_ref[0], tri[0:NEW_PAD, 0:NEW_PAD]) + carry)

    qq = q_ref[0]
    rhead = lax.broadcasted_iota(jnp.int32, (rows, 2 * LANES), 0) // t_new
    lhead = lax.broadcasted_iota(jnp.int32, (rows, 2 * LANES), 1) // HEAD_DIM
    qbd = jnp.where(rhead == lhead, jnp.concatenate([qq] * 4, axis=0), 0.0).astype(BF16)

    kct = kc_ref[0].reshape(2 * LANES, p_len).astype(BF16)
    vct = vc_ref[0].reshape(2 * LANES, p_len).astype(BF16)
    s_c = jnp.dot(qbd, kct, preferred_element_type=F32)
    s_n = lax.dot_general(qbd, kn_ref[0].astype(BF16), (((1,), (1,)), ((), ())), preferred_element_type=F32)
    bc = jnp.concatenate([jnp.broadcast_to(negc[pl.ds(4 * qd + jh, 1), :], (t_new, p_len)) for jh in range(4)], axis=0)
    bn = jnp.concatenate([jnp.broadcast_to(negn[pl.ds(4 * qd + jh, 1), :], (t_new, NEW_PAD)) for jh in range(4)], axis=0)
    s_c = s_c + bc
    qrow = lax.broadcasted_iota(jnp.int32, (rows, NEW_PAD), 0) % t_new
    kcol = lax.broadcasted_iota(jnp.int32, (rows, NEW_PAD), 1)
    s_n = jnp.where(kcol <= qrow, s_n + bn, NEG_INF)
    m = jnp.maximum(jnp.max(s_c, axis=-1, keepdims=True), jnp.max(s_n, axis=-1, keepdims=True))
    p_c = jnp.exp(s_c - m)
    p_n = jnp.exp(s_n - m)
    den = jnp.sum(p_c, axis=-1, keepdims=True) + jnp.sum(p_n, axis=-1, keepdims=True)
    o = (lax.dot_general(p_c.astype(BF16), vct, (((1,), (1,)), ((), ())), preferred_element_type=F32)
         + jnp.dot(p_n.astype(BF16), vn_ref[0].astype(BF16), preferred_element_type=F32)) / den
    lh = lax.broadcasted_iota(jnp.int32, (t_new, 2 * LANES), 1) // HEAD_DIM
    out = jnp.zeros((t_new, 2 * LANES), F32)
    for jh in range(4):
        out = jnp.where(lh == jh, o[jh * t_new:(jh + 1) * t_new], out)
    o_ref[0] = out


def _fox_attn_sample(q, cache_k, cache_v, lfc_t, k_new, v_new, lfn_t):
    b, t_new, _ = q.shape
    p_len = cache_k.shape[3]
    tri = jnp.asarray(np.triu(np.ones((CUM_BLK, CUM_BLK), np.float32)), BF16)
    quad = 2 * LANES
    in_specs = [pl.BlockSpec((1, t_new, quad), lambda r, d: (r, 0, d)),
                pl.BlockSpec((1, 4, HEAD_DIM, p_len), lambda r, d: (r, d, 0, 0)),
                pl.BlockSpec((1, 4, HEAD_DIM, p_len), lambda r, d: (r, d, 0, 0)),
                pl.BlockSpec((1, FOX_HEADS, p_len), lambda r, d: (r, 0, 0)),
                pl.BlockSpec((1, NEW_PAD, quad), lambda r, d: (r, 0, d)),
                pl.BlockSpec((1, NEW_PAD, quad), lambda r, d: (r, 0, d)),
                pl.BlockSpec((1, FOX_HEADS, NEW_PAD), lambda r, d: (r, 0, 0)),
                pl.BlockSpec((CUM_BLK, CUM_BLK), lambda r, d: (0, 0))]
    return pl.pallas_call(
        functools.partial(_fox_sample_kernel, t_new=t_new, p_len=p_len),
        grid=(b, D_MODEL // quad), in_specs=in_specs,
        out_specs=pl.BlockSpec((1, t_new, quad), lambda r, d: (r, 0, d)),
        out_shape=jax.ShapeDtypeStruct((b, t_new, D_MODEL), F32),
        scratch_shapes=[pltpu.VMEM((FOX_HEADS, p_len), F32), pltpu.VMEM((FOX_HEADS, NEW_PAD), F32)],
        compiler_params=_cparams(("arbitrary", "arbitrary")), name="fox_attn_sample")(
            q, cache_k, cache_v, lfc_t, k_new, v_new, lfn_t, tri)


def _out_proj_kernel(r_ref, a_ref, sg_ref, w_ref, y_ref):
    mixed = (a_ref[...] * sg_ref[...]).astype(BF16)
    y_ref[...] = r_ref[...] + jnp.dot(mixed, w_ref[...], preferred_element_type=F32)


def _out_proj(resid, attn, sg, w, *, tb):
    n = resid.shape[0]
    row = pl.BlockSpec((tb, D_MODEL), lambda i: (i, 0))
    return pl.pallas_call(
        _out_proj_kernel, grid=(n // tb,), in_specs=[row, row, row, _const_spec(w.shape)], out_specs=row,
        out_shape=jax.ShapeDtypeStruct((n, D_MODEL), F32),
        compiler_params=_cparams(("arbitrary",)), name="out_proj")(resid, attn, sg, w)


L0_TB = 512
FOX_TB = 256
ATTN_TQ = 512
SWA_CHUNK = 64


def kernel(x_prompt, x_sample, state_pool, cache_swa_k, cache_swa_v, cache_fox_k, cache_fox_v, cache_fox_logf,
           norm0_g, w_in0, w_pool, pool_scale, swa_qn_g, swa_kn_g, swa_sinks, w_out0,
           norm1_g, w_in1, b_forget, fox_qn_g, fox_kn_g, w_out1):
    nb, seq, _ = x_prompt.shape
    db, dseq, _ = x_sample.shape
    past_len = cache_fox_k.shape[1]
    assert nb == 1 and seq % L0_TB == 0 and seq % ATTN_TQ == 0 and dseq % 8 == 0

    seg = np.kron(np.eye(2, dtype=np.float32), np.ones((HEAD_DIM, HEAD_DIM), np.float32))
    mc = FOX_HEADS * HEAD_DIM
    wts = {
        "g0": norm0_g.reshape(1, D_MODEL), "w_in0": w_in0.astype(BF16), "w_pool": w_pool.astype(BF16),
        "pool_scale": pool_scale.reshape(1, C_POOL),
        "qg0": jnp.tile(swa_qn_g, SWA_HEADS).reshape(1, SWA_HEADS * HEAD_DIM),
        "kg0": jnp.tile(swa_kn_g, SWA_KV_HEADS).reshape(1, LANES),
        "seg": jnp.asarray(seg, BF16), "sinks": swa_sinks, "w_out0": w_out0.astype(BF16),
        "g1": norm1_g.reshape(1, D_MODEL), "w_in1": w_in1[:, :4 * mc].astype(BF16),
        "w_f": jnp.pad(w_in1[:, 4 * mc:], ((0, 0), (0, LANES - FOX_HEADS))).astype(BF16),
        "b_f": jnp.pad(b_forget.astype(F32), (0, LANES - FOX_HEADS)).reshape(1, LANES),
        "qg1": jnp.tile(fox_qn_g, FOX_HEADS).reshape(1, mc), "kg1": jnp.tile(fox_kn_g, FOX_HEADS).reshape(1, mc),
        "w_k1": w_in1[:, mc:2 * mc].astype(BF16), "w_g1": w_in1[:, 3 * mc:4 * mc].astype(BF16),
        "w_q1t": w_in1[:, 0:mc].T.astype(BF16), "w_v1t": w_in1[:, 2 * mc:3 * mc].T.astype(BF16),
        "gq1c": jnp.broadcast_to(fox_qn_g.astype(F32).reshape(HEAD_DIM, 1), (HEAD_DIM, LANES)),
    }
    w_out1 = w_out1.astype(BF16)

    xp = x_prompt.reshape(seq, D_MODEL)
    y0p, pool_p, swk_p, swv_p = _layer0(xp, None, wts, ns=1, ts=L0_TB, chunk=SWA_CHUNK, base_pos=0)
    fk_p, fvt_p, fl_p, sg_p, qa, ka, va = _fox_proj_prompt(y0p, wts, tb=FOX_TB)
    fv_p = jnp.transpose(fvt_p.reshape(FOX_HEADS, HEAD_DIM, seq), (2, 0, 1))
    attn_p = _fox_attn_prompt(qa, ka, va, tq=ATTN_TQ)
    yp = _out_proj(y0p, attn_p, sg_p, w_out1, tb=L0_TB)

    xs = x_sample.reshape(db * dseq, D_MODEL)
    prefix = (jnp.pad(state_pool, ((0, 0), (1, 0), (0, 0))),
              cache_swa_k.reshape(db, WINDOW, LANES), cache_swa_v.reshape(db, WINDOW, LANES))
    y0s, pool_s, swk_s, swv_s = _layer0(xs, prefix, wts, ns=db, ts=dseq, chunk=dseq, base_pos=past_len)
    fk_s, fv_s, fl_s, sg_s, q_s = _fox_proj_sample(y0s, wts, tb=FOX_TB)
    pad_rows = ((0, 0), (0, NEW_PAD - dseq), (0, 0))
    attn_s = _fox_attn_sample(
        q_s.reshape(db, dseq, D_MODEL),
        jnp.transpose(cache_fox_k, (0, 2, 3, 1)), jnp.transpose(cache_fox_v, (0, 2, 3, 1)),
        jnp.transpose(cache_fox_logf, (0, 2, 1)),
        jnp.pad(fk_s.reshape(db, dseq, D_MODEL), pad_rows), jnp.pad(fv_s.reshape(db, dseq, D_MODEL), pad_rows),
        jnp.pad(jnp.transpose(fl_s.reshape(db, dseq, FOX_HEADS), (0, 2, 1)), ((0, 0), (0, 0), (0, NEW_PAD - dseq))))
    ys = _out_proj(y0s, attn_s.reshape(db * dseq, D_MODEL), sg_s, w_out1, tb=db * dseq)

    return (yp.reshape(1, seq, D_MODEL), ys.reshape(db, dseq, D_MODEL),
            pool_p[:, 1:], pool_s[:, 1:],
            swk_p.reshape(1, WINDOW, SWA_KV_HEADS, HEAD_DIM), swv_p.reshape(1, WINDOW, SWA_KV_HEADS, HEAD_DIM),
            swk_s.reshape(db, WINDOW, SWA_KV_HEADS, HEAD_DIM), swv_s.reshape(db, WINDOW, SWA_KV_HEADS, HEAD_DIM),
            fk_p.reshape(1, seq, FOX_HEADS, HEAD_DIM), fv_p.reshape(1, seq, FOX_HEADS, HEAD_DIM),
            fl_p.reshape(1, seq, FOX_HEADS),
            fk_s.reshape(db, dseq, FOX_HEADS, HEAD_DIM), fv_s.reshape(db, dseq, FOX_HEADS, HEAD_DIM),
            fl_s.reshape(db, dseq, FOX_HEADS))
```

```python
import functools

import numpy as np
import jax
import jax.numpy as jnp
from jax import lax
from jax.experimental import pallas as pl
from jax.experimental.pallas import tpu as pltpu

F32 = jnp.float32
BF16 = jnp.bfloat16

D_MODEL = 1024
HEAD_DIM = 64
ATTN_SCALE = HEAD_DIM ** -0.5
POOL_WINDOWS = (2, 4, 8, 16)
C_POOL = 512
POOL_PAD = 15
SWA_HEADS = 8
SWA_KV_HEADS = 2
SWA_REP = 4
WINDOW = 128
FOX_HEADS = 16
NORM_EPS = 1e-6
NEG_INF = -1e30
LOG2E = 1.4426950408889634

LANES = 128
SWA_KEYS = 256
VMEM_LIMIT = 56 * 1024 * 1024


def _cparams(sem):
    return pltpu.CompilerParams(dimension_semantics=sem, vmem_limit_bytes=VMEM_LIMIT)


def _rms(x, g):
    ms = jnp.mean(x * x, axis=-1, keepdims=True)
    return x * lax.rsqrt(ms + NORM_EPS) * g


def _split2(x):
    hi = x.astype(BF16)
    lo = (x - hi.astype(F32)).astype(BF16)
    return hi, lo


def _split3(x):
    hi = x.astype(BF16)
    r = x - hi.astype(F32)
    mid = r.astype(BF16)
    lo = (r - mid.astype(F32)).astype(BF16)
    return hi, mid, lo


def _head_rms_tile(x, seg, g):
    hi, lo = _split2(x * x)
    ss = jnp.dot(hi, seg, preferred_element_type=F32) + jnp.dot(lo, seg, preferred_element_type=F32)
    return x * lax.rsqrt(ss * (1.0 / HEAD_DIM) + NORM_EPS) * g


def _silu(g):
    return g / (1.0 + jnp.exp(-g))


def _l0_kernel(*refs, ns, ts, chunk, has_cache, base_pos, nblk):
    if has_cache:
        (x_ref, pp_ref, kp_ref, vp_ref, g0_ref, win_ref, wpool_ref, pscale_ref, qg_ref, kg_ref, seg_ref,
         slope_ref, sink_ref, wout_ref, y_ref, ps_ref, ks_ref, vs_ref, uext, kext, vext, mix) = refs
    else:
        (x_ref, g0_ref, win_ref, wpool_ref, pscale_ref, qg_ref, kg_ref, seg_ref,
         slope_ref, sink_ref, wout_ref, y_ref, ps_ref, ks_ref, vs_ref, uext, kext, vext, mix) = refs
    padk = SWA_KEYS - WINDOW - chunk
    hist = padk + WINDOW
    i = pl.program_id(0)

    x = x_ref[...]
    h = _rms(x, g0_ref[...]).astype(BF16)
    proj = jnp.dot(h, win_ref[...], preferred_element_type=F32)
    u = proj[:, 0:C_POOL]
    gate = proj[:, 1280:2304]
    seg = seg_ref[...]
    qn = [_head_rms_tile(proj[:, 512 + LANES * j:512 + LANES * (j + 1)], seg,
                         qg_ref[:, LANES * j:LANES * (j + 1)]) * (ATTN_SCALE * LOG2E) for j in range(4)]
    kn = _head_rms_tile(proj[:, 1024:1152], seg, kg_ref[...])
    v = proj[:, 1152:1280]

    if has_cache:
        for s in range(ns):
            uext[s, 0:16, :] = pp_ref[s]
            kext[s, 0:padk, :] = jnp.zeros((padk, LANES), F32)
            vext[s, 0:padk, :] = jnp.zeros((padk, LANES), F32)
            kext[s, padk:hist, :] = kp_ref[s]
            vext[s, padk:hist, :] = vp_ref[s]
    else:
        @pl.when(i == 0)
        def _():
            uext[0, 0:16, :] = jnp.zeros((16, C_POOL), F32)
            kext[0, 0:hist, :] = jnp.zeros((hist, LANES), F32)
            vext[0, 0:hist, :] = jnp.zeros((hist, LANES), F32)

        @pl.when(i > 0)
        def _():
            uext[0, 0:16, :] = uext[0, ts:ts + 16, :]
            kext[0, 0:hist, :] = kext[0, ts:ts + hist, :]
            vext[0, 0:hist, :] = vext[0, ts:ts + hist, :]

    for s in range(ns):
        uext[s, 16:16 + ts, :] = u[s * ts:(s + 1) * ts]
        kext[s, hist:hist + ts, :] = kn[s * ts:(s + 1) * ts]
        vext[s, hist:hist + ts, :] = v[s * ts:(s + 1) * ts]

    pos = base_pos + i * ts + lax.broadcasted_iota(jnp.int32, (ts, LANES), 0)
    for s in range(ns):
        for g, w in enumerate(POOL_WINDOWS):
            cols = slice(LANES * g, LANES * (g + 1))
            acc = uext[s, 16:16 + ts, cols]
            cur = acc
            for j in range(1, w):
                acc = acc + uext[s, 16 - j:16 - j + ts, cols]
            cnt = jnp.minimum(pos + 1, w).astype(F32)
            mix[s * ts:(s + 1) * ts, cols] = acc / cnt - cur
    for g in range(4):
        cols = slice(LANES * g, LANES * (g + 1))
        d = mix[:, cols].astype(BF16)
        mix[:, cols] = jnp.dot(d, wpool_ref[g], preferred_element_type=F32) * pscale_ref[:, cols]

    rows4 = SWA_REP * chunk
    lo_c = lax.broadcasted_iota(jnp.int32, (chunk, LANES), 1) < HEAD_DIM
    lo_k = lax.broadcasted_iota(jnp.int32, (SWA_KEYS, LANES), 1) < HEAD_DIM
    kj = lax.broadcasted_iota(jnp.int32, (rows4, SWA_KEYS), 1)
    qi = lax.broadcasted_iota(jnp.int32, (rows4, SWA_KEYS), 0) % chunk
    absrel = jnp.abs(qi + WINDOW - (kj - padk)).astype(F32)
    bias = []
    sinkc = []
    for g in range(SWA_KV_HEADS):
        sl = slope_ref[g] * LOG2E
        bias.append(jnp.where(kj >= padk, -jnp.concatenate([sl, sl], axis=1) * absrel, NEG_INF))
        sinkc.append(sink_ref[g][:, 0:1] * LOG2E)
    nch = ts // chunk
    units = [(s, c, g) for s in range(ns) for c in range(nch) for g in range(SWA_KV_HEADS)]

    def swa_scores(s, c, g):
        r0 = c * chunk
        rows = slice(s * ts + r0, s * ts + r0 + chunk)
        kwin = kext[s, r0:r0 + SWA_KEYS, :]
        vwin = vext[s, r0:r0 + SWA_KEYS, :]
        krl = pltpu.roll(kwin, HEAD_DIM, 1)
        vrl = pltpu.roll(vwin, HEAD_DIM, 1)
        if g == 0:
            kd = jnp.where(lo_k, kwin, krl).astype(BF16)
            vd = jnp.where(lo_k, vwin, vrl).astype(BF16)
        else:
            kd = jnp.where(lo_k, krl, kwin).astype(BF16)
            vd = jnp.where(lo_k, vrl, vwin).astype(BF16)
        q0 = qn[2 * g][rows]
        q1 = qn[2 * g + 1][rows]
        qs = jnp.concatenate([jnp.where(lo_c, q0, 0.0), jnp.where(lo_c, 0.0, q0),
                              jnp.where(lo_c, q1, 0.0), jnp.where(lo_c, 0.0, q1)], axis=0).astype(BF16)
        return lax.dot_general(qs, kd, (((1,), (1,)), ((), ())), preferred_element_type=F32), vd

    def swa_finish(s, c, g, sc, vd):
        rows = slice(s * ts + c * chunk, s * ts + (c + 1) * chunk)
        sc = sc + bias[g]
        if not has_cache and c * chunk < WINDOW:
            sc = jnp.where(kj >= padk + jnp.maximum(0, WINDOW - (i * nch + c) * chunk), sc, NEG_INF)
        m = jnp.maximum(jnp.max(sc, axis=-1, keepdims=True), sinkc[g])
        p = jnp.exp2(sc - m)
        den = jnp.sum(p, axis=-1, keepdims=True) + jnp.exp2(sinkc[g] - m)
        o = jnp.dot(p.astype(BF16), vd, preferred_element_type=F32) / den
        for jj in range(2):
            t = 2 * g + jj
            mix[rows, C_POOL + LANES * t:C_POOL + LANES * (t + 1)] = jnp.where(
                lo_c, o[(2 * jj) * chunk:(2 * jj + 1) * chunk], o[(2 * jj + 1) * chunk:(2 * jj + 2) * chunk])

    pending = swa_scores(*units[0])
    for n, unit in enumerate(units):
        nxt = swa_scores(*units[n + 1]) if n + 1 < len(units) else None
        swa_finish(*unit, *pending)
        pending = nxt

    mixed = (mix[...] * _silu(gate)).astype(BF16)
    y_ref[...] = x + jnp.dot(mixed, wout_ref[...], preferred_element_type=F32)

    @pl.when(i == nblk - 1)
    def _():
        for s in range(ns):
            ps_ref[s] = uext[s, ts:ts + 16, :]
            ks_ref[s] = kext[s, padk + ts:padk + ts + WINDOW, :]
            vs_ref[s] = vext[s, padk + ts:padk + ts + WINDOW, :]


def _const_spec(shape):
    nd = len(shape)
    return pl.BlockSpec(shape, lambda i, _nd=nd: (0,) * _nd)


def _layer0(x2d, prefix, wts, *, ns, ts, chunk, base_pos):
    n = x2d.shape[0]
    tb = ns * ts
    nblk = n // tb
    has_cache = prefix is not None
    padk = SWA_KEYS - WINDOW - chunk
    rows4 = SWA_REP * chunk
    slope = np.repeat(2.0 ** (-(np.arange(SWA_HEADS) + 1.0)), chunk).reshape(SWA_KV_HEADS, rows4, 1)
    slope = jnp.asarray(np.broadcast_to(slope, (SWA_KV_HEADS, rows4, LANES)).astype(np.float32))
    sink = jnp.broadcast_to(jnp.repeat(wts["sinks"].astype(F32), chunk).reshape(SWA_KV_HEADS, rows4, 1),
                            (SWA_KV_HEADS, rows4, LANES))
    consts = [wts["g0"], wts["w_in0"], wts["w_pool"], wts["pool_scale"], wts["qg0"], wts["kg0"], wts["seg"],
              slope, sink, wts["w_out0"]]
    in_specs = [pl.BlockSpec((tb, D_MODEL), lambda i: (i, 0))]
    args = [x2d]
    if has_cache:
        for a in prefix:
            in_specs.append(_const_spec(a.shape))
            args.append(a)
    for a in consts:
        in_specs.append(_const_spec(a.shape))
        args.append(a)
    out_shape = [jax.ShapeDtypeStruct((n, D_MODEL), F32),
                 jax.ShapeDtypeStruct((ns, 16, C_POOL), F32),
                 jax.ShapeDtypeStruct((ns, WINDOW, LANES), F32),
                 jax.ShapeDtypeStruct((ns, WINDOW, LANES), F32)]
    out_specs = [pl.BlockSpec((tb, D_MODEL), lambda i: (i, 0)),
                 _const_spec((ns, 16, C_POOL)), _const_spec((ns, WINDOW, LANES)), _const_spec((ns, WINDOW, LANES))]
    scratch = [pltpu.VMEM((ns, 16 + ts, C_POOL), F32),
               pltpu.VMEM((ns, padk + WINDOW + ts, LANES), F32),
               pltpu.VMEM((ns, padk + WINDOW + ts, LANES), F32),
               pltpu.VMEM((tb, D_MODEL), F32)]
    kern = functools.partial(_l0_kernel, ns=ns, ts=ts, chunk=chunk, has_cache=has_cache,
                             base_pos=base_pos, nblk=nblk)
    return pl.pallas_call(
        kern, grid=(nblk,), in_specs=in_specs, out_specs=out_specs, out_shape=out_shape,
        scratch_shapes=scratch, compiler_params=_cparams(("arbitrary",)),
        name="layer0_prompt" if not has_cache else "layer0_sample")(*args)


def _log_sigmoid(z):
    return jnp.minimum(z, 0.0) - jnp.log1p(jnp.exp(-jnp.abs(z)))


def _fox_proj_sample_kernel(x_ref, g1_ref, w_ref, wf_ref, bf_ref, qg_ref, kg_ref, seg_ref,
                            k_ref, v_ref, lf_ref, q_ref):
    x = x_ref[...]
    h = _rms(x, g1_ref[...]).astype(BF16)
    z = jnp.dot(h, wf_ref[...], preferred_element_type=F32) + bf_ref[...]
    lf_ref[...] = _log_sigmoid(z)[:, 0:FOX_HEADS]
    seg = seg_ref[...]
    mc = D_MODEL
    q = jnp.dot(h, w_ref[:, 0:mc], preferred_element_type=F32)
    k = jnp.dot(h, w_ref[:, mc:2 * mc], preferred_element_type=F32)
    v_ref[...] = jnp.dot(h, w_ref[:, 2 * mc:3 * mc], preferred_element_type=F32)
    for j in range(FOX_HEADS // 2):
        cols = slice(LANES * j, LANES * (j + 1))
        q_ref[:, cols] = _head_rms_tile(q[:, cols], seg, qg_ref[:, cols]) * ATTN_SCALE
        k_ref[:, cols] = _head_rms_tile(k[:, cols], seg, kg_ref[:, cols])


def _fox_proj_sample(x2d, wts, *, tb):
    n = x2d.shape[0]
    consts = [wts["g1"], wts["w_qkv1"], wts["w_f"], wts["b_f"], wts["qg1"], wts["kg1"], wts["seg"]]
    row_spec = pl.BlockSpec((tb, D_MODEL), lambda i: (i, 0))
    in_specs = [row_spec] + [_const_spec(a.shape) for a in consts]
    row = jax.ShapeDtypeStruct((n, D_MODEL), F32)
    out_shape = [row, row, jax.ShapeDtypeStruct((n, FOX_HEADS), F32), row]
    out_specs = [row_spec, row_spec, pl.BlockSpec((tb, FOX_HEADS), lambda i: (i, 0)), row_spec]
    return pl.pallas_call(
        _fox_proj_sample_kernel, grid=(n // tb,), in_specs=in_specs, out_specs=out_specs, out_shape=out_shape,
        compiler_params=_cparams(("arbitrary",)), name="fox_proj_sample")(x2d, *consts)


def _fox_proj_prompt_kernel(x_ref, g1_ref, wk_ref, wf_ref, bf_ref, wqt_ref, wvt_ref, gq_ref, kg_ref,
                            seg_ref, tri_ref, e_ref,
                            k_ref, vt_ref, lf_ref, qa_ref, ka_ref, va_ref, carry, *, tb):
    i = pl.program_id(0)
    x = x_ref[...]
    hf = _rms(x, g1_ref[...])
    h = hf.astype(BF16)
    ht = hf.T.astype(BF16)
    logf = _log_sigmoid(jnp.dot(h, wf_ref[...], preferred_element_type=F32) + bf_ref[...])
    lf_ref[...] = logf[:, 0:FOX_HEADS]
    k = jnp.dot(h, wk_ref[...], preferred_element_type=F32)

    @pl.when(i == 0)
    def _():
        carry[...] = jnp.zeros(carry.shape, F32)

    tri = tri_ref[...]
    hi, mid, lo = _split3(logf)
    cum = (jnp.dot(tri, hi, preferred_element_type=F32) + jnp.dot(tri, mid, preferred_element_type=F32)
           + jnp.dot(tri, lo, preferred_element_type=F32)) + carry[0:1, :]
    carry[0:1, :] = cum[tb - 1:tb, :]
    nh, nm, nl = _split3(-LOG2E * cum)
    biasall = jnp.dot(jnp.concatenate([nh, nm, nl], axis=1), e_ref[...], preferred_element_type=F32)
    lo_l = lax.broadcasted_iota(jnp.int32, (tb, LANES), 1) < HEAD_DIM
    seg = seg_ref[...]
    for j in range(FOX_HEADS // 2):
        cols = slice(LANES * j, LANES * (j + 1))
        kt = _head_rms_tile(k[:, cols], seg, kg_ref[:, cols])
        k_ref[:, cols] = kt
        for half in range(2):
            hd = 2 * j + half
            kh = pltpu.roll(kt, HEAD_DIM, 1) if half else kt
            ka_ref[hd] = jnp.where(lo_l, kh, biasall[:, LANES * hd:LANES * (hd + 1)]).astype(BF16)

    qt = jnp.dot(wqt_ref[...], ht, preferred_element_type=F32)
    vt = jnp.dot(wvt_ref[...], ht, preferred_element_type=F32)
    vt_ref[...] = vt
    srow = lax.broadcasted_iota(jnp.int32, (HEAD_DIM, tb), 0)
    q_aug = jnp.where(srow < 3, 1.0, 0.0)
    v_aug = jnp.where(srow == 0, 1.0, 0.0)
    gq = jnp.concatenate([gq_ref[...]] * (tb // LANES), axis=1) * (ATTN_SCALE * LOG2E)
    for hd in range(FOX_HEADS):
        rows = slice(HEAD_DIM * hd, HEAD_DIM * (hd + 1))
        qh = qt[rows]
        ss = jnp.sum(qh * qh, axis=0, keepdims=True)
        qn = qh * lax.rsqrt(ss * (1.0 / HEAD_DIM) + NORM_EPS) * gq
        qa_ref[hd] = jnp.concatenate([qn, q_aug], axis=0).astype(BF16)
        va_ref[hd] = jnp.concatenate([vt[rows], v_aug], axis=0).astype(BF16)


def _fox_proj_prompt(x2d, wts, *, tb):
    n = x2d.shape[0]
    tri = jnp.asarray(np.tril(np.ones((tb, tb), np.float32)), BF16)
    e = np.zeros((3 * LANES, FOX_HEADS * LANES), np.float32)
    for part in range(3):
        for hd in range(FOX_HEADS):
            e[part * LANES + hd, hd * LANES + HEAD_DIM + part] = 1.0
    consts = [wts["g1"], wts["w_k1"], wts["w_f"], wts["b_f"], wts["w_q1t"], wts["w_v1t"],
              wts["gq1c"], wts["kg1"], wts["seg"], tri, jnp.asarray(e, BF16)]
    row_spec = pl.BlockSpec((tb, D_MODEL), lambda i: (i, 0))
    in_specs = [row_spec] + [_const_spec(a.shape) for a in consts]
    row = jax.ShapeDtypeStruct((n, D_MODEL), F32)
    out_shape = [row, jax.ShapeDtypeStruct((D_MODEL, n), F32), jax.ShapeDtypeStruct((n, FOX_HEADS), F32),
                 jax.ShapeDtypeStruct((FOX_HEADS, LANES, n), BF16), jax.ShapeDtypeStruct((FOX_HEADS, n, LANES), BF16),
                 jax.ShapeDtypeStruct((FOX_HEADS, LANES, n), BF16)]
    tspec = pl.BlockSpec((FOX_HEADS, LANES, tb), lambda i: (0, 0, i))
    out_specs = [row_spec, pl.BlockSpec((D_MODEL, tb), lambda i: (0, i)),
                 pl.BlockSpec((tb, FOX_HEADS), lambda i: (i, 0)),
                 tspec, pl.BlockSpec((FOX_HEADS, tb, LANES), lambda i: (0, i, 0)), tspec]
    return pl.pallas_call(
        functools.partial(_fox_proj_prompt_kernel, tb=tb), grid=(n // tb,), in_specs=in_specs,
        out_specs=out_specs, out_shape=out_shape, scratch_shapes=[pltpu.VMEM((8, LANES), F32)],
        compiler_params=_cparams(("arbitrary",)), name="fox_proj_prompt")(x2d, *consts)


def _fox_attn_kernel(q_ref, k_ref, v_ref, o_ref, m_scr, al_scr, acc_scr, s_scr, *, tq, tk, unroll):
    qi = pl.program_id(1)
    r = tq // tk
    n_u = r * qi
    m_scr[...] = jnp.full(m_scr.shape, NEG_INF, F32)
    acc_scr[...] = jnp.zeros(acc_scr.shape, F32)

    def score_matmuls(kb):
        off = pl.multiple_of(kb * tk, tk)
        return [jnp.dot(k_ref[hh, pl.ds(off, tk), :], q_ref[hh], preferred_element_type=F32)
                for hh in range(2)]

    def score_finish(sts, diag):
        for hh in range(2):
            st = sts[hh]
            if diag is not None:
                krow = lax.broadcasted_iota(jnp.int32, (tk, tq), 0) + diag * tk
                qcol = lax.broadcasted_iota(jnp.int32, (tk, tq), 1)
                st = jnp.where(krow <= qcol, st, NEG_INF)
            m_old = m_scr[hh, 0:1, :]
            m_new = jnp.maximum(m_old, jnp.max(st, axis=0, keepdims=True))
            al_scr[hh, 0:1, :] = jnp.exp2(m_old - m_new)
            m_scr[hh, 0:1, :] = m_new
            s_scr[hh] = st

    def accumulate(kb):
        off = pl.multiple_of(kb * tk, tk)
        for hh in range(2):
            p = jnp.exp2(s_scr[hh] - m_scr[hh, 0:1, :]).astype(BF16)
            pv = jnp.dot(v_ref[hh, :, pl.ds(off, tk)], p, preferred_element_type=F32)
            acc_scr[hh] = al_scr[hh, 0:1, :] * acc_scr[hh] + pv

    def advance(kb, diag=None):
        sts = score_matmuls(kb + 1)
        accumulate(kb)
        score_finish(sts, diag)

    @pl.when(qi == 0)
    def _():
        score_finish(score_matmuls(0), 0)

    @pl.when(qi > 0)
    def _():
        score_finish(score_matmuls(0), None)
        n_adv = n_u - 1

        def body(t, carry):
            for u in range(unroll):
                advance(unroll * t + u)
            return carry

        lax.fori_loop(0, n_adv // unroll, body, 0)
        done = (n_adv // unroll) * unroll
        for u in range(unroll - 1):
            @pl.when(n_adv - done > u)
            def _():
                advance(done + u)

        advance(n_u - 1, 0)

    for d in range(1, r):
        advance(n_u + d - 1, d)
    accumulate(n_u + r - 1)
    lane = lax.broadcasted_iota(jnp.int32, (tq, LANES), 1)
    outs = []
    for hh in range(2):
        a = acc_scr[hh]
        outs.append((a / a[HEAD_DIM:HEAD_DIM + 1, :]).T)
    o_ref[...] = jnp.where(lane < HEAD_DIM, outs[0], pltpu.roll(outs[1], HEAD_DIM, 1)).astype(o_ref.dtype)


def _fox_attn_prompt(qa, ka, va, *, tq, tk, unroll):
    n = ka.shape[1]
    return pl.pallas_call(
        functools.partial(_fox_attn_kernel, tq=tq, tk=tk, unroll=unroll), grid=(FOX_HEADS // 2, n // tq),
        in_specs=[pl.BlockSpec((2, LANES, tq), lambda p, i: (p, 0, i)),
                  pl.BlockSpec((2, n, LANES), lambda p, i: (p, 0, 0)),
                  pl.BlockSpec((2, LANES, n), lambda p, i: (p, 0, 0))],
        out_specs=pl.BlockSpec((tq, LANES), lambda p, i: (i, p)),
        out_shape=jax.ShapeDtypeStruct((n, D_MODEL), BF16),
        scratch_shapes=[pltpu.VMEM((2, 8, tq), F32), pltpu.VMEM((2, 8, tq), F32),
                        pltpu.VMEM((2, LANES, tq), F32), pltpu.VMEM((2, tk, tq), F32)],
        compiler_params=_cparams(("arbitrary", "arbitrary")), name="fox_attn_prompt")(qa, ka, va)


CUM_BLK = 512
NEW_PAD = 128


def _fox_sample_kernel(q_ref, kc_ref, vc_ref, lfc_ref, kn_ref, vn_ref, lfn_ref, tri_ref, o_ref, negc, negn,
                       *, t_new, p_len):
    qd = pl.program_id(1)
    rows = 4 * t_new

    @pl.when(qd == 0)
    def _():
        tri = tri_ref[...]

        def prefix_sums(x, t):
            c3 = jnp.dot(jnp.concatenate(_split3(x), axis=0), t, preferred_element_type=F32)
            return c3[0:FOX_HEADS] + c3[FOX_HEADS:2 * FOX_HEADS] + c3[2 * FOX_HEADS:3 * FOX_HEADS]

        carry = jnp.zeros((FOX_HEADS, 1), F32)
        for b in range(p_len // CUM_BLK):
            c = prefix_sums(lfc_ref[0, :, b * CUM_BLK:(b + 1) * CUM_BLK], tri) + carry
            negc[:, b * CUM_BLK:(b + 1) * CUM_BLK] = -c
            carry = c[:, CUM_BLK - 1:CUM_BLK]
        negn[...] = -(prefix_sums(lfn_ref[0], tri[0:NEW_PAD, 0:NEW_PAD]) + carry)

    qq = q_ref[0]
    rhead = lax.broadcasted_iota(jnp.int32, (rows, 2 * LANES), 0) // t_new
    lhead = lax.broadcasted_iota(jnp.int32, (rows, 2 * LANES), 1) // HEAD_DIM
    qbd = jnp.where(rhead == lhead, jnp.concatenate([qq] * 4, axis=0), 0.0).astype(BF16)

    kct = kc_ref[0].reshape(2 * LANES, p_len).astype(BF16)
    vct = vc_ref[0].reshape(2 * LANES, p_len).astype(BF16)
    s_c = jnp.dot(qbd, kct, preferred_element_type=F32)
    s_n = lax.dot_general(qbd, kn_ref[0].astype(BF16), (((1,), (1,)), ((), ())), preferred_element_type=F32)
    bc = jnp.concatenate([jnp.broadcast_to(negc[pl.ds(4 * qd + jh, 1), :], (t_new, p_len)) for jh in range(4)], axis=0)
    bn = jnp.concatenate([jnp.broadcast_to(negn[pl.ds(4 * qd + jh, 1), :], (t_new, NEW_PAD)) for jh in range(4)], axis=0)
    s_c = s_c + bc
    qrow = lax.broadcasted_iota(jnp.int32, (rows, NEW_PAD), 0) % t_new
    kcol = lax.broadcasted_iota(jnp.int32, (rows, NEW_PAD), 1)
    s_n = jnp.where(kcol <= qrow, s_n + bn, NEG_INF)
    m = jnp.maximum(jnp.max(s_c, axis=-1, keepdims=True), jnp.max(s_n, axis=-1, keepdims=True))
    p_c = jnp.exp(s_c - m)
    p_n = jnp.exp(s_n - m)
    den = jnp.sum(p_c, axis=-1, keepdims=True) + jnp.sum(p_n, axis=-1, keepdims=True)
    o = (lax.dot_general(p_c.astype(BF16), vct, (((1,), (1,)), ((), ())), preferred_element_type=F32)
         + jnp.dot(p_n.astype(BF16), vn_ref[0].astype(BF16), preferred_element_type=F32)) / den
    lh = lax.broadcasted_iota(jnp.int32, (t_new, 2 * LANES), 1) // HEAD_DIM
    out = jnp.zeros((t_new, 2 * LANES), F32)
    for jh in range(4):
        out = jnp.where(lh == jh, o[jh * t_new:(jh + 1) * t_new], out)
    o_ref[0] = out


def _fox_attn_sample(q, cache_k, cache_v, lfc_t, k_new, v_new, lfn_t):
    b, t_new, _ = q.shape
    p_len = cache_k.shape[3]
    tri = jnp.asarray(np.triu(np.ones((CUM_BLK, CUM_BLK), np.float32)), BF16)
    quad = 2 * LANES
    in_specs = [pl.BlockSpec((1, t_new, quad), lambda r, d: (r, 0, d)),
                pl.BlockSpec((1, 4, HEAD_DIM, p_len), lambda r, d: (r, d, 0, 0)),
                pl.BlockSpec((1, 4, HEAD_DIM, p_len), lambda r, d: (r, d, 0, 0)),
                pl.BlockSpec((1, FOX_HEADS, p_len), lambda r, d: (r, 0, 0)),
                pl.BlockSpec((1, NEW_PAD, quad), lambda r, d: (r, 0, d)),
                pl.BlockSpec((1, NEW_PAD, quad), lambda r, d: (r, 0, d)),
                pl.BlockSpec((1, FOX_HEADS, NEW_PAD), lambda r, d: (r, 0, 0)),
                pl.BlockSpec((CUM_BLK, CUM_BLK), lambda r, d: (0, 0))]
    return pl.pallas_call(
        functools.partial(_fox_sample_kernel, t_new=t_new, p_len=p_len),
        grid=(b, D_MODEL // quad), in_specs=in_specs,
        out_specs=pl.BlockSpec((1, t_new, quad), lambda r, d: (r, 0, d)),
        out_shape=jax.ShapeDtypeStruct((b, t_new, D_MODEL), F32),
        scratch_shapes=[pltpu.VMEM((FOX_HEADS, p_len), F32), pltpu.VMEM((FOX_HEADS, NEW_PAD), F32)],
        compiler_params=_cparams(("arbitrary", "arbitrary")), name="fox_attn_sample")(
            q, cache_k, cache_v, lfc_t, k_new, v_new, lfn_t, tri)


def _out_proj_kernel(r_ref, a_ref, g1_ref, wg_ref, w_ref, y_ref):
    x = r_ref[...]
    h = _rms(x, g1_ref[...]).astype(BF16)
    gate = jnp.dot(h, wg_ref[...], preferred_element_type=F32)
    mixed = (a_ref[...].astype(F32) * _silu(gate)).astype(BF16)
    y_ref[...] = x + jnp.dot(mixed, w_ref[...], preferred_element_type=F32)


def _out_proj(resid, attn, wts, *, tb):
    n = resid.shape[0]
    row = pl.BlockSpec((tb, D_MODEL), lambda i: (i, 0))
    consts = [wts["g1"], wts["w_g1"], wts["w_out1"]]
    return pl.pallas_call(
        _out_proj_kernel, grid=(n // tb,), in_specs=[row, row] + [_const_spec(a.shape) for a in consts],
        out_specs=row, out_shape=jax.ShapeDtypeStruct((n, D_MODEL), F32),
        compiler_params=_cparams(("arbitrary",)), name="out_proj")(resid, attn, *consts)


L0_TB = 512
FOX_TB = 256
ATTN_TQ = 512
ATTN_TK = 512
ATTN_UNROLL = 4
SWA_CHUNK = 64


def kernel(x_prompt, x_sample, state_pool, cache_swa_k, cache_swa_v, cache_fox_k, cache_fox_v, cache_fox_logf,
           norm0_g, w_in0, w_pool, pool_scale, swa_qn_g, swa_kn_g, swa_sinks, w_out0,
           norm1_g, w_in1, b_forget, fox_qn_g, fox_kn_g, w_out1):
    nb, seq, _ = x_prompt.shape
    db, dseq, _ = x_sample.shape
    past_len = cache_fox_k.shape[1]
    assert nb == 1 and seq % L0_TB == 0 and seq % ATTN_TQ == 0 and dseq % 8 == 0

    seg = np.kron(np.eye(2, dtype=np.float32), np.ones((HEAD_DIM, HEAD_DIM), np.float32))
    mc = FOX_HEADS * HEAD_DIM
    wts = {
        "g0": norm0_g.reshape(1, D_MODEL), "w_in0": w_in0.astype(BF16), "w_pool": w_pool.astype(BF16),
        "pool_scale": pool_scale.reshape(1, C_POOL),
        "qg0": jnp.tile(swa_qn_g, SWA_HEADS).reshape(1, SWA_HEADS * HEAD_DIM),
        "kg0": jnp.tile(swa_kn_g, SWA_KV_HEADS).reshape(1, LANES),
        "seg": jnp.asarray(seg, BF16), "sinks": swa_sinks, "w_out0": w_out0.astype(BF16),
        "g1": norm1_g.reshape(1, D_MODEL), "w_qkv1": w_in1[:, :3 * mc].astype(BF16),
        "w_out1": w_out1.astype(BF16),
        "w_f": jnp.pad(w_in1[:, 4 * mc:], ((0, 0), (0, LANES - FOX_HEADS))).astype(BF16),
        "b_f": jnp.pad(b_forget.astype(F32), (0, LANES - FOX_HEADS)).reshape(1, LANES),
        "qg1": jnp.tile(fox_qn_g, FOX_HEADS).reshape(1, mc), "kg1": jnp.tile(fox_kn_g, FOX_HEADS).reshape(1, mc),
        "w_k1": w_in1[:, mc:2 * mc].astype(BF16), "w_g1": w_in1[:, 3 * mc:4 * mc].astype(BF16),
        "w_q1t": w_in1[:, 0:mc].T.astype(BF16), "w_v1t": w_in1[:, 2 * mc:3 * mc].T.astype(BF16),
        "gq1c": jnp.broadcast_to(fox_qn_g.astype(F32).reshape(HEAD_DIM, 1), (HEAD_DIM, LANES)),
    }

    xp = x_prompt.reshape(seq, D_MODEL)
    y0p, pool_p, swk_p, swv_p = _layer0(xp, None, wts, ns=1, ts=L0_TB, chunk=SWA_CHUNK, base_pos=0)
    fk_p, fvt_p, fl_p, qa, ka, va = _fox_proj_prompt(y0p, wts, tb=FOX_TB)
    fv_p = jnp.transpose(fvt_p.reshape(FOX_HEADS, HEAD_DIM, seq), (2, 0, 1))
    attn_p = _fox_attn_prompt(qa, ka, va, tq=ATTN_TQ, tk=ATTN_TK, unroll=ATTN_UNROLL)
    yp = _out_proj(y0p, attn_p, wts, tb=L0_TB)

    xs = x_sample.reshape(db * dseq, D_MODEL)
    prefix = (jnp.pad(state_pool, ((0, 0), (1, 0), (0, 0))),
              cache_swa_k.reshape(db, WINDOW, LANES), cache_swa_v.reshape(db, WINDOW, LANES))
    y0s, pool_s, swk_s, swv_s = _layer0(xs, prefix, wts, ns=db, ts=dseq, chunk=dseq, base_pos=past_len)
    fk_s, fv_s, fl_s, q_s = _fox_proj_sample(y0s, wts, tb=FOX_TB)
    pad_rows = ((0, 0), (0, NEW_PAD - dseq), (0, 0))
    attn_s = _fox_attn_sample(
        q_s.reshape(db, dseq, D_MODEL),
        jnp.transpose(cache_fox_k, (0, 2, 3, 1)), jnp.transpose(cache_fox_v, (0, 2, 3, 1)),
        jnp.transpose(cache_fox_logf, (0, 2, 1)),
        jnp.pad(fk_s.reshape(db, dseq, D_MODEL), pad_rows), jnp.pad(fv_s.reshape(db, dseq, D_MODEL), pad_rows),
        jnp.pad(jnp.transpose(fl_s.reshape(db, dseq, FOX_HEADS), (0, 2, 1)), ((0, 0), (0, 0), (0, NEW_PAD - dseq))))
    ys = _out_proj(y0s, attn_s.reshape(db * dseq, D_MODEL), wts, tb=db * dseq)

    return (yp.reshape(1, seq, D_MODEL), ys.reshape(db, dseq, D_MODEL),
            pool_p[:, 1:], pool_s[:, 1:],
            swk_p.reshape(1, WINDOW, SWA_KV_HEADS, HEAD_DIM), swv_p.reshape(1, WINDOW, SWA_KV_HEADS, HEAD_DIM),
            swk_s.reshape(db, WINDOW, SWA_KV_HEADS, HEAD_DIM), swv_s.reshape(db, WINDOW, SWA_KV_HEADS, HEAD_DIM),
            fk_p.reshape(1, seq, FOX_HEADS, HEAD_DIM), fv_p.reshape(1, seq, FOX_HEADS, HEAD_DIM),
            fl_p.reshape(1, seq, FOX_HEADS),
            fk_s.reshape(db, dseq, FOX_HEADS, HEAD_DIM), fv_s.reshape(db, dseq, FOX_HEADS, HEAD_DIM),
            fl_s.reshape(db, dseq, FOX_HEADS))
```

```python
import functools

import numpy as np
import jax
import jax.numpy as jnp
from jax import lax
from jax.experimental import pallas as pl
from jax.experimental.pallas import tpu as pltpu

F32 = jnp.float32
BF16 = jnp.bfloat16

D_MODEL = 1024
HEAD_DIM = 64
ATTN_SCALE = HEAD_DIM ** -0.5
POOL_WINDOWS = (2, 4, 8, 16)
C_POOL = 512
POOL_PAD = 15
SWA_HEADS = 8
SWA_KV_HEADS = 2
SWA_REP = 4
WINDOW = 128
FOX_HEADS = 16
NORM_EPS = 1e-6
NEG_INF = -1e30
LOG2E = 1.4426950408889634
EXP2_ZERO = -160.0

LANES = 128
SWA_KEYS = 256
VMEM_LIMIT = 56 * 1024 * 1024


def _cparams(sem):
    return pltpu.CompilerParams(dimension_semantics=sem, vmem_limit_bytes=VMEM_LIMIT)


def _rms(x, g):
    ms = jnp.mean(x * x, axis=-1, keepdims=True)
    return x * lax.rsqrt(ms + NORM_EPS) * g


def _split2(x):
    hi = x.astype(BF16)
    lo = (x - hi.astype(F32)).astype(BF16)
    return hi, lo


def _split3(x):
    hi = x.astype(BF16)
    r = x - hi.astype(F32)
    mid = r.astype(BF16)
    lo = (r - mid.astype(F32)).astype(BF16)
    return hi, mid, lo


def _head_rms_tile(x, seg, g):
    hi, lo = _split2(x * x)
    ss = jnp.dot(hi, seg, preferred_element_type=F32) + jnp.dot(lo, seg, preferred_element_type=F32)
    return x * lax.rsqrt(ss * (1.0 / HEAD_DIM) + NORM_EPS) * g


def _silu(g):
    return g / (1.0 + jnp.exp(-g))


def _l0_kernel(*refs, ns, ts, chunk, has_cache, base_pos, nblk):
    if has_cache:
        (x_ref, pp_ref, kp_ref, vp_ref, g0_ref, win_ref, wpool_ref, pscale_ref, qg_ref, kg_ref, seg_ref,
         slope_ref, sink_ref, wout_ref, y_ref, ps_ref, ks_ref, vs_ref, uext, kext, vext, mix) = refs
    else:
        (x_ref, g0_ref, win_ref, wpool_ref, pscale_ref, qg_ref, kg_ref, seg_ref,
         slope_ref, sink_ref, wout_ref, y_ref, ps_ref, ks_ref, vs_ref, uext, kext, vext, mix) = refs
    padk = SWA_KEYS - WINDOW - chunk
    hist = padk + WINDOW
    i = pl.program_id(0)

    x = x_ref[...]
    h = _rms(x, g0_ref[...]).astype(BF16)
    proj = jnp.dot(h, win_ref[...], preferred_element_type=F32)
    u = proj[:, 0:C_POOL]
    gate = proj[:, 1280:2304]
    seg = seg_ref[...]
    qn = [_head_rms_tile(proj[:, 512 + LANES * j:512 + LANES * (j + 1)], seg,
                         qg_ref[:, LANES * j:LANES * (j + 1)]) * (ATTN_SCALE * LOG2E) for j in range(4)]
    kn = _head_rms_tile(proj[:, 1024:1152], seg, kg_ref[...])
    v = proj[:, 1152:1280]

    if has_cache:
        for s in range(ns):
            uext[s, 0:16, :] = pp_ref[s]
            kext[s, 0:padk, :] = jnp.zeros((padk, LANES), F32)
            vext[s, 0:padk, :] = jnp.zeros((padk, LANES), F32)
            kext[s, padk:hist, :] = kp_ref[s]
            vext[s, padk:hist, :] = vp_ref[s]
    else:
        @pl.when(i == 0)
        def _():
            uext[0, 0:16, :] = jnp.zeros((16, C_POOL), F32)
            kext[0, 0:hist, :] = jnp.zeros((hist, LANES), F32)
            vext[0, 0:hist, :] = jnp.zeros((hist, LANES), F32)

        @pl.when(i > 0)
        def _():
            uext[0, 0:16, :] = uext[0, ts:ts + 16, :]
            kext[0, 0:hist, :] = kext[0, ts:ts + hist, :]
            vext[0, 0:hist, :] = vext[0, ts:ts + hist, :]

    for s in range(ns):
        uext[s, 16:16 + ts, :] = u[s * ts:(s + 1) * ts]
        kext[s, hist:hist + ts, :] = kn[s * ts:(s + 1) * ts]
        vext[s, hist:hist + ts, :] = v[s * ts:(s + 1) * ts]

    pos = base_pos + i * ts + lax.broadcasted_iota(jnp.int32, (ts, LANES), 0)
    for s in range(ns):
        for g, w in enumerate(POOL_WINDOWS):
            cols = slice(LANES * g, LANES * (g + 1))
            acc = uext[s, 16:16 + ts, cols]
            cur = acc
            for j in range(1, w):
                acc = acc + uext[s, 16 - j:16 - j + ts, cols]
            cnt = jnp.minimum(pos + 1, w).astype(F32)
            mix[s * ts:(s + 1) * ts, cols] = acc / cnt - cur
    for g in range(4):
        cols = slice(LANES * g, LANES * (g + 1))
        d = mix[:, cols].astype(BF16)
        mix[:, cols] = jnp.dot(d, wpool_ref[g], preferred_element_type=F32) * pscale_ref[:, cols]

    rows4 = SWA_REP * chunk
    lo_c = lax.broadcasted_iota(jnp.int32, (chunk, LANES), 1) < HEAD_DIM
    lo_k = lax.broadcasted_iota(jnp.int32, (SWA_KEYS, LANES), 1) < HEAD_DIM
    kj = lax.broadcasted_iota(jnp.int32, (rows4, SWA_KEYS), 1)
    qi = lax.broadcasted_iota(jnp.int32, (rows4, SWA_KEYS), 0) % chunk
    absrel = jnp.abs(qi + WINDOW - (kj - padk)).astype(F32)
    bias = []
    sinkc = []
    for g in range(SWA_KV_HEADS):
        sl = slope_ref[g] * LOG2E
        bias.append(jnp.where(kj >= padk, -jnp.concatenate([sl, sl], axis=1) * absrel, NEG_INF))
        sinkc.append(sink_ref[g][:, 0:1] * LOG2E)
    nch = ts // chunk
    units = [(s, c, g) for s in range(ns) for c in range(nch) for g in range(SWA_KV_HEADS)]

    def swa_scores(s, c, g):
        r0 = c * chunk
        rows = slice(s * ts + r0, s * ts + r0 + chunk)
        kwin = kext[s, r0:r0 + SWA_KEYS, :]
        vwin = vext[s, r0:r0 + SWA_KEYS, :]
        krl = pltpu.roll(kwin, HEAD_DIM, 1)
        vrl = pltpu.roll(vwin, HEAD_DIM, 1)
        if g == 0:
            kd = jnp.where(lo_k, kwin, krl).astype(BF16)
            vd = jnp.where(lo_k, vwin, vrl).astype(BF16)
        else:
            kd = jnp.where(lo_k, krl, kwin).astype(BF16)
            vd = jnp.where(lo_k, vrl, vwin).astype(BF16)
        q0 = qn[2 * g][rows]
        q1 = qn[2 * g + 1][rows]
        qs = jnp.concatenate([jnp.where(lo_c, q0, 0.0), jnp.where(lo_c, 0.0, q0),
                              jnp.where(lo_c, q1, 0.0), jnp.where(lo_c, 0.0, q1)], axis=0).astype(BF16)
        return lax.dot_general(qs, kd, (((1,), (1,)), ((), ())), preferred_element_type=F32), vd

    def swa_finish(s, c, g, sc, vd):
        rows = slice(s * ts + c * chunk, s * ts + (c + 1) * chunk)
        sc = sc + bias[g]
        if not has_cache and c * chunk < WINDOW:
            sc = jnp.where(kj >= padk + jnp.maximum(0, WINDOW - (i * nch + c) * chunk), sc, NEG_INF)
        m = jnp.maximum(jnp.max(sc, axis=-1, keepdims=True), sinkc[g])
        p = jnp.exp2(sc - m)
        den = jnp.sum(p, axis=-1, keepdims=True) + jnp.exp2(sinkc[g] - m)
        o = jnp.dot(p.astype(BF16), vd, preferred_element_type=F32) / den
        for jj in range(2):
            t = 2 * g + jj
            mix[rows, C_POOL + LANES * t:C_POOL + LANES * (t + 1)] = jnp.where(
                lo_c, o[(2 * jj) * chunk:(2 * jj + 1) * chunk], o[(2 * jj + 1) * chunk:(2 * jj + 2) * chunk])

    pending = swa_scores(*units[0])
    for n, unit in enumerate(units):
        nxt = swa_scores(*units[n + 1]) if n + 1 < len(units) else None
        swa_finish(*unit, *pending)
        pending = nxt

    mixed = (mix[...] * _silu(gate)).astype(BF16)
    y_ref[...] = x + jnp.dot(mixed, wout_ref[...], preferred_element_type=F32)

    @pl.when(i == nblk - 1)
    def _():
        for s in range(ns):
            ps_ref[s] = uext[s, ts:ts + 16, :]
            ks_ref[s] = kext[s, padk + ts:padk + ts + WINDOW, :]
            vs_ref[s] = vext[s, padk + ts:padk + ts + WINDOW, :]


def _const_spec(shape):
    nd = len(shape)
    return pl.BlockSpec(shape, lambda i, _nd=nd: (0,) * _nd)


def _layer0(x2d, prefix, wts, *, ns, ts, chunk, base_pos):
    n = x2d.shape[0]
    tb = ns * ts
    nblk = n // tb
    has_cache = prefix is not None
    padk = SWA_KEYS - WINDOW - chunk
    rows4 = SWA_REP * chunk
    slope = np.repeat(2.0 ** (-(np.arange(SWA_HEADS) + 1.0)), chunk).reshape(SWA_KV_HEADS, rows4, 1)
    slope = jnp.asarray(np.broadcast_to(slope, (SWA_KV_HEADS, rows4, LANES)).astype(np.float32))
    sink = jnp.broadcast_to(jnp.repeat(wts["sinks"].astype(F32), chunk).reshape(SWA_KV_HEADS, rows4, 1),
                            (SWA_KV_HEADS, rows4, LANES))
    consts = [wts["g0"], wts["w_in0"], wts["w_pool"], wts["pool_scale"], wts["qg0"], wts["kg0"], wts["seg"],
              slope, sink, wts["w_out0"]]
    in_specs = [pl.BlockSpec((tb, D_MODEL), lambda i: (i, 0))]
    args = [x2d]
    if has_cache:
        for a in prefix:
            in_specs.append(_const_spec(a.shape))
            args.append(a)
    for a in consts:
        in_specs.append(_const_spec(a.shape))
        args.append(a)
    out_shape = [jax.ShapeDtypeStruct((n, D_MODEL), F32),
                 jax.ShapeDtypeStruct((ns, 16, C_POOL), F32),
                 jax.ShapeDtypeStruct((ns, WINDOW, LANES), F32),
                 jax.ShapeDtypeStruct((ns, WINDOW, LANES), F32)]
    out_specs = [pl.BlockSpec((tb, D_MODEL), lambda i: (i, 0)),
                 _const_spec((ns, 16, C_POOL)), _const_spec((ns, WINDOW, LANES)), _const_spec((ns, WINDOW, LANES))]
    scratch = [pltpu.VMEM((ns, 16 + ts, C_POOL), F32),
               pltpu.VMEM((ns, padk + WINDOW + ts, LANES), F32),
               pltpu.VMEM((ns, padk + WINDOW + ts, LANES), F32),
               pltpu.VMEM((tb, D_MODEL), F32)]
    kern = functools.partial(_l0_kernel, ns=ns, ts=ts, chunk=chunk, has_cache=has_cache,
                             base_pos=base_pos, nblk=nblk)
    return pl.pallas_call(
        kern, grid=(nblk,), in_specs=in_specs, out_specs=out_specs, out_shape=out_shape,
        scratch_shapes=scratch, compiler_params=_cparams(("arbitrary",)),
        name="layer0_prompt" if not has_cache else "layer0_sample")(*args)


def _log_sigmoid(z):
    return jnp.minimum(z, 0.0) - jnp.log1p(jnp.exp(-jnp.abs(z)))


def _fox_proj_sample_kernel(x_ref, g1_ref, w_ref, wf_ref, bf_ref, qg_ref, kg_ref, seg_ref,
                            k_ref, v_ref, lf_ref, q_ref):
    x = x_ref[...]
    h = _rms(x, g1_ref[...]).astype(BF16)
    z = jnp.dot(h, wf_ref[...], preferred_element_type=F32) + bf_ref[...]
    lf_ref[...] = _log_sigmoid(z)[:, 0:FOX_HEADS]
    seg = seg_ref[...]
    mc = D_MODEL
    q = jnp.dot(h, w_ref[:, 0:mc], preferred_element_type=F32)
    k = jnp.dot(h, w_ref[:, mc:2 * mc], preferred_element_type=F32)
    v_ref[...] = jnp.dot(h, w_ref[:, 2 * mc:3 * mc], preferred_element_type=F32)
    for j in range(FOX_HEADS // 2):
        cols = slice(LANES * j, LANES * (j + 1))
        q_ref[:, cols] = _head_rms_tile(q[:, cols], seg, qg_ref[:, cols]) * ATTN_SCALE
        k_ref[:, cols] = _head_rms_tile(k[:, cols], seg, kg_ref[:, cols])


def _fox_proj_sample(x2d, wts, *, tb):
    n = x2d.shape[0]
    consts = [wts["g1"], wts["w_qkv1"], wts["w_f"], wts["b_f"], wts["qg1"], wts["kg1"], wts["seg"]]
    row_spec = pl.BlockSpec((tb, D_MODEL), lambda i: (i, 0))
    in_specs = [row_spec] + [_const_spec(a.shape) for a in consts]
    row = jax.ShapeDtypeStruct((n, D_MODEL), F32)
    out_shape = [row, row, jax.ShapeDtypeStruct((n, FOX_HEADS), F32), row]
    out_specs = [row_spec, row_spec, pl.BlockSpec((tb, FOX_HEADS), lambda i: (i, 0)), row_spec]
    return pl.pallas_call(
        _fox_proj_sample_kernel, grid=(n // tb,), in_specs=in_specs, out_specs=out_specs, out_shape=out_shape,
        compiler_params=_cparams(("arbitrary",)), name="fox_proj_sample")(x2d, *consts)


def _fox_proj_prompt_kernel(x_ref, g1_ref, wk_ref, wf_ref, bf_ref, wqt_ref, wvt_ref, gq_ref, kg_ref,
                            seg_ref, tri_ref, e_ref,
                            k_ref, vt_ref, lf_ref, qa_ref, ka_ref, va_ref, ce_ref, carry, *, tb):
    i = pl.program_id(0)
    x = x_ref[...]
    hf = _rms(x, g1_ref[...])
    h = hf.astype(BF16)
    ht = hf.T.astype(BF16)
    logf = _log_sigmoid(jnp.dot(h, wf_ref[...], preferred_element_type=F32) + bf_ref[...])
    lf_ref[...] = logf[:, 0:FOX_HEADS]
    k = jnp.dot(h, wk_ref[...], preferred_element_type=F32)

    @pl.when(i == 0)
    def _():
        carry[...] = jnp.zeros(carry.shape, F32)

    tri = tri_ref[...]
    hi, mid, lo = _split3(logf)
    cum = (jnp.dot(tri, hi, preferred_element_type=F32) + jnp.dot(tri, mid, preferred_element_type=F32)
           + jnp.dot(tri, lo, preferred_element_type=F32)) + carry[0:1, :]
    carry[0:1, :] = cum[tb - 1:tb, :]
    ce_ref[0] = cum[tb - 1:tb, :]
    nh, nm, nl = _split3(-LOG2E * cum)
    biasall = jnp.dot(jnp.concatenate([nh, nm, nl], axis=1), e_ref[...], preferred_element_type=F32)
    lo_l = lax.broadcasted_iota(jnp.int32, (tb, LANES), 1) < HEAD_DIM
    seg = seg_ref[...]
    for j in range(FOX_HEADS // 2):
        cols = slice(LANES * j, LANES * (j + 1))
        kt = _head_rms_tile(k[:, cols], seg, kg_ref[:, cols])
        k_ref[:, cols] = kt
        for half in range(2):
            hd = 2 * j + half
            kh = pltpu.roll(kt, HEAD_DIM, 1) if half else kt
            ka_ref[hd] = jnp.where(lo_l, kh, biasall[:, LANES * hd:LANES * (hd + 1)]).astype(BF16)

    qt = jnp.dot(wqt_ref[...], ht, preferred_element_type=F32)
    vt = jnp.dot(wvt_ref[...], ht, preferred_element_type=F32)
    vt_ref[...] = vt
    srow = lax.broadcasted_iota(jnp.int32, (HEAD_DIM, tb), 0)
    q_aug = jnp.where(srow < 3, 1.0, 0.0)
    v_aug = jnp.where(srow == 0, 1.0, 0.0)
    gq = jnp.concatenate([gq_ref[...]] * (tb // LANES), axis=1) * (ATTN_SCALE * LOG2E)
    for hd in range(FOX_HEADS):
        rows = slice(HEAD_DIM * hd, HEAD_DIM * (hd + 1))
        qh = qt[rows]
        ss = jnp.sum(qh * qh, axis=0, keepdims=True)
        qn = qh * lax.rsqrt(ss * (1.0 / HEAD_DIM) + NORM_EPS) * gq
        qa_ref[hd] = jnp.concatenate([qn, q_aug], axis=0).astype(BF16)
        va_ref[hd] = jnp.concatenate([vt[rows], v_aug], axis=0).astype(BF16)


def _fox_proj_prompt(x2d, wts, *, tb):
    n = x2d.shape[0]
    tri = jnp.asarray(np.tril(np.ones((tb, tb), np.float32)), BF16)
    e = np.zeros((3 * LANES, FOX_HEADS * LANES), np.float32)
    for part in range(3):
        for hd in range(FOX_HEADS):
            e[part * LANES + hd, hd * LANES + HEAD_DIM + part] = 1.0
    consts = [wts["g1"], wts["w_k1"], wts["w_f"], wts["b_f"], wts["w_q1t"], wts["w_v1t"],
              wts["gq1c"], wts["kg1"], wts["seg"], tri, jnp.asarray(e, BF16)]
    row_spec = pl.BlockSpec((tb, D_MODEL), lambda i: (i, 0))
    in_specs = [row_spec] + [_const_spec(a.shape) for a in consts]
    row = jax.ShapeDtypeStruct((n, D_MODEL), F32)
    out_shape = [row, jax.ShapeDtypeStruct((D_MODEL, n), F32), jax.ShapeDtypeStruct((n, FOX_HEADS), F32),
                 jax.ShapeDtypeStruct((FOX_HEADS, LANES, n), BF16), jax.ShapeDtypeStruct((FOX_HEADS, n, LANES), BF16),
                 jax.ShapeDtypeStruct((FOX_HEADS, LANES, n), BF16),
                 jax.ShapeDtypeStruct((n // tb, 1, LANES), F32)]
    tspec = pl.BlockSpec((FOX_HEADS, LANES, tb), lambda i: (0, 0, i))
    out_specs = [row_spec, pl.BlockSpec((D_MODEL, tb), lambda i: (0, i)),
                 pl.BlockSpec((tb, FOX_HEADS), lambda i: (i, 0)),
                 tspec, pl.BlockSpec((FOX_HEADS, tb, LANES), lambda i: (0, i, 0)), tspec,
                 pl.BlockSpec((1, 1, LANES), lambda i: (i, 0, 0))]
    return pl.pallas_call(
        functools.partial(_fox_proj_prompt_kernel, tb=tb), grid=(n // tb,), in_specs=in_specs,
        out_specs=out_specs, out_shape=out_shape, scratch_shapes=[pltpu.VMEM((8, LANES), F32)],
        compiler_params=_cparams(("arbitrary",)), name="fox_proj_prompt")(x2d, *consts)


def _fox_attn_kernel(ce_ref, qk_ref, q_ref, k_ref, v_ref, o_ref, m_scr, al_scr, acc_scr, s_scr,
                     *, tq, tk, unroll, nkb):
    pair = pl.program_id(0)
    qi = pl.program_id(1)
    r = tq // tk
    n_u = r * qi
    m_scr[...] = jnp.full(m_scr.shape, NEG_INF, F32)
    acc_scr[...] = jnp.zeros(acc_scr.shape, F32)

    last = jnp.maximum(n_u - 1, 0)
    slack = 2.0 * qk_ref[0]
    first = jnp.int32(0)
    for j in range(nkb - 1):
        gap0 = LOG2E * (ce_ref[last * FOX_HEADS + 2 * pair] - ce_ref[j * FOX_HEADS + 2 * pair])
        gap1 = LOG2E * (ce_ref[last * FOX_HEADS + 2 * pair + 1] - ce_ref[j * FOX_HEADS + 2 * pair + 1])
        dead = (j < n_u - 1) & (slack + gap0 < EXP2_ZERO) & (slack + gap1 < EXP2_ZERO)
        first = first + dead.astype(jnp.int32)

    def score_matmuls(kb):
        off = pl.multiple_of(kb * tk, tk)
        return [jnp.dot(k_ref[hh, pl.ds(off, tk), :], q_ref[hh], preferred_element_type=F32)
                for hh in range(2)]

    def score_finish(sts, diag):
        for hh in range(2):
            st = sts[hh]
            if diag is not None:
                krow = lax.broadcasted_iota(jnp.int32, (tk, tq), 0) + diag * tk
                qcol = lax.broadcasted_iota(jnp.int32, (tk, tq), 1)
                st = jnp.where(krow <= qcol, st, NEG_INF)
            m_old = m_scr[hh, 0:1, :]
            m_new = jnp.maximum(m_old, jnp.max(st, axis=0, keepdims=True))
            al_scr[hh, 0:1, :] = jnp.exp2(m_old - m_new)
            m_scr[hh, 0:1, :] = m_new
            s_scr[hh] = st

    def accumulate(kb):
        off = pl.multiple_of(kb * tk, tk)
        for hh in range(2):
            p = jnp.exp2(s_scr[hh] - m_scr[hh, 0:1, :]).astype(BF16)
            pv = jnp.dot(v_ref[hh, :, pl.ds(off, tk)], p, preferred_element_type=F32)
            acc_scr[hh] = al_scr[hh, 0:1, :] * acc_scr[hh] + pv

    def advance(kb, diag=None):
        sts = score_matmuls(kb + 1)
        accumulate(kb)
        score_finish(sts, diag)

    @pl.when(qi == 0)
    def _():
        score_finish(score_matmuls(0), 0)

    @pl.when(qi > 0)
    def _():
        score_finish(score_matmuls(first), None)
        n_adv = n_u - 1 - first

        def body(t, carry):
            for u in range(unroll):
                advance(first + unroll * t + u)
            return carry

        lax.fori_loop(0, n_adv // unroll, body, 0)
        done = first + (n_adv // unroll) * unroll
        for u in range(unroll - 1):
            @pl.when(n_u - 1 - done > u)
            def _():
                advance(done + u)

        advance(n_u - 1, 0)

    for d in range(1, r):
        advance(n_u + d - 1, d)
    accumulate(n_u + r - 1)
    lane = lax.broadcasted_iota(jnp.int32, (tq, LANES), 1)
    outs = []
    for hh in range(2):
        a = acc_scr[hh]
        outs.append((a / a[HEAD_DIM:HEAD_DIM + 1, :]).T)
    o_ref[...] = jnp.where(lane < HEAD_DIM, outs[0], pltpu.roll(outs[1], HEAD_DIM, 1)).astype(o_ref.dtype)


def _fox_attn_prompt(cum_end, qk_bound, qa, ka, va, *, tq, tk, unroll):
    n = ka.shape[1]
    grid_spec = pltpu.PrefetchScalarGridSpec(
        num_scalar_prefetch=2, grid=(FOX_HEADS // 2, n // tq),
        in_specs=[pl.BlockSpec((2, LANES, tq), lambda p, i, ce, qk: (p, 0, i)),
                  pl.BlockSpec((2, n, LANES), lambda p, i, ce, qk: (p, 0, 0)),
                  pl.BlockSpec((2, LANES, n), lambda p, i, ce, qk: (p, 0, 0))],
        out_specs=pl.BlockSpec((tq, LANES), lambda p, i, ce, qk: (i, p)),
        scratch_shapes=[pltpu.VMEM((2, 8, tq), F32), pltpu.VMEM((2, 8, tq), F32),
                        pltpu.VMEM((2, LANES, tq), F32), pltpu.VMEM((2, tk, tq), F32)])
    return pl.pallas_call(
        functools.partial(_fox_attn_kernel, tq=tq, tk=tk, unroll=unroll, nkb=n // tk), grid_spec=grid_spec,
        out_shape=jax.ShapeDtypeStruct((n, D_MODEL), BF16),
        compiler_params=_cparams(("arbitrary", "arbitrary")), name="fox_attn_prompt")(
            cum_end, qk_bound, qa, ka, va)


CUM_BLK = 512
NEW_PAD = 128


def _fox_sample_kernel(q_ref, kc_ref, vc_ref, lfc_ref, kn_ref, vn_ref, lfn_ref, tri_ref, o_ref, negc, negn,
                       *, t_new, p_len):
    qd = pl.program_id(1)
    rows = 4 * t_new

    @pl.when(qd == 0)
    def _():
        tri = tri_ref[...]

        def prefix_sums(x, t):
            c3 = jnp.dot(jnp.concatenate(_split3(x), axis=0), t, preferred_element_type=F32)
            return c3[0:FOX_HEADS] + c3[FOX_HEADS:2 * FOX_HEADS] + c3[2 * FOX_HEADS:3 * FOX_HEADS]

        carry = jnp.zeros((FOX_HEADS, 1), F32)
        for b in range(p_len // CUM_BLK):
            c = prefix_sums(lfc_ref[0, :, b * CUM_BLK:(b + 1) * CUM_BLK], tri) + carry
            negc[:, b * CUM_BLK:(b + 1) * CUM_BLK] = -c
            carry = c[:, CUM_BLK - 1:CUM_BLK]
        negn[...] = -(prefix_sums(lfn_ref[0], tri[0:NEW_PAD, 0:NEW_PAD]) + carry)

    qq = q_ref[0]
    rhead = lax.broadcasted_iota(jnp.int32, (rows, 2 * LANES), 0) // t_new
    lhead = lax.broadcasted_iota(jnp.int32, (rows, 2 * LANES), 1) // HEAD_DIM
    qbd = jnp.where(rhead == lhead, jnp.concatenate([qq] * 4, axis=0), 0.0).astype(BF16)

    kct = kc_ref[0].reshape(2 * LANES, p_len).astype(BF16)
    vct = vc_ref[0].reshape(2 * LANES, p_len).astype(BF16)
    s_c = jnp.dot(qbd, kct, preferred_element_type=F32)
    s_n = lax.dot_general(qbd, kn_ref[0].astype(BF16), (((1,), (1,)), ((), ())), preferred_element_type=F32)
    bc = jnp.concatenate([jnp.broadcast_to(negc[pl.ds(4 * qd + jh, 1), :], (t_new, p_len)) for jh in range(4)], axis=0)
    bn = jnp.concatenate([jnp.broadcast_to(negn[pl.ds(4 * qd + jh, 1), :], (t_new, NEW_PAD)) for jh in range(4)], axis=0)
    s_c = s_c + bc
    qrow = lax.broadcasted_iota(jnp.int32, (rows, NEW_PAD), 0) % t_new
    kcol = lax.broadcasted_iota(jnp.int32, (rows, NEW_PAD), 1)
    s_n = jnp.where(kcol <= qrow, s_n + bn, NEG_INF)
    m = jnp.maximum(jnp.max(s_c, axis=-1, keepdims=True), jnp.max(s_n, axis=-1, keepdims=True))
    p_c = jnp.exp(s_c - m)
    p_n = jnp.exp(s_n - m)
    den = jnp.sum(p_c, axis=-1, keepdims=True) + jnp.sum(p_n, axis=-1, keepdims=True)
    o = (lax.dot_general(p_c.astype(BF16), vct, (((1,), (1,)), ((), ())), preferred_element_type=F32)
         + jnp.dot(p_n.astype(BF16), vn_ref[0].astype(BF16), preferred_element_type=F32)) / den
    lh = lax.broadcasted_iota(jnp.int32, (t_new, 2 * LANES), 1) // HEAD_DIM
    out = jnp.zeros((t_new, 2 * LANES), F32)
    for jh in range(4):
        out = jnp.where(lh == jh, o[jh * t_new:(jh + 1) * t_new], out)
    o_ref[0] = out


def _fox_attn_sample(q, cache_k, cache_v, lfc_t, k_new, v_new, lfn_t):
    b, t_new, _ = q.shape
    p_len = cache_k.shape[3]
    tri = jnp.asarray(np.triu(np.ones((CUM_BLK, CUM_BLK), np.float32)), BF16)
    quad = 2 * LANES
    in_specs = [pl.BlockSpec((1, t_new, quad), lambda r, d: (r, 0, d)),
                pl.BlockSpec((1, 4, HEAD_DIM, p_len), lambda r, d: (r, d, 0, 0)),
                pl.BlockSpec((1, 4, HEAD_DIM, p_len), lambda r, d: (r, d, 0, 0)),
                pl.BlockSpec((1, FOX_HEADS, p_len), lambda r, d: (r, 0, 0)),
                pl.BlockSpec((1, NEW_PAD, quad), lambda r, d: (r, 0, d)),
                pl.BlockSpec((1, NEW_PAD, quad), lambda r, d: (r, 0, d)),
                pl.BlockSpec((1, FOX_HEADS, NEW_PAD), lambda r, d: (r, 0, 0)),
                pl.BlockSpec((CUM_BLK, CUM_BLK), lambda r, d: (0, 0))]
    return pl.pallas_call(
        functools.partial(_fox_sample_kernel, t_new=t_new, p_len=p_len),
        grid=(b, D_MODEL // quad), in_specs=in_specs,
        out_specs=pl.BlockSpec((1, t_new, quad), lambda r, d: (r, 0, d)),
        out_shape=jax.ShapeDtypeStruct((b, t_new, D_MODEL), F32),
        scratch_shapes=[pltpu.VMEM((FOX_HEADS, p_len), F32), pltpu.VMEM((FOX_HEADS, NEW_PAD), F32)],
        compiler_params=_cparams(("arbitrary", "arbitrary")), name="fox_attn_sample")(
            q, cache_k, cache_v, lfc_t, k_new, v_new, lfn_t, tri)


def _out_proj_kernel(r_ref, a_ref, g1_ref, wg_ref, w_ref, y_ref):
    x = r_ref[...]
    h = _rms(x, g1_ref[...]).astype(BF16)
    gate = jnp.dot(h, wg_ref[...], preferred_element_type=F32)
    mixed = (a_ref[...].astype(F32) * _silu(gate)).astype(BF16)
    y_ref[...] = x + jnp.dot(mixed, w_ref[...], preferred_element_type=F32)


def _out_proj(resid, attn, wts, *, tb):
    n = resid.shape[0]
    row = pl.BlockSpec((tb, D_MODEL), lambda i: (i, 0))
    consts = [wts["g1"], wts["w_g1"], wts["w_out1"]]
    return pl.pallas_call(
        _out_proj_kernel, grid=(n // tb,), in_specs=[row, row] + [_const_spec(a.shape) for a in consts],
        out_specs=row, out_shape=jax.ShapeDtypeStruct((n, D_MODEL), F32),
        compiler_params=_cparams(("arbitrary",)), name="out_proj")(resid, attn, *consts)


L0_TB = 512
FOX_TB = 256
ATTN_TQ = 512
ATTN_TK = 512
ATTN_UNROLL = 4
SWA_CHUNK = 64


def kernel(x_prompt, x_sample, state_pool, cache_swa_k, cache_swa_v, cache_fox_k, cache_fox_v, cache_fox_logf,
           norm0_g, w_in0, w_pool, pool_scale, swa_qn_g, swa_kn_g, swa_sinks, w_out0,
           norm1_g, w_in1, b_forget, fox_qn_g, fox_kn_g, w_out1):
    nb, seq, _ = x_prompt.shape
    db, dseq, _ = x_sample.shape
    past_len = cache_fox_k.shape[1]
    assert nb == 1 and seq % L0_TB == 0 and seq % ATTN_TQ == 0 and dseq % 8 == 0

    seg = np.kron(np.eye(2, dtype=np.float32), np.ones((HEAD_DIM, HEAD_DIM), np.float32))
    mc = FOX_HEADS * HEAD_DIM
    wts = {
        "g0": norm0_g.reshape(1, D_MODEL), "w_in0": w_in0.astype(BF16), "w_pool": w_pool.astype(BF16),
        "pool_scale": pool_scale.reshape(1, C_POOL),
        "qg0": jnp.tile(swa_qn_g, SWA_HEADS).reshape(1, SWA_HEADS * HEAD_DIM),
        "kg0": jnp.tile(swa_kn_g, SWA_KV_HEADS).reshape(1, LANES),
        "seg": jnp.asarray(seg, BF16), "sinks": swa_sinks, "w_out0": w_out0.astype(BF16),
        "g1": norm1_g.reshape(1, D_MODEL), "w_qkv1": w_in1[:, :3 * mc].astype(BF16),
        "w_out1": w_out1.astype(BF16),
        "w_f": jnp.pad(w_in1[:, 4 * mc:], ((0, 0), (0, LANES - FOX_HEADS))).astype(BF16),
        "b_f": jnp.pad(b_forget.astype(F32), (0, LANES - FOX_HEADS)).reshape(1, LANES),
        "qg1": jnp.tile(fox_qn_g, FOX_HEADS).reshape(1, mc), "kg1": jnp.tile(fox_kn_g, FOX_HEADS).reshape(1, mc),
        "w_k1": w_in1[:, mc:2 * mc].astype(BF16), "w_g1": w_in1[:, 3 * mc:4 * mc].astype(BF16),
        "w_q1t": w_in1[:, 0:mc].T.astype(BF16), "w_v1t": w_in1[:, 2 * mc:3 * mc].T.astype(BF16),
        "gq1c": jnp.broadcast_to(fox_qn_g.astype(F32).reshape(HEAD_DIM, 1), (HEAD_DIM, LANES)),
    }

    xp = x_prompt.reshape(seq, D_MODEL)
    y0p, pool_p, swk_p, swv_p = _layer0(xp, None, wts, ns=1, ts=L0_TB, chunk=SWA_CHUNK, base_pos=0)
    fk_p, fvt_p, fl_p, qa, ka, va, cum_end = _fox_proj_prompt(y0p, wts, tb=FOX_TB)
    fv_p = jnp.transpose(fvt_p.reshape(FOX_HEADS, HEAD_DIM, seq), (2, 0, 1))
    per = ATTN_TK // FOX_TB
    cum_end = cum_end[per - 1::per, 0, :FOX_HEADS].reshape(-1)
    qk_bound = (1.02 * LOG2E * HEAD_DIM * ATTN_SCALE * jnp.max(jnp.abs(fox_qn_g)) * jnp.max(jnp.abs(fox_kn_g))
                ).astype(F32).reshape(1)
    attn_p = _fox_attn_prompt(cum_end, qk_bound, qa, ka, va, tq=ATTN_TQ, tk=ATTN_TK, unroll=ATTN_UNROLL)
    yp = _out_proj(y0p, attn_p, wts, tb=L0_TB)

    xs = x_sample.reshape(db * dseq, D_MODEL)
    prefix = (jnp.pad(state_pool, ((0, 0), (1, 0), (0, 0))),
              cache_swa_k.reshape(db, WINDOW, LANES), cache_swa_v.reshape(db, WINDOW, LANES))
    y0s, pool_s, swk_s, swv_s = _layer0(xs, prefix, wts, ns=db, ts=dseq, chunk=dseq, base_pos=past_len)
    fk_s, fv_s, fl_s, q_s = _fox_proj_sample(y0s, wts, tb=FOX_TB)
    pad_rows = ((0, 0), (0, NEW_PAD - dseq), (0, 0))
    attn_s = _fox_attn_sample(
        q_s.reshape(db, dseq, D_MODEL),
        jnp.transpose(cache_fox_k, (0, 2, 3, 1)), jnp.transpose(cache_fox_v, (0, 2, 3, 1)),
        jnp.transpose(cache_fox_logf, (0, 2, 1)),
        jnp.pad(fk_s.reshape(db, dseq, D_MODEL), pad_rows), jnp.pad(fv_s.reshape(db, dseq, D_MODEL), pad_rows),
        jnp.pad(jnp.transpose(fl_s.reshape(db, dseq, FOX_HEADS), (0, 2, 1)), ((0, 0), (0, 0), (0, NEW_PAD - dseq))))
    ys = _out_proj(y0s, attn_s.reshape(db * dseq, D_MODEL), wts, tb=db * dseq)

    return (yp.reshape(1, seq, D_MODEL), ys.reshape(db, dseq, D_MODEL),
            pool_p[:, 1:], pool_s[:, 1:],
            swk_p.reshape(1, WINDOW, SWA_KV_HEADS, HEAD_DIM), swv_p.reshape(1, WINDOW, SWA_KV_HEADS, HEAD_DIM),
            swk_s.reshape(db, WINDOW, SWA_KV_HEADS, HEAD_DIM), swv_s.reshape(db, WINDOW, SWA_KV_HEADS, HEAD_DIM),
            fk_p.reshape(1, seq, FOX_HEADS, HEAD_DIM), fv_p.reshape(1, seq, FOX_HEADS, HEAD_DIM),
            fl_p.reshape(1, seq, FOX_HEADS),
            fk_s.reshape(db, dseq, FOX_HEADS, HEAD_DIM), fv_s.reshape(db, dseq, FOX_HEADS, HEAD_DIM),
            fl_s.reshape(db, dseq, FOX_HEADS))
```

```python
import functools

import numpy as np
import jax
import jax.numpy as jnp
from jax import lax
from jax.experimental import pallas as pl
from jax.experimental.pallas import tpu as pltpu

F32 = jnp.float32
BF16 = jnp.bfloat16

D_MODEL = 1024
HEAD_DIM = 64
ATTN_SCALE = HEAD_DIM ** -0.5
POOL_WINDOWS = (2, 4, 8, 16)
C_POOL = 512
POOL_PAD = 15
SWA_HEADS = 8
SWA_KV_HEADS = 2
SWA_REP = 4
WINDOW = 128
FOX_HEADS = 16
NORM_EPS = 1e-6
NEG_INF = -1e30
LOG2E = 1.4426950408889634
EXP2_ZERO = -160.0

LANES = 128
SWA_KEYS = 256
VMEM_LIMIT = 56 * 1024 * 1024


def _cparams(sem):
    return pltpu.CompilerParams(dimension_semantics=sem, vmem_limit_bytes=VMEM_LIMIT)


def _rms(x, g):
    ms = jnp.mean(x * x, axis=-1, keepdims=True)
    return x * lax.rsqrt(ms + NORM_EPS) * g


def _split2(x):
    hi = x.astype(BF16)
    lo = (x - hi.astype(F32)).astype(BF16)
    return hi, lo


def _split3(x):
    hi = x.astype(BF16)
    r = x - hi.astype(F32)
    mid = r.astype(BF16)
    lo = (r - mid.astype(F32)).astype(BF16)
    return hi, mid, lo


def _head_rms_tile(x, seg, g):
    hi, lo = _split2(x * x)
    ss = jnp.dot(hi, seg, preferred_element_type=F32) + jnp.dot(lo, seg, preferred_element_type=F32)
    return x * lax.rsqrt(ss * (1.0 / HEAD_DIM) + NORM_EPS) * g


def _silu(g):
    return g / (1.0 + jnp.exp(-g))


def _l0_kernel(*refs, ns, ts, chunk, has_cache, base_pos, nblk):
    if has_cache:
        (x_ref, pp_ref, kp_ref, vp_ref, g0_ref, win_ref, wpool_ref, pscale_ref, qg_ref, kg_ref, seg_ref,
         slope_ref, sink_ref, wout_ref, y_ref, ps_ref, ks_ref, vs_ref, uext, kext, vext, mix) = refs
    else:
        (x_ref, g0_ref, win_ref, wpool_ref, pscale_ref, qg_ref, kg_ref, seg_ref,
         slope_ref, sink_ref, wout_ref, y_ref, ps_ref, ks_ref, vs_ref, uext, kext, vext, mix) = refs
    padk = SWA_KEYS - WINDOW - chunk
    hist = padk + WINDOW
    i = pl.program_id(0)

    x = x_ref[...]
    h = _rms(x, g0_ref[...]).astype(BF16)
    proj = jnp.dot(h, win_ref[...], preferred_element_type=F32)
    u = proj[:, 0:C_POOL]
    gate = proj[:, 1280:2304]
    seg = seg_ref[...]
    qn = [_head_rms_tile(proj[:, 512 + LANES * j:512 + LANES * (j + 1)], seg,
                         qg_ref[:, LANES * j:LANES * (j + 1)]) * (ATTN_SCALE * LOG2E) for j in range(4)]
    kn = _head_rms_tile(proj[:, 1024:1152], seg, kg_ref[...])
    v = proj[:, 1152:1280]

    if has_cache:
        for s in range(ns):
            uext[s, 0:16, :] = pp_ref[s]
            kext[s, 0:padk, :] = jnp.zeros((padk, LANES), F32)
            vext[s, 0:padk, :] = jnp.zeros((padk, LANES), F32)
            kext[s, padk:hist, :] = kp_ref[s]
            vext[s, padk:hist, :] = vp_ref[s]
    else:
        @pl.when(i == 0)
        def _():
            uext[0, 0:16, :] = jnp.zeros((16, C_POOL), F32)
            kext[0, 0:hist, :] = jnp.zeros((hist, LANES), F32)
            vext[0, 0:hist, :] = jnp.zeros((hist, LANES), F32)

        @pl.when(i > 0)
        def _():
            uext[0, 0:16, :] = uext[0, ts:ts + 16, :]
            kext[0, 0:hist, :] = kext[0, ts:ts + hist, :]
            vext[0, 0:hist, :] = vext[0, ts:ts + hist, :]

    for s in range(ns):
        uext[s, 16:16 + ts, :] = u[s * ts:(s + 1) * ts]
        kext[s, hist:hist + ts, :] = kn[s * ts:(s + 1) * ts]
        vext[s, hist:hist + ts, :] = v[s * ts:(s + 1) * ts]

    pos = base_pos + i * ts + lax.broadcasted_iota(jnp.int32, (ts, LANES), 0)
    for s in range(ns):
        for g, w in enumerate(POOL_WINDOWS):
            cols = slice(LANES * g, LANES * (g + 1))
            acc = uext[s, 16:16 + ts, cols]
            cur = acc
            for j in range(1, w):
                acc = acc + uext[s, 16 - j:16 - j + ts, cols]
            cnt = jnp.minimum(pos + 1, w).astype(F32)
            mix[s * ts:(s + 1) * ts, cols] = acc / cnt - cur
    for g in range(4):
        cols = slice(LANES * g, LANES * (g + 1))
        d = mix[:, cols].astype(BF16)
        mix[:, cols] = jnp.dot(d, wpool_ref[g], preferred_element_type=F32) * pscale_ref[:, cols]

    rows4 = SWA_REP * chunk
    lo_c = lax.broadcasted_iota(jnp.int32, (chunk, LANES), 1) < HEAD_DIM
    lo_k = lax.broadcasted_iota(jnp.int32, (SWA_KEYS, LANES), 1) < HEAD_DIM
    kj = lax.broadcasted_iota(jnp.int32, (SWA_KEYS, rows4), 0)
    qi = lax.broadcasted_iota(jnp.int32, (SWA_KEYS, rows4), 1) % chunk
    absrel = jnp.abs(qi + WINDOW - (kj - padk)).astype(F32)
    bias = []
    sinkc = []
    for g in range(SWA_KV_HEADS):
        bias.append(jnp.where(kj >= padk, -(slope_ref[g][0:1, :] * LOG2E) * absrel, NEG_INF))
        sinkc.append(sink_ref[g][0:1, :] * LOG2E)
    nch = ts // chunk
    units = [(s, c, g) for s in range(ns) for c in range(nch) for g in range(SWA_KV_HEADS)]

    def swa_scores(s, c, g):
        r0 = c * chunk
        rows = slice(s * ts + r0, s * ts + r0 + chunk)
        kwin = kext[s, r0:r0 + SWA_KEYS, :]
        vwin = vext[s, r0:r0 + SWA_KEYS, :]
        krl = pltpu.roll(kwin, HEAD_DIM, 1)
        vrl = pltpu.roll(vwin, HEAD_DIM, 1)
        if g == 0:
            kd = jnp.where(lo_k, kwin, krl).astype(BF16)
            vd = jnp.where(lo_k, vwin, vrl).astype(BF16)
        else:
            kd = jnp.where(lo_k, krl, kwin).astype(BF16)
            vd = jnp.where(lo_k, vrl, vwin).astype(BF16)
        q0 = qn[2 * g][rows]
        q1 = qn[2 * g + 1][rows]
        qs = jnp.concatenate([jnp.where(lo_c, q0, 0.0), jnp.where(lo_c, 0.0, q0),
                              jnp.where(lo_c, q1, 0.0), jnp.where(lo_c, 0.0, q1)], axis=0).astype(BF16)
        return lax.dot_general(kd, qs, (((1,), (1,)), ((), ())), preferred_element_type=F32), vd

    def swa_finish(s, c, g, sc, vd):
        rows = slice(s * ts + c * chunk, s * ts + (c + 1) * chunk)
        sc = sc + bias[g]
        if not has_cache and c * chunk < WINDOW:
            sc = jnp.where(kj >= padk + jnp.maximum(0, WINDOW - (i * nch + c) * chunk), sc, NEG_INF)
        m = jnp.maximum(jnp.max(sc, axis=0, keepdims=True), sinkc[g])
        p = jnp.exp2(sc - m)
        den = jnp.sum(p, axis=0, keepdims=True) + jnp.exp2(sinkc[g] - m)
        wgt = (p / den).astype(BF16)
        o = lax.dot_general(wgt, vd, (((0,), (0,)), ((), ())), preferred_element_type=F32)
        for jj in range(2):
            t = 2 * g + jj
            mix[rows, C_POOL + LANES * t:C_POOL + LANES * (t + 1)] = jnp.where(
                lo_c, o[(2 * jj) * chunk:(2 * jj + 1) * chunk], o[(2 * jj + 1) * chunk:(2 * jj + 2) * chunk])

    pending = swa_scores(*units[0])
    for n, unit in enumerate(units):
        nxt = swa_scores(*units[n + 1]) if n + 1 < len(units) else None
        swa_finish(*unit, *pending)
        pending = nxt

    mixed = (mix[...] * _silu(gate)).astype(BF16)
    y_ref[...] = x + jnp.dot(mixed, wout_ref[...], preferred_element_type=F32)

    @pl.when(i == nblk - 1)
    def _():
        for s in range(ns):
            ps_ref[s] = uext[s, ts:ts + 16, :]
            ks_ref[s] = kext[s, padk + ts:padk + ts + WINDOW, :]
            vs_ref[s] = vext[s, padk + ts:padk + ts + WINDOW, :]


def _const_spec(shape):
    nd = len(shape)
    return pl.BlockSpec(shape, lambda i, _nd=nd: (0,) * _nd)


def _layer0(x2d, prefix, wts, *, ns, ts, chunk, base_pos):
    n = x2d.shape[0]
    tb = ns * ts
    nblk = n // tb
    has_cache = prefix is not None
    padk = SWA_KEYS - WINDOW - chunk
    rows4 = SWA_REP * chunk
    slope = np.repeat(2.0 ** (-(np.arange(SWA_HEADS) + 1.0)), chunk).reshape(SWA_KV_HEADS, 1, rows4)
    slope = jnp.asarray(np.broadcast_to(slope, (SWA_KV_HEADS, 8, rows4)).astype(np.float32))
    sink = jnp.broadcast_to(jnp.repeat(wts["sinks"].astype(F32), chunk).reshape(SWA_KV_HEADS, 1, rows4),
                            (SWA_KV_HEADS, 8, rows4))
    consts = [wts["g0"], wts["w_in0"], wts["w_pool"], wts["pool_scale"], wts["qg0"], wts["kg0"], wts["seg"],
              slope, sink, wts["w_out0"]]
    in_specs = [pl.BlockSpec((tb, D_MODEL), lambda i: (i, 0))]
    args = [x2d]
    if has_cache:
        for a in prefix:
            in_specs.append(_const_spec(a.shape))
            args.append(a)
    for a in consts:
        in_specs.append(_const_spec(a.shape))
        args.append(a)
    out_shape = [jax.ShapeDtypeStruct((n, D_MODEL), F32),
                 jax.ShapeDtypeStruct((ns, 16, C_POOL), F32),
                 jax.ShapeDtypeStruct((ns, WINDOW, LANES), F32),
                 jax.ShapeDtypeStruct((ns, WINDOW, LANES), F32)]
    out_specs = [pl.BlockSpec((tb, D_MODEL), lambda i: (i, 0)),
                 _const_spec((ns, 16, C_POOL)), _const_spec((ns, WINDOW, LANES)), _const_spec((ns, WINDOW, LANES))]
    scratch = [pltpu.VMEM((ns, 16 + ts, C_POOL), F32),
               pltpu.VMEM((ns, padk + WINDOW + ts, LANES), F32),
               pltpu.VMEM((ns, padk + WINDOW + ts, LANES), F32),
               pltpu.VMEM((tb, D_MODEL), F32)]
    kern = functools.partial(_l0_kernel, ns=ns, ts=ts, chunk=chunk, has_cache=has_cache,
                             base_pos=base_pos, nblk=nblk)
    return pl.pallas_call(
        kern, grid=(nblk,), in_specs=in_specs, out_specs=out_specs, out_shape=out_shape,
        scratch_shapes=scratch, compiler_params=_cparams(("arbitrary",)),
        name="layer0_prompt" if not has_cache else "layer0_sample")(*args)


def _log_sigmoid(z):
    return jnp.minimum(z, 0.0) - jnp.log1p(jnp.exp(-jnp.abs(z)))


def _fox_proj_sample_kernel(x_ref, g1_ref, w_ref, wf_ref, bf_ref, qg_ref, kg_ref, seg_ref,
                            k_ref, v_ref, lf_ref, q_ref):
    x = x_ref[...]
    h = _rms(x, g1_ref[...]).astype(BF16)
    z = jnp.dot(h, wf_ref[...], preferred_element_type=F32) + bf_ref[...]
    lf_ref[...] = _log_sigmoid(z)[:, 0:FOX_HEADS]
    seg = seg_ref[...]
    mc = D_MODEL
    q = jnp.dot(h, w_ref[:, 0:mc], preferred_element_type=F32)
    k = jnp.dot(h, w_ref[:, mc:2 * mc], preferred_element_type=F32)
    v_ref[...] = jnp.dot(h, w_ref[:, 2 * mc:3 * mc], preferred_element_type=F32)
    for j in range(FOX_HEADS // 2):
        cols = slice(LANES * j, LANES * (j + 1))
        q_ref[:, cols] = _head_rms_tile(q[:, cols], seg, qg_ref[:, cols]) * ATTN_SCALE
        k_ref[:, cols] = _head_rms_tile(k[:, cols], seg, kg_ref[:, cols])


def _fox_proj_sample(x2d, wts, *, tb):
    n = x2d.shape[0]
    consts = [wts["g1"], wts["w_qkv1"], wts["w_f"], wts["b_f"], wts["qg1"], wts["kg1"], wts["seg"]]
    row_spec = pl.BlockSpec((tb, D_MODEL), lambda i: (i, 0))
    in_specs = [row_spec] + [_const_spec(a.shape) for a in consts]
    row = jax.ShapeDtypeStruct((n, D_MODEL), F32)
    out_shape = [row, row, jax.ShapeDtypeStruct((n, FOX_HEADS), F32), row]
    out_specs = [row_spec, row_spec, pl.BlockSpec((tb, FOX_HEADS), lambda i: (i, 0)), row_spec]
    return pl.pallas_call(
        _fox_proj_sample_kernel, grid=(n // tb,), in_specs=in_specs, out_specs=out_specs, out_shape=out_shape,
        compiler_params=_cparams(("arbitrary",)), name="fox_proj_sample")(x2d, *consts)


def _fox_proj_prompt_kernel(x_ref, g1_ref, wk_ref, wf_ref, bf_ref, wqt_ref, wvt_ref, gq_ref, kg_ref,
                            seg_ref, tri_ref, e_ref,
                            k_ref, vt_ref, lf_ref, qa_ref, ka_ref, va_ref, ce_ref, carry, *, tb):
    i = pl.program_id(0)
    x = x_ref[...]
    hf = _rms(x, g1_ref[...])
    h = hf.astype(BF16)
    ht = hf.T.astype(BF16)
    logf = _log_sigmoid(jnp.dot(h, wf_ref[...], preferred_element_type=F32) + bf_ref[...])
    lf_ref[...] = logf[:, 0:FOX_HEADS]
    k = jnp.dot(h, wk_ref[...], preferred_element_type=F32)

    @pl.when(i == 0)
    def _():
        carry[...] = jnp.zeros(carry.shape, F32)

    tri = tri_ref[...]
    hi, mid, lo = _split3(logf)
    cum = (jnp.dot(tri, hi, preferred_element_type=F32) + jnp.dot(tri, mid, preferred_element_type=F32)
           + jnp.dot(tri, lo, preferred_element_type=F32)) + carry[0:1, :]
    carry[0:1, :] = cum[tb - 1:tb, :]
    ce_ref[0] = cum[tb - 1:tb, :]
    nh, nm, nl = _split3(-LOG2E * cum)
    biasall = jnp.dot(jnp.concatenate([nh, nm, nl], axis=1), e_ref[...], preferred_element_type=F32)
    lo_l = lax.broadcasted_iota(jnp.int32, (tb, LANES), 1) < HEAD_DIM
    seg = seg_ref[...]
    for j in range(FOX_HEADS // 2):
        cols = slice(LANES * j, LANES * (j + 1))
        kt = _head_rms_tile(k[:, cols], seg, kg_ref[:, cols])
        k_ref[:, cols] = kt
        bt = biasall[:, cols]
        ka_ref[2 * j] = jnp.where(lo_l, kt, bt).astype(BF16)
        ka_ref[2 * j + 1] = pltpu.roll(jnp.where(lo_l, bt, kt), HEAD_DIM, 1).astype(BF16)

    qt = jnp.dot(wqt_ref[...], ht, preferred_element_type=F32)
    vt = jnp.dot(wvt_ref[...], ht, preferred_element_type=F32)
    vt_ref[...] = vt
    srow = lax.broadcasted_iota(jnp.int32, (HEAD_DIM, tb), 0)
    q_aug = jnp.where(srow < 3, 1.0, 0.0)
    v_aug = jnp.where(srow == 0, 1.0, 0.0)
    gq = jnp.concatenate([gq_ref[...]] * (tb // LANES), axis=1) * (ATTN_SCALE * LOG2E)
    for hd in range(FOX_HEADS):
        rows = slice(HEAD_DIM * hd, HEAD_DIM * (hd + 1))
        qh = qt[rows]
        ss = jnp.sum(qh * qh, axis=0, keepdims=True)
        qn = qh * lax.rsqrt(ss * (1.0 / HEAD_DIM) + NORM_EPS) * gq
        qa_ref[hd] = jnp.concatenate([qn, q_aug], axis=0).astype(BF16)
        va_ref[hd] = jnp.concatenate([vt[rows], v_aug], axis=0).astype(BF16)


def _fox_proj_prompt(x2d, wts, *, tb):
    n = x2d.shape[0]
    tri = jnp.asarray(np.tril(np.ones((tb, tb), np.float32)), BF16)
    e = np.zeros((3 * LANES, FOX_HEADS // 2 * LANES), np.float32)
    for part in range(3):
        for hd in range(FOX_HEADS):
            e[part * LANES + hd, (hd // 2) * LANES + (HEAD_DIM if hd % 2 == 0 else 0) + part] = 1.0
    consts = [wts["g1"], wts["w_k1"], wts["w_f"], wts["b_f"], wts["w_q1t"], wts["w_v1t"],
              wts["gq1c"], wts["kg1"], wts["seg"], tri, jnp.asarray(e, BF16)]
    row_spec = pl.BlockSpec((tb, D_MODEL), lambda i: (i, 0))
    in_specs = [row_spec] + [_const_spec(a.shape) for a in consts]
    row = jax.ShapeDtypeStruct((n, D_MODEL), F32)
    out_shape = [row, jax.ShapeDtypeStruct((D_MODEL, n), F32), jax.ShapeDtypeStruct((n, FOX_HEADS), F32),
                 jax.ShapeDtypeStruct((FOX_HEADS, LANES, n), BF16), jax.ShapeDtypeStruct((FOX_HEADS, n, LANES), BF16),
                 jax.ShapeDtypeStruct((FOX_HEADS, LANES, n), BF16),
                 jax.ShapeDtypeStruct((n // tb, 1, LANES), F32)]
    tspec = pl.BlockSpec((FOX_HEADS, LANES, tb), lambda i: (0, 0, i))
    out_specs = [row_spec, pl.BlockSpec((D_MODEL, tb), lambda i: (0, i)),
                 pl.BlockSpec((tb, FOX_HEADS), lambda i: (i, 0)),
                 tspec, pl.BlockSpec((FOX_HEADS, tb, LANES), lambda i: (0, i, 0)), tspec,
                 pl.BlockSpec((1, 1, LANES), lambda i: (i, 0, 0))]
    return pl.pallas_call(
        functools.partial(_fox_proj_prompt_kernel, tb=tb), grid=(n // tb,), in_specs=in_specs,
        out_specs=out_specs, out_shape=out_shape, scratch_shapes=[pltpu.VMEM((8, LANES), F32)],
        compiler_params=_cparams(("arbitrary",)), name="fox_proj_prompt")(x2d, *consts)


def _fox_attn_kernel(ce_ref, qk_ref, q_ref, k_ref, v_ref, o_ref, m_scr, al_scr, acc_scr, s_scr,
                     *, tq, tk, unroll, nkb, nh):
    pair = pl.program_id(0)
    qi = pl.program_id(1)
    r = tq // tk
    n_u = r * qi
    m_scr[...] = jnp.full(m_scr.shape, NEG_INF, F32)
    acc_scr[...] = jnp.zeros(acc_scr.shape, F32)

    last = jnp.maximum(n_u - 1, 0)
    slack = 2.0 * qk_ref[0]

    def dead(j):
        jj = jnp.minimum(j, last)
        ok = j < n_u - 1
        for hh in range(nh):
            hd = nh * pair + hh
            gap = LOG2E * (ce_ref[last * FOX_HEADS + hd] - ce_ref[jj * FOX_HEADS + hd])
            ok = ok & (slack + gap < EXP2_ZERO)
        return ok

    lo = jnp.int32(0)
    hi = last
    for _ in range(max(1, (nkb - 1).bit_length())):
        mid = (lo + hi) >> 1
        d = dead(mid)
        lo = jnp.where(d, mid + 1, lo)
        hi = jnp.where(d, hi, mid)
    first = lo

    def score_matmuls(kb):
        off = pl.multiple_of(kb * tk, tk)
        return [jnp.dot(k_ref[hh, pl.ds(off, tk), :], q_ref[hh], preferred_element_type=F32)
                for hh in range(nh)]

    def score_finish(sts, diag):
        for hh in range(nh):
            st = sts[hh]
            if diag is not None:
                krow = lax.broadcasted_iota(jnp.int32, (tk, tq), 0) + diag * tk
                qcol = lax.broadcasted_iota(jnp.int32, (tk, tq), 1)
                st = jnp.where(krow <= qcol, st, NEG_INF)
            m_old = m_scr[hh, 0:1, :]
            m_new = jnp.maximum(m_old, jnp.max(st, axis=0, keepdims=True))
            al_scr[hh, 0:1, :] = jnp.exp2(m_old - m_new)
            m_scr[hh, 0:1, :] = m_new
            s_scr[hh] = st

    def accumulate(kb):
        off = pl.multiple_of(kb * tk, tk)
        for hh in range(nh):
            p = jnp.exp2(s_scr[hh] - m_scr[hh, 0:1, :]).astype(BF16)
            pv = jnp.dot(v_ref[hh, :, pl.ds(off, tk)], p, preferred_element_type=F32)
            acc_scr[hh] = al_scr[hh, 0:1, :] * acc_scr[hh] + pv

    def advance(kb, diag=None):
        sts = score_matmuls(kb + 1)
        accumulate(kb)
        score_finish(sts, diag)

    @pl.when(qi == 0)
    def _():
        score_finish(score_matmuls(0), 0)

    @pl.when(qi > 0)
    def _():
        score_finish(score_matmuls(first), None)
        n_adv = n_u - 1 - first

        def body(t, carry):
            for u in range(unroll):
                advance(first + unroll * t + u)
            return carry

        lax.fori_loop(0, n_adv // unroll, body, 0)
        done = first + (n_adv // unroll) * unroll
        for u in range(unroll - 1):
            @pl.when(n_u - 1 - done > u)
            def _():
                advance(done + u)

        advance(n_u - 1, 0)

    for d in range(1, r):
        advance(n_u + d - 1, d)
    accumulate(n_u + r - 1)
    lane = lax.broadcasted_iota(jnp.int32, (tq, LANES), 1)
    outs = []
    for hh in range(nh):
        a = acc_scr[hh]
        outs.append((a / a[HEAD_DIM:HEAD_DIM + 1, :]).T)
    for t in range(nh // 2):
        o_ref[:, LANES * t:LANES * (t + 1)] = jnp.where(
            lane < HEAD_DIM, outs[2 * t], pltpu.roll(outs[2 * t + 1], HEAD_DIM, 1)).astype(o_ref.dtype)


def _fox_attn_prompt(cum_end, qk_bound, qa, ka, va, *, tq, tk, unroll, nh):
    n = ka.shape[1]
    resident = dict(pipeline_mode=pl.Buffered(1)) if nh > 2 else {}
    grid_spec = pltpu.PrefetchScalarGridSpec(
        num_scalar_prefetch=2, grid=(FOX_HEADS // nh, n // tq),
        in_specs=[pl.BlockSpec((nh, LANES, tq), lambda p, i, ce, qk: (p, 0, i)),
                  pl.BlockSpec((nh, n, LANES), lambda p, i, ce, qk: (p, 0, 0), **resident),
                  pl.BlockSpec((nh, LANES, n), lambda p, i, ce, qk: (p, 0, 0), **resident)],
        out_specs=pl.BlockSpec((tq, HEAD_DIM * nh), lambda p, i, ce, qk: (i, p)),
        scratch_shapes=[pltpu.VMEM((nh, 8, tq), F32), pltpu.VMEM((nh, 8, tq), F32),
                        pltpu.VMEM((nh, LANES, tq), F32), pltpu.VMEM((nh, tk, tq), F32)])
    return pl.pallas_call(
        functools.partial(_fox_attn_kernel, tq=tq, tk=tk, unroll=unroll, nkb=n // tk, nh=nh),
        grid_spec=grid_spec,
        out_shape=jax.ShapeDtypeStruct((n, D_MODEL), BF16),
        compiler_params=_cparams(("arbitrary", "arbitrary")), name="fox_attn_prompt")(
            cum_end, qk_bound, qa, ka, va)


CUM_BLK = 512
NEW_PAD = 128


def _fox_sample_kernel(q_ref, kc_ref, vc_ref, lfc_ref, kn_ref, vn_ref, lfn_ref, tri_ref, o_ref, negc, negn,
                       *, t_new, p_len):
    qd = pl.program_id(1)
    rows = 4 * t_new

    @pl.when(qd == 0)
    def _():
        tri = tri_ref[...]

        def prefix_sums(x, t):
            c3 = jnp.dot(jnp.concatenate(_split3(x), axis=0), t, preferred_element_type=F32)
            return c3[0:FOX_HEADS] + c3[FOX_HEADS:2 * FOX_HEADS] + c3[2 * FOX_HEADS:3 * FOX_HEADS]

        carry = jnp.zeros((FOX_HEADS, 1), F32)
        for b in range(p_len // CUM_BLK):
            c = prefix_sums(lfc_ref[0, :, b * CUM_BLK:(b + 1) * CUM_BLK], tri) + carry
            negc[:, b * CUM_BLK:(b + 1) * CUM_BLK] = -c
            carry = c[:, CUM_BLK - 1:CUM_BLK]
        negn[...] = -(prefix_sums(lfn_ref[0], tri[0:NEW_PAD, 0:NEW_PAD]) + carry)

    qq = q_ref[0]
    rhead = lax.broadcasted_iota(jnp.int32, (rows, 2 * LANES), 0) // t_new
    lhead = lax.broadcasted_iota(jnp.int32, (rows, 2 * LANES), 1) // HEAD_DIM
    qbd = jnp.where(rhead == lhead, jnp.concatenate([qq] * 4, axis=0), 0.0).astype(BF16)

    kct = kc_ref[0].reshape(2 * LANES, p_len).astype(BF16)
    vct = vc_ref[0].reshape(2 * LANES, p_len).astype(BF16)
    s_c = jnp.dot(qbd, kct, preferred_element_type=F32)
    s_n = lax.dot_general(qbd, kn_ref[0].astype(BF16), (((1,), (1,)), ((), ())), preferred_element_type=F32)
    bc = jnp.concatenate([jnp.broadcast_to(negc[pl.ds(4 * qd + jh, 1), :], (t_new, p_len)) for jh in range(4)], axis=0)
    bn = jnp.concatenate([jnp.broadcast_to(negn[pl.ds(4 * qd + jh, 1), :], (t_new, NEW_PAD)) for jh in range(4)], axis=0)
    s_c = s_c + bc
    qrow = lax.broadcasted_iota(jnp.int32, (rows, NEW_PAD), 0) % t_new
    kcol = lax.broadcasted_iota(jnp.int32, (rows, NEW_PAD), 1)
    s_n = jnp.where(kcol <= qrow, s_n + bn, NEG_INF)
    m = jnp.maximum(jnp.max(s_c, axis=-1, keepdims=True), jnp.max(s_n, axis=-1, keepdims=True))
    p_c = jnp.exp(s_c - m)
    p_n = jnp.exp(s_n - m)
    den = jnp.sum(p_c, axis=-1, keepdims=True) + jnp.sum(p_n, axis=-1, keepdims=True)
    o = (lax.dot_general(p_c.astype(BF16), vct, (((1,), (1,)), ((), ())), preferred_element_type=F32)
         + jnp.dot(p_n.astype(BF16), vn_ref[0].astype(BF16), preferred_element_type=F32)) / den
    lh = lax.broadcasted_iota(jnp.int32, (t_new, 2 * LANES), 1) // HEAD_DIM
    out = jnp.zeros((t_new, 2 * LANES), F32)
    for jh in range(4):
        out = jnp.where(lh == jh, o[jh * t_new:(jh + 1) * t_new], out)
    o_ref[0] = out


def _fox_attn_sample(q, cache_k, cache_v, lfc_t, k_new, v_new, lfn_t):
    b, t_new, _ = q.shape
    p_len = cache_k.shape[3]
    tri = jnp.asarray(np.triu(np.ones((CUM_BLK, CUM_BLK), np.float32)), BF16)
    quad = 2 * LANES
    in_specs = [pl.BlockSpec((1, t_new, quad), lambda r, d: (r, 0, d)),
                pl.BlockSpec((1, 4, HEAD_DIM, p_len), lambda r, d: (r, d, 0, 0)),
                pl.BlockSpec((1, 4, HEAD_DIM, p_len), lambda r, d: (r, d, 0, 0)),
                pl.BlockSpec((1, FOX_HEADS, p_len), lambda r, d: (r, 0, 0)),
                pl.BlockSpec((1, NEW_PAD, quad), lambda r, d: (r, 0, d)),
                pl.BlockSpec((1, NEW_PAD, quad), lambda r, d: (r, 0, d)),
                pl.BlockSpec((1, FOX_HEADS, NEW_PAD), lambda r, d: (r, 0, 0)),
                pl.BlockSpec((CUM_BLK, CUM_BLK), lambda r, d: (0, 0))]
    return pl.pallas_call(
        functools.partial(_fox_sample_kernel, t_new=t_new, p_len=p_len),
        grid=(b, D_MODEL // quad), in_specs=in_specs,
        out_specs=pl.BlockSpec((1, t_new, quad), lambda r, d: (r, 0, d)),
        out_shape=jax.ShapeDtypeStruct((b, t_new, D_MODEL), F32),
        scratch_shapes=[pltpu.VMEM((FOX_HEADS, p_len), F32), pltpu.VMEM((FOX_HEADS, NEW_PAD), F32)],
        compiler_params=_cparams(("arbitrary", "arbitrary")), name="fox_attn_sample")(
            q, cache_k, cache_v, lfc_t, k_new, v_new, lfn_t, tri)


def _out_proj_kernel(r_ref, a_ref, g1_ref, wg_ref, w_ref, y_ref):
    x = r_ref[...]
    h = _rms(x, g1_ref[...]).astype(BF16)
    gate = jnp.dot(h, wg_ref[...], preferred_element_type=F32)
    mixed = (a_ref[...].astype(F32) * _silu(gate)).astype(BF16)
    y_ref[...] = x + jnp.dot(mixed, w_ref[...], preferred_element_type=F32)


def _out_proj(resid, attn, wts, *, tb):
    n = resid.shape[0]
    row = pl.BlockSpec((tb, D_MODEL), lambda i: (i, 0))
    consts = [wts["g1"], wts["w_g1"], wts["w_out1"]]
    return pl.pallas_call(
        _out_proj_kernel, grid=(n // tb,), in_specs=[row, row] + [_const_spec(a.shape) for a in consts],
        out_specs=row, out_shape=jax.ShapeDtypeStruct((n, D_MODEL), F32),
        compiler_params=_cparams(("arbitrary",)), name="out_proj")(resid, attn, *consts)


L0_TB = 512
FOX_TB = 256
ATTN_TQ = 512
ATTN_TK = 512
ATTN_UNROLL = 4
ATTN_HEADS = 2
SWA_CHUNK = 64


def kernel(x_prompt, x_sample, state_pool, cache_swa_k, cache_swa_v, cache_fox_k, cache_fox_v, cache_fox_logf,
           norm0_g, w_in0, w_pool, pool_scale, swa_qn_g, swa_kn_g, swa_sinks, w_out0,
           norm1_g, w_in1, b_forget, fox_qn_g, fox_kn_g, w_out1):
    nb, seq, _ = x_prompt.shape
    db, dseq, _ = x_sample.shape
    past_len = cache_fox_k.shape[1]
    assert nb == 1 and seq % L0_TB == 0 and seq % ATTN_TQ == 0 and dseq % 8 == 0

    seg = np.kron(np.eye(2, dtype=np.float32), np.ones((HEAD_DIM, HEAD_DIM), np.float32))
    mc = FOX_HEADS * HEAD_DIM
    wts = {
        "g0": norm0_g.reshape(1, D_MODEL), "w_in0": w_in0.astype(BF16), "w_pool": w_pool.astype(BF16),
        "pool_scale": pool_scale.reshape(1, C_POOL),
        "qg0": jnp.tile(swa_qn_g, SWA_HEADS).reshape(1, SWA_HEADS * HEAD_DIM),
        "kg0": jnp.tile(swa_kn_g, SWA_KV_HEADS).reshape(1, LANES),
        "seg": jnp.asarray(seg, BF16), "sinks": swa_sinks, "w_out0": w_out0.astype(BF16),
        "g1": norm1_g.reshape(1, D_MODEL), "w_qkv1": w_in1[:, :3 * mc].astype(BF16),
        "w_out1": w_out1.astype(BF16),
        "w_f": jnp.pad(w_in1[:, 4 * mc:], ((0, 0), (0, LANES - FOX_HEADS))).astype(BF16),
        "b_f": jnp.pad(b_forget.astype(F32), (0, LANES - FOX_HEADS)).reshape(1, LANES),
        "qg1": jnp.tile(fox_qn_g, FOX_HEADS).reshape(1, mc), "kg1": jnp.tile(fox_kn_g, FOX_HEADS).reshape(1, mc),
        "w_k1": w_in1[:, mc:2 * mc].astype(BF16), "w_g1": w_in1[:, 3 * mc:4 * mc].astype(BF16),
        "w_q1t": w_in1[:, 0:mc].T.astype(BF16), "w_v1t": w_in1[:, 2 * mc:3 * mc].T.astype(BF16),
        "gq1c": jnp.broadcast_to(fox_qn_g.astype(F32).reshape(HEAD_DIM, 1), (HEAD_DIM, LANES)),
    }

    xp = x_prompt.reshape(seq, D_MODEL)
    y0p, pool_p, swk_p, swv_p = _layer0(xp, None, wts, ns=1, ts=L0_TB, chunk=SWA_CHUNK, base_pos=0)
    fk_p, fvt_p, fl_p, qa, ka, va, cum_end = _fox_proj_prompt(y0p, wts, tb=FOX_TB)
    fv_p = jnp.transpose(fvt_p.reshape(FOX_HEADS, HEAD_DIM, seq), (2, 0, 1))
    per = ATTN_TK // FOX_TB
    cum_end = cum_end[per - 1::per, 0, :FOX_HEADS].reshape(-1)
    qk_bound = (1.02 * LOG2E * HEAD_DIM * ATTN_SCALE * jnp.max(jnp.abs(fox_qn_g)) * jnp.max(jnp.abs(fox_kn_g))
                ).astype(F32).reshape(1)
    attn_p = _fox_attn_prompt(cum_end, qk_bound, qa, ka, va, tq=ATTN_TQ, tk=ATTN_TK, unroll=ATTN_UNROLL,
                              nh=ATTN_HEADS)
    yp = _out_proj(y0p, attn_p, wts, tb=L0_TB)

    xs = x_sample.reshape(db * dseq, D_MODEL)
    prefix = (jnp.pad(state_pool, ((0, 0), (1, 0), (0, 0))),
              cache_swa_k.reshape(db, WINDOW, LANES), cache_swa_v.reshape(db, WINDOW, LANES))
    y0s, pool_s, swk_s, swv_s = _layer0(xs, prefix, wts, ns=db, ts=dseq, chunk=dseq, base_pos=past_len)
    fk_s, fv_s, fl_s, q_s = _fox_proj_sample(y0s, wts, tb=FOX_TB)
    pad_rows = ((0, 0), (0, NEW_PAD - dseq), (0, 0))
    attn_s = _fox_attn_sample(
        q_s.reshape(db, dseq, D_MODEL),
        jnp.transpose(cache_fox_k, (0, 2, 3, 1)), jnp.transpose(cache_fox_v, (0, 2, 3, 1)),
        jnp.transpose(cache_fox_logf, (0, 2, 1)),
        jnp.pad(fk_s.reshape(db, dseq, D_MODEL), pad_rows), jnp.pad(fv_s.reshape(db, dseq, D_MODEL), pad_rows),
        jnp.pad(jnp.transpose(fl_s.reshape(db, dseq, FOX_HEADS), (0, 2, 1)), ((0, 0), (0, 0), (0, NEW_PAD - dseq))))
    ys = _out_proj(y0s, attn_s.reshape(db * dseq, D_MODEL), wts, tb=db * dseq)

    return (yp.reshape(1, seq, D_MODEL), ys.reshape(db, dseq, D_MODEL),
            pool_p[:, 1:], pool_s[:, 1:],
            swk_p.reshape(1, WINDOW, SWA_KV_HEADS, HEAD_DIM), swv_p.reshape(1, WINDOW, SWA_KV_HEADS, HEAD_DIM),
            swk_s.reshape(db, WINDOW, SWA_KV_HEADS, HEAD_DIM), swv_s.reshape(db, WINDOW, SWA_KV_HEADS, HEAD_DIM),
            fk_p.reshape(1, seq, FOX_HEADS, HEAD_DIM), fv_p.reshape(1, seq, FOX_HEADS, HEAD_DIM),
            fl_p.reshape(1, seq, FOX_HEADS),
            fk_s.reshape(db, dseq, FOX_HEADS, HEAD_DIM), fv_s.reshape(db, dseq, FOX_HEADS, HEAD_DIM),
            fl_s.reshape(db, dseq, FOX_HEADS))
```

```python
import functools

import numpy as np
import jax
import jax.numpy as jnp
from jax import lax
from jax.experimental import pallas as pl
from jax.experimental.pallas import tpu as pltpu

F32 = jnp.float32
BF16 = jnp.bfloat16

D_MODEL = 1024
HEAD_DIM = 64
ATTN_SCALE = HEAD_DIM ** -0.5
POOL_WINDOWS = (2, 4, 8, 16)
C_POOL = 512
POOL_PAD = 15
SWA_HEADS = 8
SWA_KV_HEADS = 2
SWA_REP = 4
WINDOW = 128
FOX_HEADS = 16
NORM_EPS = 1e-6
NEG_INF = -1e30
LOG2E = 1.4426950408889634
EXP2_ZERO = -160.0

LANES = 128
SWA_KEYS = 256
VMEM_LIMIT = 56 * 1024 * 1024


def _cparams(sem):
    return pltpu.CompilerParams(dimension_semantics=sem, vmem_limit_bytes=VMEM_LIMIT)


def _rms(x, g):
    ms = jnp.mean(x * x, axis=-1, keepdims=True)
    return x * lax.rsqrt(ms + NORM_EPS) * g


def _split3(x):
    hi = x.astype(BF16)
    r = x - hi.astype(F32)
    mid = r.astype(BF16)
    lo = (r - mid.astype(F32)).astype(BF16)
    return hi, mid, lo


def _head_rms_tile(x, seg, g):
    ss = jnp.dot((x * x).astype(BF16), seg, preferred_element_type=F32)
    return x * lax.rsqrt(ss * (1.0 / HEAD_DIM) + NORM_EPS) * g


def _silu(g):
    return g / (1.0 + jnp.exp(-g))


def _l0_kernel(*refs, ns, ts, chunk, has_cache, base_pos, nblk):
    if has_cache:
        (x_ref, pp_ref, kp_ref, vp_ref, g0_ref, win_ref, wpool_ref, pscale_ref, qg_ref, kg_ref, seg_ref,
         slope_ref, sink_ref, wout_ref, y_ref, ps_ref, ks_ref, vs_ref, uext, kext, vext, mix) = refs
    else:
        (x_ref, g0_ref, win_ref, wpool_ref, pscale_ref, qg_ref, kg_ref, seg_ref,
         slope_ref, sink_ref, wout_ref, y_ref, ps_ref, ks_ref, vs_ref, uext, kext, vext, mix) = refs
    padk = SWA_KEYS - WINDOW - chunk
    hist = padk + WINDOW
    i = pl.program_id(0)

    x = x_ref[...]
    h = _rms(x, g0_ref[...]).astype(BF16)
    proj = jnp.dot(h, win_ref[...], preferred_element_type=F32)
    u = proj[:, 0:C_POOL]
    gate = proj[:, 1280:2304]
    seg = seg_ref[...]
    qn = [_head_rms_tile(proj[:, 512 + LANES * j:512 + LANES * (j + 1)], seg,
                         qg_ref[:, LANES * j:LANES * (j + 1)]) * (ATTN_SCALE * LOG2E) for j in range(4)]
    kn = _head_rms_tile(proj[:, 1024:1152], seg, kg_ref[...])
    v = proj[:, 1152:1280]

    if has_cache:
        for s in range(ns):
            uext[s, 0:16, :] = pp_ref[s]
            kext[s, 0:padk, :] = jnp.zeros((padk, LANES), F32)
            vext[s, 0:padk, :] = jnp.zeros((padk, LANES), F32)
            kext[s, padk:hist, :] = kp_ref[s]
            vext[s, padk:hist, :] = vp_ref[s]
    else:
        @pl.when(i == 0)
        def _():
            uext[0, 0:16, :] = jnp.zeros((16, C_POOL), F32)
            kext[0, 0:hist, :] = jnp.zeros((hist, LANES), F32)
            vext[0, 0:hist, :] = jnp.zeros((hist, LANES), F32)

        @pl.when(i > 0)
        def _():
            uext[0, 0:16, :] = uext[0, ts:ts + 16, :]
            kext[0, 0:hist, :] = kext[0, ts:ts + hist, :]
            vext[0, 0:hist, :] = vext[0, ts:ts + hist, :]

    for s in range(ns):
        uext[s, 16:16 + ts, :] = u[s * ts:(s + 1) * ts]
        kext[s, hist:hist + ts, :] = kn[s * ts:(s + 1) * ts]
        vext[s, hist:hist + ts, :] = v[s * ts:(s + 1) * ts]

    pos = base_pos + i * ts + lax.broadcasted_iota(jnp.int32, (ts, LANES), 0)
    for s in range(ns):
        for g, w in enumerate(POOL_WINDOWS):
            cols = slice(LANES * g, LANES * (g + 1))
            acc = uext[s, 16:16 + ts, cols]
            cur = acc
            for j in range(1, w):
                acc = acc + uext[s, 16 - j:16 - j + ts, cols]
            cnt = jnp.minimum(pos + 1, w).astype(F32)
            mix[s * ts:(s + 1) * ts, cols] = acc / cnt - cur
    for g in range(4):
        cols = slice(LANES * g, LANES * (g + 1))
        d = mix[:, cols].astype(BF16)
        mix[:, cols] = jnp.dot(d, wpool_ref[g], preferred_element_type=F32) * pscale_ref[:, cols]

    rows4 = SWA_REP * chunk
    lo_c = lax.broadcasted_iota(jnp.int32, (chunk, LANES), 1) < HEAD_DIM
    lo_k = lax.broadcasted_iota(jnp.int32, (SWA_KEYS, LANES), 1) < HEAD_DIM
    kj = lax.broadcasted_iota(jnp.int32, (SWA_KEYS, rows4), 0)
    qi = lax.broadcasted_iota(jnp.int32, (SWA_KEYS, rows4), 1) % chunk
    absrel = jnp.abs(qi + WINDOW - (kj - padk)).astype(F32)
    bias = []
    sinkc = []
    for g in range(SWA_KV_HEADS):
        bias.append(jnp.where(kj >= padk, -(slope_ref[g][0:1, :] * LOG2E) * absrel, NEG_INF))
        sinkc.append(sink_ref[g][0:1, :] * LOG2E)
    nch = ts // chunk
    units = [(s, c, g) for s in range(ns) for c in range(nch) for g in range(SWA_KV_HEADS)]

    def swa_scores(s, c, g):
        r0 = c * chunk
        rows = slice(s * ts + r0, s * ts + r0 + chunk)
        kwin = kext[s, r0:r0 + SWA_KEYS, :]
        vwin = vext[s, r0:r0 + SWA_KEYS, :]
        krl = pltpu.roll(kwin, HEAD_DIM, 1)
        vrl = pltpu.roll(vwin, HEAD_DIM, 1)
        if g == 0:
            kd = jnp.where(lo_k, kwin, krl).astype(BF16)
            vd = jnp.where(lo_k, vwin, vrl).astype(BF16)
        else:
            kd = jnp.where(lo_k, krl, kwin).astype(BF16)
            vd = jnp.where(lo_k, vrl, vwin).astype(BF16)
        q0 = qn[2 * g][rows]
        q1 = qn[2 * g + 1][rows]
        qs = jnp.concatenate([jnp.where(lo_c, q0, 0.0), jnp.where(lo_c, 0.0, q0),
                              jnp.where(lo_c, q1, 0.0), jnp.where(lo_c, 0.0, q1)], axis=0).astype(BF16)
        return lax.dot_general(kd, qs, (((1,), (1,)), ((), ())), preferred_element_type=F32), vd

    def swa_finish(s, c, g, sc, vd):
        rows = slice(s * ts + c * chunk, s * ts + (c + 1) * chunk)
        sc = sc + bias[g]
        if not has_cache and c * chunk < WINDOW:
            sc = jnp.where(kj >= padk + jnp.maximum(0, WINDOW - (i * nch + c) * chunk), sc, NEG_INF)
        m = jnp.maximum(jnp.max(sc, axis=0, keepdims=True), sinkc[g])
        p = jnp.exp2(sc - m)
        den = jnp.sum(p, axis=0, keepdims=True) + jnp.exp2(sinkc[g] - m)
        wgt = (p / den).astype(BF16)
        o = lax.dot_general(wgt, vd, (((0,), (0,)), ((), ())), preferred_element_type=F32)
        for jj in range(2):
            t = 2 * g + jj
            mix[rows, C_POOL + LANES * t:C_POOL + LANES * (t + 1)] = jnp.where(
                lo_c, o[(2 * jj) * chunk:(2 * jj + 1) * chunk], o[(2 * jj + 1) * chunk:(2 * jj + 2) * chunk])

    pending = swa_scores(*units[0])
    for n, unit in enumerate(units):
        nxt = swa_scores(*units[n + 1]) if n + 1 < len(units) else None
        swa_finish(*unit, *pending)
        pending = nxt

    mixed = (mix[...] * _silu(gate)).astype(BF16)
    y_ref[...] = x + jnp.dot(mixed, wout_ref[...], preferred_element_type=F32)

    @pl.when(i == nblk - 1)
    def _():
        for s in range(ns):
            ps_ref[s] = uext[s, ts:ts + 16, :]
            ks_ref[s] = kext[s, padk + ts:padk + ts + WINDOW, :]
            vs_ref[s] = vext[s, padk + ts:padk + ts + WINDOW, :]


def _const_spec(shape):
    nd = len(shape)
    return pl.BlockSpec(shape, lambda i, _nd=nd: (0,) * _nd)


def _layer0(x2d, prefix, wts, *, ns, ts, chunk, base_pos):
    n = x2d.shape[0]
    tb = ns * ts
    nblk = n // tb
    has_cache = prefix is not None
    padk = SWA_KEYS - WINDOW - chunk
    rows4 = SWA_REP * chunk
    slope = np.repeat(2.0 ** (-(np.arange(SWA_HEADS) + 1.0)), chunk).reshape(SWA_KV_HEADS, 1, rows4)
    slope = jnp.asarray(np.broadcast_to(slope, (SWA_KV_HEADS, 8, rows4)).astype(np.float32))
    sink = jnp.broadcast_to(jnp.repeat(wts["sinks"].astype(F32), chunk).reshape(SWA_KV_HEADS, 1, rows4),
                            (SWA_KV_HEADS, 8, rows4))
    consts = [wts["g0"], wts["w_in0"], wts["w_pool"], wts["pool_scale"], wts["qg0"], wts["kg0"], wts["seg"],
              slope, sink, wts["w_out0"]]
    in_specs = [pl.BlockSpec((tb, D_MODEL), lambda i: (i, 0))]
    args = [x2d]
    if has_cache:
        for a in prefix:
            in_specs.append(_const_spec(a.shape))
            args.append(a)
    for a in consts:
        in_specs.append(_const_spec(a.shape))
        args.append(a)
    out_shape = [jax.ShapeDtypeStruct((n, D_MODEL), F32),
                 jax.ShapeDtypeStruct((ns, 16, C_POOL), F32),
                 jax.ShapeDtypeStruct((ns, WINDOW, LANES), F32),
                 jax.ShapeDtypeStruct((ns, WINDOW, LANES), F32)]
    out_specs = [pl.BlockSpec((tb, D_MODEL), lambda i: (i, 0)),
                 _const_spec((ns, 16, C_POOL)), _const_spec((ns, WINDOW, LANES)), _const_spec((ns, WINDOW, LANES))]
    scratch = [pltpu.VMEM((ns, 16 + ts, C_POOL), F32),
               pltpu.VMEM((ns, padk + WINDOW + ts, LANES), F32),
               pltpu.VMEM((ns, padk + WINDOW + ts, LANES), F32),
               pltpu.VMEM((tb, D_MODEL), F32)]
    kern = functools.partial(_l0_kernel, ns=ns, ts=ts, chunk=chunk, has_cache=has_cache,
                             base_pos=base_pos, nblk=nblk)
    return pl.pallas_call(
        kern, grid=(nblk,), in_specs=in_specs, out_specs=out_specs, out_shape=out_shape,
        scratch_shapes=scratch, compiler_params=_cparams(("arbitrary",)),
        name="layer0_prompt" if not has_cache else "layer0_sample")(*args)


def _log_sigmoid(z):
    return jnp.minimum(z, 0.0) - jnp.log1p(jnp.exp(-jnp.abs(z)))


def _fox_proj_sample_kernel(x_ref, g1_ref, w_ref, wf_ref, bf_ref, qg_ref, kg_ref, seg_ref,
                            k_ref, v_ref, lf_ref, q_ref):
    x = x_ref[...]
    h = _rms(x, g1_ref[...]).astype(BF16)
    z = jnp.dot(h, wf_ref[...], preferred_element_type=F32) + bf_ref[...]
    lf_ref[...] = _log_sigmoid(z)[:, 0:FOX_HEADS]
    seg = seg_ref[...]
    mc = D_MODEL
    q = jnp.dot(h, w_ref[:, 0:mc], preferred_element_type=F32)
    k = jnp.dot(h, w_ref[:, mc:2 * mc], preferred_element_type=F32)
    v_ref[...] = jnp.dot(h, w_ref[:, 2 * mc:3 * mc], preferred_element_type=F32)
    for j in range(FOX_HEADS // 2):
        cols = slice(LANES * j, LANES * (j + 1))
        q_ref[:, cols] = _head_rms_tile(q[:, cols], seg, qg_ref[:, cols]) * ATTN_SCALE
        k_ref[:, cols] = _head_rms_tile(k[:, cols], seg, kg_ref[:, cols])


def _fox_proj_sample(x2d, wts, *, tb):
    n = x2d.shape[0]
    consts = [wts["g1"], wts["w_qkv1"], wts["w_f"], wts["b_f"], wts["qg1"], wts["kg1"], wts["seg"]]
    row_spec = pl.BlockSpec((tb, D_MODEL), lambda i: (i, 0))
    in_specs = [row_spec] + [_const_spec(a.shape) for a in consts]
    row = jax.ShapeDtypeStruct((n, D_MODEL), F32)
    out_shape = [row, row, jax.ShapeDtypeStruct((n, FOX_HEADS), F32), row]
    out_specs = [row_spec, row_spec, pl.BlockSpec((tb, FOX_HEADS), lambda i: (i, 0)), row_spec]
    return pl.pallas_call(
        _fox_proj_sample_kernel, grid=(n // tb,), in_specs=in_specs, out_specs=out_specs, out_shape=out_shape,
        compiler_params=_cparams(("arbitrary",)), name="fox_proj_sample")(x2d, *consts)


def _fox_proj_prompt_kernel(x_ref, g1_ref, wk_ref, wf_ref, bf_ref, wqt_ref, wvt_ref, gq_ref, kg_ref,
                            seg_ref, tri_ref, e_ref,
                            k_ref, vt_ref, lf_ref, qa_ref, ka_ref, va_ref, ce_ref, carry, *, tb):
    i = pl.program_id(0)
    x = x_ref[...]
    hf = _rms(x, g1_ref[...])
    h = hf.astype(BF16)
    ht = hf.T.astype(BF16)
    logf = _log_sigmoid(jnp.dot(h, wf_ref[...], preferred_element_type=F32) + bf_ref[...])
    lf_ref[...] = logf[:, 0:FOX_HEADS]
    k = jnp.dot(h, wk_ref[...], preferred_element_type=F32)

    @pl.when(i == 0)
    def _():
        carry[...] = jnp.zeros(carry.shape, F32)

    tri = tri_ref[...]
    hi, mid, lo = _split3(logf)
    cum = (jnp.dot(tri, hi, preferred_element_type=F32) + jnp.dot(tri, mid, preferred_element_type=F32)
           + jnp.dot(tri, lo, preferred_element_type=F32)) + carry[0:1, :]
    carry[0:1, :] = cum[tb - 1:tb, :]
    ce_ref[0] = cum[tb - 1:tb, :]
    nh, nm, nl = _split3(-LOG2E * cum)
    biasall = jnp.dot(jnp.concatenate([nh, nm, nl], axis=1), e_ref[...], preferred_element_type=F32)
    lo_l = lax.broadcasted_iota(jnp.int32, (tb, LANES), 1) < HEAD_DIM
    seg = seg_ref[...]
    for j in range(FOX_HEADS // 2):
        cols = slice(LANES * j, LANES * (j + 1))
        kt = _head_rms_tile(k[:, cols], seg, kg_ref[:, cols])
        k_ref[:, cols] = kt
        bt = biasall[:, cols]
        ka_ref[2 * j] = jnp.where(lo_l, kt, bt).astype(BF16)
        ka_ref[2 * j + 1] = pltpu.roll(jnp.where(lo_l, bt, kt), HEAD_DIM, 1).astype(BF16)

    qt = jnp.dot(wqt_ref[...], ht, preferred_element_type=F32)
    vt = jnp.dot(wvt_ref[...], ht, preferred_element_type=F32)
    vt_ref[...] = vt
    srow = lax.broadcasted_iota(jnp.int32, (HEAD_DIM, tb), 0)
    q_aug = jnp.where(srow < 3, 1.0, 0.0)
    v_aug = jnp.where(srow == 0, 1.0, 0.0)
    gq = jnp.concatenate([gq_ref[...]] * (tb // LANES), axis=1) * (ATTN_SCALE * LOG2E)
    for hd in range(FOX_HEADS):
        rows = slice(HEAD_DIM * hd, HEAD_DIM * (hd + 1))
        qh = qt[rows]
        ss = jnp.sum(qh * qh, axis=0, keepdims=True)
        qn = qh * lax.rsqrt(ss * (1.0 / HEAD_DIM) + NORM_EPS) * gq
        qa_ref[hd] = jnp.concatenate([qn, q_aug], axis=0).astype(BF16)
        va_ref[hd] = jnp.concatenate([vt[rows], v_aug], axis=0).astype(BF16)


def _fox_proj_prompt(x2d, wts, *, tb):
    n = x2d.shape[0]
    tri = jnp.asarray(np.tril(np.ones((tb, tb), np.float32)), BF16)
    e = np.zeros((3 * LANES, FOX_HEADS // 2 * LANES), np.float32)
    for part in range(3):
        for hd in range(FOX_HEADS):
            e[part * LANES + hd, (hd // 2) * LANES + (HEAD_DIM if hd % 2 == 0 else 0) + part] = 1.0
    consts = [wts["g1"], wts["w_k1"], wts["w_f"], wts["b_f"], wts["w_q1t"], wts["w_v1t"],
              wts["gq1c"], wts["kg1"], wts["seg"], tri, jnp.asarray(e, BF16)]
    row_spec = pl.BlockSpec((tb, D_MODEL), lambda i: (i, 0))
    in_specs = [row_spec] + [_const_spec(a.shape) for a in consts]
    row = jax.ShapeDtypeStruct((n, D_MODEL), F32)
    out_shape = [row, jax.ShapeDtypeStruct((D_MODEL, n), F32), jax.ShapeDtypeStruct((n, FOX_HEADS), F32),
                 jax.ShapeDtypeStruct((FOX_HEADS, LANES, n), BF16), jax.ShapeDtypeStruct((FOX_HEADS, n, LANES), BF16),
                 jax.ShapeDtypeStruct((FOX_HEADS, LANES, n), BF16),
                 jax.ShapeDtypeStruct((n // tb, 1, LANES), F32)]
    tspec = pl.BlockSpec((FOX_HEADS, LANES, tb), lambda i: (0, 0, i))
    out_specs = [row_spec, pl.BlockSpec((D_MODEL, tb), lambda i: (0, i)),
                 pl.BlockSpec((tb, FOX_HEADS), lambda i: (i, 0)),
                 tspec, pl.BlockSpec((FOX_HEADS, tb, LANES), lambda i: (0, i, 0)), tspec,
                 pl.BlockSpec((1, 1, LANES), lambda i: (i, 0, 0))]
    return pl.pallas_call(
        functools.partial(_fox_proj_prompt_kernel, tb=tb), grid=(n // tb,), in_specs=in_specs,
        out_specs=out_specs, out_shape=out_shape, scratch_shapes=[pltpu.VMEM((8, LANES), F32)],
        compiler_params=_cparams(("arbitrary",)), name="fox_proj_prompt")(x2d, *consts)


def _fox_attn_kernel(ce_ref, qk_ref, q_ref, k_ref, v_ref, o_ref, m_scr, al_scr, acc_scr, s_scr,
                     *, tq, tk, unroll, nkb, nh):
    pair = pl.program_id(0)
    qi = pl.program_id(1)
    r = tq // tk
    n_u = r * qi
    m_scr[...] = jnp.full(m_scr.shape, NEG_INF, F32)
    acc_scr[...] = jnp.zeros(acc_scr.shape, F32)

    last = jnp.maximum(n_u - 1, 0)
    slack = 2.0 * qk_ref[0]

    def dead(j):
        jj = jnp.minimum(j, last)
        ok = j < n_u - 1
        for hh in range(nh):
            hd = nh * pair + hh
            gap = LOG2E * (ce_ref[last * FOX_HEADS + hd] - ce_ref[jj * FOX_HEADS + hd])
            ok = ok & (slack + gap < EXP2_ZERO)
        return ok

    lo = jnp.int32(0)
    hi = last
    for _ in range(max(1, (nkb - 1).bit_length())):
        mid = (lo + hi) >> 1
        d = dead(mid)
        lo = jnp.where(d, mid + 1, lo)
        hi = jnp.where(d, hi, mid)
    first = lo

    def score_matmuls(kb):
        off = pl.multiple_of(kb * tk, tk)
        return [jnp.dot(k_ref[hh, pl.ds(off, tk), :], q_ref[hh], preferred_element_type=F32)
                for hh in range(nh)]

    def score_finish(sts, diag):
        for hh in range(nh):
            st = sts[hh]
            if diag is not None:
                krow = lax.broadcasted_iota(jnp.int32, (tk, tq), 0) + diag * tk
                qcol = lax.broadcasted_iota(jnp.int32, (tk, tq), 1)
                st = jnp.where(krow <= qcol, st, NEG_INF)
            m_old = m_scr[hh, 0:1, :]
            m_new = jnp.maximum(m_old, jnp.max(st, axis=0, keepdims=True))
            al_scr[hh, 0:1, :] = jnp.exp2(m_old - m_new)
            m_scr[hh, 0:1, :] = m_new
            s_scr[hh] = st

    def accumulate(kb):
        off = pl.multiple_of(kb * tk, tk)
        for hh in range(nh):
            p = jnp.exp2(s_scr[hh] - m_scr[hh, 0:1, :]).astype(BF16)
            pv = jnp.dot(v_ref[hh, :, pl.ds(off, tk)], p, preferred_element_type=F32)
            acc_scr[hh] = al_scr[hh, 0:1, :] * acc_scr[hh] + pv

    def advance(kb, diag=None):
        sts = score_matmuls(kb + 1)
        accumulate(kb)
        score_finish(sts, diag)

    def finish():
        for d in range(1, r):
            advance(n_u + d - 1, d)
        accumulate(n_u + r - 1)
        lane = lax.broadcasted_iota(jnp.int32, (tq, LANES), 1)
        outs = []
        for hh in range(nh):
            a = acc_scr[hh]
            outs.append((a / a[HEAD_DIM:HEAD_DIM + 1, :]).T)
        for t in range(nh // 2):
            o_ref[:, LANES * t:LANES * (t + 1)] = jnp.where(
                lane < HEAD_DIM, outs[2 * t], pltpu.roll(outs[2 * t + 1], HEAD_DIM, 1)).astype(o_ref.dtype)

    @pl.when(qi == 0)
    def _():
        score_finish(score_matmuls(0), 0)
        finish()

    @pl.when(qi > 0)
    def _():
        n_adv = n_u - 1 - first
        rem = lax.rem(n_adv, unroll)
        for k in range(unroll):
            @pl.when(rem == k)
            def _():
                score_finish(score_matmuls(first), None)
                for u in range(k):
                    advance(first + u)

        def body(t, carry):
            for u in range(unroll):
                advance(first + rem + unroll * t + u)
            return carry

        lax.fori_loop(0, n_adv // unroll, body, 0)
        advance(n_u - 1, 0)
        finish()


def _fox_attn_prompt(cum_end, qk_bound, qa, ka, va, *, tq, tk, unroll, nh):
    n = ka.shape[1]
    resident = dict(pipeline_mode=pl.Buffered(1)) if nh > 2 else {}
    grid_spec = pltpu.PrefetchScalarGridSpec(
        num_scalar_prefetch=2, grid=(FOX_HEADS // nh, n // tq),
        in_specs=[pl.BlockSpec((nh, LANES, tq), lambda p, i, ce, qk: (p, 0, i)),
                  pl.BlockSpec((nh, n, LANES), lambda p, i, ce, qk: (p, 0, 0), **resident),
                  pl.BlockSpec((nh, LANES, n), lambda p, i, ce, qk: (p, 0, 0), **resident)],
        out_specs=pl.BlockSpec((tq, HEAD_DIM * nh), lambda p, i, ce, qk: (i, p)),
        scratch_shapes=[pltpu.VMEM((nh, 8, tq), F32), pltpu.VMEM((nh, 8, tq), F32),
                        pltpu.VMEM((nh, LANES, tq), F32), pltpu.VMEM((nh, tk, tq), F32)])
    return pl.pallas_call(
        functools.partial(_fox_attn_kernel, tq=tq, tk=tk, unroll=unroll, nkb=n // tk, nh=nh),
        grid_spec=grid_spec,
        out_shape=jax.ShapeDtypeStruct((n, D_MODEL), BF16),
        compiler_params=_cparams(("arbitrary", "arbitrary")), name="fox_attn_prompt")(
            cum_end, qk_bound, qa, ka, va)


CUM_BLK = 512
NEW_PAD = 128


def _fox_sample_kernel(q_ref, kc_ref, vc_ref, lfc_ref, kn_ref, vn_ref, lfn_ref, tri_ref, o_ref, negc, negn,
                       *, t_new, p_len):
    qd = pl.program_id(1)
    rows = 4 * t_new

    @pl.when(qd == 0)
    def _():
        tri = tri_ref[...]

        def prefix_sums(x, t):
            c3 = jnp.dot(jnp.concatenate(_split3(x), axis=0), t, preferred_element_type=F32)
            return c3[0:FOX_HEADS] + c3[FOX_HEADS:2 * FOX_HEADS] + c3[2 * FOX_HEADS:3 * FOX_HEADS]

        carry = jnp.zeros((FOX_HEADS, 1), F32)
        for b in range(p_len // CUM_BLK):
            c = prefix_sums(lfc_ref[0, :, b * CUM_BLK:(b + 1) * CUM_BLK], tri) + carry
            negc[:, b * CUM_BLK:(b + 1) * CUM_BLK] = -c
            carry = c[:, CUM_BLK - 1:CUM_BLK]
        negn[...] = -(prefix_sums(lfn_ref[0], tri[0:NEW_PAD, 0:NEW_PAD]) + carry)

    qq = q_ref[0]
    rhead = lax.broadcasted_iota(jnp.int32, (rows, 2 * LANES), 0) // t_new
    lhead = lax.broadcasted_iota(jnp.int32, (rows, 2 * LANES), 1) // HEAD_DIM
    qbd = jnp.where(rhead == lhead, jnp.concatenate([qq] * 4, axis=0), 0.0).astype(BF16)

    kct = kc_ref[0].reshape(2 * LANES, p_len).astype(BF16)
    vct = vc_ref[0].reshape(2 * LANES, p_len).astype(BF16)
    s_c = jnp.dot(qbd, kct, preferred_element_type=F32)
    s_n = lax.dot_general(qbd, kn_ref[0].astype(BF16), (((1,), (1,)), ((), ())), preferred_element_type=F32)
    bc = jnp.concatenate([jnp.broadcast_to(negc[pl.ds(4 * qd + jh, 1), :], (t_new, p_len)) for jh in range(4)], axis=0)
    bn = jnp.concatenate([jnp.broadcast_to(negn[pl.ds(4 * qd + jh, 1), :], (t_new, NEW_PAD)) for jh in range(4)], axis=0)
    s_c = s_c + bc
    qrow = lax.broadcasted_iota(jnp.int32, (rows, NEW_PAD), 0) % t_new
    kcol = lax.broadcasted_iota(jnp.int32, (rows, NEW_PAD), 1)
    s_n = jnp.where(kcol <= qrow, s_n + bn, NEG_INF)
    m = jnp.maximum(jnp.max(s_c, axis=-1, keepdims=True), jnp.max(s_n, axis=-1, keepdims=True))
    p_c = jnp.exp(s_c - m)
    p_n = jnp.exp(s_n - m)
    den = jnp.sum(p_c, axis=-1, keepdims=True) + jnp.sum(p_n, axis=-1, keepdims=True)
    o = (lax.dot_general(p_c.astype(BF16), vct, (((1,), (1,)), ((), ())), preferred_element_type=F32)
         + jnp.dot(p_n.astype(BF16), vn_ref[0].astype(BF16), preferred_element_type=F32)) / den
    lh = lax.broadcasted_iota(jnp.int32, (t_new, 2 * LANES), 1) // HEAD_DIM
    out = jnp.zeros((t_new, 2 * LANES), F32)
    for jh in range(4):
        out = jnp.where(lh == jh, o[jh * t_new:(jh + 1) * t_new], out)
    o_ref[0] = out


def _fox_attn_sample(q, cache_k, cache_v, lfc_t, k_new, v_new, lfn_t):
    b, t_new, _ = q.shape
    p_len = cache_k.shape[3]
    tri = jnp.asarray(np.triu(np.ones((CUM_BLK, CUM_BLK), np.float32)), BF16)
    quad = 2 * LANES
    in_specs = [pl.BlockSpec((1, t_new, quad), lambda r, d: (r, 0, d)),
                pl.BlockSpec((1, 4, HEAD_DIM, p_len), lambda r, d: (r, d, 0, 0)),
                pl.BlockSpec((1, 4, HEAD_DIM, p_len), lambda r, d: (r, d, 0, 0)),
                pl.BlockSpec((1, FOX_HEADS, p_len), lambda r, d: (r, 0, 0)),
                pl.BlockSpec((1, NEW_PAD, quad), lambda r, d: (r, 0, d)),
                pl.BlockSpec((1, NEW_PAD, quad), lambda r, d: (r, 0, d)),
                pl.BlockSpec((1, FOX_HEADS, NEW_PAD), lambda r, d: (r, 0, 0)),
                pl.BlockSpec((CUM_BLK, CUM_BLK), lambda r, d: (0, 0))]
    return pl.pallas_call(
        functools.partial(_fox_sample_kernel, t_new=t_new, p_len=p_len),
        grid=(b, D_MODEL // quad), in_specs=in_specs,
        out_specs=pl.BlockSpec((1, t_new, quad), lambda r, d: (r, 0, d)),
        out_shape=jax.ShapeDtypeStruct((b, t_new, D_MODEL), F32),
        scratch_shapes=[pltpu.VMEM((FOX_HEADS, p_len), F32), pltpu.VMEM((FOX_HEADS, NEW_PAD), F32)],
        compiler_params=_cparams(("arbitrary", "arbitrary")), name="fox_attn_sample")(
            q, cache_k, cache_v, lfc_t, k_new, v_new, lfn_t, tri)


def _out_proj_kernel(r_ref, a_ref, g1_ref, wg_ref, w_ref, y_ref):
    x = r_ref[...]
    h = _rms(x, g1_ref[...]).astype(BF16)
    gate = jnp.dot(h, wg_ref[...], preferred_element_type=F32)
    mixed = (a_ref[...].astype(F32) * _silu(gate)).astype(BF16)
    y_ref[...] = x + jnp.dot(mixed, w_ref[...], preferred_element_type=F32)


def _out_proj(resid, attn, wts, *, tb):
    n = resid.shape[0]
    row = pl.BlockSpec((tb, D_MODEL), lambda i: (i, 0))
    consts = [wts["g1"], wts["w_g1"], wts["w_out1"]]
    return pl.pallas_call(
        _out_proj_kernel, grid=(n // tb,), in_specs=[row, row] + [_const_spec(a.shape) for a in consts],
        out_specs=row, out_shape=jax.ShapeDtypeStruct((n, D_MODEL), F32),
        compiler_params=_cparams(("arbitrary",)), name="out_proj")(resid, attn, *consts)


L0_TB = 512
FOX_TB = 256
ATTN_TQ = 512
ATTN_TK = 512
ATTN_UNROLL = 4
ATTN_HEADS = 2
SWA_CHUNK = 64


def kernel(x_prompt, x_sample, state_pool, cache_swa_k, cache_swa_v, cache_fox_k, cache_fox_v, cache_fox_logf,
           norm0_g, w_in0, w_pool, pool_scale, swa_qn_g, swa_kn_g, swa_sinks, w_out0,
           norm1_g, w_in1, b_forget, fox_qn_g, fox_kn_g, w_out1):
    nb, seq, _ = x_prompt.shape
    db, dseq, _ = x_sample.shape
    past_len = cache_fox_k.shape[1]
    assert nb == 1 and seq % L0_TB == 0 and seq % ATTN_TQ == 0 and dseq % 8 == 0

    seg = np.kron(np.eye(2, dtype=np.float32), np.ones((HEAD_DIM, HEAD_DIM), np.float32))
    mc = FOX_HEADS * HEAD_DIM
    wts = {
        "g0": norm0_g.reshape(1, D_MODEL), "w_in0": w_in0.astype(BF16), "w_pool": w_pool.astype(BF16),
        "pool_scale": pool_scale.reshape(1, C_POOL),
        "qg0": jnp.tile(swa_qn_g, SWA_HEADS).reshape(1, SWA_HEADS * HEAD_DIM),
        "kg0": jnp.tile(swa_kn_g, SWA_KV_HEADS).reshape(1, LANES),
        "seg": jnp.asarray(seg, BF16), "sinks": swa_sinks, "w_out0": w_out0.astype(BF16),
        "g1": norm1_g.reshape(1, D_MODEL), "w_qkv1": w_in1[:, :3 * mc].astype(BF16),
        "w_out1": w_out1.astype(BF16),
        "w_f": jnp.pad(w_in1[:, 4 * mc:], ((0, 0), (0, LANES - FOX_HEADS))).astype(BF16),
        "b_f": jnp.pad(b_forget.astype(F32), (0, LANES - FOX_HEADS)).reshape(1, LANES),
        "qg1": jnp.tile(fox_qn_g, FOX_HEADS).reshape(1, mc), "kg1": jnp.tile(fox_kn_g, FOX_HEADS).reshape(1, mc),
        "w_k1": w_in1[:, mc:2 * mc].astype(BF16), "w_g1": w_in1[:, 3 * mc:4 * mc].astype(BF16),
        "w_q1t": w_in1[:, 0:mc].T.astype(BF16), "w_v1t": w_in1[:, 2 * mc:3 * mc].T.astype(BF16),
        "gq1c": jnp.broadcast_to(fox_qn_g.astype(F32).reshape(HEAD_DIM, 1), (HEAD_DIM, LANES)),
    }

    xp = x_prompt.reshape(seq, D_MODEL)
    y0p, pool_p, swk_p, swv_p = _layer0(xp, None, wts, ns=1, ts=L0_TB, chunk=SWA_CHUNK, base_pos=0)
    fk_p, fvt_p, fl_p, qa, ka, va, cum_end = _fox_proj_prompt(y0p, wts, tb=FOX_TB)
    fv_p = jnp.transpose(fvt_p.reshape(FOX_HEADS, HEAD_DIM, seq), (2, 0, 1))
    per = ATTN_TK // FOX_TB
    cum_end = cum_end[per - 1::per, 0, :FOX_HEADS].reshape(-1)
    qk_bound = (1.02 * LOG2E * HEAD_DIM * ATTN_SCALE * jnp.max(jnp.abs(fox_qn_g)) * jnp.max(jnp.abs(fox_kn_g))
                ).astype(F32).reshape(1)
    attn_p = _fox_attn_prompt(cum_end, qk_bound, qa, ka, va, tq=ATTN_TQ, tk=ATTN_TK, unroll=ATTN_UNROLL,
                              nh=ATTN_HEADS)
    yp = _out_proj(y0p, attn_p, wts, tb=L0_TB)

    xs = x_sample.reshape(db * dseq, D_MODEL)
    prefix = (jnp.pad(state_pool, ((0, 0), (1, 0), (0, 0))),
              cache_swa_k.reshape(db, WINDOW, LANES), cache_swa_v.reshape(db, WINDOW, LANES))
    y0s, pool_s, swk_s, swv_s = _layer0(xs, prefix, wts, ns=db, ts=dseq, chunk=dseq, base_pos=past_len)
    fk_s, fv_s, fl_s, q_s = _fox_proj_sample(y0s, wts, tb=FOX_TB)
    pad_rows = ((0, 0), (0, NEW_PAD - dseq), (0, 0))
    attn_s = _fox_attn_sample(
        q_s.reshape(db, dseq, D_MODEL),
        jnp.transpose(cache_fox_k, (0, 2, 3, 1)), jnp.transpose(cache_fox_v, (0, 2, 3, 1)),
        jnp.transpose(cache_fox_logf, (0, 2, 1)),
        jnp.pad(fk_s.reshape(db, dseq, D_MODEL), pad_rows), jnp.pad(fv_s.reshape(db, dseq, D_MODEL), pad_rows),
        jnp.pad(jnp.transpose(fl_s.reshape(db, dseq, FOX_HEADS), (0, 2, 1)), ((0, 0), (0, 0), (0, NEW_PAD - dseq))))
    ys = _out_proj(y0s, attn_s.reshape(db * dseq, D_MODEL), wts, tb=db * dseq)

    return (yp.reshape(1, seq, D_MODEL), ys.reshape(db, dseq, D_MODEL),
            pool_p[:, 1:], pool_s[:, 1:],
            swk_p.reshape(1, WINDOW, SWA_KV_HEADS, HEAD_DIM), swv_p.reshape(1, WINDOW, SWA_KV_HEADS, HEAD_DIM),
            swk_s.reshape(db, WINDOW, SWA_KV_HEADS, HEAD_DIM), swv_s.reshape(db, WINDOW, SWA_KV_HEADS, HEAD_DIM),
            fk_p.reshape(1, seq, FOX_HEADS, HEAD_DIM), fv_p.reshape(1, seq, FOX_HEADS, HEAD_DIM),
            fl_p.reshape(1, seq, FOX_HEADS),
            fk_s.reshape(db, dseq, FOX_HEADS, HEAD_DIM), fv_s.reshape(db, dseq, FOX_HEADS, HEAD_DIM),
            fl_s.reshape(db, dseq, FOX_HEADS))
```

```python
import functools

import numpy as np
import jax
import jax.numpy as jnp
from jax import lax
from jax.experimental import pallas as pl
from jax.experimental.pallas import tpu as pltpu

F32 = jnp.float32
BF16 = jnp.bfloat16

D_MODEL = 1024
HEAD_DIM = 64
ATTN_SCALE = HEAD_DIM ** -0.5
POOL_WINDOWS = (2, 4, 8, 16)
C_POOL = 512
POOL_PAD = 15
SWA_HEADS = 8
SWA_KV_HEADS = 2
SWA_REP = 4
WINDOW = 128
FOX_HEADS = 16
NORM_EPS = 1e-6
NEG_INF = -1e30
LOG2E = 1.4426950408889634
EXP2_ZERO = -152.0

LANES = 128
SWA_KEYS = 256
VMEM_LIMIT = 56 * 1024 * 1024


def _cparams(sem):
    return pltpu.CompilerParams(dimension_semantics=sem, vmem_limit_bytes=VMEM_LIMIT)


def _rms(x, g):
    ms = jnp.mean(x * x, axis=-1, keepdims=True)
    return x * lax.rsqrt(ms + NORM_EPS) * g


def _split3(x):
    hi = x.astype(BF16)
    r = x - hi.astype(F32)
    mid = r.astype(BF16)
    lo = (r - mid.astype(F32)).astype(BF16)
    return hi, mid, lo


def _head_rms_tile(x, seg, g):
    ss = jnp.dot((x * x).astype(BF16), seg, preferred_element_type=F32)
    return x * lax.rsqrt(ss * (1.0 / HEAD_DIM) + NORM_EPS) * g


def _silu(g):
    return g / (1.0 + jnp.exp(-g))


def _l0_kernel(*refs, ns, ts, chunk, has_cache, base_pos, nblk):
    if has_cache:
        (x_ref, pp_ref, kp_ref, vp_ref, g0_ref, win_ref, wpool_ref, pscale_ref, qg_ref, kg_ref, seg_ref,
         slope_ref, sink_ref, wout_ref, y_ref, ps_ref, ks_ref, vs_ref, uext, kext, vext, mix) = refs
    else:
        (x_ref, g0_ref, win_ref, wpool_ref, pscale_ref, qg_ref, kg_ref, seg_ref,
         slope_ref, sink_ref, wout_ref, y_ref, ps_ref, ks_ref, vs_ref, uext, kext, vext, mix) = refs
    padk = SWA_KEYS - WINDOW - chunk
    hist = padk + WINDOW
    i = pl.program_id(0)

    x = x_ref[...]
    h = _rms(x, g0_ref[...]).astype(BF16)
    proj = jnp.dot(h, win_ref[...], preferred_element_type=F32)
    u = proj[:, 0:C_POOL]
    gate = proj[:, 1280:2304]
    seg = seg_ref[...]
    qn = [_head_rms_tile(proj[:, 512 + LANES * j:512 + LANES * (j + 1)], seg,
                         qg_ref[:, LANES * j:LANES * (j + 1)]) * (ATTN_SCALE * LOG2E) for j in range(4)]
    kn = _head_rms_tile(proj[:, 1024:1152], seg, kg_ref[...])
    v = proj[:, 1152:1280]

    if has_cache:
        for s in range(ns):
            uext[s, 0:16, :] = pp_ref[s]
            kext[s, 0:padk, :] = jnp.zeros((padk, LANES), F32)
            vext[s, 0:padk, :] = jnp.zeros((padk, LANES), F32)
            kext[s, padk:hist, :] = kp_ref[s]
            vext[s, padk:hist, :] = vp_ref[s]
    else:
        @pl.when(i == 0)
        def _():
            uext[0, 0:16, :] = jnp.zeros((16, C_POOL), F32)
            kext[0, 0:hist, :] = jnp.zeros((hist, LANES), F32)
            vext[0, 0:hist, :] = jnp.zeros((hist, LANES), F32)

        @pl.when(i > 0)
        def _():
            uext[0, 0:16, :] = uext[0, ts:ts + 16, :]
            kext[0, 0:hist, :] = kext[0, ts:ts + hist, :]
            vext[0, 0:hist, :] = vext[0, ts:ts + hist, :]

    for s in range(ns):
        uext[s, 16:16 + ts, :] = u[s * ts:(s + 1) * ts]
        kext[s, hist:hist + ts, :] = kn[s * ts:(s + 1) * ts]
        vext[s, hist:hist + ts, :] = v[s * ts:(s + 1) * ts]

    pos = base_pos + i * ts + lax.broadcasted_iota(jnp.int32, (ts, LANES), 0)
    for s in range(ns):
        for g, w in enumerate(POOL_WINDOWS):
            cols = slice(LANES * g, LANES * (g + 1))
            acc = uext[s, 16:16 + ts, cols]
            cur = acc
            for j in range(1, w):
                acc = acc + uext[s, 16 - j:16 - j + ts, cols]
            cnt = jnp.minimum(pos + 1, w).astype(F32)
            mix[s * ts:(s + 1) * ts, cols] = acc / cnt - cur
    for g in range(4):
        cols = slice(LANES * g, LANES * (g + 1))
        d = mix[:, cols].astype(BF16)
        mix[:, cols] = jnp.dot(d, wpool_ref[g], preferred_element_type=F32) * pscale_ref[:, cols]

    rows4 = SWA_REP * chunk
    lo_c = lax.broadcasted_iota(jnp.int32, (chunk, LANES), 1) < HEAD_DIM
    lo_k = lax.broadcasted_iota(jnp.int32, (SWA_KEYS, LANES), 1) < HEAD_DIM
    kj = lax.broadcasted_iota(jnp.int32, (SWA_KEYS, rows4), 0)
    qi = lax.broadcasted_iota(jnp.int32, (SWA_KEYS, rows4), 1) % chunk
    absrel = jnp.abs(qi + WINDOW - (kj - padk)).astype(F32)
    bias = []
    sinkc = []
    for g in range(SWA_KV_HEADS):
        bias.append(jnp.where(kj >= padk, -(slope_ref[g][0:1, :] * LOG2E) * absrel, NEG_INF))
        sinkc.append(sink_ref[g][0:1, :] * LOG2E)
    nch = ts // chunk
    units = [(s, c, g) for s in range(ns) for c in range(nch) for g in range(SWA_KV_HEADS)]

    def swa_scores(s, c, g):
        r0 = c * chunk
        rows = slice(s * ts + r0, s * ts + r0 + chunk)
        kwin = kext[s, r0:r0 + SWA_KEYS, :]
        vwin = vext[s, r0:r0 + SWA_KEYS, :]
        krl = pltpu.roll(kwin, HEAD_DIM, 1)
        vrl = pltpu.roll(vwin, HEAD_DIM, 1)
        if g == 0:
            kd = jnp.where(lo_k, kwin, krl).astype(BF16)
            vd = jnp.where(lo_k, vwin, vrl).astype(BF16)
        else:
            kd = jnp.where(lo_k, krl, kwin).astype(BF16)
            vd = jnp.where(lo_k, vrl, vwin).astype(BF16)
        q0 = qn[2 * g][rows]
        q1 = qn[2 * g + 1][rows]
        qs = jnp.concatenate([jnp.where(lo_c, q0, 0.0), jnp.where(lo_c, 0.0, q0),
                              jnp.where(lo_c, q1, 0.0), jnp.where(lo_c, 0.0, q1)], axis=0).astype(BF16)
        return lax.dot_general(kd, qs, (((1,), (1,)), ((), ())), preferred_element_type=F32), vd

    def swa_finish(s, c, g, sc, vd):
        rows = slice(s * ts + c * chunk, s * ts + (c + 1) * chunk)
        sc = sc + bias[g]
        if not has_cache and c * chunk < WINDOW:
            sc = jnp.where(kj >= padk + jnp.maximum(0, WINDOW - (i * nch + c) * chunk), sc, NEG_INF)
        m = jnp.maximum(jnp.max(sc, axis=0, keepdims=True), sinkc[g])
        p = jnp.exp2(sc - m)
        den = jnp.sum(p, axis=0, keepdims=True) + jnp.exp2(sinkc[g] - m)
        wgt = (p / den).astype(BF16)
        o = lax.dot_general(wgt, vd, (((0,), (0,)), ((), ())), preferred_element_type=F32)
        for jj in range(2):
            t = 2 * g + jj
            mix[rows, C_POOL + LANES * t:C_POOL + LANES * (t + 1)] = jnp.where(
                lo_c, o[(2 * jj) * chunk:(2 * jj + 1) * chunk], o[(2 * jj + 1) * chunk:(2 * jj + 2) * chunk])

    pending = swa_scores(*units[0])
    for n, unit in enumerate(units):
        nxt = swa_scores(*units[n + 1]) if n + 1 < len(units) else None
        swa_finish(*unit, *pending)
        pending = nxt

    mixed = (mix[...] * _silu(gate)).astype(BF16)
    y_ref[...] = x + jnp.dot(mixed, wout_ref[...], preferred_element_type=F32)

    @pl.when(i == nblk - 1)
    def _():
        for s in range(ns):
            ps_ref[s] = uext[s, ts:ts + 16, :]
            ks_ref[s] = kext[s, padk + ts:padk + ts + WINDOW, :]
            vs_ref[s] = vext[s, padk + ts:padk + ts + WINDOW, :]


def _const_spec(shape):
    nd = len(shape)
    return pl.BlockSpec(shape, lambda i, _nd=nd: (0,) * _nd, pipeline_mode=pl.Buffered(1))


def _layer0(x2d, prefix, wts, *, ns, ts, chunk, base_pos):
    n = x2d.shape[0]
    tb = ns * ts
    nblk = n // tb
    has_cache = prefix is not None
    padk = SWA_KEYS - WINDOW - chunk
    rows4 = SWA_REP * chunk
    slope = np.repeat(2.0 ** (-(np.arange(SWA_HEADS) + 1.0)), chunk).reshape(SWA_KV_HEADS, 1, rows4)
    slope = jnp.asarray(np.broadcast_to(slope, (SWA_KV_HEADS, 8, rows4)).astype(np.float32))
    sink = jnp.broadcast_to(jnp.repeat(wts["sinks"].astype(F32), chunk).reshape(SWA_KV_HEADS, 1, rows4),
                            (SWA_KV_HEADS, 8, rows4))
    consts = [wts["g0"], wts["w_in0"], wts["w_pool"], wts["pool_scale"], wts["qg0"], wts["kg0"], wts["seg"],
              slope, sink, wts["w_out0"]]
    in_specs = [pl.BlockSpec((tb, D_MODEL), lambda i: (i, 0))]
    args = [x2d]
    if has_cache:
        for a in prefix:
            in_specs.append(_const_spec(a.shape))
            args.append(a)
    for a in consts:
        in_specs.append(_const_spec(a.shape))
        args.append(a)
    out_shape = [jax.ShapeDtypeStruct((n, D_MODEL), F32),
                 jax.ShapeDtypeStruct((ns, 16, C_POOL), F32),
                 jax.ShapeDtypeStruct((ns, WINDOW, LANES), F32),
                 jax.ShapeDtypeStruct((ns, WINDOW, LANES), F32)]
    out_specs = [pl.BlockSpec((tb, D_MODEL), lambda i: (i, 0)),
                 pl.BlockSpec((ns, 16, C_POOL), lambda i: (0, 0, 0)),
                 pl.BlockSpec((ns, WINDOW, LANES), lambda i: (0, 0, 0)),
                 pl.BlockSpec((ns, WINDOW, LANES), lambda i: (0, 0, 0))]
    scratch = [pltpu.VMEM((ns, 16 + ts, C_POOL), F32),
               pltpu.VMEM((ns, padk + WINDOW + ts, LANES), F32),
               pltpu.VMEM((ns, padk + WINDOW + ts, LANES), F32),
               pltpu.VMEM((tb, D_MODEL), F32)]
    kern = functools.partial(_l0_kernel, ns=ns, ts=ts, chunk=chunk, has_cache=has_cache,
                             base_pos=base_pos, nblk=nblk)
    return pl.pallas_call(
        kern, grid=(nblk,), in_specs=in_specs, out_specs=out_specs, out_shape=out_shape,
        scratch_shapes=scratch, compiler_params=_cparams(("arbitrary",)),
        name="layer0_prompt" if not has_cache else "layer0_sample")(*args)


def _log_sigmoid(z):
    return jnp.minimum(z, 0.0) - jnp.log1p(jnp.exp(-jnp.abs(z)))


def _fox_proj_sample_kernel(x_ref, g1_ref, wqt_ref, wk_ref, wvt_ref, wf_ref, bf_ref, qg_ref, kg_ref, seg_ref,
                            k_ref, v_ref, lf_ref, q_ref):
    x = x_ref[...]
    h = _rms(x, g1_ref[...]).astype(BF16)
    z = jnp.dot(h, wf_ref[...], preferred_element_type=F32) + bf_ref[...]
    lf_ref[...] = _log_sigmoid(z)[:, 0:FOX_HEADS]
    seg = seg_ref[...]
    nt = (((1,), (1,)), ((), ()))
    q = lax.dot_general(h, wqt_ref[...], nt, preferred_element_type=F32)
    k = jnp.dot(h, wk_ref[...], preferred_element_type=F32)
    v_ref[...] = lax.dot_general(h, wvt_ref[...], nt, preferred_element_type=F32)
    for j in range(FOX_HEADS // 2):
        cols = slice(LANES * j, LANES * (j + 1))
        q_ref[:, cols] = _head_rms_tile(q[:, cols], seg, qg_ref[:, cols]) * ATTN_SCALE
        k_ref[:, cols] = _head_rms_tile(k[:, cols], seg, kg_ref[:, cols])


def _fox_proj_sample(x2d, wts, *, tb):
    n = x2d.shape[0]
    consts = [wts["g1"], wts["w_q1t"], wts["w_k1"], wts["w_v1t"], wts["w_f"], wts["b_f"], wts["qg1"], wts["kg1"],
              wts["seg"]]
    row_spec = pl.BlockSpec((tb, D_MODEL), lambda i: (i, 0))
    in_specs = [row_spec] + [_const_spec(a.shape) for a in consts]
    row = jax.ShapeDtypeStruct((n, D_MODEL), F32)
    out_shape = [row, row, jax.ShapeDtypeStruct((n, FOX_HEADS), F32), row]
    out_specs = [row_spec, row_spec, pl.BlockSpec((tb, FOX_HEADS), lambda i: (i, 0)), row_spec]
    return pl.pallas_call(
        _fox_proj_sample_kernel, grid=(n // tb,), in_specs=in_specs, out_specs=out_specs, out_shape=out_shape,
        compiler_params=_cparams(("arbitrary",)), name="fox_proj_sample")(x2d, *consts)


def _fox_proj_prompt_kernel(x_ref, g1_ref, wk_ref, wf_ref, bf_ref, wqt_ref, wvt_ref, gq_ref, kg_ref,
                            seg_ref, tri_ref, e_ref,
                            k_ref, vt_ref, lf_ref, qa_ref, ka_ref, va_ref, ce_ref, carry, *, tb):
    i = pl.program_id(0)
    x = x_ref[...]
    hf = _rms(x, g1_ref[...])
    h = hf.astype(BF16)
    ht = hf.T.astype(BF16)
    logf = _log_sigmoid(jnp.dot(h, wf_ref[...], preferred_element_type=F32) + bf_ref[...])
    lf_ref[...] = logf[:, 0:FOX_HEADS]
    k = jnp.dot(h, wk_ref[...], preferred_element_type=F32)

    @pl.when(i == 0)
    def _():
        carry[...] = jnp.zeros(carry.shape, F32)

    tri = tri_ref[...]
    hi, mid, lo = _split3(logf)
    cum = (jnp.dot(tri, hi, preferred_element_type=F32) + jnp.dot(tri, mid, preferred_element_type=F32)
           + jnp.dot(tri, lo, preferred_element_type=F32)) + carry[0:1, :]
    carry[0:1, :] = cum[tb - 1:tb, :]
    ce_ref[0] = cum[tb - 1:tb, :]
    nh, nm, nl = _split3(-LOG2E * cum)
    biasall = jnp.dot(jnp.concatenate([nh, nm, nl], axis=1), e_ref[...], preferred_element_type=F32)
    lo_l = lax.broadcasted_iota(jnp.int32, (tb, LANES), 1) < HEAD_DIM
    seg = seg_ref[...]
    for j in range(FOX_HEADS // 2):
        cols = slice(LANES * j, LANES * (j + 1))
        kt = _head_rms_tile(k[:, cols], seg, kg_ref[:, cols])
        k_ref[:, cols] = kt
        bt = biasall[:, cols]
        ka_ref[2 * j] = jnp.where(lo_l, kt, bt).astype(BF16)
        ka_ref[2 * j + 1] = pltpu.roll(jnp.where(lo_l, bt, kt), HEAD_DIM, 1).astype(BF16)

    qt = jnp.dot(wqt_ref[...], ht, preferred_element_type=F32)
    vt = jnp.dot(wvt_ref[...], ht, preferred_element_type=F32)
    vt_ref[...] = vt
    srow = lax.broadcasted_iota(jnp.int32, (HEAD_DIM, tb), 0)
    q_aug = jnp.where(srow < 3, 1.0, 0.0)
    v_aug = jnp.where(srow == 0, 1.0, 0.0)
    gq = jnp.concatenate([gq_ref[...]] * (tb // LANES), axis=1) * (ATTN_SCALE * LOG2E)
    for hd in range(FOX_HEADS):
        rows = slice(HEAD_DIM * hd, HEAD_DIM * (hd + 1))
        qh = qt[rows]
        ss = jnp.sum(qh * qh, axis=0, keepdims=True)
        qn = qh * lax.rsqrt(ss * (1.0 / HEAD_DIM) + NORM_EPS) * gq
        qa_ref[hd] = jnp.concatenate([qn, q_aug], axis=0).astype(BF16)
        va_ref[hd] = jnp.concatenate([vt[rows], v_aug], axis=0).astype(BF16)


def _fox_proj_prompt(x2d, wts, *, tb):
    n = x2d.shape[0]
    tri = jnp.asarray(np.tril(np.ones((tb, tb), np.float32)), BF16)
    e = np.zeros((3 * LANES, FOX_HEADS // 2 * LANES), np.float32)
    for part in range(3):
        for hd in range(FOX_HEADS):
            e[part * LANES + hd, (hd // 2) * LANES + (HEAD_DIM if hd % 2 == 0 else 0) + part] = 1.0
    consts = [wts["g1"], wts["w_k1"], wts["w_f"], wts["b_f"], wts["w_q1t"], wts["w_v1t"],
              wts["gq1c"], wts["kg1"], wts["seg"], tri, jnp.asarray(e, BF16)]
    row_spec = pl.BlockSpec((tb, D_MODEL), lambda i: (i, 0))
    in_specs = [row_spec] + [_const_spec(a.shape) for a in consts]
    row = jax.ShapeDtypeStruct((n, D_MODEL), F32)
    out_shape = [row, jax.ShapeDtypeStruct((D_MODEL, n), F32), jax.ShapeDtypeStruct((n, FOX_HEADS), F32),
                 jax.ShapeDtypeStruct((FOX_HEADS, LANES, n), BF16), jax.ShapeDtypeStruct((FOX_HEADS, n, LANES), BF16),
                 jax.ShapeDtypeStruct((FOX_HEADS, LANES, n), BF16),
                 jax.ShapeDtypeStruct((n // tb, 1, LANES), F32)]
    tspec = pl.BlockSpec((FOX_HEADS, LANES, tb), lambda i: (0, 0, i))
    out_specs = [row_spec, pl.BlockSpec((D_MODEL, tb), lambda i: (0, i)),
                 pl.BlockSpec((tb, FOX_HEADS), lambda i: (i, 0)),
                 tspec, pl.BlockSpec((FOX_HEADS, tb, LANES), lambda i: (0, i, 0)), tspec,
                 pl.BlockSpec((1, 1, LANES), lambda i: (i, 0, 0))]
    return pl.pallas_call(
        functools.partial(_fox_proj_prompt_kernel, tb=tb), grid=(n // tb,), in_specs=in_specs,
        out_specs=out_specs, out_shape=out_shape, scratch_shapes=[pltpu.VMEM((8, LANES), F32)],
        compiler_params=_cparams(("arbitrary",)), name="fox_proj_prompt")(x2d, *consts)


def _fox_attn_kernel(ce_ref, qk_ref, q_ref, k_ref, v_ref, o_ref, m_scr, al_scr, acc_scr, s_scr,
                     *, tq, tk, unroll, nkb, nh):
    pair = pl.program_id(0)
    qi = pl.program_id(1)
    n_u = qi
    m_scr[...] = jnp.full(m_scr.shape, NEG_INF, F32)
    acc_scr[...] = jnp.zeros(acc_scr.shape, F32)

    last = jnp.maximum(n_u - 1, 0)
    slack = 2.0 * qk_ref[0]

    def dead(j):
        jj = jnp.minimum(j, last)
        ok = j < n_u - 1
        for hh in range(nh):
            hd = nh * pair + hh
            gap = LOG2E * (ce_ref[last * FOX_HEADS + hd] - ce_ref[jj * FOX_HEADS + hd])
            ok = ok & (slack + gap < EXP2_ZERO)
        return ok

    lo = jnp.int32(0)
    hi = last
    for _ in range(max(1, (nkb - 1).bit_length())):
        mid = (lo + hi) >> 1
        d = dead(mid)
        lo = jnp.where(d, mid + 1, lo)
        hi = jnp.where(d, hi, mid)
    first = lo

    def score_matmuls(kb):
        off = pl.multiple_of(kb * tk, tk)
        return [jnp.dot(k_ref[hh, pl.ds(off, tk), :], q_ref[hh], preferred_element_type=F32)
                for hh in range(nh)]

    def score_finish(sts):
        for hh in range(nh):
            st = sts[hh]
            m_old = m_scr[hh, 0:1, :]
            m_new = jnp.maximum(m_old, jnp.max(st, axis=0, keepdims=True))
            al_scr[hh, 0:1, :] = jnp.exp2(m_old - m_new)
            m_scr[hh, 0:1, :] = m_new
            s_scr[hh] = st

    def accumulate(kb):
        off = pl.multiple_of(kb * tk, tk)
        for hh in range(nh):
            p = jnp.exp2(s_scr[hh] - m_scr[hh, 0:1, :]).astype(BF16)
            pv = jnp.dot(v_ref[hh, :, pl.ds(off, tk)], p, preferred_element_type=F32)
            acc_scr[hh] = al_scr[hh, 0:1, :] * acc_scr[hh] + pv

    def advance(kb):
        sts = score_matmuls(kb + 1)
        accumulate(kb)
        score_finish(sts)

    def finish(have_prev):
        half = tq // 2
        off = pl.multiple_of(n_u * tk, tk)
        sts = [(jnp.dot(k_ref[hh, pl.ds(off, half), :], q_ref[hh, :, 0:half], preferred_element_type=F32),
                jnp.dot(k_ref[hh, pl.ds(off, tk), :], q_ref[hh, :, half:tq], preferred_element_type=F32))
               for hh in range(nh)]
        if have_prev:
            accumulate(n_u - 1)
        tri = jnp.where(lax.broadcasted_iota(jnp.int32, (half, half), 0)
                        <= lax.broadcasted_iota(jnp.int32, (half, half), 1), 0.0, NEG_INF)
        for hh in range(nh):
            st_a, st_b = sts[hh]
            halves = ((st_a + tri, 0), (jnp.concatenate([st_b[0:half], st_b[half:tk] + tri], axis=0), half))
            for st, c0 in halves:
                nk = st.shape[0]
                m_old = m_scr[hh, 0:1, c0:c0 + half]
                m_new = jnp.maximum(m_old, jnp.max(st, axis=0, keepdims=True))
                p = jnp.exp2(st - m_new).astype(BF16)
                pv = jnp.dot(v_ref[hh, :, pl.ds(off, nk)], p, preferred_element_type=F32)
                acc_scr[hh, :, c0:c0 + half] = jnp.exp2(m_old - m_new) * acc_scr[hh, :, c0:c0 + half] + pv
        for t in range(nh // 2):
            pair_rows = []
            for hh in (2 * t, 2 * t + 1):
                a = acc_scr[hh]
                pair_rows.append(a[0:HEAD_DIM] / a[HEAD_DIM:HEAD_DIM + 1, :])
            o_ref[:, LANES * t:LANES * (t + 1)] = jnp.concatenate(pair_rows, axis=0).T.astype(o_ref.dtype)

    @pl.when(qi == 0)
    def _():
        finish(False)

    @pl.when(qi > 0)
    def _():
        n_adv = n_u - 1 - first
        rem = lax.rem(n_adv, unroll)
        for k in range(unroll):
            @pl.when(rem == k)
            def _():
                score_finish(score_matmuls(first))
                for u in range(k):
                    advance(first + u)

        def body(t, carry):
            for u in range(unroll):
                advance(first + rem + unroll * t + u)
            return carry

        lax.fori_loop(0, n_adv // unroll, body, 0)
        finish(True)


def _fox_attn_prompt(cum_end, qk_bound, qa, ka, va, *, tq, tk, unroll, nh):
    n = ka.shape[1]
    resident = dict(pipeline_mode=pl.Buffered(1)) if nh > 2 else {}
    grid_spec = pltpu.PrefetchScalarGridSpec(
        num_scalar_prefetch=2, grid=(FOX_HEADS // nh, n // tq),
        in_specs=[pl.BlockSpec((nh, LANES, tq), lambda p, i, ce, qk: (p, 0, i)),
                  pl.BlockSpec((nh, n, LANES), lambda p, i, ce, qk: (p, 0, 0), **resident),
                  pl.BlockSpec((nh, LANES, n), lambda p, i, ce, qk: (p, 0, 0), **resident)],
        out_specs=pl.BlockSpec((tq, HEAD_DIM * nh), lambda p, i, ce, qk: (i, p)),
        scratch_shapes=[pltpu.VMEM((nh, 8, tq), F32), pltpu.VMEM((nh, 8, tq), F32),
                        pltpu.VMEM((nh, LANES, tq), F32), pltpu.VMEM((nh, tk, tq), F32)])
    return pl.pallas_call(
        functools.partial(_fox_attn_kernel, tq=tq, tk=tk, unroll=unroll, nkb=n // tk, nh=nh),
        grid_spec=grid_spec,
        out_shape=jax.ShapeDtypeStruct((n, D_MODEL), BF16),
        compiler_params=_cparams(("arbitrary", "arbitrary")), name="fox_attn_prompt")(
            cum_end, qk_bound, qa, ka, va)


CUM_BLK = 512
NEW_PAD = 128


def _fox_sample_kernel(q_ref, kc_ref, vc_ref, lfc_ref, kn_ref, vn_ref, lfn_ref, tri_ref, o_ref, negc, negn,
                       *, t_new, p_len):
    qd = pl.program_id(1)
    rows = 4 * t_new

    @pl.when(qd == 0)
    def _():
        tri = tri_ref[...]

        def prefix_sums(x, t):
            c3 = jnp.dot(jnp.concatenate(_split3(x), axis=0), t, preferred_element_type=F32)
            return c3[0:FOX_HEADS] + c3[FOX_HEADS:2 * FOX_HEADS] + c3[2 * FOX_HEADS:3 * FOX_HEADS]

        carry = jnp.zeros((FOX_HEADS, 1), F32)
        for b in range(p_len // CUM_BLK):
            c = prefix_sums(lfc_ref[0, :, b * CUM_BLK:(b + 1) * CUM_BLK], tri) + carry
            negc[:, b * CUM_BLK:(b + 1) * CUM_BLK] = -c
            carry = c[:, CUM_BLK - 1:CUM_BLK]
        negn[...] = -(prefix_sums(lfn_ref[0], tri[0:NEW_PAD, 0:NEW_PAD]) + carry)

    qq = q_ref[0]
    rhead = lax.broadcasted_iota(jnp.int32, (rows, 2 * LANES), 0) // t_new
    lhead = lax.broadcasted_iota(jnp.int32, (rows, 2 * LANES), 1) // HEAD_DIM
    qbd = jnp.where(rhead == lhead, jnp.concatenate([qq] * 4, axis=0), 0.0).astype(BF16)

    kct = kc_ref[0].reshape(2 * LANES, p_len).astype(BF16)
    vct = vc_ref[0].reshape(2 * LANES, p_len).astype(BF16)
    s_c = jnp.dot(qbd, kct, preferred_element_type=F32)
    zpad = jnp.zeros((NEW_PAD - t_new, 2 * LANES), F32)
    kn = jnp.concatenate([kn_ref[0], zpad], axis=0).astype(BF16)
    vn = jnp.concatenate([vn_ref[0], zpad], axis=0).astype(BF16)
    s_n = lax.dot_general(qbd, kn, (((1,), (1,)), ((), ())), preferred_element_type=F32)
    bc = jnp.concatenate([jnp.broadcast_to(negc[pl.ds(4 * qd + jh, 1), :], (t_new, p_len)) for jh in range(4)], axis=0)
    bn = jnp.concatenate([jnp.broadcast_to(negn[pl.ds(4 * qd + jh, 1), :], (t_new, NEW_PAD)) for jh in range(4)], axis=0)
    s_c = s_c + bc
    qrow = lax.broadcasted_iota(jnp.int32, (rows, NEW_PAD), 0) % t_new
    kcol = lax.broadcasted_iota(jnp.int32, (rows, NEW_PAD), 1)
    s_n = jnp.where(kcol <= qrow, s_n + bn, NEG_INF)
    m = jnp.maximum(jnp.max(s_c, axis=-1, keepdims=True), jnp.max(s_n, axis=-1, keepdims=True))
    p_c = jnp.exp(s_c - m)
    p_n = jnp.exp(s_n - m)
    den = jnp.sum(p_c, axis=-1, keepdims=True) + jnp.sum(p_n, axis=-1, keepdims=True)
    o = (lax.dot_general(p_c.astype(BF16), vct, (((1,), (1,)), ((), ())), preferred_element_type=F32)
         + jnp.dot(p_n.astype(BF16), vn, preferred_element_type=F32)) / den
    lh = lax.broadcasted_iota(jnp.int32, (t_new, 2 * LANES), 1) // HEAD_DIM
    out = jnp.zeros((t_new, 2 * LANES), F32)
    for jh in range(4):
        out = jnp.where(lh == jh, o[jh * t_new:(jh + 1) * t_new], out)
    o_ref[0] = out


def _fox_attn_sample(q, cache_k, cache_v, lfc_t, k_new, v_new, lfn_t):
    b, t_new, _ = q.shape
    p_len = cache_k.shape[3]
    tri = jnp.asarray(np.triu(np.ones((CUM_BLK, CUM_BLK), np.float32)), BF16)
    quad = 2 * LANES
    in_specs = [pl.BlockSpec((1, t_new, quad), lambda r, d: (r, 0, d)),
                pl.BlockSpec((1, 4, HEAD_DIM, p_len), lambda r, d: (r, d, 0, 0)),
                pl.BlockSpec((1, 4, HEAD_DIM, p_len), lambda r, d: (r, d, 0, 0)),
                pl.BlockSpec((1, FOX_HEADS, p_len), lambda r, d: (r, 0, 0)),
                pl.BlockSpec((1, t_new, quad), lambda r, d: (r, 0, d)),
                pl.BlockSpec((1, t_new, quad), lambda r, d: (r, 0, d)),
                pl.BlockSpec((1, FOX_HEADS, NEW_PAD), lambda r, d: (r, 0, 0)),
                pl.BlockSpec((CUM_BLK, CUM_BLK), lambda r, d: (0, 0))]
    return pl.pallas_call(
        functools.partial(_fox_sample_kernel, t_new=t_new, p_len=p_len),
        grid=(b, D_MODEL // quad), in_specs=in_specs,
        out_specs=pl.BlockSpec((1, t_new, quad), lambda r, d: (r, 0, d)),
        out_shape=jax.ShapeDtypeStruct((b, t_new, D_MODEL), F32),
        scratch_shapes=[pltpu.VMEM((FOX_HEADS, p_len), F32), pltpu.VMEM((FOX_HEADS, NEW_PAD), F32)],
        compiler_params=_cparams(("arbitrary", "arbitrary")), name="fox_attn_sample")(
            q, cache_k, cache_v, lfc_t, k_new, v_new, lfn_t, tri)


def _out_proj_kernel(r_ref, a_ref, g1_ref, wg_ref, w_ref, y_ref):
    x = r_ref[...]
    h = _rms(x, g1_ref[...]).astype(BF16)
    gate = jnp.dot(h, wg_ref[...], preferred_element_type=F32)
    mixed = (a_ref[...].astype(F32) * _silu(gate)).astype(BF16)
    y_ref[...] = x + jnp.dot(mixed, w_ref[...], preferred_element_type=F32)


def _out_proj(resid, attn, wts, *, tb):
    n = resid.shape[0]
    row = pl.BlockSpec((tb, D_MODEL), lambda i: (i, 0))
    consts = [wts["g1"], wts["w_g1"], wts["w_out1"]]
    return pl.pallas_call(
        _out_proj_kernel, grid=(n // tb,), in_specs=[row, row] + [_const_spec(a.shape) for a in consts],
        out_specs=row, out_shape=jax.ShapeDtypeStruct((n, D_MODEL), F32),
        compiler_params=_cparams(("arbitrary",)), name="out_proj")(resid, attn, *consts)


L0_TB = 512
FOX_TB = 256
ATTN_TQ = 512
ATTN_TK = 512
ATTN_UNROLL = 4
ATTN_HEADS = 2
SWA_CHUNK = 64


def kernel(x_prompt, x_sample, state_pool, cache_swa_k, cache_swa_v, cache_fox_k, cache_fox_v, cache_fox_logf,
           norm0_g, w_in0, w_pool, pool_scale, swa_qn_g, swa_kn_g, swa_sinks, w_out0,
           norm1_g, w_in1, b_forget, fox_qn_g, fox_kn_g, w_out1):
    nb, seq, _ = x_prompt.shape
    db, dseq, _ = x_sample.shape
    past_len = cache_fox_k.shape[1]
    assert nb == 1 and seq % L0_TB == 0 and seq % ATTN_TQ == 0 and dseq % 8 == 0

    seg = np.kron(np.eye(2, dtype=np.float32), np.ones((HEAD_DIM, HEAD_DIM), np.float32))
    mc = FOX_HEADS * HEAD_DIM
    wts = {
        "g0": norm0_g.reshape(1, D_MODEL), "w_in0": w_in0.astype(BF16), "w_pool": w_pool.astype(BF16),
        "pool_scale": pool_scale.reshape(1, C_POOL),
        "qg0": jnp.tile(swa_qn_g, SWA_HEADS).reshape(1, SWA_HEADS * HEAD_DIM),
        "kg0": jnp.tile(swa_kn_g, SWA_KV_HEADS).reshape(1, LANES),
        "seg": jnp.asarray(seg, BF16), "sinks": swa_sinks, "w_out0": w_out0.astype(BF16),
        "g1": norm1_g.reshape(1, D_MODEL),
        "w_out1": w_out1.astype(BF16),
        "w_f": jnp.pad(w_in1[:, 4 * mc:], ((0, 0), (0, LANES - FOX_HEADS))).astype(BF16),
        "b_f": jnp.pad(b_forget.astype(F32), (0, LANES - FOX_HEADS)).reshape(1, LANES),
        "qg1": jnp.tile(fox_qn_g, FOX_HEADS).reshape(1, mc), "kg1": jnp.tile(fox_kn_g, FOX_HEADS).reshape(1, mc),
        "w_k1": w_in1[:, mc:2 * mc].astype(BF16), "w_g1": w_in1[:, 3 * mc:4 * mc].astype(BF16),
        "w_q1t": w_in1[:, 0:mc].T.astype(BF16), "w_v1t": w_in1[:, 2 * mc:3 * mc].T.astype(BF16),
        "gq1c": jnp.broadcast_to(fox_qn_g.astype(F32).reshape(HEAD_DIM, 1), (HEAD_DIM, LANES)),
    }

    xp = x_prompt.reshape(seq, D_MODEL)
    y0p, pool_p, swk_p, swv_p = _layer0(xp, None, wts, ns=1, ts=L0_TB, chunk=SWA_CHUNK, base_pos=0)
    fk_p, fvt_p, fl_p, qa, ka, va, cum_end = _fox_proj_prompt(y0p, wts, tb=FOX_TB)
    fv_p = jnp.transpose(fvt_p.reshape(FOX_HEADS, HEAD_DIM, seq), (2, 0, 1))
    per = ATTN_TK // FOX_TB
    cum_end = cum_end[per - 1::per, 0, :FOX_HEADS].reshape(-1)
    qk_bound = (1.02 * LOG2E * HEAD_DIM * ATTN_SCALE * jnp.max(jnp.abs(fox_qn_g)) * jnp.max(jnp.abs(fox_kn_g))
                ).astype(F32).reshape(1)
    attn_p = _fox_attn_prompt(cum_end, qk_bound, qa, ka, va, tq=ATTN_TQ, tk=ATTN_TK, unroll=ATTN_UNROLL,
                              nh=ATTN_HEADS)
    yp = _out_proj(y0p, attn_p, wts, tb=L0_TB)

    xs = x_sample.reshape(db * dseq, D_MODEL)
    prefix = (jnp.pad(state_pool, ((0, 0), (1, 0), (0, 0))),
              cache_swa_k.reshape(db, WINDOW, LANES), cache_swa_v.reshape(db, WINDOW, LANES))
    y0s, pool_s, swk_s, swv_s = _layer0(xs, prefix, wts, ns=db, ts=dseq, chunk=dseq, base_pos=past_len)
    fk_s, fv_s, fl_s, q_s = _fox_proj_sample(y0s, wts, tb=db * dseq)
    attn_s = _fox_attn_sample(
        q_s.reshape(db, dseq, D_MODEL),
        jnp.transpose(cache_fox_k, (0, 2, 3, 1)), jnp.transpose(cache_fox_v, (0, 2, 3, 1)),
        jnp.transpose(cache_fox_logf, (0, 2, 1)),
        fk_s.reshape(db, dseq, D_MODEL), fv_s.reshape(db, dseq, D_MODEL),
        jnp.pad(jnp.transpose(fl_s.reshape(db, dseq, FOX_HEADS), (0, 2, 1)), ((0, 0), (0, 0), (0, NEW_PAD - dseq))))
    ys = _out_proj(y0s, attn_s.reshape(db * dseq, D_MODEL), wts, tb=db * dseq)

    return (yp.reshape(1, seq, D_MODEL), ys.reshape(db, dseq, D_MODEL),
            pool_p[:, 1:], pool_s[:, 1:],
            swk_p.reshape(1, WINDOW, SWA_KV_HEADS, HEAD_DIM), swv_p.reshape(1, WINDOW, SWA_KV_HEADS, HEAD_DIM),
            swk_s.reshape(db, WINDOW, SWA_KV_HEADS, HEAD_DIM), swv_s.reshape(db, WINDOW, SWA_KV_HEADS, HEAD_DIM),
            fk_p.reshape(1, seq, FOX_HEADS, HEAD_DIM), fv_p.reshape(1, seq, FOX_HEADS, HEAD_DIM),
            fl_p.reshape(1, seq, FOX_HEADS),
            fk_s.reshape(db, dseq, FOX_HEADS, HEAD_DIM), fv_s.reshape(db, dseq, FOX_HEADS, HEAD_DIM),
            fl_s.reshape(db, dseq, FOX_HEADS))
```

```python
import functools

import numpy as np
import jax
import jax.numpy as jnp
from jax import lax
from jax.experimental import pallas as pl
from jax.experimental.pallas import tpu as pltpu

F32 = jnp.float32
BF16 = jnp.bfloat16

D_MODEL = 1024
HEAD_DIM = 64
ATTN_SCALE = HEAD_DIM ** -0.5
POOL_WINDOWS = (2, 4, 8, 16)
C_POOL = 512
POOL_PAD = 15
SWA_HEADS = 8
SWA_KV_HEADS = 2
SWA_REP = 4
WINDOW = 128
FOX_HEADS = 16
NORM_EPS = 1e-6
NEG_INF = -1e30
LOG2E = 1.4426950408889634
EXP2_ZERO = -152.0

LANES = 128
SWA_KEYS = 256
VMEM_LIMIT = 56 * 1024 * 1024


def _cparams(sem):
    return pltpu.CompilerParams(dimension_semantics=sem, vmem_limit_bytes=VMEM_LIMIT)


def _rms(x, g):
    ms = jnp.mean(x * x, axis=-1, keepdims=True)
    return x * lax.rsqrt(ms + NORM_EPS) * g


def _split3(x):
    hi = x.astype(BF16)
    r = x - hi.astype(F32)
    mid = r.astype(BF16)
    lo = (r - mid.astype(F32)).astype(BF16)
    return hi, mid, lo


def _head_rms_tile(x, seg, g):
    ss = jnp.dot((x * x).astype(BF16), seg, preferred_element_type=F32)
    return x * lax.rsqrt(ss * (1.0 / HEAD_DIM) + NORM_EPS) * g


def _silu(g):
    return g / (1.0 + jnp.exp(-g))


def _l0_kernel(*refs, ns, ts, chunk, has_cache, base_pos, nblk):
    if has_cache:
        (x_ref, pp_ref, kp_ref, vp_ref, g0_ref, win_ref, wpool_ref, pscale_ref, qg_ref, kg_ref, seg_ref,
         slope_ref, sink_ref, wout_ref, y_ref, ps_ref, ks_ref, vs_ref, uext, kext, vext, mix) = refs
    else:
        (x_ref, g0_ref, win_ref, wpool_ref, pscale_ref, qg_ref, kg_ref, seg_ref,
         slope_ref, sink_ref, wout_ref, y_ref, ps_ref, ks_ref, vs_ref, uext, kext, vext, mix) = refs
    padk = SWA_KEYS - WINDOW - chunk
    hist = padk + WINDOW
    i = pl.program_id(0)

    x = x_ref[...]
    h = _rms(x, g0_ref[...]).astype(BF16)
    proj = jnp.dot(h, win_ref[...], preferred_element_type=F32)
    u = proj[:, 0:C_POOL]
    gate = proj[:, 1280:2304]
    seg = seg_ref[...]
    qn = [_head_rms_tile(proj[:, 512 + LANES * j:512 + LANES * (j + 1)], seg,
                         qg_ref[:, LANES * j:LANES * (j + 1)]) * (ATTN_SCALE * LOG2E) for j in range(4)]
    kn = _head_rms_tile(proj[:, 1024:1152], seg, kg_ref[...])
    v = proj[:, 1152:1280]

    if has_cache:
        for s in range(ns):
            uext[s, 0:16, :] = pp_ref[s]
            kext[s, 0:padk, :] = jnp.zeros((padk, LANES), F32)
            vext[s, 0:padk, :] = jnp.zeros((padk, LANES), F32)
            kext[s, padk:hist, :] = kp_ref[s]
            vext[s, padk:hist, :] = vp_ref[s]
    else:
        @pl.when(i == 0)
        def _():
            uext[0, 0:16, :] = jnp.zeros((16, C_POOL), F32)
            kext[0, 0:hist, :] = jnp.zeros((hist, LANES), F32)
            vext[0, 0:hist, :] = jnp.zeros((hist, LANES), F32)

        @pl.when(i > 0)
        def _():
            uext[0, 0:16, :] = uext[0, ts:ts + 16, :]
            kext[0, 0:hist, :] = kext[0, ts:ts + hist, :]
            vext[0, 0:hist, :] = vext[0, ts:ts + hist, :]

    for s in range(ns):
        uext[s, 16:16 + ts, :] = u[s * ts:(s + 1) * ts]
        kext[s, hist:hist + ts, :] = kn[s * ts:(s + 1) * ts]
        vext[s, hist:hist + ts, :] = v[s * ts:(s + 1) * ts]

    pos = base_pos + i * ts + lax.broadcasted_iota(jnp.int32, (ts, LANES), 0)
    for s in range(ns):
        for g, w in enumerate(POOL_WINDOWS):
            cols = slice(LANES * g, LANES * (g + 1))
            acc = uext[s, 16:16 + ts, cols]
            cur = acc
            for j in range(1, w):
                acc = acc + uext[s, 16 - j:16 - j + ts, cols]
            cnt = jnp.minimum(pos + 1, w).astype(F32)
            mix[s * ts:(s + 1) * ts, cols] = acc / cnt - cur
    for g in range(4):
        cols = slice(LANES * g, LANES * (g + 1))
        d = mix[:, cols].astype(BF16)
        mix[:, cols] = jnp.dot(d, wpool_ref[g], preferred_element_type=F32) * pscale_ref[:, cols]

    rows4 = SWA_REP * chunk
    lo_c = lax.broadcasted_iota(jnp.int32, (chunk, LANES), 1) < HEAD_DIM
    lo_k = lax.broadcasted_iota(jnp.int32, (SWA_KEYS, LANES), 1) < HEAD_DIM
    kj = lax.broadcasted_iota(jnp.int32, (SWA_KEYS, rows4), 0)
    qi = lax.broadcasted_iota(jnp.int32, (SWA_KEYS, rows4), 1) % chunk
    absrel = jnp.abs(qi + WINDOW - (kj - padk)).astype(F32)
    bias = []
    sinkc = []
    for g in range(SWA_KV_HEADS):
        bias.append(jnp.where(kj >= padk, -(slope_ref[g][0:1, :] * LOG2E) * absrel, NEG_INF))
        sinkc.append(sink_ref[g][0:1, :] * LOG2E)
    nch = ts // chunk
    units = [(s, c, g) for s in range(ns) for c in range(nch) for g in range(SWA_KV_HEADS)]

    def swa_scores(s, c, g):
        r0 = c * chunk
        rows = slice(s * ts + r0, s * ts + r0 + chunk)
        kwin = kext[s, r0:r0 + SWA_KEYS, :]
        vwin = vext[s, r0:r0 + SWA_KEYS, :]
        krl = pltpu.roll(kwin, HEAD_DIM, 1)
        vrl = pltpu.roll(vwin, HEAD_DIM, 1)
        if g == 0:
            kd = jnp.where(lo_k, kwin, krl).astype(BF16)
            vd = jnp.where(lo_k, vwin, vrl).astype(BF16)
        else:
            kd = jnp.where(lo_k, krl, kwin).astype(BF16)
            vd = jnp.where(lo_k, vrl, vwin).astype(BF16)
        q0 = qn[2 * g][rows]
        q1 = qn[2 * g + 1][rows]
        qs = jnp.concatenate([jnp.where(lo_c, q0, 0.0), jnp.where(lo_c, 0.0, q0),
                              jnp.where(lo_c, q1, 0.0), jnp.where(lo_c, 0.0, q1)], axis=0).astype(BF16)
        return lax.dot_general(kd, qs, (((1,), (1,)), ((), ())), preferred_element_type=F32), vd

    def swa_finish(s, c, g, sc, vd):
        rows = slice(s * ts + c * chunk, s * ts + (c + 1) * chunk)
        sc = sc + bias[g]
        if not has_cache and c * chunk < WINDOW:
            sc = jnp.where(kj >= padk + jnp.maximum(0, WINDOW - (i * nch + c) * chunk), sc, NEG_INF)
        m = jnp.maximum(jnp.max(sc, axis=0, keepdims=True), sinkc[g])
        p = jnp.exp2(sc - m)
        den = jnp.sum(p, axis=0, keepdims=True) + jnp.exp2(sinkc[g] - m)
        wgt = (p / den).astype(BF16)
        o = lax.dot_general(wgt, vd, (((0,), (0,)), ((), ())), preferred_element_type=F32)
        for jj in range(2):
            t = 2 * g + jj
            mix[rows, C_POOL + LANES * t:C_POOL + LANES * (t + 1)] = jnp.where(
                lo_c, o[(2 * jj) * chunk:(2 * jj + 1) * chunk], o[(2 * jj + 1) * chunk:(2 * jj + 2) * chunk])

    pending = swa_scores(*units[0])
    for n, unit in enumerate(units):
        nxt = swa_scores(*units[n + 1]) if n + 1 < len(units) else None
        swa_finish(*unit, *pending)
        pending = nxt

    mixed = (mix[...] * _silu(gate)).astype(BF16)
    y_ref[...] = x + jnp.dot(mixed, wout_ref[...], preferred_element_type=F32)

    @pl.when(i == nblk - 1)
    def _():
        for s in range(ns):
            ps_ref[s] = uext[s, ts:ts + 16, :]
            ks_ref[s] = kext[s, padk + ts:padk + ts + WINDOW, :]
            vs_ref[s] = vext[s, padk + ts:padk + ts + WINDOW, :]


def _const_spec(shape):
    nd = len(shape)
    return pl.BlockSpec(shape, lambda i, _nd=nd: (0,) * _nd, pipeline_mode=pl.Buffered(1))


def _layer0(x2d, prefix, wts, *, ns, ts, chunk, base_pos):
    n = x2d.shape[0]
    tb = ns * ts
    nblk = n // tb
    has_cache = prefix is not None
    padk = SWA_KEYS - WINDOW - chunk
    rows4 = SWA_REP * chunk
    slope = np.repeat(2.0 ** (-(np.arange(SWA_HEADS) + 1.0)), chunk).reshape(SWA_KV_HEADS, 1, rows4)
    slope = jnp.asarray(np.broadcast_to(slope, (SWA_KV_HEADS, 8, rows4)).astype(np.float32))
    sink = jnp.broadcast_to(jnp.repeat(wts["sinks"].astype(F32), chunk).reshape(SWA_KV_HEADS, 1, rows4),
                            (SWA_KV_HEADS, 8, rows4))
    consts = [wts["g0"], wts["w_in0"], wts["w_pool"], wts["pool_scale"], wts["qg0"], wts["kg0"], wts["seg"],
              slope, sink, wts["w_out0"]]
    in_specs = [pl.BlockSpec((tb, D_MODEL), lambda i: (i, 0))]
    args = [x2d]
    if has_cache:
        for a in prefix:
            in_specs.append(_const_spec(a.shape))
            args.append(a)
    for a in consts:
        in_specs.append(_const_spec(a.shape))
        args.append(a)
    out_shape = [jax.ShapeDtypeStruct((n, D_MODEL), F32),
                 jax.ShapeDtypeStruct((ns, 16, C_POOL), F32),
                 jax.ShapeDtypeStruct((ns, WINDOW, LANES), F32),
                 jax.ShapeDtypeStruct((ns, WINDOW, LANES), F32)]
    out_specs = [pl.BlockSpec((tb, D_MODEL), lambda i: (i, 0)),
                 pl.BlockSpec((ns, 16, C_POOL), lambda i: (0, 0, 0)),
                 pl.BlockSpec((ns, WINDOW, LANES), lambda i: (0, 0, 0)),
                 pl.BlockSpec((ns, WINDOW, LANES), lambda i: (0, 0, 0))]
    scratch = [pltpu.VMEM((ns, 16 + ts, C_POOL), F32),
               pltpu.VMEM((ns, padk + WINDOW + ts, LANES), F32),
               pltpu.VMEM((ns, padk + WINDOW + ts, LANES), F32),
               pltpu.VMEM((tb, D_MODEL), F32)]
    kern = functools.partial(_l0_kernel, ns=ns, ts=ts, chunk=chunk, has_cache=has_cache,
                             base_pos=base_pos, nblk=nblk)
    return pl.pallas_call(
        kern, grid=(nblk,), in_specs=in_specs, out_specs=out_specs, out_shape=out_shape,
        scratch_shapes=scratch, compiler_params=_cparams(("arbitrary",)),
        name="layer0_prompt" if not has_cache else "layer0_sample")(*args)


def _log_sigmoid(z):
    return jnp.minimum(z, 0.0) - jnp.log1p(jnp.exp(-jnp.abs(z)))


def _fox_proj_sample_kernel(x_ref, g1_ref, wqt_ref, wk_ref, wvt_ref, wf_ref, bf_ref, qg_ref, kg_ref, seg_ref,
                            k_ref, v_ref, lf_ref, q_ref):
    x = x_ref[...]
    h = _rms(x, g1_ref[...]).astype(BF16)
    z = jnp.dot(h, wf_ref[...], preferred_element_type=F32) + bf_ref[...]
    lf_ref[...] = _log_sigmoid(z)[:, 0:FOX_HEADS]
    seg = seg_ref[...]
    nt = (((1,), (1,)), ((), ()))
    q = lax.dot_general(h, wqt_ref[...], nt, preferred_element_type=F32)
    k = jnp.dot(h, wk_ref[...], preferred_element_type=F32)
    v_ref[...] = lax.dot_general(h, wvt_ref[...], nt, preferred_element_type=F32)
    for j in range(FOX_HEADS // 2):
        cols = slice(LANES * j, LANES * (j + 1))
        q_ref[:, cols] = _head_rms_tile(q[:, cols], seg, qg_ref[:, cols]) * ATTN_SCALE
        k_ref[:, cols] = _head_rms_tile(k[:, cols], seg, kg_ref[:, cols])


def _fox_proj_sample(x2d, wts, *, tb):
    n = x2d.shape[0]
    consts = [wts["g1"], wts["w_q1t"], wts["w_k1"], wts["w_v1t"], wts["w_f"], wts["b_f"], wts["qg1"], wts["kg1"],
              wts["seg"]]
    row_spec = pl.BlockSpec((tb, D_MODEL), lambda i: (i, 0))
    in_specs = [row_spec] + [_const_spec(a.shape) for a in consts]
    row = jax.ShapeDtypeStruct((n, D_MODEL), F32)
    out_shape = [row, row, jax.ShapeDtypeStruct((n, FOX_HEADS), F32), row]
    out_specs = [row_spec, row_spec, pl.BlockSpec((tb, FOX_HEADS), lambda i: (i, 0)), row_spec]
    return pl.pallas_call(
        _fox_proj_sample_kernel, grid=(n // tb,), in_specs=in_specs, out_specs=out_specs, out_shape=out_shape,
        compiler_params=_cparams(("arbitrary",)), name="fox_proj_sample")(x2d, *consts)


def _fox_proj_prompt_kernel(x_ref, g1_ref, wk_ref, wf_ref, bf_ref, wqt_ref, wvt_ref, gq_ref, kg_ref,
                            seg_ref, tri_ref, e_ref,
                            k_ref, vt_ref, lf_ref, qa_ref, ka_ref, va_ref, ce_ref, carry, *, tb):
    i = pl.program_id(0)
    x = x_ref[...]
    hf = _rms(x, g1_ref[...])
    h = hf.astype(BF16)
    ht = hf.T.astype(BF16)
    logf = _log_sigmoid(jnp.dot(h, wf_ref[...], preferred_element_type=F32) + bf_ref[...])
    lf_ref[...] = logf[:, 0:FOX_HEADS]
    k = jnp.dot(h, wk_ref[...], preferred_element_type=F32)

    @pl.when(i == 0)
    def _():
        carry[...] = jnp.zeros(carry.shape, F32)

    tri = tri_ref[...]
    hi, mid, lo = _split3(logf)
    cum = (jnp.dot(tri, hi, preferred_element_type=F32) + jnp.dot(tri, mid, preferred_element_type=F32)
           + jnp.dot(tri, lo, preferred_element_type=F32)) + carry[0:1, :]
    carry[0:1, :] = cum[tb - 1:tb, :]
    ce_ref[0] = cum[tb - 1:tb, :]
    nh, nm, nl = _split3(-LOG2E * cum)
    biasall = jnp.dot(jnp.concatenate([nh, nm, nl], axis=1), e_ref[...], preferred_element_type=F32)
    lo_l = lax.broadcasted_iota(jnp.int32, (tb, LANES), 1) < HEAD_DIM
    seg = seg_ref[...]
    for j in range(FOX_HEADS // 2):
        cols = slice(LANES * j, LANES * (j + 1))
        kt = _head_rms_tile(k[:, cols], seg, kg_ref[:, cols])
        k_ref[:, cols] = kt
        bt = biasall[:, cols]
        ka_ref[2 * j] = jnp.where(lo_l, kt, bt).astype(BF16)
        ka_ref[2 * j + 1] = pltpu.roll(jnp.where(lo_l, bt, kt), HEAD_DIM, 1).astype(BF16)

    qt = jnp.dot(wqt_ref[...], ht, preferred_element_type=F32)
    vt = jnp.dot(wvt_ref[...], ht, preferred_element_type=F32)
    vt_ref[...] = vt
    srow = lax.broadcasted_iota(jnp.int32, (HEAD_DIM, tb), 0)
    q_aug = jnp.where(srow < 3, 1.0, 0.0)
    v_aug = jnp.where(srow == 0, 1.0, 0.0)
    gq = jnp.concatenate([gq_ref[...]] * (tb // LANES), axis=1) * (ATTN_SCALE * LOG2E)
    for hd in range(FOX_HEADS):
        rows = slice(HEAD_DIM * hd, HEAD_DIM * (hd + 1))
        qh = qt[rows]
        ss = jnp.sum(qh * qh, axis=0, keepdims=True)
        qn = qh * lax.rsqrt(ss * (1.0 / HEAD_DIM) + NORM_EPS) * gq
        qa_ref[hd] = jnp.concatenate([qn, q_aug], axis=0).astype(BF16)
        va_ref[hd] = jnp.concatenate([vt[rows], v_aug], axis=0).astype(BF16)


def _fox_proj_prompt(x2d, wts, *, tb):
    n = x2d.shape[0]
    tri = jnp.asarray(np.tril(np.ones((tb, tb), np.float32)), BF16)
    e = np.zeros((3 * LANES, FOX_HEADS // 2 * LANES), np.float32)
    for part in range(3):
        for hd in range(FOX_HEADS):
            e[part * LANES + hd, (hd // 2) * LANES + (HEAD_DIM if hd % 2 == 0 else 0) + part] = 1.0
    consts = [wts["g1"], wts["w_k1"], wts["w_f"], wts["b_f"], wts["w_q1t"], wts["w_v1t"],
              wts["gq1c"], wts["kg1"], wts["seg"], tri, jnp.asarray(e, BF16)]
    row_spec = pl.BlockSpec((tb, D_MODEL), lambda i: (i, 0))
    in_specs = [row_spec] + [_const_spec(a.shape) for a in consts]
    row = jax.ShapeDtypeStruct((n, D_MODEL), F32)
    out_shape = [row, jax.ShapeDtypeStruct((D_MODEL, n), F32), jax.ShapeDtypeStruct((n, FOX_HEADS), F32),
                 jax.ShapeDtypeStruct((FOX_HEADS, LANES, n), BF16), jax.ShapeDtypeStruct((FOX_HEADS, n, LANES), BF16),
                 jax.ShapeDtypeStruct((FOX_HEADS, LANES, n), BF16),
                 jax.ShapeDtypeStruct((n // tb, 1, LANES), F32)]
    tspec = pl.BlockSpec((FOX_HEADS, LANES, tb), lambda i: (0, 0, i))
    out_specs = [row_spec, pl.BlockSpec((D_MODEL, tb), lambda i: (0, i)),
                 pl.BlockSpec((tb, FOX_HEADS), lambda i: (i, 0)),
                 tspec, pl.BlockSpec((FOX_HEADS, tb, LANES), lambda i: (0, i, 0)), tspec,
                 pl.BlockSpec((1, 1, LANES), lambda i: (i, 0, 0))]
    return pl.pallas_call(
        functools.partial(_fox_proj_prompt_kernel, tb=tb), grid=(n // tb,), in_specs=in_specs,
        out_specs=out_specs, out_shape=out_shape, scratch_shapes=[pltpu.VMEM((8, LANES), F32)],
        compiler_params=_cparams(("arbitrary",)), name="fox_proj_prompt")(x2d, *consts)


def _fox_attn_kernel(ce_ref, qk_ref, q_ref, k_ref, v_ref, o_ref, m_scr, al_scr, acc_scr, s_scr,
                     *, tq, qpb, **kw):
    def one(qb, carry):
        _fox_attn_block(pl.program_id(0), pl.program_id(1) * qpb + qb, pl.multiple_of(qb * tq, tq),
                        ce_ref, qk_ref, q_ref, k_ref, v_ref, o_ref, m_scr, al_scr, acc_scr, s_scr, tq=tq, **kw)
        return carry

    lax.fori_loop(0, qpb, one, 0)


def _fox_attn_block(pair, qi, qoff, ce_ref, qk_ref, q_ref, k_ref, v_ref, o_ref, m_scr, al_scr, acc_scr, s_scr,
                    *, tq, tk, unroll, nkb, nh):
    n_u = qi
    m_scr[...] = jnp.full(m_scr.shape, NEG_INF, F32)
    acc_scr[...] = jnp.zeros(acc_scr.shape, F32)

    last = jnp.maximum(n_u - 1, 0)
    slack = 2.0 * qk_ref[0]

    def dead(j):
        jj = jnp.minimum(j, last)
        ok = j < n_u - 1
        for hh in range(nh):
            hd = nh * pair + hh
            gap = LOG2E * (ce_ref[last * FOX_HEADS + hd] - ce_ref[jj * FOX_HEADS + hd])
            ok = ok & (slack + gap < EXP2_ZERO)
        return ok

    lo = jnp.int32(0)
    hi = last
    for _ in range(max(1, (nkb - 1).bit_length())):
        mid = (lo + hi) >> 1
        d = dead(mid)
        lo = jnp.where(d, mid + 1, lo)
        hi = jnp.where(d, hi, mid)
    first = lo

    def score_matmuls(kb):
        off = pl.multiple_of(kb * tk, tk)
        return [jnp.dot(k_ref[hh, pl.ds(off, tk), :], q_ref[hh, :, pl.ds(qoff, tq)],
                        preferred_element_type=F32)
                for hh in range(nh)]

    def score_finish(sts):
        for hh in range(nh):
            st = sts[hh]
            m_old = m_scr[hh, 0:1, :]
            m_new = jnp.maximum(m_old, jnp.max(st, axis=0, keepdims=True))
            al_scr[hh, 0:1, :] = jnp.exp2(m_old - m_new)
            m_scr[hh, 0:1, :] = m_new
            s_scr[hh] = st

    def accumulate(kb):
        off = pl.multiple_of(kb * tk, tk)
        for hh in range(nh):
            p = jnp.exp2(s_scr[hh] - m_scr[hh, 0:1, :]).astype(BF16)
            pv = jnp.dot(v_ref[hh, :, pl.ds(off, tk)], p, preferred_element_type=F32)
            acc_scr[hh] = al_scr[hh, 0:1, :] * acc_scr[hh] + pv

    def advance(kb):
        sts = score_matmuls(kb + 1)
        accumulate(kb)
        score_finish(sts)

    def finish(have_prev):
        half = tq // 2
        off = pl.multiple_of(n_u * tk, tk)
        sts = [(jnp.dot(k_ref[hh, pl.ds(off, half), :], q_ref[hh, :, pl.ds(qoff, half)],
                        preferred_element_type=F32),
                jnp.dot(k_ref[hh, pl.ds(off, tk), :], q_ref[hh, :, pl.ds(pl.multiple_of(qoff + half, half), half)],
                        preferred_element_type=F32))
               for hh in range(nh)]
        if have_prev:
            accumulate(n_u - 1)
        tri = jnp.where(lax.broadcasted_iota(jnp.int32, (half, half), 0)
                        <= lax.broadcasted_iota(jnp.int32, (half, half), 1), 0.0, NEG_INF)
        for hh in range(nh):
            st_a, st_b = sts[hh]
            halves = ((st_a + tri, 0), (jnp.concatenate([st_b[0:half], st_b[half:tk] + tri], axis=0), half))
            for st, c0 in halves:
                nk = st.shape[0]
                m_old = m_scr[hh, 0:1, c0:c0 + half]
                m_new = jnp.maximum(m_old, jnp.max(st, axis=0, keepdims=True))
                p = jnp.exp2(st - m_new).astype(BF16)
                pv = jnp.dot(v_ref[hh, :, pl.ds(off, nk)], p, preferred_element_type=F32)
                acc_scr[hh, :, c0:c0 + half] = jnp.exp2(m_old - m_new) * acc_scr[hh, :, c0:c0 + half] + pv
        for t in range(nh // 2):
            pair_rows = []
            for hh in (2 * t, 2 * t + 1):
                a = acc_scr[hh]
                pair_rows.append(a[0:HEAD_DIM] / a[HEAD_DIM:HEAD_DIM + 1, :])
            o_ref[pl.ds(qoff, tq), LANES * t:LANES * (t + 1)] = jnp.concatenate(
                pair_rows, axis=0).T.astype(o_ref.dtype)

    @pl.when(qi == 0)
    def _():
        finish(False)

    @pl.when(qi > 0)
    def _():
        n_adv = n_u - 1 - first
        rem = lax.rem(n_adv, unroll)
        for k in range(unroll):
            @pl.when(rem == k)
            def _():
                score_finish(score_matmuls(first))
                for u in range(k):
                    advance(first + u)

        def body(t, carry):
            for u in range(unroll):
                advance(first + rem + unroll * t + u)
            return carry

        lax.fori_loop(0, n_adv // unroll, body, 0)
        finish(True)


def _fox_attn_prompt(cum_end, qk_bound, qa, ka, va, *, tq, tk, unroll, nh, qpb):
    n = ka.shape[1]
    assert tq == tk and n % (qpb * tq) == 0
    resident = dict(pipeline_mode=pl.Buffered(1)) if nh > 2 else {}
    grid_spec = pltpu.PrefetchScalarGridSpec(
        num_scalar_prefetch=2, grid=(FOX_HEADS // nh, n // (qpb * tq)),
        in_specs=[pl.BlockSpec((nh, LANES, qpb * tq), lambda p, i, ce, qk: (p, 0, i)),
                  pl.BlockSpec((nh, n, LANES), lambda p, i, ce, qk: (p, 0, 0), **resident),
                  pl.BlockSpec((nh, LANES, n), lambda p, i, ce, qk: (p, 0, 0), **resident)],
        out_specs=pl.BlockSpec((qpb * tq, HEAD_DIM * nh), lambda p, i, ce, qk: (i, p)),
        scratch_shapes=[pltpu.VMEM((nh, 8, tq), F32), pltpu.VMEM((nh, 8, tq), F32),
                        pltpu.VMEM((nh, LANES, tq), F32), pltpu.VMEM((nh, tk, tq), F32)])
    return pl.pallas_call(
        functools.partial(_fox_attn_kernel, tq=tq, tk=tk, unroll=unroll, nkb=n // tk, nh=nh, qpb=qpb),
        grid_spec=grid_spec,
        out_shape=jax.ShapeDtypeStruct((n, D_MODEL), BF16),
        compiler_params=_cparams(("arbitrary", "arbitrary")), name="fox_attn_prompt")(
            cum_end, qk_bound, qa, ka, va)


CUM_BLK = 512
NEW_PAD = 128


def _fox_sample_kernel(q_ref, kc_ref, vc_ref, lfc_ref, kn_ref, vn_ref, lfn_ref, tri_ref, o_ref, negc, negn,
                       *, t_new, p_len):
    qd = pl.program_id(1)
    rows = 4 * t_new

    @pl.when(qd == 0)
    def _():
        tri = tri_ref[...]

        def prefix_sums(x, t):
            c3 = jnp.dot(jnp.concatenate(_split3(x), axis=0), t, preferred_element_type=F32)
            return c3[0:FOX_HEADS] + c3[FOX_HEADS:2 * FOX_HEADS] + c3[2 * FOX_HEADS:3 * FOX_HEADS]

        carry = jnp.zeros((FOX_HEADS, 1), F32)
        for b in range(p_len // CUM_BLK):
            c = prefix_sums(lfc_ref[0, :, b * CUM_BLK:(b + 1) * CUM_BLK], tri) + carry
            negc[:, b * CUM_BLK:(b + 1) * CUM_BLK] = -c
            carry = c[:, CUM_BLK - 1:CUM_BLK]
        negn[...] = -(prefix_sums(lfn_ref[0], tri[0:NEW_PAD, 0:NEW_PAD]) + carry)

    qq = q_ref[0]
    rhead = lax.broadcasted_iota(jnp.int32, (rows, 2 * LANES), 0) // t_new
    lhead = lax.broadcasted_iota(jnp.int32, (rows, 2 * LANES), 1) // HEAD_DIM
    qbd = jnp.where(rhead == lhead, jnp.concatenate([qq] * 4, axis=0), 0.0).astype(BF16)

    kct = kc_ref[0].reshape(2 * LANES, p_len).astype(BF16)
    vct = vc_ref[0].reshape(2 * LANES, p_len).astype(BF16)
    s_c = jnp.dot(qbd, kct, preferred_element_type=F32)
    zpad = jnp.zeros((NEW_PAD - t_new, 2 * LANES), F32)
    kn = jnp.concatenate([kn_ref[0], zpad], axis=0).astype(BF16)
    vn = jnp.concatenate([vn_ref[0], zpad], axis=0).astype(BF16)
    s_n = lax.dot_general(qbd, kn, (((1,), (1,)), ((), ())), preferred_element_type=F32)
    bc = jnp.concatenate([jnp.broadcast_to(negc[pl.ds(4 * qd + jh, 1), :], (t_new, p_len)) for jh in range(4)], axis=0)
    bn = jnp.concatenate([jnp.broadcast_to(negn[pl.ds(4 * qd + jh, 1), :], (t_new, NEW_PAD)) for jh in range(4)], axis=0)
    s_c = s_c + bc
    qrow = lax.broadcasted_iota(jnp.int32, (rows, NEW_PAD), 0) % t_new
    kcol = lax.broadcasted_iota(jnp.int32, (rows, NEW_PAD), 1)
    s_n = jnp.where(kcol <= qrow, s_n + bn, NEG_INF)
    m = jnp.maximum(jnp.max(s_c, axis=-1, keepdims=True), jnp.max(s_n, axis=-1, keepdims=True))
    p_c = jnp.exp(s_c - m)
    p_n = jnp.exp(s_n - m)
    den = jnp.sum(p_c, axis=-1, keepdims=True) + jnp.sum(p_n, axis=-1, keepdims=True)
    o = (lax.dot_general(p_c.astype(BF16), vct, (((1,), (1,)), ((), ())), preferred_element_type=F32)
         + jnp.dot(p_n.astype(BF16), vn, preferred_element_type=F32)) / den
    lh = lax.broadcasted_iota(jnp.int32, (t_new, 2 * LANES), 1) // HEAD_DIM
    out = jnp.zeros((t_new, 2 * LANES), F32)
    for jh in range(4):
        out = jnp.where(lh == jh, o[jh * t_new:(jh + 1) * t_new], out)
    o_ref[0] = out


def _fox_attn_sample(q, cache_k, cache_v, lfc_t, k_new, v_new, lfn_t):
    b, t_new, _ = q.shape
    p_len = cache_k.shape[3]
    tri = jnp.asarray(np.triu(np.ones((CUM_BLK, CUM_BLK), np.float32)), BF16)
    quad = 2 * LANES
    in_specs = [pl.BlockSpec((1, t_new, quad), lambda r, d: (r, 0, d)),
                pl.BlockSpec((1, 4, HEAD_DIM, p_len), lambda r, d: (r, d, 0, 0)),
                pl.BlockSpec((1, 4, HEAD_DIM, p_len), lambda r, d: (r, d, 0, 0)),
                pl.BlockSpec((1, FOX_HEADS, p_len), lambda r, d: (r, 0, 0)),
                pl.BlockSpec((1, t_new, quad), lambda r, d: (r, 0, d)),
                pl.BlockSpec((1, t_new, quad), lambda r, d: (r, 0, d)),
                pl.BlockSpec((1, FOX_HEADS, NEW_PAD), lambda r, d: (r, 0, 0)),
                pl.BlockSpec((CUM_BLK, CUM_BLK), lambda r, d: (0, 0))]
    return pl.pallas_call(
        functools.partial(_fox_sample_kernel, t_new=t_new, p_len=p_len),
        grid=(b, D_MODEL // quad), in_specs=in_specs,
        out_specs=pl.BlockSpec((1, t_new, quad), lambda r, d: (r, 0, d)),
        out_shape=jax.ShapeDtypeStruct((b, t_new, D_MODEL), F32),
        scratch_shapes=[pltpu.VMEM((FOX_HEADS, p_len), F32), pltpu.VMEM((FOX_HEADS, NEW_PAD), F32)],
        compiler_params=_cparams(("arbitrary", "arbitrary")), name="fox_attn_sample")(
            q, cache_k, cache_v, lfc_t, k_new, v_new, lfn_t, tri)


def _out_proj_kernel(r_ref, a_ref, g1_ref, wg_ref, w_ref, y_ref):
    x = r_ref[...]
    h = _rms(x, g1_ref[...]).astype(BF16)
    gate = jnp.dot(h, wg_ref[...], preferred_element_type=F32)
    mixed = (a_ref[...].astype(F32) * _silu(gate)).astype(BF16)
    y_ref[...] = x + jnp.dot(mixed, w_ref[...], preferred_element_type=F32)


def _out_proj(resid, attn, wts, *, tb):
    n = resid.shape[0]
    row = pl.BlockSpec((tb, D_MODEL), lambda i: (i, 0))
    consts = [wts["g1"], wts["w_g1"], wts["w_out1"]]
    return pl.pallas_call(
        _out_proj_kernel, grid=(n // tb,), in_specs=[row, row] + [_const_spec(a.shape) for a in consts],
        out_specs=row, out_shape=jax.ShapeDtypeStruct((n, D_MODEL), F32),
        compiler_params=_cparams(("arbitrary",)), name="out_proj")(resid, attn, *consts)


L0_TB = 512
FOX_TB = 256
ATTN_TQ = 512
ATTN_TK = 512
ATTN_UNROLL = 4
ATTN_HEADS = 2
ATTN_QPB = 8
SWA_CHUNK = 64


def kernel(x_prompt, x_sample, state_pool, cache_swa_k, cache_swa_v, cache_fox_k, cache_fox_v, cache_fox_logf,
           norm0_g, w_in0, w_pool, pool_scale, swa_qn_g, swa_kn_g, swa_sinks, w_out0,
           norm1_g, w_in1, b_forget, fox_qn_g, fox_kn_g, w_out1):
    nb, seq, _ = x_prompt.shape
    db, dseq, _ = x_sample.shape
    past_len = cache_fox_k.shape[1]
    assert nb == 1 and seq % L0_TB == 0 and seq % ATTN_TQ == 0 and dseq % 8 == 0

    seg = np.kron(np.eye(2, dtype=np.float32), np.ones((HEAD_DIM, HEAD_DIM), np.float32))
    mc = FOX_HEADS * HEAD_DIM
    wts = {
        "g0": norm0_g.reshape(1, D_MODEL), "w_in0": w_in0.astype(BF16), "w_pool": w_pool.astype(BF16),
        "pool_scale": pool_scale.reshape(1, C_POOL),
        "qg0": jnp.tile(swa_qn_g, SWA_HEADS).reshape(1, SWA_HEADS * HEAD_DIM),
        "kg0": jnp.tile(swa_kn_g, SWA_KV_HEADS).reshape(1, LANES),
        "seg": jnp.asarray(seg, BF16), "sinks": swa_sinks, "w_out0": w_out0.astype(BF16),
        "g1": norm1_g.reshape(1, D_MODEL),
        "w_out1": w_out1.astype(BF16),
        "w_f": jnp.pad(w_in1[:, 4 * mc:], ((0, 0), (0, LANES - FOX_HEADS))).astype(BF16),
        "b_f": jnp.pad(b_forget.astype(F32), (0, LANES - FOX_HEADS)).reshape(1, LANES),
        "qg1": jnp.tile(fox_qn_g, FOX_HEADS).reshape(1, mc), "kg1": jnp.tile(fox_kn_g, FOX_HEADS).reshape(1, mc),
        "w_k1": w_in1[:, mc:2 * mc].astype(BF16), "w_g1": w_in1[:, 3 * mc:4 * mc].astype(BF16),
        "w_q1t": w_in1[:, 0:mc].T.astype(BF16), "w_v1t": w_in1[:, 2 * mc:3 * mc].T.astype(BF16),
        "gq1c": jnp.broadcast_to(fox_qn_g.astype(F32).reshape(HEAD_DIM, 1), (HEAD_DIM, LANES)),
    }

    xp = x_prompt.reshape(seq, D_MODEL)
    y0p, pool_p, swk_p, swv_p = _layer0(xp, None, wts, ns=1, ts=L0_TB, chunk=SWA_CHUNK, base_pos=0)
    fk_p, fvt_p, fl_p, qa, ka, va, cum_end = _fox_proj_prompt(y0p, wts, tb=FOX_TB)
    fv_p = jnp.transpose(fvt_p.reshape(FOX_HEADS, HEAD_DIM, seq), (2, 0, 1))
    per = ATTN_TK // FOX_TB
    cum_end = cum_end[per - 1::per, 0, :FOX_HEADS].reshape(-1)
    qk_bound = (1.02 * LOG2E * HEAD_DIM * ATTN_SCALE * jnp.max(jnp.abs(fox_qn_g)) * jnp.max(jnp.abs(fox_kn_g))
                ).astype(F32).reshape(1)
    attn_p = _fox_attn_prompt(cum_end, qk_bound, qa, ka, va, tq=ATTN_TQ, tk=ATTN_TK, unroll=ATTN_UNROLL,
                              nh=ATTN_HEADS, qpb=ATTN_QPB)
    yp = _out_proj(y0p, attn_p, wts, tb=L0_TB)

    xs = x_sample.reshape(db * dseq, D_MODEL)
    prefix = (jnp.pad(state_pool, ((0, 0), (1, 0), (0, 0))),
              cache_swa_k.reshape(db, WINDOW, LANES), cache_swa_v.reshape(db, WINDOW, LANES))
    y0s, pool_s, swk_s, swv_s = _layer0(xs, prefix, wts, ns=db, ts=dseq, chunk=dseq, base_pos=past_len)
    fk_s, fv_s, fl_s, q_s = _fox_proj_sample(y0s, wts, tb=db * dseq)
    attn_s = _fox_attn_sample(
        q_s.reshape(db, dseq, D_MODEL),
        jnp.transpose(cache_fox_k, (0, 2, 3, 1)), jnp.transpose(cache_fox_v, (0, 2, 3, 1)),
        jnp.transpose(cache_fox_logf, (0, 2, 1)),
        fk_s.reshape(db, dseq, D_MODEL), fv_s.reshape(db, dseq, D_MODEL),
        jnp.pad(jnp.transpose(fl_s.reshape(db, dseq, FOX_HEADS), (0, 2, 1)), ((0, 0), (0, 0), (0, NEW_PAD - dseq))))
    ys = _out_proj(y0s, attn_s.reshape(db * dseq, D_MODEL), wts, tb=db * dseq)

    return (yp.reshape(1, seq, D_MODEL), ys.reshape(db, dseq, D_MODEL),
            pool_p[:, 1:], pool_s[:, 1:],
            swk_p.reshape(1, WINDOW, SWA_KV_HEADS, HEAD_DIM), swv_p.reshape(1, WINDOW, SWA_KV_HEADS, HEAD_DIM),
            swk_s.reshape(db, WINDOW, SWA_KV_HEADS, HEAD_DIM), swv_s.reshape(db, WINDOW, SWA_KV_HEADS, HEAD_DIM),
            fk_p.reshape(1, seq, FOX_HEADS, HEAD_DIM), fv_p.reshape(1, seq, FOX_HEADS, HEAD_DIM),
            fl_p.reshape(1, seq, FOX_HEADS),
            fk_s.reshape(db, dseq, FOX_HEADS, HEAD_DIM), fv_s.reshape(db, dseq, FOX_HEADS, HEAD_DIM),
            fl_s.reshape(db, dseq, FOX_HEADS))
```

```python
import functools

import numpy as np
import jax
import jax.numpy as jnp
from jax import lax
from jax.experimental import pallas as pl
from jax.experimental.pallas import tpu as pltpu

F32 = jnp.float32
BF16 = jnp.bfloat16

D_MODEL = 1024
HEAD_DIM = 64
ATTN_SCALE = HEAD_DIM ** -0.5
POOL_WINDOWS = (2, 4, 8, 16)
C_POOL = 512
POOL_PAD = 15
SWA_HEADS = 8
SWA_KV_HEADS = 2
SWA_REP = 4
WINDOW = 128
FOX_HEADS = 16
NORM_EPS = 1e-6
NEG_INF = -1e30
LOG2E = 1.4426950408889634
EXP2_ZERO = -152.0

LANES = 128
SWA_KEYS = 256
VMEM_LIMIT = 56 * 1024 * 1024


def _cparams(sem):
    return pltpu.CompilerParams(dimension_semantics=sem, vmem_limit_bytes=VMEM_LIMIT)


def _rms(x, g):
    ms = jnp.mean(x * x, axis=-1, keepdims=True)
    return x * lax.rsqrt(ms + NORM_EPS) * g


def _split3(x):
    hi = x.astype(BF16)
    r = x - hi.astype(F32)
    mid = r.astype(BF16)
    lo = (r - mid.astype(F32)).astype(BF16)
    return hi, mid, lo


def _head_rms_tile(x, seg, g):
    ss = jnp.dot((x * x).astype(BF16), seg, preferred_element_type=F32)
    return x * lax.rsqrt(ss * (1.0 / HEAD_DIM) + NORM_EPS) * g


def _silu(g):
    return g / (1.0 + jnp.exp(-g))


def _l0_kernel(*refs, ns, ts, chunk, has_cache, base_pos, nblk):
    if has_cache:
        (x_ref, pp_ref, kp_ref, vp_ref, g0_ref, win_ref, wpool_ref, pscale_ref, qg_ref, kg_ref, seg_ref,
         slope_ref, sink_ref, wout_ref, y_ref, ps_ref, ks_ref, vs_ref, uext, kext, vext, mix) = refs
    else:
        (x_ref, g0_ref, win_ref, wpool_ref, pscale_ref, qg_ref, kg_ref, seg_ref,
         slope_ref, sink_ref, wout_ref, y_ref, ps_ref, ks_ref, vs_ref, uext, kext, vext, mix) = refs
    padk = SWA_KEYS - WINDOW - chunk
    hist = padk + WINDOW
    i = pl.program_id(0)

    x = x_ref[...]
    h = _rms(x, g0_ref[...]).astype(BF16)
    proj = jnp.dot(h, win_ref[...], preferred_element_type=F32)
    u = proj[:, 0:C_POOL]
    gate = proj[:, 1280:2304]
    seg = seg_ref[...]
    qn = [_head_rms_tile(proj[:, 512 + LANES * j:512 + LANES * (j + 1)], seg,
                         qg_ref[:, LANES * j:LANES * (j + 1)]) * (ATTN_SCALE * LOG2E) for j in range(4)]
    kn = _head_rms_tile(proj[:, 1024:1152], seg, kg_ref[...])
    v = proj[:, 1152:1280]

    if has_cache:
        for s in range(ns):
            uext[s, 0:16, :] = pp_ref[s]
            kext[s, 0:padk, :] = jnp.zeros((padk, LANES), F32)
            vext[s, 0:padk, :] = jnp.zeros((padk, LANES), F32)
            kext[s, padk:hist, :] = kp_ref[s]
            vext[s, padk:hist, :] = vp_ref[s]
    else:
        @pl.when(i == 0)
        def _():
            uext[0, 0:16, :] = jnp.zeros((16, C_POOL), F32)
            kext[0, 0:hist, :] = jnp.zeros((hist, LANES), F32)
            vext[0, 0:hist, :] = jnp.zeros((hist, LANES), F32)

        @pl.when(i > 0)
        def _():
            uext[0, 0:16, :] = uext[0, ts:ts + 16, :]
            kext[0, 0:hist, :] = kext[0, ts:ts + hist, :]
            vext[0, 0:hist, :] = vext[0, ts:ts + hist, :]

    for s in range(ns):
        uext[s, 16:16 + ts, :] = u[s * ts:(s + 1) * ts]
        kext[s, hist:hist + ts, :] = kn[s * ts:(s + 1) * ts]
        vext[s, hist:hist + ts, :] = v[s * ts:(s + 1) * ts]

    pos = base_pos + i * ts + lax.broadcasted_iota(jnp.int32, (ts, LANES), 0)
    for s in range(ns):
        for g, w in enumerate(POOL_WINDOWS):
            cols = slice(LANES * g, LANES * (g + 1))
            acc = uext[s, 16:16 + ts, cols]
            cur = acc
            for j in range(1, w):
                acc = acc + uext[s, 16 - j:16 - j + ts, cols]
            cnt = jnp.minimum(pos + 1, w).astype(F32)
            mix[s * ts:(s + 1) * ts, cols] = acc / cnt - cur
    for g in range(4):
        cols = slice(LANES * g, LANES * (g + 1))
        d = mix[:, cols].astype(BF16)
        mix[:, cols] = jnp.dot(d, wpool_ref[g], preferred_element_type=F32) * pscale_ref[:, cols]

    rows4 = SWA_REP * chunk
    lo_c = lax.broadcasted_iota(jnp.int32, (chunk, LANES), 1) < HEAD_DIM
    lo_k = lax.broadcasted_iota(jnp.int32, (SWA_KEYS, LANES), 1) < HEAD_DIM
    kj = lax.broadcasted_iota(jnp.int32, (SWA_KEYS, rows4), 0)
    qi = lax.broadcasted_iota(jnp.int32, (SWA_KEYS, rows4), 1) % chunk
    absrel = jnp.abs(qi + WINDOW - (kj - padk)).astype(F32)
    bias = []
    sinkc = []
    for g in range(SWA_KV_HEADS):
        bias.append(jnp.where(kj >= padk, -(slope_ref[g][0:1, :] * LOG2E) * absrel, NEG_INF))
        sinkc.append(sink_ref[g][0:1, :] * LOG2E)
    nch = ts // chunk
    units = [(s, c, g) for s in range(ns) for c in range(nch) for g in range(SWA_KV_HEADS)]

    def swa_scores(s, c, g):
        r0 = c * chunk
        rows = slice(s * ts + r0, s * ts + r0 + chunk)
        kwin = kext[s, r0:r0 + SWA_KEYS, :]
        vwin = vext[s, r0:r0 + SWA_KEYS, :]
        krl = pltpu.roll(kwin, HEAD_DIM, 1)
        vrl = pltpu.roll(vwin, HEAD_DIM, 1)
        if g == 0:
            kd = jnp.where(lo_k, kwin, krl).astype(BF16)
            vd = jnp.where(lo_k, vwin, vrl).astype(BF16)
        else:
            kd = jnp.where(lo_k, krl, kwin).astype(BF16)
            vd = jnp.where(lo_k, vrl, vwin).astype(BF16)
        q0 = qn[2 * g][rows]
        q1 = qn[2 * g + 1][rows]
        qs = jnp.concatenate([jnp.where(lo_c, q0, 0.0), jnp.where(lo_c, 0.0, q0),
                              jnp.where(lo_c, q1, 0.0), jnp.where(lo_c, 0.0, q1)], axis=0).astype(BF16)
        return lax.dot_general(kd, qs, (((1,), (1,)), ((), ())), preferred_element_type=F32), vd

    def swa_finish(s, c, g, sc, vd):
        rows = slice(s * ts + c * chunk, s * ts + (c + 1) * chunk)
        sc = sc + bias[g]
        if not has_cache and c * chunk < WINDOW:
            sc = jnp.where(kj >= padk + jnp.maximum(0, WINDOW - (i * nch + c) * chunk), sc, NEG_INF)
        m = jnp.maximum(jnp.max(sc, axis=0, keepdims=True), sinkc[g])
        p = jnp.exp2(sc - m)
        den = jnp.sum(p, axis=0, keepdims=True) + jnp.exp2(sinkc[g] - m)
        wgt = (p / den).astype(BF16)
        o = lax.dot_general(wgt, vd, (((0,), (0,)), ((), ())), preferred_element_type=F32)
        for jj in range(2):
            t = 2 * g + jj
            mix[rows, C_POOL + LANES * t:C_POOL + LANES * (t + 1)] = jnp.where(
                lo_c, o[(2 * jj) * chunk:(2 * jj + 1) * chunk], o[(2 * jj + 1) * chunk:(2 * jj + 2) * chunk])

    pending = swa_scores(*units[0])
    for n, unit in enumerate(units):
        nxt = swa_scores(*units[n + 1]) if n + 1 < len(units) else None
        swa_finish(*unit, *pending)
        pending = nxt

    mixed = (mix[...] * _silu(gate)).astype(BF16)
    y_ref[...] = x + jnp.dot(mixed, wout_ref[...], preferred_element_type=F32)

    @pl.when(i == nblk - 1)
    def _():
        for s in range(ns):
            ps_ref[s] = uext[s, ts:ts + 16, :]
            ks_ref[s] = kext[s, padk + ts:padk + ts + WINDOW, :]
            vs_ref[s] = vext[s, padk + ts:padk + ts + WINDOW, :]


def _const_spec(shape):
    nd = len(shape)
    return pl.BlockSpec(shape, lambda i, _nd=nd: (0,) * _nd, pipeline_mode=pl.Buffered(1))


def _layer0(x2d, prefix, wts, *, ns, ts, chunk, base_pos):
    n = x2d.shape[0]
    tb = ns * ts
    nblk = n // tb
    has_cache = prefix is not None
    padk = SWA_KEYS - WINDOW - chunk
    rows4 = SWA_REP * chunk
    slope = np.repeat(2.0 ** (-(np.arange(SWA_HEADS) + 1.0)), chunk).reshape(SWA_KV_HEADS, 1, rows4)
    slope = jnp.asarray(np.broadcast_to(slope, (SWA_KV_HEADS, 8, rows4)).astype(np.float32))
    sink = jnp.broadcast_to(jnp.repeat(wts["sinks"].astype(F32), chunk).reshape(SWA_KV_HEADS, 1, rows4),
                            (SWA_KV_HEADS, 8, rows4))
    consts = [wts["g0"], wts["w_in0"], wts["w_pool"], wts["pool_scale"], wts["qg0"], wts["kg0"], wts["seg"],
              slope, sink, wts["w_out0"]]
    in_specs = [pl.BlockSpec((tb, D_MODEL), lambda i: (i, 0))]
    args = [x2d]
    if has_cache:
        for a in prefix:
            in_specs.append(_const_spec(a.shape))
            args.append(a)
    for a in consts:
        in_specs.append(_const_spec(a.shape))
        args.append(a)
    out_shape = [jax.ShapeDtypeStruct((n, D_MODEL), F32),
                 jax.ShapeDtypeStruct((ns, 16, C_POOL), F32),
                 jax.ShapeDtypeStruct((ns, WINDOW, LANES), F32),
                 jax.ShapeDtypeStruct((ns, WINDOW, LANES), F32)]
    out_specs = [pl.BlockSpec((tb, D_MODEL), lambda i: (i, 0)),
                 pl.BlockSpec((ns, 16, C_POOL), lambda i: (0, 0, 0)),
                 pl.BlockSpec((ns, WINDOW, LANES), lambda i: (0, 0, 0)),
                 pl.BlockSpec((ns, WINDOW, LANES), lambda i: (0, 0, 0))]
    scratch = [pltpu.VMEM((ns, 16 + ts, C_POOL), F32),
               pltpu.VMEM((ns, padk + WINDOW + ts, LANES), F32),
               pltpu.VMEM((ns, padk + WINDOW + ts, LANES), F32),
               pltpu.VMEM((tb, D_MODEL), F32)]
    kern = functools.partial(_l0_kernel, ns=ns, ts=ts, chunk=chunk, has_cache=has_cache,
                             base_pos=base_pos, nblk=nblk)
    return pl.pallas_call(
        kern, grid=(nblk,), in_specs=in_specs, out_specs=out_specs, out_shape=out_shape,
        scratch_shapes=scratch, compiler_params=_cparams(("arbitrary",)),
        name="layer0_prompt" if not has_cache else "layer0_sample")(*args)


def _log_sigmoid(z):
    return jnp.minimum(z, 0.0) - jnp.log1p(jnp.exp(-jnp.abs(z)))


def _fox_proj_sample_kernel(x_ref, g1_ref, wqt_ref, wk_ref, wvt_ref, wf_ref, bf_ref, qg_ref, kg_ref, seg_ref,
                            k_ref, v_ref, lf_ref, q_ref):
    x = x_ref[...]
    h = _rms(x, g1_ref[...]).astype(BF16)
    z = jnp.dot(h, wf_ref[...], preferred_element_type=F32) + bf_ref[...]
    lf_ref[...] = _log_sigmoid(z)[:, 0:FOX_HEADS]
    seg = seg_ref[...]
    nt = (((1,), (1,)), ((), ()))
    q = lax.dot_general(h, wqt_ref[...], nt, preferred_element_type=F32)
    k = jnp.dot(h, wk_ref[...], preferred_element_type=F32)
    v_ref[...] = lax.dot_general(h, wvt_ref[...], nt, preferred_element_type=F32)
    for j in range(FOX_HEADS // 2):
        cols = slice(LANES * j, LANES * (j + 1))
        q_ref[:, cols] = _head_rms_tile(q[:, cols], seg, qg_ref[:, cols]) * ATTN_SCALE
        k_ref[:, cols] = _head_rms_tile(k[:, cols], seg, kg_ref[:, cols])


def _fox_proj_sample(x2d, wts, *, tb):
    n = x2d.shape[0]
    consts = [wts["g1"], wts["w_q1t"], wts["w_k1"], wts["w_v1t"], wts["w_f"], wts["b_f"], wts["qg1"], wts["kg1"],
              wts["seg"]]
    row_spec = pl.BlockSpec((tb, D_MODEL), lambda i: (i, 0))
    in_specs = [row_spec] + [_const_spec(a.shape) for a in consts]
    row = jax.ShapeDtypeStruct((n, D_MODEL), F32)
    out_shape = [row, row, jax.ShapeDtypeStruct((n, FOX_HEADS), F32), row]
    out_specs = [row_spec, row_spec, pl.BlockSpec((tb, FOX_HEADS), lambda i: (i, 0)), row_spec]
    return pl.pallas_call(
        _fox_proj_sample_kernel, grid=(n // tb,), in_specs=in_specs, out_specs=out_specs, out_shape=out_shape,
        compiler_params=_cparams(("arbitrary",)), name="fox_proj_sample")(x2d, *consts)


def _fox_proj_prompt_kernel(x_ref, g1_ref, wk_ref, wft_ref, bfc_ref, wqt_ref, wvt_ref, gq_ref, kg_ref,
                            seg_ref, tri_ref, e_ref,
                            k_ref, vt_ref, lft_ref, qa_ref, ka_ref, va_ref, ce_ref, carry, *, tb):
    i = pl.program_id(0)
    x = x_ref[...]
    hf = _rms(x, g1_ref[...])
    h = hf.astype(BF16)
    ht = hf.T.astype(BF16)
    bcol = jnp.concatenate([bfc_ref[...]] * (tb // LANES), axis=1)
    logft = _log_sigmoid(jnp.dot(wft_ref[...], ht, preferred_element_type=F32) + bcol)
    lft_ref[...] = logft
    k = jnp.dot(h, wk_ref[...], preferred_element_type=F32)

    @pl.when(i == 0)
    def _():
        carry[...] = jnp.zeros(carry.shape, F32)

    c3 = jnp.dot(jnp.concatenate(_split3(logft), axis=0), tri_ref[...], preferred_element_type=F32)

    qt = jnp.dot(wqt_ref[...], ht, preferred_element_type=F32)
    vt = jnp.dot(wvt_ref[...], ht, preferred_element_type=F32)
    vt_ref[...] = vt
    srow = lax.broadcasted_iota(jnp.int32, (HEAD_DIM, tb), 0)
    q_aug = jnp.where(srow < 3, 1.0, 0.0)
    v_aug = jnp.where(srow == 0, 1.0, 0.0)
    gq = jnp.concatenate([gq_ref[...]] * (tb // LANES), axis=1) * (ATTN_SCALE * LOG2E)
    for hd in range(FOX_HEADS):
        rows = slice(HEAD_DIM * hd, HEAD_DIM * (hd + 1))
        qh = qt[rows]
        ss = jnp.sum(qh * qh, axis=0, keepdims=True)
        qn = qh * lax.rsqrt(ss * (1.0 / HEAD_DIM) + NORM_EPS) * gq
        qa_ref[hd] = jnp.concatenate([qn, q_aug], axis=0).astype(BF16)
        va_ref[hd] = jnp.concatenate([vt[rows], v_aug], axis=0).astype(BF16)

    cumt = c3[0:FOX_HEADS] + c3[FOX_HEADS:2 * FOX_HEADS] + c3[2 * FOX_HEADS:3 * FOX_HEADS] + carry[:, 0:1]
    end = jnp.broadcast_to(cumt[:, tb - 1:tb], (FOX_HEADS, LANES))
    carry[...] = end
    ce_ref[0] = end
    parts = [p.astype(F32) for p in _split3(-LOG2E * cumt)]
    n3 = jnp.concatenate(parts + [jnp.zeros((LANES - 3 * FOX_HEADS, tb), F32)], axis=0).T.astype(BF16)
    biasall = jnp.dot(n3, e_ref[...], preferred_element_type=F32)
    lo_l = lax.broadcasted_iota(jnp.int32, (tb, LANES), 1) < HEAD_DIM
    seg = seg_ref[...]
    for j in range(FOX_HEADS // 2):
        cols = slice(LANES * j, LANES * (j + 1))
        kt = _head_rms_tile(k[:, cols], seg, kg_ref[:, cols])
        k_ref[:, cols] = kt
        bt = biasall[:, cols]
        ka_ref[2 * j] = jnp.where(lo_l, kt, bt).astype(BF16)
        ka_ref[2 * j + 1] = pltpu.roll(jnp.where(lo_l, bt, kt), HEAD_DIM, 1).astype(BF16)


def _fox_proj_prompt(x2d, wts, *, tb):
    n = x2d.shape[0]
    tri = jnp.asarray(np.triu(np.ones((tb, tb), np.float32)), BF16)
    e = np.zeros((LANES, FOX_HEADS // 2 * LANES), np.float32)
    for part in range(3):
        for hd in range(FOX_HEADS):
            e[part * FOX_HEADS + hd, (hd // 2) * LANES + (HEAD_DIM if hd % 2 == 0 else 0) + part] = 1.0
    consts = [wts["g1"], wts["w_k1"], wts["w_ft"], wts["b_fc"], wts["w_q1t"], wts["w_v1t"],
              wts["gq1c"], wts["kg1"], wts["seg"], tri, jnp.asarray(e, BF16)]
    row_spec = pl.BlockSpec((tb, D_MODEL), lambda i: (i, 0))
    in_specs = [row_spec] + [_const_spec(a.shape) for a in consts]
    row = jax.ShapeDtypeStruct((n, D_MODEL), F32)
    out_shape = [row, jax.ShapeDtypeStruct((D_MODEL, n), F32), jax.ShapeDtypeStruct((FOX_HEADS, n), F32),
                 jax.ShapeDtypeStruct((FOX_HEADS, LANES, n), BF16), jax.ShapeDtypeStruct((FOX_HEADS, n, LANES), BF16),
                 jax.ShapeDtypeStruct((FOX_HEADS, LANES, n), BF16),
                 jax.ShapeDtypeStruct((n // tb, FOX_HEADS, LANES), F32)]
    tspec = pl.BlockSpec((FOX_HEADS, LANES, tb), lambda i: (0, 0, i))
    out_specs = [row_spec, pl.BlockSpec((D_MODEL, tb), lambda i: (0, i)),
                 pl.BlockSpec((FOX_HEADS, tb), lambda i: (0, i)),
                 tspec, pl.BlockSpec((FOX_HEADS, tb, LANES), lambda i: (0, i, 0)), tspec,
                 pl.BlockSpec((1, FOX_HEADS, LANES), lambda i: (i, 0, 0))]
    return pl.pallas_call(
        functools.partial(_fox_proj_prompt_kernel, tb=tb), grid=(n // tb,), in_specs=in_specs,
        out_specs=out_specs, out_shape=out_shape, scratch_shapes=[pltpu.VMEM((FOX_HEADS, LANES), F32)],
        compiler_params=_cparams(("arbitrary",)), name="fox_proj_prompt")(x2d, *consts)


def _fox_attn_kernel(ce_ref, qk_ref, q_ref, k_ref, v_ref, o_ref, m_scr, al_scr, acc_scr, s_scr,
                     *, tq, qpb, **kw):
    def one(qb, carry):
        _fox_attn_block(pl.program_id(0), pl.program_id(1) * qpb + qb, pl.multiple_of(qb * tq, tq),
                        ce_ref, qk_ref, q_ref, k_ref, v_ref, o_ref, m_scr, al_scr, acc_scr, s_scr, tq=tq, **kw)
        return carry

    lax.fori_loop(0, qpb, one, 0)


def _fox_attn_block(pair, qi, qoff, ce_ref, qk_ref, q_ref, k_ref, v_ref, o_ref, m_scr, al_scr, acc_scr, s_scr,
                    *, tq, tk, unroll, nkb, nh):
    n_u = qi
    m_scr[...] = jnp.full(m_scr.shape, NEG_INF, F32)
    acc_scr[...] = jnp.zeros(acc_scr.shape, F32)

    last = jnp.maximum(n_u - 1, 0)
    slack = 2.0 * qk_ref[0]

    def dead(j):
        jj = jnp.minimum(j, last)
        ok = j < n_u - 1
        for hh in range(nh):
            hd = nh * pair + hh
            gap = LOG2E * (ce_ref[last * FOX_HEADS + hd] - ce_ref[jj * FOX_HEADS + hd])
            ok = ok & (slack + gap < EXP2_ZERO)
        return ok

    lo = jnp.int32(0)
    hi = last
    for _ in range(max(1, (nkb - 1).bit_length())):
        mid = (lo + hi) >> 1
        d = dead(mid)
        lo = jnp.where(d, mid + 1, lo)
        hi = jnp.where(d, hi, mid)
    first = lo

    def score_matmuls(kb):
        off = pl.multiple_of(kb * tk, tk)
        return [jnp.dot(k_ref[hh, pl.ds(off, tk), :], q_ref[hh, :, pl.ds(qoff, tq)],
                        preferred_element_type=F32)
                for hh in range(nh)]

    def score_finish(sts):
        for hh in range(nh):
            st = sts[hh]
            m_old = m_scr[hh, 0:1, :]
            m_new = jnp.maximum(m_old, jnp.max(st, axis=0, keepdims=True))
            al_scr[hh, 0:1, :] = jnp.exp2(m_old - m_new)
            m_scr[hh, 0:1, :] = m_new
            s_scr[hh] = st

    def accumulate(kb):
        off = pl.multiple_of(kb * tk, tk)
        for hh in range(nh):
            p = jnp.exp2(s_scr[hh] - m_scr[hh, 0:1, :]).astype(BF16)
            pv = jnp.dot(v_ref[hh, :, pl.ds(off, tk)], p, preferred_element_type=F32)
            acc_scr[hh] = al_scr[hh, 0:1, :] * acc_scr[hh] + pv

    def advance(kb):
        sts = score_matmuls(kb + 1)
        accumulate(kb)
        score_finish(sts)

    def finish(have_prev):
        half = tq // 2
        off = pl.multiple_of(n_u * tk, tk)
        sts = [(jnp.dot(k_ref[hh, pl.ds(off, half), :], q_ref[hh, :, pl.ds(qoff, half)],
                        preferred_element_type=F32),
                jnp.dot(k_ref[hh, pl.ds(off, tk), :], q_ref[hh, :, pl.ds(pl.multiple_of(qoff + half, half), half)],
                        preferred_element_type=F32))
               for hh in range(nh)]
        if have_prev:
            accumulate(n_u - 1)
        tri = jnp.where(lax.broadcasted_iota(jnp.int32, (half, half), 0)
                        <= lax.broadcasted_iota(jnp.int32, (half, half), 1), 0.0, NEG_INF)
        for hh in range(nh):
            st_a, st_b = sts[hh]
            halves = ((st_a + tri, 0), (jnp.concatenate([st_b[0:half], st_b[half:tk] + tri], axis=0), half))
            for st, c0 in halves:
                nk = st.shape[0]
                m_old = m_scr[hh, 0:1, c0:c0 + half]
                m_new = jnp.maximum(m_old, jnp.max(st, axis=0, keepdims=True))
                p = jnp.exp2(st - m_new).astype(BF16)
                pv = jnp.dot(v_ref[hh, :, pl.ds(off, nk)], p, preferred_element_type=F32)
                acc_scr[hh, :, c0:c0 + half] = jnp.exp2(m_old - m_new) * acc_scr[hh, :, c0:c0 + half] + pv
        for t in range(nh // 2):
            pair_rows = []
            for hh in (2 * t, 2 * t + 1):
                a = acc_scr[hh]
                pair_rows.append(a[0:HEAD_DIM] / a[HEAD_DIM:HEAD_DIM + 1, :])
            o_ref[pl.ds(qoff, tq), LANES * t:LANES * (t + 1)] = jnp.concatenate(
                pair_rows, axis=0).T.astype(o_ref.dtype)

    @pl.when(qi == 0)
    def _():
        finish(False)

    @pl.when(qi > 0)
    def _():
        n_adv = n_u - 1 - first
        rem = lax.rem(n_adv, unroll)
        for k in range(unroll):
            @pl.when(rem == k)
            def _():
                score_finish(score_matmuls(first))
                for u in range(k):
                    advance(first + u)

        def body(t, carry):
            for u in range(unroll):
                advance(first + rem + unroll * t + u)
            return carry

        lax.fori_loop(0, n_adv // unroll, body, 0)
        finish(True)


def _fox_attn_prompt(cum_end, qk_bound, qa, ka, va, *, tq, tk, unroll, nh, qpb):
    n = ka.shape[1]
    assert tq == tk and n % (qpb * tq) == 0
    resident = dict(pipeline_mode=pl.Buffered(1)) if nh > 2 else {}
    grid_spec = pltpu.PrefetchScalarGridSpec(
        num_scalar_prefetch=2, grid=(FOX_HEADS // nh, n // (qpb * tq)),
        in_specs=[pl.BlockSpec((nh, LANES, qpb * tq), lambda p, i, ce, qk: (p, 0, i)),
                  pl.BlockSpec((nh, n, LANES), lambda p, i, ce, qk: (p, 0, 0), **resident),
                  pl.BlockSpec((nh, LANES, n), lambda p, i, ce, qk: (p, 0, 0), **resident)],
        out_specs=pl.BlockSpec((qpb * tq, HEAD_DIM * nh), lambda p, i, ce, qk: (i, p)),
        scratch_shapes=[pltpu.VMEM((nh, 8, tq), F32), pltpu.VMEM((nh, 8, tq), F32),
                        pltpu.VMEM((nh, LANES, tq), F32), pltpu.VMEM((nh, tk, tq), F32)])
    return pl.pallas_call(
        functools.partial(_fox_attn_kernel, tq=tq, tk=tk, unroll=unroll, nkb=n // tk, nh=nh, qpb=qpb),
        grid_spec=grid_spec,
        out_shape=jax.ShapeDtypeStruct((n, D_MODEL), BF16),
        compiler_params=_cparams(("arbitrary", "arbitrary")), name="fox_attn_prompt")(
            cum_end, qk_bound, qa, ka, va)


CUM_BLK = 512
NEW_PAD = 128


def _fox_sample_kernel(q_ref, kc_ref, vc_ref, lfc_ref, kn_ref, vn_ref, lfn_ref, tri_ref, o_ref, negc, negn,
                       *, t_new, p_len):
    qd = pl.program_id(1)
    rows = 4 * t_new

    @pl.when(qd == 0)
    def _():
        tri = tri_ref[...]

        def prefix_sums(x, t):
            c3 = jnp.dot(jnp.concatenate(_split3(x), axis=0), t, preferred_element_type=F32)
            return c3[0:FOX_HEADS] + c3[FOX_HEADS:2 * FOX_HEADS] + c3[2 * FOX_HEADS:3 * FOX_HEADS]

        carry = jnp.zeros((FOX_HEADS, 1), F32)
        for b in range(p_len // CUM_BLK):
            c = prefix_sums(lfc_ref[0, :, b * CUM_BLK:(b + 1) * CUM_BLK], tri) + carry
            negc[:, b * CUM_BLK:(b + 1) * CUM_BLK] = -c
            carry = c[:, CUM_BLK - 1:CUM_BLK]
        negn[...] = -(prefix_sums(lfn_ref[0], tri[0:NEW_PAD, 0:NEW_PAD]) + carry)

    qq = q_ref[0]
    rhead = lax.broadcasted_iota(jnp.int32, (rows, 2 * LANES), 0) // t_new
    lhead = lax.broadcasted_iota(jnp.int32, (rows, 2 * LANES), 1) // HEAD_DIM
    qbd = jnp.where(rhead == lhead, jnp.concatenate([qq] * 4, axis=0), 0.0).astype(BF16)

    kct = kc_ref[0].reshape(2 * LANES, p_len).astype(BF16)
    vct = vc_ref[0].reshape(2 * LANES, p_len).astype(BF16)
    s_c = jnp.dot(qbd, kct, preferred_element_type=F32)
    zpad = jnp.zeros((NEW_PAD - t_new, 2 * LANES), F32)
    kn = jnp.concatenate([kn_ref[0], zpad], axis=0).astype(BF16)
    vn = jnp.concatenate([vn_ref[0], zpad], axis=0).astype(BF16)
    s_n = lax.dot_general(qbd, kn, (((1,), (1,)), ((), ())), preferred_element_type=F32)
    bc = jnp.concatenate([jnp.broadcast_to(negc[pl.ds(4 * qd + jh, 1), :], (t_new, p_len)) for jh in range(4)], axis=0)
    bn = jnp.concatenate([jnp.broadcast_to(negn[pl.ds(4 * qd + jh, 1), :], (t_new, NEW_PAD)) for jh in range(4)], axis=0)
    s_c = s_c + bc
    qrow = lax.broadcasted_iota(jnp.int32, (rows, NEW_PAD), 0) % t_new
    kcol = lax.broadcasted_iota(jnp.int32, (rows, NEW_PAD), 1)
    s_n = jnp.where(kcol <= qrow, s_n + bn, NEG_INF)
    m = jnp.maximum(jnp.max(s_c, axis=-1, keepdims=True), jnp.max(s_n, axis=-1, keepdims=True))
    p_c = jnp.exp(s_c - m)
    p_n = jnp.exp(s_n - m)
    den = jnp.sum(p_c, axis=-1, keepdims=True) + jnp.sum(p_n, axis=-1, keepdims=True)
    o = (lax.dot_general(p_c.astype(BF16), vct, (((1,), (1,)), ((), ())), preferred_element_type=F32)
         + jnp.dot(p_n.astype(BF16), vn, preferred_element_type=F32)) / den
    lh = lax.broadcasted_iota(jnp.int32, (t_new, 2 * LANES), 1) // HEAD_DIM
    out = jnp.zeros((t_new, 2 * LANES), F32)
    for jh in range(4):
        out = jnp.where(lh == jh, o[jh * t_new:(jh + 1) * t_new], out)
    o_ref[0] = out


def _fox_attn_sample(q, cache_k, cache_v, lfc_t, k_new, v_new, lfn_t):
    b, t_new, _ = q.shape
    p_len = cache_k.shape[3]
    tri = jnp.asarray(np.triu(np.ones((CUM_BLK, CUM_BLK), np.float32)), BF16)
    quad = 2 * LANES
    in_specs = [pl.BlockSpec((1, t_new, quad), lambda r, d: (r, 0, d)),
                pl.BlockSpec((1, 4, HEAD_DIM, p_len), lambda r, d: (r, d, 0, 0)),
                pl.BlockSpec((1, 4, HEAD_DIM, p_len), lambda r, d: (r, d, 0, 0)),
                pl.BlockSpec((1, FOX_HEADS, p_len), lambda r, d: (r, 0, 0)),
                pl.BlockSpec((1, t_new, quad), lambda r, d: (r, 0, d)),
                pl.BlockSpec((1, t_new, quad), lambda r, d: (r, 0, d)),
                pl.BlockSpec((1, FOX_HEADS, NEW_PAD), lambda r, d: (r, 0, 0)),
                pl.BlockSpec((CUM_BLK, CUM_BLK), lambda r, d: (0, 0))]
    return pl.pallas_call(
        functools.partial(_fox_sample_kernel, t_new=t_new, p_len=p_len),
        grid=(b, D_MODEL // quad), in_specs=in_specs,
        out_specs=pl.BlockSpec((1, t_new, quad), lambda r, d: (r, 0, d)),
        out_shape=jax.ShapeDtypeStruct((b, t_new, D_MODEL), F32),
        scratch_shapes=[pltpu.VMEM((FOX_HEADS, p_len), F32), pltpu.VMEM((FOX_HEADS, NEW_PAD), F32)],
        compiler_params=_cparams(("arbitrary", "arbitrary")), name="fox_attn_sample")(
            q, cache_k, cache_v, lfc_t, k_new, v_new, lfn_t, tri)


def _out_proj_kernel(r_ref, a_ref, g1_ref, wg_ref, w_ref, y_ref):
    x = r_ref[...]
    h = _rms(x, g1_ref[...]).astype(BF16)
    gate = jnp.dot(h, wg_ref[...], preferred_element_type=F32)
    mixed = (a_ref[...].astype(F32) * _silu(gate)).astype(BF16)
    y_ref[...] = x + jnp.dot(mixed, w_ref[...], preferred_element_type=F32)


def _out_proj(resid, attn, wts, *, tb):
    n = resid.shape[0]
    row = pl.BlockSpec((tb, D_MODEL), lambda i: (i, 0))
    consts = [wts["g1"], wts["w_g1"], wts["w_out1"]]
    return pl.pallas_call(
        _out_proj_kernel, grid=(n // tb,), in_specs=[row, row] + [_const_spec(a.shape) for a in consts],
        out_specs=row, out_shape=jax.ShapeDtypeStruct((n, D_MODEL), F32),
        compiler_params=_cparams(("arbitrary",)), name="out_proj")(resid, attn, *consts)


L0_TB = 512
FOX_TB = 256
ATTN_TQ = 512
ATTN_TK = 512
ATTN_UNROLL = 4
ATTN_HEADS = 2
ATTN_QPB = 8
SWA_CHUNK = 64


def kernel(x_prompt, x_sample, state_pool, cache_swa_k, cache_swa_v, cache_fox_k, cache_fox_v, cache_fox_logf,
           norm0_g, w_in0, w_pool, pool_scale, swa_qn_g, swa_kn_g, swa_sinks, w_out0,
           norm1_g, w_in1, b_forget, fox_qn_g, fox_kn_g, w_out1):
    nb, seq, _ = x_prompt.shape
    db, dseq, _ = x_sample.shape
    past_len = cache_fox_k.shape[1]
    assert nb == 1 and seq % L0_TB == 0 and seq % ATTN_TQ == 0 and dseq % 8 == 0

    seg = np.kron(np.eye(2, dtype=np.float32), np.ones((HEAD_DIM, HEAD_DIM), np.float32))
    mc = FOX_HEADS * HEAD_DIM
    wts = {
        "g0": norm0_g.reshape(1, D_MODEL), "w_in0": w_in0.astype(BF16), "w_pool": w_pool.astype(BF16),
        "pool_scale": pool_scale.reshape(1, C_POOL),
        "qg0": jnp.tile(swa_qn_g, SWA_HEADS).reshape(1, SWA_HEADS * HEAD_DIM),
        "kg0": jnp.tile(swa_kn_g, SWA_KV_HEADS).reshape(1, LANES),
        "seg": jnp.asarray(seg, BF16), "sinks": swa_sinks, "w_out0": w_out0.astype(BF16),
        "g1": norm1_g.reshape(1, D_MODEL),
        "w_out1": w_out1.astype(BF16),
        "w_f": jnp.pad(w_in1[:, 4 * mc:], ((0, 0), (0, LANES - FOX_HEADS))).astype(BF16),
        "b_f": jnp.pad(b_forget.astype(F32), (0, LANES - FOX_HEADS)).reshape(1, LANES),
        "w_ft": w_in1[:, 4 * mc:].T.astype(BF16),
        "b_fc": jnp.broadcast_to(b_forget.astype(F32).reshape(FOX_HEADS, 1), (FOX_HEADS, LANES)),
        "qg1": jnp.tile(fox_qn_g, FOX_HEADS).reshape(1, mc), "kg1": jnp.tile(fox_kn_g, FOX_HEADS).reshape(1, mc),
        "w_k1": w_in1[:, mc:2 * mc].astype(BF16), "w_g1": w_in1[:, 3 * mc:4 * mc].astype(BF16),
        "w_q1t": w_in1[:, 0:mc].T.astype(BF16), "w_v1t": w_in1[:, 2 * mc:3 * mc].T.astype(BF16),
        "gq1c": jnp.broadcast_to(fox_qn_g.astype(F32).reshape(HEAD_DIM, 1), (HEAD_DIM, LANES)),
    }

    xp = x_prompt.reshape(seq, D_MODEL)
    y0p, pool_p, swk_p, swv_p = _layer0(xp, None, wts, ns=1, ts=L0_TB, chunk=SWA_CHUNK, base_pos=0)
    fk_p, fvt_p, flt_p, qa, ka, va, cum_end = _fox_proj_prompt(y0p, wts, tb=FOX_TB)
    fl_p = jnp.transpose(flt_p)
    fv_p = jnp.transpose(fvt_p.reshape(FOX_HEADS, HEAD_DIM, seq), (2, 0, 1))
    per = ATTN_TK // FOX_TB
    cum_end = cum_end[per - 1::per, :, 0].reshape(-1)
    qk_bound = (1.02 * LOG2E * HEAD_DIM * ATTN_SCALE * jnp.max(jnp.abs(fox_qn_g)) * jnp.max(jnp.abs(fox_kn_g))
                ).astype(F32).reshape(1)
    attn_p = _fox_attn_prompt(cum_end, qk_bound, qa, ka, va, tq=ATTN_TQ, tk=ATTN_TK, unroll=ATTN_UNROLL,
                              nh=ATTN_HEADS, qpb=ATTN_QPB)
    yp = _out_proj(y0p, attn_p, wts, tb=L0_TB)

    xs = x_sample.reshape(db * dseq, D_MODEL)
    prefix = (jnp.pad(state_pool, ((0, 0), (1, 0), (0, 0))),
              cache_swa_k.reshape(db, WINDOW, LANES), cache_swa_v.reshape(db, WINDOW, LANES))
    y0s, pool_s, swk_s, swv_s = _layer0(xs, prefix, wts, ns=db, ts=dseq, chunk=dseq, base_pos=past_len)
    fk_s, fv_s, fl_s, q_s = _fox_proj_sample(y0s, wts, tb=db * dseq)
    attn_s = _fox_attn_sample(
        q_s.reshape(db, dseq, D_MODEL),
        jnp.transpose(cache_fox_k, (0, 2, 3, 1)), jnp.transpose(cache_fox_v, (0, 2, 3, 1)),
        jnp.transpose(cache_fox_logf, (0, 2, 1)),
        fk_s.reshape(db, dseq, D_MODEL), fv_s.reshape(db, dseq, D_MODEL),
        jnp.pad(jnp.transpose(fl_s.reshape(db, dseq, FOX_HEADS), (0, 2, 1)), ((0, 0), (0, 0), (0, NEW_PAD - dseq))))
    ys = _out_proj(y0s, attn_s.reshape(db * dseq, D_MODEL), wts, tb=db * dseq)

    return (yp.reshape(1, seq, D_MODEL), ys.reshape(db, dseq, D_MODEL),
            pool_p[:, 1:], pool_s[:, 1:],
            swk_p.reshape(1, WINDOW, SWA_KV_HEADS, HEAD_DIM), swv_p.reshape(1, WINDOW, SWA_KV_HEADS, HEAD_DIM),
            swk_s.reshape(db, WINDOW, SWA_KV_HEADS, HEAD_DIM), swv_s.reshape(db, WINDOW, SWA_KV_HEADS, HEAD_DIM),
            fk_p.reshape(1, seq, FOX_HEADS, HEAD_DIM), fv_p.reshape(1, seq, FOX_HEADS, HEAD_DIM),
            fl_p.reshape(1, seq, FOX_HEADS),
            fk_s.reshape(db, dseq, FOX_HEADS, HEAD_DIM), fv_s.reshape(db, dseq, FOX_HEADS, HEAD_DIM),
            fl_s.reshape(db, dseq, FOX_HEADS))
```

```python
import functools

import numpy as np
import jax
import jax.numpy as jnp
from jax import lax
from jax.experimental import pallas as pl
from jax.experimental.pallas import tpu as pltpu

F32 = jnp.float32
BF16 = jnp.bfloat16

D_MODEL = 1024
HEAD_DIM = 64
ATTN_SCALE = HEAD_DIM ** -0.5
POOL_WINDOWS = (2, 4, 8, 16)
C_POOL = 512
POOL_PAD = 15
SWA_HEADS = 8
SWA_KV_HEADS = 2
SWA_REP = 4
WINDOW = 128
FOX_HEADS = 16
NORM_EPS = 1e-6
NEG_INF = -1e30
LOG2E = 1.4426950408889634
EXP2_ZERO = -152.0

LANES = 128
SWA_KEYS = 256
VMEM_LIMIT = 56 * 1024 * 1024


def _cparams(sem):
    return pltpu.CompilerParams(dimension_semantics=sem, vmem_limit_bytes=VMEM_LIMIT)


def _rms(x, g):
    ms = jnp.mean(x * x, axis=-1, keepdims=True)
    return x * lax.rsqrt(ms + NORM_EPS) * g


def _split3(x):
    hi = x.astype(BF16)
    r = x - hi.astype(F32)
    mid = r.astype(BF16)
    lo = (r - mid.astype(F32)).astype(BF16)
    return hi, mid, lo


def _head_rms_tile(x, seg, g):
    ss = jnp.dot((x * x).astype(BF16), seg, preferred_element_type=F32)
    return x * lax.rsqrt(ss * (1.0 / HEAD_DIM) + NORM_EPS) * g


def _silu(g):
    return g / (1.0 + jnp.exp(-g))


def _l0_kernel(*refs, ns, ts, chunk, has_cache, base_pos, nblk):
    if has_cache:
        (x_ref, pp_ref, kp_ref, vp_ref, g0_ref, win_ref, wpool_ref, pscale_ref, qg_ref, kg_ref, seg_ref,
         slope_ref, sink_ref, wout_ref, y_ref, ps_ref, ks_ref, vs_ref, uext, kext, vext, mix) = refs
    else:
        (x_ref, g0_ref, win_ref, wpool_ref, pscale_ref, qg_ref, kg_ref, seg_ref,
         slope_ref, sink_ref, wout_ref, y_ref, ps_ref, ks_ref, vs_ref, uext, kext, vext, mix) = refs
    padk = SWA_KEYS - WINDOW - chunk
    hist = padk + WINDOW
    i = pl.program_id(0)

    x = x_ref[...]
    h = _rms(x, g0_ref[...]).astype(BF16)
    proj = jnp.dot(h, win_ref[...], preferred_element_type=F32)
    u = proj[:, 0:C_POOL]
    gate = proj[:, 1280:2304]
    seg = seg_ref[...]
    qn = [_head_rms_tile(proj[:, 512 + LANES * j:512 + LANES * (j + 1)], seg,
                         qg_ref[:, LANES * j:LANES * (j + 1)]) * (ATTN_SCALE * LOG2E) for j in range(4)]
    kn = _head_rms_tile(proj[:, 1024:1152], seg, kg_ref[...])
    v = proj[:, 1152:1280]

    if has_cache:
        for s in range(ns):
            uext[s, 0:16, :] = pp_ref[s]
            kext[s, 0:padk, :] = jnp.zeros((padk, LANES), F32)
            vext[s, 0:padk, :] = jnp.zeros((padk, LANES), F32)
            kext[s, padk:hist, :] = kp_ref[s]
            vext[s, padk:hist, :] = vp_ref[s]
    else:
        @pl.when(i == 0)
        def _():
            uext[0, 0:16, :] = jnp.zeros((16, C_POOL), F32)
            kext[0, 0:hist, :] = jnp.zeros((hist, LANES), F32)
            vext[0, 0:hist, :] = jnp.zeros((hist, LANES), F32)

        @pl.when(i > 0)
        def _():
            uext[0, 0:16, :] = uext[0, ts:ts + 16, :]
            kext[0, 0:hist, :] = kext[0, ts:ts + hist, :]
            vext[0, 0:hist, :] = vext[0, ts:ts + hist, :]

    for s in range(ns):
        uext[s, 16:16 + ts, :] = u[s * ts:(s + 1) * ts]
        kext[s, hist:hist + ts, :] = kn[s * ts:(s + 1) * ts]
        vext[s, hist:hist + ts, :] = v[s * ts:(s + 1) * ts]

    pos = base_pos + i * ts + lax.broadcasted_iota(jnp.int32, (ts, LANES), 0)
    for s in range(ns):
        for g, w in enumerate(POOL_WINDOWS):
            cols = slice(LANES * g, LANES * (g + 1))
            acc = uext[s, 16:16 + ts, cols]
            cur = acc
            for j in range(1, w):
                acc = acc + uext[s, 16 - j:16 - j + ts, cols]
            cnt = jnp.minimum(pos + 1, w).astype(F32)
            mix[s * ts:(s + 1) * ts, cols] = acc / cnt - cur
    for g in range(4):
        cols = slice(LANES * g, LANES * (g + 1))
        d = mix[:, cols].astype(BF16)
        mix[:, cols] = jnp.dot(d, wpool_ref[g], preferred_element_type=F32) * pscale_ref[:, cols]

    rows4 = SWA_REP * chunk
    lo_c = lax.broadcasted_iota(jnp.int32, (chunk, LANES), 1) < HEAD_DIM
    lo_k = lax.broadcasted_iota(jnp.int32, (SWA_KEYS, LANES), 1) < HEAD_DIM
    kj = lax.broadcasted_iota(jnp.int32, (SWA_KEYS, rows4), 0)
    qi = lax.broadcasted_iota(jnp.int32, (SWA_KEYS, rows4), 1) % chunk
    absrel = jnp.abs(qi + WINDOW - (kj - padk)).astype(F32)
    bias = []
    sinkc = []
    for g in range(SWA_KV_HEADS):
        bias.append(jnp.where(kj >= padk, -(slope_ref[g][0:1, :] * LOG2E) * absrel, NEG_INF))
        sinkc.append(sink_ref[g][0:1, :] * LOG2E)
    nch = ts // chunk
    units = [(s, c, g) for s in range(ns) for c in range(nch) for g in range(SWA_KV_HEADS)]

    def swa_scores(s, c, g):
        r0 = c * chunk
        rows = slice(s * ts + r0, s * ts + r0 + chunk)
        kwin = kext[s, r0:r0 + SWA_KEYS, :]
        vwin = vext[s, r0:r0 + SWA_KEYS, :]
        krl = pltpu.roll(kwin, HEAD_DIM, 1)
        vrl = pltpu.roll(vwin, HEAD_DIM, 1)
        if g == 0:
            kd = jnp.where(lo_k, kwin, krl).astype(BF16)
            vd = jnp.where(lo_k, vwin, vrl).astype(BF16)
        else:
            kd = jnp.where(lo_k, krl, kwin).astype(BF16)
            vd = jnp.where(lo_k, vrl, vwin).astype(BF16)
        q0 = qn[2 * g][rows]
        q1 = qn[2 * g + 1][rows]
        qs = jnp.concatenate([jnp.where(lo_c, q0, 0.0), jnp.where(lo_c, 0.0, q0),
                              jnp.where(lo_c, q1, 0.0), jnp.where(lo_c, 0.0, q1)], axis=0).astype(BF16)
        return lax.dot_general(kd, qs, (((1,), (1,)), ((), ())), preferred_element_type=F32), vd

    def swa_finish(s, c, g, sc, vd):
        rows = slice(s * ts + c * chunk, s * ts + (c + 1) * chunk)
        sc = sc + bias[g]
        if not has_cache and c * chunk < WINDOW:
            sc = jnp.where(kj >= padk + jnp.maximum(0, WINDOW - (i * nch + c) * chunk), sc, NEG_INF)
        m = jnp.maximum(jnp.max(sc, axis=0, keepdims=True), sinkc[g])
        p = jnp.exp2(sc - m)
        den = jnp.sum(p, axis=0, keepdims=True) + jnp.exp2(sinkc[g] - m)
        wgt = (p / den).astype(BF16)
        o = lax.dot_general(wgt, vd, (((0,), (0,)), ((), ())), preferred_element_type=F32)
        for jj in range(2):
            t = 2 * g + jj
            mix[rows, C_POOL + LANES * t:C_POOL + LANES * (t + 1)] = jnp.where(
                lo_c, o[(2 * jj) * chunk:(2 * jj + 1) * chunk], o[(2 * jj + 1) * chunk:(2 * jj + 2) * chunk])

    pending = swa_scores(*units[0])
    for n, unit in enumerate(units):
        nxt = swa_scores(*units[n + 1]) if n + 1 < len(units) else None
        swa_finish(*unit, *pending)
        pending = nxt

    mixed = (mix[...] * _silu(gate)).astype(BF16)
    y_ref[...] = x + jnp.dot(mixed, wout_ref[...], preferred_element_type=F32)

    @pl.when(i == nblk - 1)
    def _():
        for s in range(ns):
            ps_ref[s] = uext[s, ts:ts + 16, :]
            ks_ref[s] = kext[s, padk + ts:padk + ts + WINDOW, :]
            vs_ref[s] = vext[s, padk + ts:padk + ts + WINDOW, :]


def _const_spec(shape):
    nd = len(shape)
    return pl.BlockSpec(shape, lambda i, _nd=nd: (0,) * _nd, pipeline_mode=pl.Buffered(1))


def _layer0(x2d, prefix, wts, *, ns, ts, chunk, base_pos):
    n = x2d.shape[0]
    tb = ns * ts
    nblk = n // tb
    has_cache = prefix is not None
    padk = SWA_KEYS - WINDOW - chunk
    rows4 = SWA_REP * chunk
    slope = np.repeat(2.0 ** (-(np.arange(SWA_HEADS) + 1.0)), chunk).reshape(SWA_KV_HEADS, 1, rows4)
    slope = jnp.asarray(np.broadcast_to(slope, (SWA_KV_HEADS, 8, rows4)).astype(np.float32))
    sink = jnp.broadcast_to(jnp.repeat(wts["sinks"].astype(F32), chunk).reshape(SWA_KV_HEADS, 1, rows4),
                            (SWA_KV_HEADS, 8, rows4))
    consts = [wts["g0"], wts["w_in0"], wts["w_pool"], wts["pool_scale"], wts["qg0"], wts["kg0"], wts["seg"],
              slope, sink, wts["w_out0"]]
    in_specs = [pl.BlockSpec((tb, D_MODEL), lambda i: (i, 0))]
    args = [x2d]
    if has_cache:
        for a in prefix:
            in_specs.append(_const_spec(a.shape))
            args.append(a)
    for a in consts:
        in_specs.append(_const_spec(a.shape))
        args.append(a)
    out_shape = [jax.ShapeDtypeStruct((n, D_MODEL), F32),
                 jax.ShapeDtypeStruct((ns, 16, C_POOL), F32),
                 jax.ShapeDtypeStruct((ns, WINDOW, LANES), F32),
                 jax.ShapeDtypeStruct((ns, WINDOW, LANES), F32)]
    out_specs = [pl.BlockSpec((tb, D_MODEL), lambda i: (i, 0)),
                 pl.BlockSpec((ns, 16, C_POOL), lambda i: (0, 0, 0)),
                 pl.BlockSpec((ns, WINDOW, LANES), lambda i: (0, 0, 0)),
                 pl.BlockSpec((ns, WINDOW, LANES), lambda i: (0, 0, 0))]
    scratch = [pltpu.VMEM((ns, 16 + ts, C_POOL), F32),
               pltpu.VMEM((ns, padk + WINDOW + ts, LANES), F32),
               pltpu.VMEM((ns, padk + WINDOW + ts, LANES), F32),
               pltpu.VMEM((tb, D_MODEL), F32)]
    kern = functools.partial(_l0_kernel, ns=ns, ts=ts, chunk=chunk, has_cache=has_cache,
                             base_pos=base_pos, nblk=nblk)
    return pl.pallas_call(
        kern, grid=(nblk,), in_specs=in_specs, out_specs=out_specs, out_shape=out_shape,
        scratch_shapes=scratch, compiler_params=_cparams(("arbitrary",)),
        name="layer0_prompt" if not has_cache else "layer0_sample")(*args)


def _log_sigmoid(z):
    return jnp.minimum(z, 0.0) - jnp.log1p(jnp.exp(-jnp.abs(z)))


def _fox_proj_sample_kernel(x_ref, g1_ref, wqt_ref, wk_ref, wvt_ref, wf_ref, bf_ref, qg_ref, kg_ref, seg_ref,
                            k_ref, v_ref, lf_ref, q_ref):
    x = x_ref[...]
    h = _rms(x, g1_ref[...]).astype(BF16)
    z = jnp.dot(h, wf_ref[...], preferred_element_type=F32) + bf_ref[...]
    lf_ref[...] = _log_sigmoid(z)[:, 0:FOX_HEADS]
    seg = seg_ref[...]
    nt = (((1,), (1,)), ((), ()))
    q = lax.dot_general(h, wqt_ref[...], nt, preferred_element_type=F32)
    k = jnp.dot(h, wk_ref[...], preferred_element_type=F32)
    v_ref[...] = lax.dot_general(h, wvt_ref[...], nt, preferred_element_type=F32)
    for j in range(FOX_HEADS // 2):
        cols = slice(LANES * j, LANES * (j + 1))
        q_ref[:, cols] = _head_rms_tile(q[:, cols], seg, qg_ref[:, cols]) * ATTN_SCALE
        k_ref[:, cols] = _head_rms_tile(k[:, cols], seg, kg_ref[:, cols])


def _fox_proj_sample(x2d, wts, *, tb):
    n = x2d.shape[0]
    consts = [wts["g1"], wts["w_q1t"], wts["w_k1"], wts["w_v1t"], wts["w_f"], wts["b_f"], wts["qg1"], wts["kg1"],
              wts["seg"]]
    row_spec = pl.BlockSpec((tb, D_MODEL), lambda i: (i, 0))
    in_specs = [row_spec] + [_const_spec(a.shape) for a in consts]
    row = jax.ShapeDtypeStruct((n, D_MODEL), F32)
    out_shape = [row, row, jax.ShapeDtypeStruct((n, FOX_HEADS), F32), row]
    out_specs = [row_spec, row_spec, pl.BlockSpec((tb, FOX_HEADS), lambda i: (i, 0)), row_spec]
    return pl.pallas_call(
        _fox_proj_sample_kernel, grid=(n // tb,), in_specs=in_specs, out_specs=out_specs, out_shape=out_shape,
        compiler_params=_cparams(("arbitrary",)), name="fox_proj_sample")(x2d, *consts)


def _fox_proj_prompt_kernel(x_ref, g1_ref, wk_ref, wft_ref, bfc_ref, wqt_ref, wvt_ref, gq_ref, kg_ref,
                            seg_ref, tri_ref, e_ref,
                            k_ref, vt_ref, lft_ref, qa_ref, ka_ref, va_ref, ce_ref, carry, *, tb):
    i = pl.program_id(0)
    x = x_ref[...]
    hf = _rms(x, g1_ref[...])
    h = hf.astype(BF16)
    ht = hf.T.astype(BF16)
    bcol = jnp.concatenate([bfc_ref[...]] * (tb // LANES), axis=1)
    logft = _log_sigmoid(jnp.dot(wft_ref[...], ht, preferred_element_type=F32) + bcol)
    lft_ref[...] = logft
    k = jnp.dot(h, wk_ref[...], preferred_element_type=F32)

    @pl.when(i == 0)
    def _():
        carry[...] = jnp.zeros(carry.shape, F32)

    c3 = jnp.dot(jnp.concatenate(_split3(logft), axis=0), tri_ref[...], preferred_element_type=F32)

    qt = jnp.dot(wqt_ref[...], ht, preferred_element_type=F32)
    vt = jnp.dot(wvt_ref[...], ht, preferred_element_type=F32)
    vt_ref[...] = vt
    srow = lax.broadcasted_iota(jnp.int32, (HEAD_DIM, tb), 0)
    q_aug = jnp.where(srow < 3, 1.0, 0.0)
    v_aug = jnp.where(srow == 0, 1.0, 0.0)
    gq = jnp.concatenate([gq_ref[...]] * (tb // LANES), axis=1) * (ATTN_SCALE * LOG2E)
    for hd in range(FOX_HEADS):
        rows = slice(HEAD_DIM * hd, HEAD_DIM * (hd + 1))
        qh = qt[rows]
        ss = jnp.sum(qh * qh, axis=0, keepdims=True)
        qn = qh * lax.rsqrt(ss * (1.0 / HEAD_DIM) + NORM_EPS) * gq
        qa_ref[hd] = jnp.concatenate([qn, q_aug], axis=0).astype(BF16)
        va_ref[hd] = jnp.concatenate([vt[rows], v_aug], axis=0).astype(BF16)

    cumt = c3[0:FOX_HEADS] + c3[FOX_HEADS:2 * FOX_HEADS] + c3[2 * FOX_HEADS:3 * FOX_HEADS] + carry[:, 0:1]
    end = jnp.broadcast_to(cumt[:, tb - 1:tb], (FOX_HEADS, LANES))
    carry[...] = end
    ce_ref[0] = end
    parts = [p.astype(F32) for p in _split3(-LOG2E * cumt)]
    n3 = jnp.concatenate(parts + [jnp.zeros((LANES - 3 * FOX_HEADS, tb), F32)], axis=0).T.astype(BF16)
    biasall = jnp.dot(n3, e_ref[...], preferred_element_type=F32)
    lo_l = lax.broadcasted_iota(jnp.int32, (tb, LANES), 1) < HEAD_DIM
    seg = seg_ref[...]
    for j in range(FOX_HEADS // 2):
        cols = slice(LANES * j, LANES * (j + 1))
        kt = _head_rms_tile(k[:, cols], seg, kg_ref[:, cols])
        k_ref[:, cols] = kt
        bt = biasall[:, cols]
        ka_ref[2 * j] = jnp.where(lo_l, kt, bt).astype(BF16)
        ka_ref[2 * j + 1] = pltpu.roll(jnp.where(lo_l, bt, kt), HEAD_DIM, 1).astype(BF16)


def _fox_proj_prompt(x2d, wts, *, tb):
    n = x2d.shape[0]
    tri = jnp.asarray(np.triu(np.ones((tb, tb), np.float32)), BF16)
    e = np.zeros((LANES, FOX_HEADS // 2 * LANES), np.float32)
    for part in range(3):
        for hd in range(FOX_HEADS):
            e[part * FOX_HEADS + hd, (hd // 2) * LANES + (HEAD_DIM if hd % 2 == 0 else 0) + part] = 1.0
    consts = [wts["g1"], wts["w_k1"], wts["w_ft"], wts["b_fc"], wts["w_q1t"], wts["w_v1t"],
              wts["gq1c"], wts["kg1"], wts["seg"], tri, jnp.asarray(e, BF16)]
    row_spec = pl.BlockSpec((tb, D_MODEL), lambda i: (i, 0))
    in_specs = [row_spec] + [_const_spec(a.shape) for a in consts]
    row = jax.ShapeDtypeStruct((n, D_MODEL), F32)
    out_shape = [row, jax.ShapeDtypeStruct((D_MODEL, n), F32), jax.ShapeDtypeStruct((FOX_HEADS, n), F32),
                 jax.ShapeDtypeStruct((FOX_HEADS, LANES, n), BF16), jax.ShapeDtypeStruct((FOX_HEADS, n, LANES), BF16),
                 jax.ShapeDtypeStruct((FOX_HEADS, LANES, n), BF16),
                 jax.ShapeDtypeStruct((n // tb, FOX_HEADS, LANES), F32)]
    tspec = pl.BlockSpec((FOX_HEADS, LANES, tb), lambda i: (0, 0, i))
    out_specs = [row_spec, pl.BlockSpec((D_MODEL, tb), lambda i: (0, i)),
                 pl.BlockSpec((FOX_HEADS, tb), lambda i: (0, i)),
                 tspec, pl.BlockSpec((FOX_HEADS, tb, LANES), lambda i: (0, i, 0)), tspec,
                 pl.BlockSpec((1, FOX_HEADS, LANES), lambda i: (i, 0, 0))]
    return pl.pallas_call(
        functools.partial(_fox_proj_prompt_kernel, tb=tb), grid=(n // tb,), in_specs=in_specs,
        out_specs=out_specs, out_shape=out_shape, scratch_shapes=[pltpu.VMEM((FOX_HEADS, LANES), F32)],
        compiler_params=_cparams(("arbitrary",)), name="fox_proj_prompt")(x2d, *consts)


def _fox_attn_kernel(ce_ref, qk_ref, q_ref, k_ref, v_ref, o_ref, m_scr, al_scr, acc_scr, s_scr,
                     *, tq, qpb, **kw):
    def blk(qb):
        return _fox_attn_block(pl.program_id(0), pl.program_id(1) * qpb + qb, pl.multiple_of(qb * tq, tq),
                               ce_ref, qk_ref, q_ref, k_ref, v_ref, o_ref, m_scr, al_scr, acc_scr, s_scr,
                               tq=tq, **kw)

    blk(0)[0](None)

    def body(qb, carry):
        blk(qb)[0](blk(qb - 1)[1])
        return carry

    lax.fori_loop(1, qpb, body, 0)
    blk(qpb - 1)[1]()


def _fox_attn_block(pair, qi, qoff, ce_ref, qk_ref, q_ref, k_ref, v_ref, o_ref, m_scr, al_scr, acc_scr, s_scr,
                    *, tq, tk, unroll, nkb, nh):
    n_u = qi

    last = jnp.maximum(n_u - 1, 0)
    slack = 2.0 * qk_ref[0]

    def dead(j):
        jj = jnp.minimum(j, last)
        ok = j < n_u - 1
        for hh in range(nh):
            hd = nh * pair + hh
            gap = LOG2E * (ce_ref[last * FOX_HEADS + hd] - ce_ref[jj * FOX_HEADS + hd])
            ok = ok & (slack + gap < EXP2_ZERO)
        return ok

    lo = jnp.int32(0)
    hi = last
    for _ in range(max(1, (nkb - 1).bit_length())):
        mid = (lo + hi) >> 1
        d = dead(mid)
        lo = jnp.where(d, mid + 1, lo)
        hi = jnp.where(d, hi, mid)
    first = lo

    def score_matmuls(kb):
        off = pl.multiple_of(kb * tk, tk)
        return [jnp.dot(k_ref[hh, pl.ds(off, tk), :], q_ref[hh, :, pl.ds(qoff, tq)],
                        preferred_element_type=F32)
                for hh in range(nh)]

    def score_finish(sts):
        for hh in range(nh):
            st = sts[hh]
            m_old = m_scr[hh, 0:1, :]
            m_new = jnp.maximum(m_old, jnp.max(st, axis=0, keepdims=True))
            al_scr[hh, 0:1, :] = jnp.exp2(m_old - m_new)
            m_scr[hh, 0:1, :] = m_new
            s_scr[hh] = st

    def accumulate(kb):
        off = pl.multiple_of(kb * tk, tk)
        for hh in range(nh):
            p = jnp.exp2(s_scr[hh] - m_scr[hh, 0:1, :]).astype(BF16)
            pv = jnp.dot(v_ref[hh, :, pl.ds(off, tk)], p, preferred_element_type=F32)
            acc_scr[hh] = al_scr[hh, 0:1, :] * acc_scr[hh] + pv

    def advance(kb):
        sts = score_matmuls(kb + 1)
        accumulate(kb)
        score_finish(sts)

    def tail():
        half = tq // 2
        off = pl.multiple_of(n_u * tk, tk)
        sts = [(jnp.dot(k_ref[hh, pl.ds(off, half), :], q_ref[hh, :, pl.ds(qoff, half)],
                        preferred_element_type=F32),
                jnp.dot(k_ref[hh, pl.ds(off, tk), :], q_ref[hh, :, pl.ds(pl.multiple_of(qoff + half, half), half)],
                        preferred_element_type=F32))
               for hh in range(nh)]
        accumulate(last)
        tri = jnp.where(lax.broadcasted_iota(jnp.int32, (half, half), 0)
                        <= lax.broadcasted_iota(jnp.int32, (half, half), 1), 0.0, NEG_INF)
        for hh in range(nh):
            st_a, st_b = sts[hh]
            halves = ((st_a + tri, 0), (jnp.concatenate([st_b[0:half], st_b[half:tk] + tri], axis=0), half))
            for st, c0 in halves:
                nk = st.shape[0]
                m_old = m_scr[hh, 0:1, c0:c0 + half]
                m_new = jnp.maximum(m_old, jnp.max(st, axis=0, keepdims=True))
                p = jnp.exp2(st - m_new).astype(BF16)
                pv = jnp.dot(v_ref[hh, :, pl.ds(off, nk)], p, preferred_element_type=F32)
                acc_scr[hh, :, c0:c0 + half] = jnp.exp2(m_old - m_new) * acc_scr[hh, :, c0:c0 + half] + pv
        for t in range(nh // 2):
            pair_rows = []
            for hh in (2 * t, 2 * t + 1):
                a = acc_scr[hh]
                pair_rows.append(a[0:HEAD_DIM] / a[HEAD_DIM:HEAD_DIM + 1, :])
            o_ref[pl.ds(qoff, tq), LANES * t:LANES * (t + 1)] = jnp.concatenate(
                pair_rows, axis=0).T.astype(o_ref.dtype)

    def head(prev_tail):
        def start():
            if prev_tail is not None:
                prev_tail()
            m_scr[...] = jnp.full(m_scr.shape, NEG_INF, F32)
            acc_scr[...] = jnp.zeros(acc_scr.shape, F32)

        def below_diagonal():
            n_adv = n_u - 1 - first
            rem = lax.rem(n_adv, unroll)
            for k in range(unroll):
                @pl.when(rem == k)
                def _():
                    start()
                    score_finish(score_matmuls(first))
                    for u in range(k):
                        advance(first + u)

            def body(t, carry):
                for u in range(unroll):
                    advance(first + rem + unroll * t + u)
                return carry

            lax.fori_loop(0, n_adv // unroll, body, 0)

        if prev_tail is None:
            @pl.when(qi == 0)
            def _():
                start()
                m_scr[...] = jnp.zeros(m_scr.shape, F32)
                al_scr[...] = jnp.ones(al_scr.shape, F32)
                s_scr[...] = jnp.full(s_scr.shape, NEG_INF, F32)

            pl.when(qi > 0)(below_diagonal)
        else:
            below_diagonal()

    return head, tail


def _fox_attn_prompt(cum_end, qk_bound, qa, ka, va, *, tq, tk, unroll, nh, qpb):
    n = ka.shape[1]
    assert tq == tk and n % (qpb * tq) == 0
    resident = dict(pipeline_mode=pl.Buffered(1)) if nh > 2 else {}
    grid_spec = pltpu.PrefetchScalarGridSpec(
        num_scalar_prefetch=2, grid=(FOX_HEADS // nh, n // (qpb * tq)),
        in_specs=[pl.BlockSpec((nh, LANES, qpb * tq), lambda p, i, ce, qk: (p, 0, i)),
                  pl.BlockSpec((nh, n, LANES), lambda p, i, ce, qk: (p, 0, 0), **resident),
                  pl.BlockSpec((nh, LANES, n), lambda p, i, ce, qk: (p, 0, 0), **resident)],
        out_specs=pl.BlockSpec((qpb * tq, HEAD_DIM * nh), lambda p, i, ce, qk: (i, p)),
        scratch_shapes=[pltpu.VMEM((nh, 8, tq), F32), pltpu.VMEM((nh, 8, tq), F32),
                        pltpu.VMEM((nh, LANES, tq), F32), pltpu.VMEM((nh, tk, tq), F32)])
    return pl.pallas_call(
        functools.partial(_fox_attn_kernel, tq=tq, tk=tk, unroll=unroll, nkb=n // tk, nh=nh, qpb=qpb),
        grid_spec=grid_spec,
        out_shape=jax.ShapeDtypeStruct((n, D_MODEL), BF16),
        compiler_params=_cparams(("arbitrary", "arbitrary")), name="fox_attn_prompt")(
            cum_end, qk_bound, qa, ka, va)


CUM_BLK = 512
NEW_PAD = 128


def _fox_sample_kernel(q_ref, kc_ref, vc_ref, lfc_ref, kn_ref, vn_ref, lfn_ref, tri_ref, o_ref, negc, negn,
                       *, t_new, p_len):
    qd = pl.program_id(1)
    rows = 4 * t_new

    @pl.when(qd == 0)
    def _():
        tri = tri_ref[...]

        def prefix_sums(x, t):
            c3 = jnp.dot(jnp.concatenate(_split3(x), axis=0), t, preferred_element_type=F32)
            return c3[0:FOX_HEADS] + c3[FOX_HEADS:2 * FOX_HEADS] + c3[2 * FOX_HEADS:3 * FOX_HEADS]

        carry = jnp.zeros((FOX_HEADS, 1), F32)
        for b in range(p_len // CUM_BLK):
            c = prefix_sums(lfc_ref[0, :, b * CUM_BLK:(b + 1) * CUM_BLK], tri) + carry
            negc[:, b * CUM_BLK:(b + 1) * CUM_BLK] = -c
            carry = c[:, CUM_BLK - 1:CUM_BLK]
        negn[...] = -(prefix_sums(lfn_ref[0], tri[0:NEW_PAD, 0:NEW_PAD]) + carry)

    qq = q_ref[0]
    rhead = lax.broadcasted_iota(jnp.int32, (rows, 2 * LANES), 0) // t_new
    lhead = lax.broadcasted_iota(jnp.int32, (rows, 2 * LANES), 1) // HEAD_DIM
    qbd = jnp.where(rhead == lhead, jnp.concatenate([qq] * 4, axis=0), 0.0).astype(BF16)

    kct = kc_ref[0].reshape(2 * LANES, p_len).astype(BF16)
    vct = vc_ref[0].reshape(2 * LANES, p_len).astype(BF16)
    s_c = jnp.dot(qbd, kct, preferred_element_type=F32)
    zpad = jnp.zeros((NEW_PAD - t_new, 2 * LANES), F32)
    kn = jnp.concatenate([kn_ref[0], zpad], axis=0).astype(BF16)
    vn = jnp.concatenate([vn_ref[0], zpad], axis=0).astype(BF16)
    s_n = lax.dot_general(qbd, kn, (((1,), (1,)), ((), ())), preferred_element_type=F32)
    bc = jnp.concatenate([jnp.broadcast_to(negc[pl.ds(4 * qd + jh, 1), :], (t_new, p_len)) for jh in range(4)], axis=0)
    bn = jnp.concatenate([jnp.broadcast_to(negn[pl.ds(4 * qd + jh, 1), :], (t_new, NEW_PAD)) for jh in range(4)], axis=0)
    s_c = s_c + bc
    qrow = lax.broadcasted_iota(jnp.int32, (rows, NEW_PAD), 0) % t_new
    kcol = lax.broadcasted_iota(jnp.int32, (rows, NEW_PAD), 1)
    s_n = jnp.where(kcol <= qrow, s_n + bn, NEG_INF)
    m = jnp.maximum(jnp.max(s_c, axis=-1, keepdims=True), jnp.max(s_n, axis=-1, keepdims=True))
    p_c = jnp.exp(s_c - m)
    p_n = jnp.exp(s_n - m)
    den = jnp.sum(p_c, axis=-1, keepdims=True) + jnp.sum(p_n, axis=-1, keepdims=True)
    o = (lax.dot_general(p_c.astype(BF16), vct, (((1,), (1,)), ((), ())), preferred_element_type=F32)
         + jnp.dot(p_n.astype(BF16), vn, preferred_element_type=F32)) / den
    lh = lax.broadcasted_iota(jnp.int32, (t_new, 2 * LANES), 1) // HEAD_DIM
    out = jnp.zeros((t_new, 2 * LANES), F32)
    for jh in range(4):
        out = jnp.where(lh == jh, o[jh * t_new:(jh + 1) * t_new], out)
    o_ref[0] = out


def _fox_attn_sample(q, cache_k, cache_v, lfc_t, k_new, v_new, lfn_t):
    b, t_new, _ = q.shape
    p_len = cache_k.shape[3]
    tri = jnp.asarray(np.triu(np.ones((CUM_BLK, CUM_BLK), np.float32)), BF16)
    quad = 2 * LANES
    in_specs = [pl.BlockSpec((1, t_new, quad), lambda r, d: (r, 0, d)),
                pl.BlockSpec((1, 4, HEAD_DIM, p_len), lambda r, d: (r, d, 0, 0)),
                pl.BlockSpec((1, 4, HEAD_DIM, p_len), lambda r, d: (r, d, 0, 0)),
                pl.BlockSpec((1, FOX_HEADS, p_len), lambda r, d: (r, 0, 0)),
                pl.BlockSpec((1, t_new, quad), lambda r, d: (r, 0, d)),
                pl.BlockSpec((1, t_new, quad), lambda r, d: (r, 0, d)),
                pl.BlockSpec((1, FOX_HEADS, NEW_PAD), lambda r, d: (r, 0, 0)),
                pl.BlockSpec((CUM_BLK, CUM_BLK), lambda r, d: (0, 0))]
    return pl.pallas_call(
        functools.partial(_fox_sample_kernel, t_new=t_new, p_len=p_len),
        grid=(b, D_MODEL // quad), in_specs=in_specs,
        out_specs=pl.BlockSpec((1, t_new, quad), lambda r, d: (r, 0, d)),
        out_shape=jax.ShapeDtypeStruct((b, t_new, D_MODEL), F32),
        scratch_shapes=[pltpu.VMEM((FOX_HEADS, p_len), F32), pltpu.VMEM((FOX_HEADS, NEW_PAD), F32)],
        compiler_params=_cparams(("arbitrary", "arbitrary")), name="fox_attn_sample")(
            q, cache_k, cache_v, lfc_t, k_new, v_new, lfn_t, tri)


def _out_proj_kernel(r_ref, a_ref, g1_ref, wg_ref, w_ref, y_ref):
    x = r_ref[...]
    h = _rms(x, g1_ref[...]).astype(BF16)
    gate = jnp.dot(h, wg_ref[...], preferred_element_type=F32)
    mixed = (a_ref[...].astype(F32) * _silu(gate)).astype(BF16)
    y_ref[...] = x + jnp.dot(mixed, w_ref[...], preferred_element_type=F32)


def _out_proj(resid, attn, wts, *, tb):
    n = resid.shape[0]
    row = pl.BlockSpec((tb, D_MODEL), lambda i: (i, 0))
    consts = [wts["g1"], wts["w_g1"], wts["w_out1"]]
    return pl.pallas_call(
        _out_proj_kernel, grid=(n // tb,), in_specs=[row, row] + [_const_spec(a.shape) for a in consts],
        out_specs=row, out_shape=jax.ShapeDtypeStruct((n, D_MODEL), F32),
        compiler_params=_cparams(("arbitrary",)), name="out_proj")(resid, attn, *consts)


L0_TB = 512
FOX_TB = 256
ATTN_TQ = 512
ATTN_TK = 512
ATTN_UNROLL = 4
ATTN_HEADS = 2
ATTN_QPB = 8
SWA_CHUNK = 64


def kernel(x_prompt, x_sample, state_pool, cache_swa_k, cache_swa_v, cache_fox_k, cache_fox_v, cache_fox_logf,
           norm0_g, w_in0, w_pool, pool_scale, swa_qn_g, swa_kn_g, swa_sinks, w_out0,
           norm1_g, w_in1, b_forget, fox_qn_g, fox_kn_g, w_out1):
    nb, seq, _ = x_prompt.shape
    db, dseq, _ = x_sample.shape
    past_len = cache_fox_k.shape[1]
    assert nb == 1 and seq % L0_TB == 0 and seq % ATTN_TQ == 0 and dseq % 8 == 0

    seg = np.kron(np.eye(2, dtype=np.float32), np.ones((HEAD_DIM, HEAD_DIM), np.float32))
    mc = FOX_HEADS * HEAD_DIM
    wts = {
        "g0": norm0_g.reshape(1, D_MODEL), "w_in0": w_in0.astype(BF16), "w_pool": w_pool.astype(BF16),
        "pool_scale": pool_scale.reshape(1, C_POOL),
        "qg0": jnp.tile(swa_qn_g, SWA_HEADS).reshape(1, SWA_HEADS * HEAD_DIM),
        "kg0": jnp.tile(swa_kn_g, SWA_KV_HEADS).reshape(1, LANES),
        "seg": jnp.asarray(seg, BF16), "sinks": swa_sinks, "w_out0": w_out0.astype(BF16),
        "g1": norm1_g.reshape(1, D_MODEL),
        "w_out1": w_out1.astype(BF16),
        "w_f": jnp.pad(w_in1[:, 4 * mc:], ((0, 0), (0, LANES - FOX_HEADS))).astype(BF16),
        "b_f": jnp.pad(b_forget.astype(F32), (0, LANES - FOX_HEADS)).reshape(1, LANES),
        "w_ft": w_in1[:, 4 * mc:].T.astype(BF16),
        "b_fc": jnp.broadcast_to(b_forget.astype(F32).reshape(FOX_HEADS, 1), (FOX_HEADS, LANES)),
        "qg1": jnp.tile(fox_qn_g, FOX_HEADS).reshape(1, mc), "kg1": jnp.tile(fox_kn_g, FOX_HEADS).reshape(1, mc),
        "w_k1": w_in1[:, mc:2 * mc].astype(BF16), "w_g1": w_in1[:, 3 * mc:4 * mc].astype(BF16),
        "w_q1t": w_in1[:, 0:mc].T.astype(BF16), "w_v1t": w_in1[:, 2 * mc:3 * mc].T.astype(BF16),
        "gq1c": jnp.broadcast_to(fox_qn_g.astype(F32).reshape(HEAD_DIM, 1), (HEAD_DIM, LANES)),
    }

    xp = x_prompt.reshape(seq, D_MODEL)
    y0p, pool_p, swk_p, swv_p = _layer0(xp, None, wts, ns=1, ts=L0_TB, chunk=SWA_CHUNK, base_pos=0)
    fk_p, fvt_p, flt_p, qa, ka, va, cum_end = _fox_proj_prompt(y0p, wts, tb=FOX_TB)
    fl_p = jnp.transpose(flt_p)
    fv_p = jnp.transpose(fvt_p.reshape(FOX_HEADS, HEAD_DIM, seq), (2, 0, 1))
    per = ATTN_TK // FOX_TB
    cum_end = cum_end[per - 1::per, :, 0].reshape(-1)
    qk_bound = (1.02 * LOG2E * HEAD_DIM * ATTN_SCALE * jnp.max(jnp.abs(fox_qn_g)) * jnp.max(jnp.abs(fox_kn_g))
                ).astype(F32).reshape(1)
    attn_p = _fox_attn_prompt(cum_end, qk_bound, qa, ka, va, tq=ATTN_TQ, tk=ATTN_TK, unroll=ATTN_UNROLL,
                              nh=ATTN_HEADS, qpb=ATTN_QPB)
    yp = _out_proj(y0p, attn_p, wts, tb=L0_TB)

    xs = x_sample.reshape(db * dseq, D_MODEL)
    prefix = (jnp.pad(state_pool, ((0, 0), (1, 0), (0, 0))),
              cache_swa_k.reshape(db, WINDOW, LANES), cache_swa_v.reshape(db, WINDOW, LANES))
    y0s, pool_s, swk_s, swv_s = _layer0(xs, prefix, wts, ns=db, ts=dseq, chunk=dseq, base_pos=past_len)
    fk_s, fv_s, fl_s, q_s = _fox_proj_sample(y0s, wts, tb=db * dseq)
    attn_s = _fox_attn_sample(
        q_s.reshape(db, dseq, D_MODEL),
        jnp.transpose(cache_fox_k, (0, 2, 3, 1)), jnp.transpose(cache_fox_v, (0, 2, 3, 1)),
        jnp.transpose(cache_fox_logf, (0, 2, 1)),
        fk_s.reshape(db, dseq, D_MODEL), fv_s.reshape(db, dseq, D_MODEL),
        jnp.pad(jnp.transpose(fl_s.reshape(db, dseq, FOX_HEADS), (0, 2, 1)), ((0, 0), (0, 0), (0, NEW_PAD - dseq))))
    ys = _out_proj(y0s, attn_s.reshape(db * dseq, D_MODEL), wts, tb=db * dseq)

    return (yp.reshape(1, seq, D_MODEL), ys.reshape(db, dseq, D_MODEL),
            pool_p[:, 1:], pool_s[:, 1:],
            swk_p.reshape(1, WINDOW, SWA_KV_HEADS, HEAD_DIM), swv_p.reshape(1, WINDOW, SWA_KV_HEADS, HEAD_DIM),
            swk_s.reshape(db, WINDOW, SWA_KV_HEADS, HEAD_DIM), swv_s.reshape(db, WINDOW, SWA_KV_HEADS, HEAD_DIM),
            fk_p.reshape(1, seq, FOX_HEADS, HEAD_DIM), fv_p.reshape(1, seq, FOX_HEADS, HEAD_DIM),
            fl_p.reshape(1, seq, FOX_HEADS),
            fk_s.reshape(db, dseq, FOX_HEADS, HEAD_DIM), fv_s.reshape(db, dseq, FOX_HEADS, HEAD_DIM),
            fl_s.reshape(db, dseq, FOX_HEADS))
```

```python
import functools

import numpy as np
import jax
import jax.numpy as jnp
from jax import lax
from jax.experimental import pallas as pl
from jax.experimental.pallas import tpu as pltpu

F32 = jnp.float32
BF16 = jnp.bfloat16

D_MODEL = 1024
HEAD_DIM = 64
ATTN_SCALE = HEAD_DIM ** -0.5
POOL_WINDOWS = (2, 4, 8, 16)
C_POOL = 512
POOL_PAD = 15
POOL_HIST = POOL_PAD + 1
SWA_HEADS = 8
SWA_KV_HEADS = 2
SWA_REP = 4
WINDOW = 128
FOX_HEADS = 16
NORM_EPS = 1e-6
Q_OFF = C_POOL
K_OFF = Q_OFF + SWA_HEADS * HEAD_DIM
V_OFF = K_OFF + SWA_KV_HEADS * HEAD_DIM
GATE_OFF = V_OFF + SWA_KV_HEADS * HEAD_DIM
AB_IN = GATE_OFF + D_MODEL
NEG_INF = -1e30
LOG2E = 1.4426950408889634
EXP2_ZERO = -152.0

LANES = 128
SWA_KEYS = 256
VMEM_LIMIT = 56 * 1024 * 1024


def _cparams(sem):
    return pltpu.CompilerParams(dimension_semantics=sem, vmem_limit_bytes=VMEM_LIMIT)


def _rms(x, g):
    ms = jnp.mean(x * x, axis=-1, keepdims=True)
    return x * lax.rsqrt(ms + NORM_EPS) * g


def _split3(x):
    hi = x.astype(BF16)
    r = x - hi.astype(F32)
    mid = r.astype(BF16)
    lo = (r - mid.astype(F32)).astype(BF16)
    return hi, mid, lo


def _head_rms_tile(x, seg, g):
    ss = jnp.dot((x * x).astype(BF16), seg, preferred_element_type=F32)
    return x * lax.rsqrt(ss * (1.0 / HEAD_DIM) + NORM_EPS) * g


def _silu(g):
    return g / (1.0 + jnp.exp(-g))


def _l0_kernel(*refs, ns, ts, chunk, has_cache, base_pos, nblk):
    if has_cache:
        (x_ref, pp_ref, kp_ref, vp_ref, g0_ref, win_ref, wpool_ref, pscale_ref, qg_ref, kg_ref, seg_ref,
         bias_ref, sink_ref, wout_ref, y_ref, ps_ref, ks_ref, vs_ref, uext, kext, vext, mix) = refs
    else:
        (x_ref, g0_ref, win_ref, wpool_ref, pscale_ref, qg_ref, kg_ref, seg_ref,
         bias_ref, sink_ref, wout_ref, y_ref, ps_ref, ks_ref, vs_ref, uext, kext, vext, mix) = refs
    padk = SWA_KEYS - WINDOW - chunk
    hist = padk + WINDOW
    i = pl.program_id(0)

    x = x_ref[...]
    h = _rms(x, g0_ref[...]).astype(BF16)
    proj = jnp.dot(h, win_ref[...], preferred_element_type=F32)
    u = proj[:, 0:Q_OFF]
    gate = proj[:, GATE_OFF:AB_IN]
    seg = seg_ref[...]
    qn = [_head_rms_tile(proj[:, Q_OFF + LANES * j:Q_OFF + LANES * (j + 1)], seg,
                         qg_ref[:, LANES * j:LANES * (j + 1)]) * (ATTN_SCALE * LOG2E)
          for j in range(SWA_HEADS // 2)]
    kn = _head_rms_tile(proj[:, K_OFF:V_OFF], seg, kg_ref[...])
    v = proj[:, V_OFF:GATE_OFF]

    if has_cache:
        for s in range(ns):
            uext[s, 0:POOL_HIST, :] = pp_ref[s]
            kext[s, 0:padk, :] = jnp.zeros((padk, LANES), F32)
            vext[s, 0:padk, :] = jnp.zeros((padk, LANES), F32)
            kext[s, padk:hist, :] = kp_ref[s]
            vext[s, padk:hist, :] = vp_ref[s]
    else:
        @pl.when(i == 0)
        def _():
            uext[0, 0:POOL_HIST, :] = jnp.zeros((POOL_HIST, C_POOL), F32)
            kext[0, 0:hist, :] = jnp.zeros((hist, LANES), F32)
            vext[0, 0:hist, :] = jnp.zeros((hist, LANES), F32)

        @pl.when(i > 0)
        def _():
            uext[0, 0:POOL_HIST, :] = uext[0, ts:ts + POOL_HIST, :]
            kext[0, 0:hist, :] = kext[0, ts:ts + hist, :]
            vext[0, 0:hist, :] = vext[0, ts:ts + hist, :]

    for s in range(ns):
        uext[s, POOL_HIST:POOL_HIST + ts, :] = u[s * ts:(s + 1) * ts]
        kext[s, hist:hist + ts, :] = kn[s * ts:(s + 1) * ts]
        vext[s, hist:hist + ts, :] = v[s * ts:(s + 1) * ts]

    pos = base_pos + i * ts + lax.broadcasted_iota(jnp.int32, (ts, LANES), 0)
    for s in range(ns):
        for g, w in enumerate(POOL_WINDOWS):
            cols = slice(LANES * g, LANES * (g + 1))
            acc = uext[s, POOL_HIST:POOL_HIST + ts, cols]
            cur = acc
            for j in range(1, w):
                acc = acc + uext[s, POOL_HIST - j:POOL_HIST - j + ts, cols]
            cnt = jnp.minimum(pos + 1, w).astype(F32)
            mix[s * ts:(s + 1) * ts, cols] = acc / cnt - cur
    for g in range(4):
        cols = slice(LANES * g, LANES * (g + 1))
        d = mix[:, cols].astype(BF16)
        mix[:, cols] = jnp.dot(d, wpool_ref[g], preferred_element_type=F32) * pscale_ref[:, cols]

    rows4 = SWA_REP * chunk
    lo_c = lax.broadcasted_iota(jnp.int32, (chunk, LANES), 1) < HEAD_DIM
    lo_k = lax.broadcasted_iota(jnp.int32, (SWA_KEYS, LANES), 1) < HEAD_DIM
    kj = lax.broadcasted_iota(jnp.int32, (SWA_KEYS, rows4), 0)
    bias = [bias_ref[g] for g in range(SWA_KV_HEADS)]
    sinkc = [sink_ref[g][0:1, :] * LOG2E for g in range(SWA_KV_HEADS)]
    nch = ts // chunk
    units = [(s, c, g) for s in range(ns) for c in range(nch) for g in range(SWA_KV_HEADS)]

    def swa_scores(s, c, g):
        r0 = c * chunk
        rows = slice(s * ts + r0, s * ts + r0 + chunk)
        kwin = kext[s, r0:r0 + SWA_KEYS, :]
        vwin = vext[s, r0:r0 + SWA_KEYS, :]
        krl = pltpu.roll(kwin, HEAD_DIM, 1)
        vrl = pltpu.roll(vwin, HEAD_DIM, 1)
        if g == 0:
            kd = jnp.where(lo_k, kwin, krl).astype(BF16)
            vd = jnp.where(lo_k, vwin, vrl).astype(BF16)
        else:
            kd = jnp.where(lo_k, krl, kwin).astype(BF16)
            vd = jnp.where(lo_k, vrl, vwin).astype(BF16)
        q0 = qn[2 * g][rows]
        q1 = qn[2 * g + 1][rows]
        qs = jnp.concatenate([jnp.where(lo_c, q0, 0.0), jnp.where(lo_c, 0.0, q0),
                              jnp.where(lo_c, q1, 0.0), jnp.where(lo_c, 0.0, q1)], axis=0).astype(BF16)
        return lax.dot_general(kd, qs, (((1,), (1,)), ((), ())), preferred_element_type=F32), vd

    def swa_finish(s, c, g, sc, vd):
        rows = slice(s * ts + c * chunk, s * ts + (c + 1) * chunk)
        sc = sc + bias[g]
        if not has_cache and c * chunk < WINDOW:
            sc = jnp.where(kj >= padk + jnp.maximum(0, WINDOW - (i * nch + c) * chunk), sc, NEG_INF)
        m = jnp.maximum(jnp.max(sc, axis=0, keepdims=True), sinkc[g])
        p = jnp.exp2(sc - m)
        den = jnp.sum(p, axis=0, keepdims=True) + jnp.exp2(sinkc[g] - m)
        wgt = (p / den).astype(BF16)
        o = lax.dot_general(wgt, vd, (((0,), (0,)), ((), ())), preferred_element_type=F32)
        for jj in range(2):
            t = 2 * g + jj
            mix[rows, C_POOL + LANES * t:C_POOL + LANES * (t + 1)] = jnp.where(
                lo_c, o[(2 * jj) * chunk:(2 * jj + 1) * chunk], o[(2 * jj + 1) * chunk:(2 * jj + 2) * chunk])

    pending = swa_scores(*units[0])
    for n, unit in enumerate(units):
        nxt = swa_scores(*units[n + 1]) if n + 1 < len(units) else None
        swa_finish(*unit, *pending)
        pending = nxt

    mixed = (mix[...] * _silu(gate)).astype(BF16)
    y_ref[...] = x + jnp.dot(mixed, wout_ref[...], preferred_element_type=F32)

    @pl.when(i == nblk - 1)
    def _():
        for s in range(ns):
            ps_ref[s] = uext[s, ts:ts + POOL_HIST, :]
            ks_ref[s] = kext[s, padk + ts:padk + ts + WINDOW, :]
            vs_ref[s] = vext[s, padk + ts:padk + ts + WINDOW, :]


def _const_spec(shape):
    nd = len(shape)
    return pl.BlockSpec(shape, lambda i, _nd=nd: (0,) * _nd, pipeline_mode=pl.Buffered(1))


def _layer0(x2d, prefix, wts, *, ns, ts, chunk, base_pos):
    n = x2d.shape[0]
    tb = ns * ts
    nblk = n // tb
    has_cache = prefix is not None
    padk = SWA_KEYS - WINDOW - chunk
    rows4 = SWA_REP * chunk
    slope = np.repeat(2.0 ** (-(np.arange(SWA_HEADS) + 1.0)), chunk).reshape(SWA_KV_HEADS, 1, rows4)
    kj = np.arange(SWA_KEYS).reshape(1, SWA_KEYS, 1)
    rel = (np.arange(rows4) % chunk).reshape(1, 1, rows4) + WINDOW - (kj - padk)
    bias = jnp.asarray(np.where(kj >= padk, -(slope * LOG2E) * np.abs(rel), NEG_INF).astype(np.float32))
    sink = jnp.broadcast_to(jnp.repeat(wts["sinks"].astype(F32), chunk).reshape(SWA_KV_HEADS, 1, rows4),
                            (SWA_KV_HEADS, 8, rows4))
    consts = [wts["g0"], wts["w_in0"], wts["w_pool"], wts["pool_scale"], wts["qg0"], wts["kg0"], wts["seg"],
              bias, sink, wts["w_out0"]]
    in_specs = [pl.BlockSpec((tb, D_MODEL), lambda i: (i, 0))]
    args = [x2d]
    if has_cache:
        for a in prefix:
            in_specs.append(_const_spec(a.shape))
            args.append(a)
    for a in consts:
        in_specs.append(_const_spec(a.shape))
        args.append(a)
    out_shape = [jax.ShapeDtypeStruct((n, D_MODEL), F32),
                 jax.ShapeDtypeStruct((ns, POOL_HIST, C_POOL), F32),
                 jax.ShapeDtypeStruct((ns, WINDOW, LANES), F32),
                 jax.ShapeDtypeStruct((ns, WINDOW, LANES), F32)]
    out_specs = [pl.BlockSpec((tb, D_MODEL), lambda i: (i, 0)),
                 pl.BlockSpec((ns, POOL_HIST, C_POOL), lambda i: (0, 0, 0)),
                 pl.BlockSpec((ns, WINDOW, LANES), lambda i: (0, 0, 0)),
                 pl.BlockSpec((ns, WINDOW, LANES), lambda i: (0, 0, 0))]
    scratch = [pltpu.VMEM((ns, POOL_HIST + ts, C_POOL), F32),
               pltpu.VMEM((ns, padk + WINDOW + ts, LANES), F32),
               pltpu.VMEM((ns, padk + WINDOW + ts, LANES), F32),
               pltpu.VMEM((tb, D_MODEL), F32)]
    kern = functools.partial(_l0_kernel, ns=ns, ts=ts, chunk=chunk, has_cache=has_cache,
                             base_pos=base_pos, nblk=nblk)
    return pl.pallas_call(
        kern, grid=(nblk,), in_specs=in_specs, out_specs=out_specs, out_shape=out_shape,
        scratch_shapes=scratch, compiler_params=_cparams(("arbitrary",)),
        name="layer0_prompt" if not has_cache else "layer0_sample")(*args)


def _log_sigmoid(z):
    return jnp.minimum(z, 0.0) - jnp.log1p(jnp.exp(-jnp.abs(z)))


def _fox_proj_sample_kernel(x_ref, g1_ref, wqt_ref, wk_ref, wvt_ref, wf_ref, bf_ref, qg_ref, kg_ref, seg_ref,
                            k_ref, v_ref, lf_ref, q_ref):
    x = x_ref[...]
    h = _rms(x, g1_ref[...]).astype(BF16)
    z = jnp.dot(h, wf_ref[...], preferred_element_type=F32) + bf_ref[...]
    lf_ref[...] = _log_sigmoid(z)[:, 0:FOX_HEADS]
    seg = seg_ref[...]
    nt = (((1,), (1,)), ((), ()))
    q = lax.dot_general(h, wqt_ref[...], nt, preferred_element_type=F32)
    k = jnp.dot(h, wk_ref[...], preferred_element_type=F32)
    v_ref[...] = lax.dot_general(h, wvt_ref[...], nt, preferred_element_type=F32)
    for j in range(FOX_HEADS // 2):
        cols = slice(LANES * j, LANES * (j + 1))
        q_ref[:, cols] = _head_rms_tile(q[:, cols], seg, qg_ref[:, cols]) * ATTN_SCALE
        k_ref[:, cols] = _head_rms_tile(k[:, cols], seg, kg_ref[:, cols])


def _fox_proj_sample(x2d, wts, *, tb):
    n = x2d.shape[0]
    consts = [wts["g1"], wts["w_q1t"], wts["w_k1"], wts["w_v1t"], wts["w_f"], wts["b_f"], wts["qg1"], wts["kg1"],
              wts["seg"]]
    row_spec = pl.BlockSpec((tb, D_MODEL), lambda i: (i, 0))
    in_specs = [row_spec] + [_const_spec(a.shape) for a in consts]
    row = jax.ShapeDtypeStruct((n, D_MODEL), F32)
    out_shape = [row, row, jax.ShapeDtypeStruct((n, FOX_HEADS), F32), row]
    out_specs = [row_spec, row_spec, pl.BlockSpec((tb, FOX_HEADS), lambda i: (i, 0)), row_spec]
    return pl.pallas_call(
        _fox_proj_sample_kernel, grid=(n // tb,), in_specs=in_specs, out_specs=out_specs, out_shape=out_shape,
        compiler_params=_cparams(("arbitrary",)), name="fox_proj_sample")(x2d, *consts)


def _fox_proj_prompt_kernel(x_ref, g1_ref, wk_ref, wft_ref, bfc_ref, wqt_ref, wvt_ref, gq_ref, kg_ref,
                            seg_ref, tri_ref, e_ref,
                            k_ref, vt_ref, lft_ref, qa_ref, ka_ref, va_ref, ce_ref, carry, *, tb):
    i = pl.program_id(0)
    x = x_ref[...]
    hf = _rms(x, g1_ref[...])
    h = hf.astype(BF16)
    ht = hf.T.astype(BF16)
    bcol = jnp.concatenate([bfc_ref[...]] * (tb // LANES), axis=1)
    logft = _log_sigmoid(jnp.dot(wft_ref[...], ht, preferred_element_type=F32) + bcol)
    lft_ref[...] = logft
    k = jnp.dot(h, wk_ref[...], preferred_element_type=F32)

    @pl.when(i == 0)
    def _():
        carry[...] = jnp.zeros(carry.shape, F32)

    c3 = jnp.dot(jnp.concatenate(_split3(logft), axis=0), tri_ref[...], preferred_element_type=F32)

    qt = jnp.dot(wqt_ref[...], ht, preferred_element_type=F32)
    vt = jnp.dot(wvt_ref[...], ht, preferred_element_type=F32)
    vt_ref[...] = vt
    srow = lax.broadcasted_iota(jnp.int32, (HEAD_DIM, tb), 0)
    q_aug = jnp.where(srow < 3, 1.0, 0.0)
    v_aug = jnp.where(srow == 0, 1.0, 0.0)
    gq = jnp.concatenate([gq_ref[...]] * (tb // LANES), axis=1) * (ATTN_SCALE * LOG2E)
    for hd in range(FOX_HEADS):
        rows = slice(HEAD_DIM * hd, HEAD_DIM * (hd + 1))
        qh = qt[rows]
        ss = jnp.sum(qh * qh, axis=0, keepdims=True)
        qn = qh * lax.rsqrt(ss * (1.0 / HEAD_DIM) + NORM_EPS) * gq
        qa_ref[hd] = jnp.concatenate([qn, q_aug], axis=0).astype(BF16)
        va_ref[hd] = jnp.concatenate([vt[rows], v_aug], axis=0).astype(BF16)

    cumt = c3[0:FOX_HEADS] + c3[FOX_HEADS:2 * FOX_HEADS] + c3[2 * FOX_HEADS:3 * FOX_HEADS] + carry[:, 0:1]
    end = jnp.broadcast_to(cumt[:, tb - 1:tb], (FOX_HEADS, LANES))
    carry[...] = end
    ce_ref[0] = end
    parts = [p.astype(F32) for p in _split3(-LOG2E * cumt)]
    n3 = jnp.concatenate(parts + [jnp.zeros((LANES - 3 * FOX_HEADS, tb), F32)], axis=0).T.astype(BF16)
    biasall = jnp.dot(n3, e_ref[...], preferred_element_type=F32)
    lo_l = lax.broadcasted_iota(jnp.int32, (tb, LANES), 1) < HEAD_DIM
    seg = seg_ref[...]
    for j in range(FOX_HEADS // 2):
        cols = slice(LANES * j, LANES * (j + 1))
        kt = _head_rms_tile(k[:, cols], seg, kg_ref[:, cols])
        k_ref[:, cols] = kt
        bt = biasall[:, cols]
        ka_ref[2 * j] = jnp.where(lo_l, kt, bt).astype(BF16)
        ka_ref[2 * j + 1] = pltpu.roll(jnp.where(lo_l, bt, kt), HEAD_DIM, 1).astype(BF16)


def _fox_proj_prompt(x2d, wts, *, tb):
    n = x2d.shape[0]
    tri = jnp.asarray(np.triu(np.ones((tb, tb), np.float32)), BF16)
    e = np.zeros((LANES, FOX_HEADS // 2 * LANES), np.float32)
    for part in range(3):
        for hd in range(FOX_HEADS):
            e[part * FOX_HEADS + hd, (hd // 2) * LANES + (HEAD_DIM if hd % 2 == 0 else 0) + part] = 1.0
    consts = [wts["g1"], wts["w_k1"], wts["w_ft"], wts["b_fc"], wts["w_q1t"], wts["w_v1t"],
              wts["gq1c"], wts["kg1"], wts["seg"], tri, jnp.asarray(e, BF16)]
    row_spec = pl.BlockSpec((tb, D_MODEL), lambda i: (i, 0))
    in_specs = [row_spec] + [_const_spec(a.shape) for a in consts]
    row = jax.ShapeDtypeStruct((n, D_MODEL), F32)
    out_shape = [row, jax.ShapeDtypeStruct((D_MODEL, n), F32), jax.ShapeDtypeStruct((FOX_HEADS, n), F32),
                 jax.ShapeDtypeStruct((FOX_HEADS, LANES, n), BF16), jax.ShapeDtypeStruct((FOX_HEADS, n, LANES), BF16),
                 jax.ShapeDtypeStruct((FOX_HEADS, LANES, n), BF16),
                 jax.ShapeDtypeStruct((n // tb, FOX_HEADS, LANES), F32)]
    tspec = pl.BlockSpec((FOX_HEADS, LANES, tb), lambda i: (0, 0, i))
    out_specs = [row_spec, pl.BlockSpec((D_MODEL, tb), lambda i: (0, i)),
                 pl.BlockSpec((FOX_HEADS, tb), lambda i: (0, i)),
                 tspec, pl.BlockSpec((FOX_HEADS, tb, LANES), lambda i: (0, i, 0)), tspec,
                 pl.BlockSpec((1, FOX_HEADS, LANES), lambda i: (i, 0, 0))]
    return pl.pallas_call(
        functools.partial(_fox_proj_prompt_kernel, tb=tb), grid=(n // tb,), in_specs=in_specs,
        out_specs=out_specs, out_shape=out_shape, scratch_shapes=[pltpu.VMEM((FOX_HEADS, LANES), F32)],
        compiler_params=_cparams(("arbitrary",)), name="fox_proj_prompt")(x2d, *consts)


def _fox_attn_kernel(ce_ref, qk_ref, q_ref, k_ref, v_ref, o_ref, m_scr, al_scr, acc_scr, s_scr,
                     *, tq, qpb, **kw):
    def blk(qb):
        return _fox_attn_block(pl.program_id(0), pl.program_id(1) * qpb + qb, pl.multiple_of(qb * tq, tq),
                               ce_ref, qk_ref, q_ref, k_ref, v_ref, o_ref, m_scr, al_scr, acc_scr, s_scr,
                               tq=tq, **kw)

    blk(0)[0](None)

    def body(qb, carry):
        blk(qb)[0](blk(qb - 1)[1])
        return carry

    lax.fori_loop(1, qpb, body, 0)
    blk(qpb - 1)[1]()


def _fox_attn_block(pair, qi, qoff, ce_ref, qk_ref, q_ref, k_ref, v_ref, o_ref, m_scr, al_scr, acc_scr, s_scr,
                    *, tq, tk, unroll, nkb, nh):
    n_u = qi

    last = jnp.maximum(n_u - 1, 0)
    slack = 2.0 * qk_ref[0]

    def dead(j):
        jj = jnp.minimum(j, last)
        ok = j < n_u - 1
        for hh in range(nh):
            hd = nh * pair + hh
            gap = LOG2E * (ce_ref[last * FOX_HEADS + hd] - ce_ref[jj * FOX_HEADS + hd])
            ok = ok & (slack + gap < EXP2_ZERO)
        return ok

    lo = jnp.int32(0)
    hi = last
    for _ in range(max(1, (nkb - 1).bit_length())):
        mid = (lo + hi) >> 1
        d = dead(mid)
        lo = jnp.where(d, mid + 1, lo)
        hi = jnp.where(d, hi, mid)
    first = lo

    def score_matmuls(kb):
        off = pl.multiple_of(kb * tk, tk)
        return [jnp.dot(k_ref[hh, pl.ds(off, tk), :], q_ref[hh, :, pl.ds(qoff, tq)],
                        preferred_element_type=F32)
                for hh in range(nh)]

    def score_finish(sts):
        for hh in range(nh):
            st = sts[hh]
            m_old = m_scr[hh, 0:1, :]
            m_new = jnp.maximum(m_old, jnp.max(st, axis=0, keepdims=True))
            al_scr[hh, 0:1, :] = jnp.exp2(m_old - m_new)
            m_scr[hh, 0:1, :] = m_new
            s_scr[hh] = st

    def accumulate(kb):
        off = pl.multiple_of(kb * tk, tk)
        for hh in range(nh):
            p = jnp.exp2(s_scr[hh] - m_scr[hh, 0:1, :]).astype(BF16)
            pv = jnp.dot(v_ref[hh, :, pl.ds(off, tk)], p, preferred_element_type=F32)
            acc_scr[hh] = al_scr[hh, 0:1, :] * acc_scr[hh] + pv

    def advance(kb):
        sts = score_matmuls(kb + 1)
        accumulate(kb)
        score_finish(sts)

    def tail():
        half = tq // 2
        off = pl.multiple_of(n_u * tk, tk)
        sts = [(jnp.dot(k_ref[hh, pl.ds(off, half), :], q_ref[hh, :, pl.ds(qoff, half)],
                        preferred_element_type=F32),
                jnp.dot(k_ref[hh, pl.ds(off, tk), :], q_ref[hh, :, pl.ds(pl.multiple_of(qoff + half, half), half)],
                        preferred_element_type=F32))
               for hh in range(nh)]
        accumulate(last)
        tri = jnp.where(lax.broadcasted_iota(jnp.int32, (half, half), 0)
                        <= lax.broadcasted_iota(jnp.int32, (half, half), 1), 0.0, NEG_INF)
        for hh in range(nh):
            st_a, st_b = sts[hh]
            halves = ((st_a + tri, 0), (jnp.concatenate([st_b[0:half], st_b[half:tk] + tri], axis=0), half))
            for st, c0 in halves:
                nk = st.shape[0]
                m_old = m_scr[hh, 0:1, c0:c0 + half]
                m_new = jnp.maximum(m_old, jnp.max(st, axis=0, keepdims=True))
                p = jnp.exp2(st - m_new).astype(BF16)
                pv = jnp.dot(v_ref[hh, :, pl.ds(off, nk)], p, preferred_element_type=F32)
                acc_scr[hh, :, c0:c0 + half] = jnp.exp2(m_old - m_new) * acc_scr[hh, :, c0:c0 + half] + pv
        for t in range(nh // 2):
            pair_rows = []
            for hh in (2 * t, 2 * t + 1):
                a = acc_scr[hh]
                pair_rows.append(a[0:HEAD_DIM] / a[HEAD_DIM:HEAD_DIM + 1, :])
            o_ref[pl.ds(qoff, tq), LANES * t:LANES * (t + 1)] = jnp.concatenate(
                pair_rows, axis=0).T.astype(o_ref.dtype)

    def head(prev_tail):
        def start():
            if prev_tail is not None:
                prev_tail()
            m_scr[...] = jnp.full(m_scr.shape, NEG_INF, F32)
            acc_scr[...] = jnp.zeros(acc_scr.shape, F32)

        def below_diagonal():
            n_adv = n_u - 1 - first
            rem = lax.rem(n_adv, unroll)
            for k in range(unroll):
                @pl.when(rem == k)
                def _():
                    start()
                    score_finish(score_matmuls(first))
                    for u in range(k):
                        advance(first + u)

            def body(t, carry):
                for u in range(unroll):
                    advance(first + rem + unroll * t + u)
                return carry

            lax.fori_loop(0, n_adv // unroll, body, 0)

        if prev_tail is None:
            @pl.when(qi == 0)
            def _():
                start()
                m_scr[...] = jnp.zeros(m_scr.shape, F32)
                al_scr[...] = jnp.ones(al_scr.shape, F32)
                s_scr[...] = jnp.full(s_scr.shape, NEG_INF, F32)

            pl.when(qi > 0)(below_diagonal)
        else:
            below_diagonal()

    return head, tail


def _fox_attn_prompt(cum_end, qk_bound, qa, ka, va, *, tq, tk, unroll, nh, qpb):
    n = ka.shape[1]
    assert tq == tk and n % (qpb * tq) == 0
    resident = dict(pipeline_mode=pl.Buffered(1)) if nh > 2 else {}
    grid_spec = pltpu.PrefetchScalarGridSpec(
        num_scalar_prefetch=2, grid=(FOX_HEADS // nh, n // (qpb * tq)),
        in_specs=[pl.BlockSpec((nh, LANES, qpb * tq), lambda p, i, ce, qk: (p, 0, i)),
                  pl.BlockSpec((nh, n, LANES), lambda p, i, ce, qk: (p, 0, 0), **resident),
                  pl.BlockSpec((nh, LANES, n), lambda p, i, ce, qk: (p, 0, 0), **resident)],
        out_specs=pl.BlockSpec((qpb * tq, HEAD_DIM * nh), lambda p, i, ce, qk: (i, p)),
        scratch_shapes=[pltpu.VMEM((nh, 8, tq), F32), pltpu.VMEM((nh, 8, tq), F32),
                        pltpu.VMEM((nh, LANES, tq), F32), pltpu.VMEM((nh, tk, tq), F32)])
    return pl.pallas_call(
        functools.partial(_fox_attn_kernel, tq=tq, tk=tk, unroll=unroll, nkb=n // tk, nh=nh, qpb=qpb),
        grid_spec=grid_spec,
        out_shape=jax.ShapeDtypeStruct((n, D_MODEL), BF16),
        compiler_params=_cparams(("arbitrary", "arbitrary")), name="fox_attn_prompt")(
            cum_end, qk_bound, qa, ka, va)


CUM_BLK = 512
NEW_PAD = 128


def _fox_sample_kernel(q_ref, kc_ref, vc_ref, lfc_ref, kn_ref, vn_ref, lfn_ref, tri_ref, o_ref, negc, negn,
                       *, t_new, p_len):
    qd = pl.program_id(1)
    rows = 4 * t_new

    @pl.when(qd == 0)
    def _():
        tri = tri_ref[...]

        def prefix_sums(x, t):
            c3 = jnp.dot(jnp.concatenate(_split3(x), axis=0), t, preferred_element_type=F32)
            return c3[0:FOX_HEADS] + c3[FOX_HEADS:2 * FOX_HEADS] + c3[2 * FOX_HEADS:3 * FOX_HEADS]

        carry = jnp.zeros((FOX_HEADS, 1), F32)
        for b in range(p_len // CUM_BLK):
            c = prefix_sums(lfc_ref[0, :, b * CUM_BLK:(b + 1) * CUM_BLK], tri) + carry
            negc[:, b * CUM_BLK:(b + 1) * CUM_BLK] = -c
            carry = c[:, CUM_BLK - 1:CUM_BLK]
        negn[...] = -(prefix_sums(lfn_ref[0], tri[0:NEW_PAD, 0:NEW_PAD]) + carry)

    qq = q_ref[0]
    rhead = lax.broadcasted_iota(jnp.int32, (rows, 2 * LANES), 0) // t_new
    lhead = lax.broadcasted_iota(jnp.int32, (rows, 2 * LANES), 1) // HEAD_DIM
    qbd = jnp.where(rhead == lhead, jnp.concatenate([qq] * 4, axis=0), 0.0).astype(BF16)

    kct = kc_ref[0].reshape(2 * LANES, p_len).astype(BF16)
    vct = vc_ref[0].reshape(2 * LANES, p_len).astype(BF16)
    s_c = jnp.dot(qbd, kct, preferred_element_type=F32)
    zpad = jnp.zeros((NEW_PAD - t_new, 2 * LANES), F32)
    kn = jnp.concatenate([kn_ref[0], zpad], axis=0).astype(BF16)
    vn = jnp.concatenate([vn_ref[0], zpad], axis=0).astype(BF16)
    s_n = lax.dot_general(qbd, kn, (((1,), (1,)), ((), ())), preferred_element_type=F32)
    bc = jnp.concatenate([jnp.broadcast_to(negc[pl.ds(4 * qd + jh, 1), :], (t_new, p_len)) for jh in range(4)], axis=0)
    bn = jnp.concatenate([jnp.broadcast_to(negn[pl.ds(4 * qd + jh, 1), :], (t_new, NEW_PAD)) for jh in range(4)], axis=0)
    s_c = s_c + bc
    qrow = lax.broadcasted_iota(jnp.int32, (rows, NEW_PAD), 0) % t_new
    kcol = lax.broadcasted_iota(jnp.int32, (rows, NEW_PAD), 1)
    s_n = jnp.where(kcol <= qrow, s_n + bn, NEG_INF)
    m = jnp.maximum(jnp.max(s_c, axis=-1, keepdims=True), jnp.max(s_n, axis=-1, keepdims=True))
    p_c = jnp.exp(s_c - m)
    p_n = jnp.exp(s_n - m)
    den = jnp.sum(p_c, axis=-1, keepdims=True) + jnp.sum(p_n, axis=-1, keepdims=True)
    o = (lax.dot_general(p_c.astype(BF16), vct, (((1,), (1,)), ((), ())), preferred_element_type=F32)
         + jnp.dot(p_n.astype(BF16), vn, preferred_element_type=F32)) / den
    lh = lax.broadcasted_iota(jnp.int32, (t_new, 2 * LANES), 1) // HEAD_DIM
    out = jnp.zeros((t_new, 2 * LANES), F32)
    for jh in range(4):
        out = jnp.where(lh == jh, o[jh * t_new:(jh + 1) * t_new], out)
    o_ref[0] = out


def _fox_attn_sample(q, cache_k, cache_v, lfc_t, k_new, v_new, lfn_t):
    b, t_new, _ = q.shape
    p_len = cache_k.shape[3]
    tri = jnp.asarray(np.triu(np.ones((CUM_BLK, CUM_BLK), np.float32)), BF16)
    quad = 2 * LANES
    in_specs = [pl.BlockSpec((1, t_new, quad), lambda r, d: (r, 0, d)),
                pl.BlockSpec((1, 4, HEAD_DIM, p_len), lambda r, d: (r, d, 0, 0)),
                pl.BlockSpec((1, 4, HEAD_DIM, p_len), lambda r, d: (r, d, 0, 0)),
                pl.BlockSpec((1, FOX_HEADS, p_len), lambda r, d: (r, 0, 0)),
                pl.BlockSpec((1, t_new, quad), lambda r, d: (r, 0, d)),
                pl.BlockSpec((1, t_new, quad), lambda r, d: (r, 0, d)),
                pl.BlockSpec((1, FOX_HEADS, NEW_PAD), lambda r, d: (r, 0, 0)),
                pl.BlockSpec((CUM_BLK, CUM_BLK), lambda r, d: (0, 0))]
    return pl.pallas_call(
        functools.partial(_fox_sample_kernel, t_new=t_new, p_len=p_len),
        grid=(b, D_MODEL // quad), in_specs=in_specs,
        out_specs=pl.BlockSpec((1, t_new, quad), lambda r, d: (r, 0, d)),
        out_shape=jax.ShapeDtypeStruct((b, t_new, D_MODEL), F32),
        scratch_shapes=[pltpu.VMEM((FOX_HEADS, p_len), F32), pltpu.VMEM((FOX_HEADS, NEW_PAD), F32)],
        compiler_params=_cparams(("arbitrary", "arbitrary")), name="fox_attn_sample")(
            q, cache_k, cache_v, lfc_t, k_new, v_new, lfn_t, tri)


def _out_proj_kernel(r_ref, a_ref, g1_ref, wg_ref, w_ref, y_ref):
    x = r_ref[...]
    h = _rms(x, g1_ref[...]).astype(BF16)
    gate = jnp.dot(h, wg_ref[...], preferred_element_type=F32)
    mixed = (a_ref[...].astype(F32) * _silu(gate)).astype(BF16)
    y_ref[...] = x + jnp.dot(mixed, w_ref[...], preferred_element_type=F32)


def _out_proj(resid, attn, wts, *, tb):
    n = resid.shape[0]
    row = pl.BlockSpec((tb, D_MODEL), lambda i: (i, 0))
    consts = [wts["g1"], wts["w_g1"], wts["w_out1"]]
    return pl.pallas_call(
        _out_proj_kernel, grid=(n // tb,), in_specs=[row, row] + [_const_spec(a.shape) for a in consts],
        out_specs=row, out_shape=jax.ShapeDtypeStruct((n, D_MODEL), F32),
        compiler_params=_cparams(("arbitrary",)), name="out_proj")(resid, attn, *consts)


L0_TB = 512
FOX_TB = 256
ATTN_TQ = 512
ATTN_TK = 512
ATTN_UNROLL = 4
ATTN_HEADS = 2
ATTN_QPB = 8
SWA_CHUNK = 64


def kernel(x_prompt, x_sample, state_pool, cache_swa_k, cache_swa_v, cache_fox_k, cache_fox_v, cache_fox_logf,
           norm0_g, w_in0, w_pool, pool_scale, swa_qn_g, swa_kn_g, swa_sinks, w_out0,
           norm1_g, w_in1, b_forget, fox_qn_g, fox_kn_g, w_out1):
    nb, seq, _ = x_prompt.shape
    db, dseq, _ = x_sample.shape
    past_len = cache_fox_k.shape[1]
    assert nb == 1 and seq % L0_TB == 0 and seq % ATTN_TQ == 0 and dseq % 8 == 0

    seg = np.kron(np.eye(2, dtype=np.float32), np.ones((HEAD_DIM, HEAD_DIM), np.float32))
    mc = FOX_HEADS * HEAD_DIM
    wts = {
        "g0": norm0_g.reshape(1, D_MODEL), "w_in0": w_in0.astype(BF16), "w_pool": w_pool.astype(BF16),
        "pool_scale": pool_scale.reshape(1, C_POOL),
        "qg0": jnp.tile(swa_qn_g, SWA_HEADS).reshape(1, SWA_HEADS * HEAD_DIM),
        "kg0": jnp.tile(swa_kn_g, SWA_KV_HEADS).reshape(1, LANES),
        "seg": jnp.asarray(seg, BF16), "sinks": swa_sinks, "w_out0": w_out0.astype(BF16),
        "g1": norm1_g.reshape(1, D_MODEL),
        "w_out1": w_out1.astype(BF16),
        "w_f": jnp.pad(w_in1[:, 4 * mc:], ((0, 0), (0, LANES - FOX_HEADS))).astype(BF16),
        "b_f": jnp.pad(b_forget.astype(F32), (0, LANES - FOX_HEADS)).reshape(1, LANES),
        "w_ft": w_in1[:, 4 * mc:].T.astype(BF16),
        "b_fc": jnp.broadcast_to(b_forget.astype(F32).reshape(FOX_HEADS, 1), (FOX_HEADS, LANES)),
        "qg1": jnp.tile(fox_qn_g, FOX_HEADS).reshape(1, mc), "kg1": jnp.tile(fox_kn_g, FOX_HEADS).reshape(1, mc),
        "w_k1": w_in1[:, mc:2 * mc].astype(BF16), "w_g1": w_in1[:, 3 * mc:4 * mc].astype(BF16),
        "w_q1t": w_in1[:, 0:mc].T.astype(BF16), "w_v1t": w_in1[:, 2 * mc:3 * mc].T.astype(BF16),
        "gq1c": jnp.broadcast_to(fox_qn_g.astype(F32).reshape(HEAD_DIM, 1), (HEAD_DIM, LANES)),
    }

    xp = x_prompt.reshape(seq, D_MODEL)
    y0p, pool_p, swk_p, swv_p = _layer0(xp, None, wts, ns=1, ts=L0_TB, chunk=SWA_CHUNK, base_pos=0)
    fk_p, fvt_p, flt_p, qa, ka, va, cum_end = _fox_proj_prompt(y0p, wts, tb=FOX_TB)
    fl_p = jnp.transpose(flt_p)
    fv_p = jnp.transpose(fvt_p.reshape(FOX_HEADS, HEAD_DIM, seq), (2, 0, 1))
    per = ATTN_TK // FOX_TB
    cum_end = cum_end[per - 1::per, :, 0].reshape(-1)
    qk_bound = (1.02 * LOG2E * HEAD_DIM * ATTN_SCALE * jnp.max(jnp.abs(fox_qn_g)) * jnp.max(jnp.abs(fox_kn_g))
                ).astype(F32).reshape(1)
    attn_p = _fox_attn_prompt(cum_end, qk_bound, qa, ka, va, tq=ATTN_TQ, tk=ATTN_TK, unroll=ATTN_UNROLL,
                              nh=ATTN_HEADS, qpb=ATTN_QPB)
    yp = _out_proj(y0p, attn_p, wts, tb=L0_TB)

    xs = x_sample.reshape(db * dseq, D_MODEL)
    prefix = (jnp.pad(state_pool, ((0, 0), (POOL_HIST - POOL_PAD, 0), (0, 0))),
              cache_swa_k.reshape(db, WINDOW, LANES), cache_swa_v.reshape(db, WINDOW, LANES))
    y0s, pool_s, swk_s, swv_s = _layer0(xs, prefix, wts, ns=db, ts=dseq, chunk=dseq, base_pos=past_len)
    fk_s, fv_s, fl_s, q_s = _fox_proj_sample(y0s, wts, tb=db * dseq)
    attn_s = _fox_attn_sample(
        q_s.reshape(db, dseq, D_MODEL),
        jnp.transpose(cache_fox_k, (0, 2, 3, 1)), jnp.transpose(cache_fox_v, (0, 2, 3, 1)),
        jnp.transpose(cache_fox_logf, (0, 2, 1)),
        fk_s.reshape(db, dseq, D_MODEL), fv_s.reshape(db, dseq, D_MODEL),
        jnp.pad(jnp.transpose(fl_s.reshape(db, dseq, FOX_HEADS), (0, 2, 1)), ((0, 0), (0, 0), (0, NEW_PAD - dseq))))
    ys = _out_proj(y0s, attn_s.reshape(db * dseq, D_MODEL), wts, tb=db * dseq)

    return (yp.reshape(1, seq, D_MODEL), ys.reshape(db, dseq, D_MODEL),
            pool_p[:, POOL_HIST - POOL_PAD:], pool_s[:, POOL_HIST - POOL_PAD:],
            swk_p.reshape(1, WINDOW, SWA_KV_HEADS, HEAD_DIM), swv_p.reshape(1, WINDOW, SWA_KV_HEADS, HEAD_DIM),
            swk_s.reshape(db, WINDOW, SWA_KV_HEADS, HEAD_DIM), swv_s.reshape(db, WINDOW, SWA_KV_HEADS, HEAD_DIM),
            fk_p.reshape(1, seq, FOX_HEADS, HEAD_DIM), fv_p.reshape(1, seq, FOX_HEADS, HEAD_DIM),
            fl_p.reshape(1, seq, FOX_HEADS),
            fk_s.reshape(db, dseq, FOX_HEADS, HEAD_DIM), fv_s.reshape(db, dseq, FOX_HEADS, HEAD_DIM),
            fl_s.reshape(db, dseq, FOX_HEADS))
```

```python
import functools

import numpy as np
import jax
import jax.numpy as jnp
from jax import lax
from jax.experimental import pallas as pl
from jax.experimental.pallas import tpu as pltpu

F32 = jnp.float32
BF16 = jnp.bfloat16

D_MODEL = 1024
HEAD_DIM = 64
ATTN_SCALE = HEAD_DIM ** -0.5
POOL_WINDOWS = (2, 4, 8, 16)
C_POOL = 512
POOL_PAD = 15
POOL_HIST = POOL_PAD + 1
SWA_HEADS = 8
SWA_KV_HEADS = 2
SWA_REP = 4
WINDOW = 128
FOX_HEADS = 16
NORM_EPS = 1e-6
Q_OFF = C_POOL
K_OFF = Q_OFF + SWA_HEADS * HEAD_DIM
V_OFF = K_OFF + SWA_KV_HEADS * HEAD_DIM
GATE_OFF = V_OFF + SWA_KV_HEADS * HEAD_DIM
AB_IN = GATE_OFF + D_MODEL
NEG_INF = -1e30
LOG2E = 1.4426950408889634
EXP2_ZERO = -152.0

LANES = 128
SWA_KEYS = 256
VMEM_LIMIT = 56 * 1024 * 1024


def _cparams(sem):
    return pltpu.CompilerParams(dimension_semantics=sem, vmem_limit_bytes=VMEM_LIMIT)


def _rms(x, g):
    ms = jnp.mean(x * x, axis=-1, keepdims=True)
    return x * lax.rsqrt(ms + NORM_EPS) * g


def _split3(x):
    hi = x.astype(BF16)
    r = x - hi.astype(F32)
    mid = r.astype(BF16)
    lo = (r - mid.astype(F32)).astype(BF16)
    return hi, mid, lo


def _head_rms_tile(x, seg, g):
    ss = jnp.dot((x * x).astype(BF16), seg, preferred_element_type=F32)
    return x * lax.rsqrt(ss * (1.0 / HEAD_DIM) + NORM_EPS) * g


def _silu(g):
    return g / (1.0 + jnp.exp(-g))


def _l0_kernel(*refs, ns, ts, chunk, has_cache, base_pos, nblk):
    if has_cache:
        (x_ref, pp_ref, kp_ref, vp_ref, g0_ref, win_ref, wpool_ref, pscale_ref, qg_ref, kg_ref, seg_ref,
         bias_ref, sink_ref, wout_ref, y_ref, ps_ref, ks_ref, vs_ref, uext, kext, vext, mix) = refs
    else:
        (x_ref, g0_ref, win_ref, wpool_ref, pscale_ref, qg_ref, kg_ref, seg_ref,
         bias_ref, sink_ref, wout_ref, y_ref, ps_ref, ks_ref, vs_ref, uext, kext, vext, mix) = refs
    padk = SWA_KEYS - WINDOW - chunk
    hist = padk + WINDOW
    i = pl.program_id(0)

    x = x_ref[...]
    h = _rms(x, g0_ref[...]).astype(BF16)
    proj = jnp.dot(h, win_ref[...], preferred_element_type=F32)
    u = proj[:, 0:Q_OFF]
    gate = proj[:, GATE_OFF:AB_IN]
    seg = seg_ref[...]
    qn = [_head_rms_tile(proj[:, Q_OFF + LANES * j:Q_OFF + LANES * (j + 1)], seg,
                         qg_ref[:, LANES * j:LANES * (j + 1)]) * (ATTN_SCALE * LOG2E)
          for j in range(SWA_HEADS // 2)]
    kn = _head_rms_tile(proj[:, K_OFF:V_OFF], seg, kg_ref[...])
    v = proj[:, V_OFF:GATE_OFF]

    if has_cache:
        for s in range(ns):
            uext[s, 0:POOL_HIST, :] = pp_ref[s]
            kext[s, 0:padk, :] = jnp.zeros((padk, LANES), F32)
            vext[s, 0:padk, :] = jnp.zeros((padk, LANES), F32)
            kext[s, padk:hist, :] = kp_ref[s]
            vext[s, padk:hist, :] = vp_ref[s]
    else:
        @pl.when(i == 0)
        def _():
            uext[0, 0:POOL_HIST, :] = jnp.zeros((POOL_HIST, C_POOL), F32)
            kext[0, 0:hist, :] = jnp.zeros((hist, LANES), F32)
            vext[0, 0:hist, :] = jnp.zeros((hist, LANES), F32)

        @pl.when(i > 0)
        def _():
            uext[0, 0:POOL_HIST, :] = uext[0, ts:ts + POOL_HIST, :]
            kext[0, 0:hist, :] = kext[0, ts:ts + hist, :]
            vext[0, 0:hist, :] = vext[0, ts:ts + hist, :]

    for s in range(ns):
        uext[s, POOL_HIST:POOL_HIST + ts, :] = u[s * ts:(s + 1) * ts]
        kext[s, hist:hist + ts, :] = kn[s * ts:(s + 1) * ts]
        vext[s, hist:hist + ts, :] = v[s * ts:(s + 1) * ts]

    pos = base_pos + i * ts + lax.broadcasted_iota(jnp.int32, (ts, LANES), 0)
    for s in range(ns):
        for g, w in enumerate(POOL_WINDOWS):
            cols = slice(LANES * g, LANES * (g + 1))
            acc = uext[s, POOL_HIST:POOL_HIST + ts, cols]
            cur = acc
            for j in range(1, w):
                acc = acc + uext[s, POOL_HIST - j:POOL_HIST - j + ts, cols]
            cnt = jnp.minimum(pos + 1, w).astype(F32)
            mix[s * ts:(s + 1) * ts, cols] = acc / cnt - cur
    for g in range(4):
        cols = slice(LANES * g, LANES * (g + 1))
        d = mix[:, cols].astype(BF16)
        mix[:, cols] = jnp.dot(d, wpool_ref[g], preferred_element_type=F32) * pscale_ref[:, cols]

    rows4 = SWA_REP * chunk
    lo_c = lax.broadcasted_iota(jnp.int32, (chunk, LANES), 1) < HEAD_DIM
    lo_k = lax.broadcasted_iota(jnp.int32, (SWA_KEYS, LANES), 1) < HEAD_DIM
    kj = lax.broadcasted_iota(jnp.int32, (SWA_KEYS, rows4), 0)
    bias = [bias_ref[g] for g in range(SWA_KV_HEADS)]
    sinkc = [sink_ref[g][0:1, :] * LOG2E for g in range(SWA_KV_HEADS)]
    nch = ts // chunk
    units = [(s, c, g) for s in range(ns) for c in range(nch) for g in range(SWA_KV_HEADS)]

    def swa_scores(s, c, g):
        r0 = c * chunk
        rows = slice(s * ts + r0, s * ts + r0 + chunk)
        kwin = kext[s, r0:r0 + SWA_KEYS, :]
        vwin = vext[s, r0:r0 + SWA_KEYS, :]
        krl = pltpu.roll(kwin, HEAD_DIM, 1)
        vrl = pltpu.roll(vwin, HEAD_DIM, 1)
        if g == 0:
            kd = jnp.where(lo_k, kwin, krl).astype(BF16)
            vd = jnp.where(lo_k, vwin, vrl).astype(BF16)
        else:
            kd = jnp.where(lo_k, krl, kwin).astype(BF16)
            vd = jnp.where(lo_k, vrl, vwin).astype(BF16)
        q0 = qn[2 * g][rows]
        q1 = qn[2 * g + 1][rows]
        qs = jnp.concatenate([jnp.where(lo_c, q0, 0.0), jnp.where(lo_c, 0.0, q0),
                              jnp.where(lo_c, q1, 0.0), jnp.where(lo_c, 0.0, q1)], axis=0).astype(BF16)
        return lax.dot_general(kd, qs, (((1,), (1,)), ((), ())), preferred_element_type=F32), vd

    def swa_finish(s, c, g, sc, vd):
        rows = slice(s * ts + c * chunk, s * ts + (c + 1) * chunk)
        sc = sc + bias[g]
        if not has_cache and c * chunk < WINDOW:
            sc = jnp.where(kj >= padk + jnp.maximum(0, WINDOW - (i * nch + c) * chunk), sc, NEG_INF)
        m = jnp.maximum(jnp.max(sc, axis=0, keepdims=True), sinkc[g])
        p = jnp.exp2(sc - m)
        den = jnp.sum(p, axis=0, keepdims=True) + jnp.exp2(sinkc[g] - m)
        wgt = (p / den).astype(BF16)
        o = lax.dot_general(wgt, vd, (((0,), (0,)), ((), ())), preferred_element_type=F32)
        for jj in range(2):
            t = 2 * g + jj
            mix[rows, C_POOL + LANES * t:C_POOL + LANES * (t + 1)] = jnp.where(
                lo_c, o[(2 * jj) * chunk:(2 * jj + 1) * chunk], o[(2 * jj + 1) * chunk:(2 * jj + 2) * chunk])

    pending = swa_scores(*units[0])
    for n, unit in enumerate(units):
        nxt = swa_scores(*units[n + 1]) if n + 1 < len(units) else None
        swa_finish(*unit, *pending)
        pending = nxt

    mixed = (mix[...] * _silu(gate)).astype(BF16)
    y_ref[...] = x + jnp.dot(mixed, wout_ref[...], preferred_element_type=F32)

    @pl.when(i == nblk - 1)
    def _():
        for s in range(ns):
            ps_ref[s] = uext[s, ts:ts + POOL_HIST, :]
            ks_ref[s] = kext[s, padk + ts:padk + ts + WINDOW, :]
            vs_ref[s] = vext[s, padk + ts:padk + ts + WINDOW, :]


def _const_spec(shape):
    nd = len(shape)
    return pl.BlockSpec(shape, lambda i, _nd=nd: (0,) * _nd, pipeline_mode=pl.Buffered(1))


def _layer0(x2d, prefix, wts, *, ns, ts, chunk, base_pos):
    n = x2d.shape[0]
    tb = ns * ts
    nblk = n // tb
    has_cache = prefix is not None
    padk = SWA_KEYS - WINDOW - chunk
    rows4 = SWA_REP * chunk
    slope = np.repeat(2.0 ** (-(np.arange(SWA_HEADS) + 1.0)), chunk).reshape(SWA_KV_HEADS, 1, rows4)
    kj = np.arange(SWA_KEYS).reshape(1, SWA_KEYS, 1)
    rel = (np.arange(rows4) % chunk).reshape(1, 1, rows4) + WINDOW - (kj - padk)
    bias = jnp.asarray(np.where(kj >= padk, -(slope * LOG2E) * np.abs(rel), NEG_INF).astype(np.float32))
    sink = jnp.broadcast_to(jnp.repeat(wts["sinks"].astype(F32), chunk).reshape(SWA_KV_HEADS, 1, rows4),
                            (SWA_KV_HEADS, 8, rows4))
    consts = [wts["g0"], wts["w_in0"], wts["w_pool"], wts["pool_scale"], wts["qg0"], wts["kg0"], wts["seg"],
              bias, sink, wts["w_out0"]]
    in_specs = [pl.BlockSpec((tb, D_MODEL), lambda i: (i, 0))]
    args = [x2d]
    if has_cache:
        for a in prefix:
            in_specs.append(_const_spec(a.shape))
            args.append(a)
    for a in consts:
        in_specs.append(_const_spec(a.shape))
        args.append(a)
    out_shape = [jax.ShapeDtypeStruct((n, D_MODEL), F32),
                 jax.ShapeDtypeStruct((ns, POOL_HIST, C_POOL), F32),
                 jax.ShapeDtypeStruct((ns, WINDOW, LANES), F32),
                 jax.ShapeDtypeStruct((ns, WINDOW, LANES), F32)]
    out_specs = [pl.BlockSpec((tb, D_MODEL), lambda i: (i, 0)),
                 pl.BlockSpec((ns, POOL_HIST, C_POOL), lambda i: (0, 0, 0)),
                 pl.BlockSpec((ns, WINDOW, LANES), lambda i: (0, 0, 0)),
                 pl.BlockSpec((ns, WINDOW, LANES), lambda i: (0, 0, 0))]
    scratch = [pltpu.VMEM((ns, POOL_HIST + ts, C_POOL), F32),
               pltpu.VMEM((ns, padk + WINDOW + ts, LANES), F32),
               pltpu.VMEM((ns, padk + WINDOW + ts, LANES), F32),
               pltpu.VMEM((tb, D_MODEL), F32)]
    kern = functools.partial(_l0_kernel, ns=ns, ts=ts, chunk=chunk, has_cache=has_cache,
                             base_pos=base_pos, nblk=nblk)
    return pl.pallas_call(
        kern, grid=(nblk,), in_specs=in_specs, out_specs=out_specs, out_shape=out_shape,
        scratch_shapes=scratch, compiler_params=_cparams(("arbitrary",)),
        name="layer0_prompt" if not has_cache else "layer0_sample")(*args)


def _log_sigmoid(z):
    return jnp.minimum(z, 0.0) - jnp.log1p(jnp.exp(-jnp.abs(z)))


def _fox_proj_sample_kernel(x_ref, g1_ref, wqt_ref, wk_ref, wvt_ref, wf_ref, bf_ref, qg_ref, kg_ref, seg_ref,
                            k_ref, v_ref, lf_ref, q_ref):
    x = x_ref[...]
    h = _rms(x, g1_ref[...]).astype(BF16)
    z = jnp.dot(h, wf_ref[...], preferred_element_type=F32) + bf_ref[...]
    lf_ref[...] = _log_sigmoid(z)[:, 0:FOX_HEADS]
    seg = seg_ref[...]
    nt = (((1,), (1,)), ((), ()))
    q = lax.dot_general(h, wqt_ref[...], nt, preferred_element_type=F32)
    k = jnp.dot(h, wk_ref[...], preferred_element_type=F32)
    v_ref[...] = lax.dot_general(h, wvt_ref[...], nt, preferred_element_type=F32)
    for j in range(FOX_HEADS // 2):
        cols = slice(LANES * j, LANES * (j + 1))
        q_ref[:, cols] = _head_rms_tile(q[:, cols], seg, qg_ref[:, cols]) * ATTN_SCALE
        k_ref[:, cols] = _head_rms_tile(k[:, cols], seg, kg_ref[:, cols])


def _fox_proj_sample(x2d, wts, *, tb):
    n = x2d.shape[0]
    consts = [wts["g1"], wts["w_q1t"], wts["w_k1"], wts["w_v1t"], wts["w_f"], wts["b_f"], wts["qg1"], wts["kg1"],
              wts["seg"]]
    row_spec = pl.BlockSpec((tb, D_MODEL), lambda i: (i, 0))
    in_specs = [row_spec] + [_const_spec(a.shape) for a in consts]
    row = jax.ShapeDtypeStruct((n, D_MODEL), F32)
    out_shape = [row, row, jax.ShapeDtypeStruct((n, FOX_HEADS), F32), row]
    out_specs = [row_spec, row_spec, pl.BlockSpec((tb, FOX_HEADS), lambda i: (i, 0)), row_spec]
    return pl.pallas_call(
        _fox_proj_sample_kernel, grid=(n // tb,), in_specs=in_specs, out_specs=out_specs, out_shape=out_shape,
        compiler_params=_cparams(("arbitrary",)), name="fox_proj_sample")(x2d, *consts)


def _fox_proj_prompt_kernel(x_ref, g1_ref, wk_ref, wft_ref, bfc_ref, wqt_ref, wvt_ref, gq_ref, kg_ref,
                            seg_ref, tri_ref, e_ref,
                            k_ref, vt_ref, lft_ref, qa_ref, ka_ref, va_ref, ce_ref, carry, *, tb):
    i = pl.program_id(0)
    x = x_ref[...]
    hf = _rms(x, g1_ref[...])
    h = hf.astype(BF16)
    ht = hf.T.astype(BF16)
    bcol = jnp.concatenate([bfc_ref[...]] * (tb // LANES), axis=1)
    logft = _log_sigmoid(jnp.dot(wft_ref[...], ht, preferred_element_type=F32) + bcol)
    lft_ref[...] = logft
    k = jnp.dot(h, wk_ref[...], preferred_element_type=F32)

    @pl.when(i == 0)
    def _():
        carry[...] = jnp.zeros(carry.shape, F32)

    c3 = jnp.dot(jnp.concatenate(_split3(logft), axis=0), tri_ref[...], preferred_element_type=F32)

    qt = jnp.dot(wqt_ref[...], ht, preferred_element_type=F32)
    vt = jnp.dot(wvt_ref[...], ht, preferred_element_type=F32)
    vt_ref[...] = vt
    srow = lax.broadcasted_iota(jnp.int32, (HEAD_DIM, tb), 0)
    q_aug = jnp.where(srow < 3, 1.0, 0.0)
    v_aug = jnp.where(srow == 0, 1.0, 0.0)
    gq = jnp.concatenate([gq_ref[...]] * (tb // LANES), axis=1) * (ATTN_SCALE * LOG2E)
    for hd in range(FOX_HEADS):
        rows = slice(HEAD_DIM * hd, HEAD_DIM * (hd + 1))
        qh = qt[rows]
        ss = jnp.sum(qh * qh, axis=0, keepdims=True)
        qn = qh * lax.rsqrt(ss * (1.0 / HEAD_DIM) + NORM_EPS) * gq
        qa_ref[hd] = jnp.concatenate([qn, q_aug], axis=0).astype(BF16)
        va_ref[hd] = jnp.concatenate([vt[rows], v_aug], axis=0).astype(BF16)

    cumt = c3[0:FOX_HEADS] + c3[FOX_HEADS:2 * FOX_HEADS] + c3[2 * FOX_HEADS:3 * FOX_HEADS] + carry[:, 0:1]
    end = jnp.broadcast_to(cumt[:, tb - 1:tb], (FOX_HEADS, LANES))
    carry[...] = end
    ce_ref[0] = end
    parts = [p.astype(F32) for p in _split3(-LOG2E * cumt)]
    n3 = jnp.concatenate(parts + [jnp.zeros((LANES - 3 * FOX_HEADS, tb), F32)], axis=0).T.astype(BF16)
    biasall = jnp.dot(n3, e_ref[...], preferred_element_type=F32)
    lo_l = lax.broadcasted_iota(jnp.int32, (tb, LANES), 1) < HEAD_DIM
    seg = seg_ref[...]
    for j in range(FOX_HEADS // 2):
        cols = slice(LANES * j, LANES * (j + 1))
        kt = _head_rms_tile(k[:, cols], seg, kg_ref[:, cols])
        k_ref[:, cols] = kt
        bt = biasall[:, cols]
        ka_ref[2 * j] = jnp.where(lo_l, kt, bt).astype(BF16)
        ka_ref[2 * j + 1] = pltpu.roll(jnp.where(lo_l, bt, kt), HEAD_DIM, 1).astype(BF16)


def _fox_proj_prompt(x2d, wts, *, tb):
    n = x2d.shape[0]
    tri = jnp.asarray(np.triu(np.ones((tb, tb), np.float32)), BF16)
    e = np.zeros((LANES, FOX_HEADS // 2 * LANES), np.float32)
    for part in range(3):
        for hd in range(FOX_HEADS):
            e[part * FOX_HEADS + hd, (hd // 2) * LANES + (HEAD_DIM if hd % 2 == 0 else 0) + part] = 1.0
    consts = [wts["g1"], wts["w_k1"], wts["w_ft"], wts["b_fc"], wts["w_q1t"], wts["w_v1t"],
              wts["gq1c"], wts["kg1"], wts["seg"], tri, jnp.asarray(e, BF16)]
    row_spec = pl.BlockSpec((tb, D_MODEL), lambda i: (i, 0))
    in_specs = [row_spec] + [_const_spec(a.shape) for a in consts]
    row = jax.ShapeDtypeStruct((n, D_MODEL), F32)
    out_shape = [row, jax.ShapeDtypeStruct((D_MODEL, n), F32), jax.ShapeDtypeStruct((FOX_HEADS, n), F32),
                 jax.ShapeDtypeStruct((FOX_HEADS, LANES, n), BF16), jax.ShapeDtypeStruct((FOX_HEADS, n, LANES), BF16),
                 jax.ShapeDtypeStruct((FOX_HEADS, LANES, n), BF16),
                 jax.ShapeDtypeStruct((n // tb, FOX_HEADS, LANES), F32)]
    tspec = pl.BlockSpec((FOX_HEADS, LANES, tb), lambda i: (0, 0, i))
    out_specs = [row_spec, pl.BlockSpec((D_MODEL, tb), lambda i: (0, i)),
                 pl.BlockSpec((FOX_HEADS, tb), lambda i: (0, i)),
                 tspec, pl.BlockSpec((FOX_HEADS, tb, LANES), lambda i: (0, i, 0)), tspec,
                 pl.BlockSpec((1, FOX_HEADS, LANES), lambda i: (i, 0, 0))]
    return pl.pallas_call(
        functools.partial(_fox_proj_prompt_kernel, tb=tb), grid=(n // tb,), in_specs=in_specs,
        out_specs=out_specs, out_shape=out_shape, scratch_shapes=[pltpu.VMEM((FOX_HEADS, LANES), F32)],
        compiler_params=_cparams(("arbitrary",)), name="fox_proj_prompt")(x2d, *consts)


def _fox_attn_kernel(ce_ref, qk_ref, q_ref, k_ref, v_ref, o_ref, m_scr, al_scr, acc_scr, s_scr,
                     *, tq, qpb, **kw):
    def blk(qb):
        return _fox_attn_block(pl.program_id(0), pl.program_id(1) * qpb + qb, pl.multiple_of(qb * tq, tq),
                               ce_ref, qk_ref, q_ref, k_ref, v_ref, o_ref, m_scr, al_scr, acc_scr, s_scr,
                               tq=tq, **kw)

    blk(0)[0](None)

    def body(qb, carry):
        blk(qb)[0](blk(qb - 1)[1])
        return carry

    lax.fori_loop(1, qpb, body, 0)
    blk(qpb - 1)[1]()


def _fox_attn_block(pair, qi, qoff, ce_ref, qk_ref, q_ref, k_ref, v_ref, o_ref, m_scr, al_scr, acc_scr, s_scr,
                    *, tq, tk, unroll, nkb, nh):
    n_u = qi

    last = jnp.maximum(n_u - 1, 0)
    slack = 2.0 * qk_ref[0]

    def dead(j):
        jj = jnp.minimum(j, last)
        ok = j < n_u - 1
        for hh in range(nh):
            hd = nh * pair + hh
            gap = LOG2E * (ce_ref[last * FOX_HEADS + hd] - ce_ref[jj * FOX_HEADS + hd])
            ok = ok & (slack + gap < EXP2_ZERO)
        return ok

    lo = jnp.int32(0)
    hi = last
    for _ in range(max(1, (nkb - 1).bit_length())):
        mid = (lo + hi) >> 1
        d = dead(mid)
        lo = jnp.where(d, mid + 1, lo)
        hi = jnp.where(d, hi, mid)
    first = lo

    def score_matmuls(kb):
        off = pl.multiple_of(kb * tk, tk)
        return [jnp.dot(k_ref[hh, pl.ds(off, tk), :], q_ref[hh, :, pl.ds(qoff, tq)],
                        preferred_element_type=F32)
                for hh in range(nh)]

    def score_finish(sts):
        for hh in range(nh):
            st = sts[hh]
            m_old = m_scr[hh, 0:1, :]
            m_new = jnp.maximum(m_old, jnp.max(st, axis=0, keepdims=True))
            al_scr[hh, 0:1, :] = jnp.exp2(m_old - m_new)
            m_scr[hh, 0:1, :] = m_new
            s_scr[hh] = st

    def accumulate(kb):
        off = pl.multiple_of(kb * tk, tk)
        for hh in range(nh):
            p = jnp.exp2(s_scr[hh] - m_scr[hh, 0:1, :]).astype(BF16)
            pv = jnp.dot(v_ref[hh, :, pl.ds(off, tk)], p, preferred_element_type=F32)
            acc_scr[hh] = al_scr[hh, 0:1, :] * acc_scr[hh] + pv

    def advance(kb):
        sts = score_matmuls(kb + 1)
        accumulate(kb)
        score_finish(sts)

    def tail():
        half = tq // 2
        off = pl.multiple_of(n_u * tk, tk)
        sts = [(jnp.dot(k_ref[hh, pl.ds(off, half), :], q_ref[hh, :, pl.ds(qoff, half)],
                        preferred_element_type=F32),
                jnp.dot(k_ref[hh, pl.ds(off, tk), :], q_ref[hh, :, pl.ds(pl.multiple_of(qoff + half, half), half)],
                        preferred_element_type=F32))
               for hh in range(nh)]
        accumulate(last)
        tri = jnp.where(lax.broadcasted_iota(jnp.int32, (half, half), 0)
                        <= lax.broadcasted_iota(jnp.int32, (half, half), 1), 0.0, NEG_INF)
        for hh in range(nh):
            st_a, st_b = sts[hh]
            halves = ((st_a + tri, 0), (jnp.concatenate([st_b[0:half], st_b[half:tk] + tri], axis=0), half))
            for st, c0 in halves:
                nk = st.shape[0]
                m_old = m_scr[hh, 0:1, c0:c0 + half]
                m_new = jnp.maximum(m_old, jnp.max(st, axis=0, keepdims=True))
                p = jnp.exp2(st - m_new).astype(BF16)
                pv = jnp.dot(v_ref[hh, :, pl.ds(off, nk)], p, preferred_element_type=F32)
                acc_scr[hh, :, c0:c0 + half] = jnp.exp2(m_old - m_new) * acc_scr[hh, :, c0:c0 + half] + pv
        for t in range(nh // 2):
            pair_rows = []
            for hh in (2 * t, 2 * t + 1):
                a = acc_scr[hh]
                pair_rows.append(a[0:HEAD_DIM] / a[HEAD_DIM:HEAD_DIM + 1, :])
            o_ref[pl.ds(qoff, tq), LANES * t:LANES * (t + 1)] = jnp.concatenate(
                pair_rows, axis=0).T.astype(o_ref.dtype)

    def head(prev_tail):
        def start():
            if prev_tail is not None:
                prev_tail()
            m_scr[...] = jnp.full(m_scr.shape, NEG_INF, F32)
            acc_scr[...] = jnp.zeros(acc_scr.shape, F32)

        def below_diagonal():
            n_adv = n_u - 1 - first
            rem = lax.rem(n_adv, unroll)
            for k in range(unroll):
                @pl.when(rem == k)
                def _():
                    start()
                    score_finish(score_matmuls(first))
                    for u in range(k):
                        advance(first + u)

            def body(t, carry):
                for u in range(unroll):
                    advance(first + rem + unroll * t + u)
                return carry

            lax.fori_loop(0, n_adv // unroll, body, 0)

        if prev_tail is None:
            @pl.when(qi == 0)
            def _():
                start()
                m_scr[...] = jnp.zeros(m_scr.shape, F32)
                al_scr[...] = jnp.ones(al_scr.shape, F32)
                s_scr[...] = jnp.full(s_scr.shape, NEG_INF, F32)

            pl.when(qi > 0)(below_diagonal)
        else:
            below_diagonal()

    return head, tail


def _fox_attn_prompt(cum_end, qk_bound, qa, ka, va, *, tq, tk, unroll, nh, qpb):
    n = ka.shape[1]
    assert tq == tk and n % (qpb * tq) == 0
    resident = dict(pipeline_mode=pl.Buffered(1)) if nh > 2 else {}
    grid_spec = pltpu.PrefetchScalarGridSpec(
        num_scalar_prefetch=2, grid=(FOX_HEADS // nh, n // (qpb * tq)),
        in_specs=[pl.BlockSpec((nh, LANES, qpb * tq), lambda p, i, ce, qk: (p, 0, i)),
                  pl.BlockSpec((nh, n, LANES), lambda p, i, ce, qk: (p, 0, 0), **resident),
                  pl.BlockSpec((nh, LANES, n), lambda p, i, ce, qk: (p, 0, 0), **resident)],
        out_specs=pl.BlockSpec((qpb * tq, HEAD_DIM * nh), lambda p, i, ce, qk: (i, p)),
        scratch_shapes=[pltpu.VMEM((nh, 8, tq), F32), pltpu.VMEM((nh, 8, tq), F32),
                        pltpu.VMEM((nh, LANES, tq), F32), pltpu.VMEM((nh, tk, tq), F32)])
    return pl.pallas_call(
        functools.partial(_fox_attn_kernel, tq=tq, tk=tk, unroll=unroll, nkb=n // tk, nh=nh, qpb=qpb),
        grid_spec=grid_spec,
        out_shape=jax.ShapeDtypeStruct((n, D_MODEL), BF16),
        compiler_params=_cparams(("arbitrary", "arbitrary")), name="fox_attn_prompt")(
            cum_end, qk_bound, qa, ka, va)


CUM_BLK = 512
NEW_PAD = 128
SAMPLE_GROUPS_PER_STEP = 2


def _fox_sample_kernel(q_ref, kc_ref, vc_ref, lfc_ref, kn_ref, vn_ref, lfn_ref, tri_ref, o_ref, negc, negn,
                       *, t_new, p_len, gps):
    rows = 4 * t_new

    @pl.when(pl.program_id(1) == 0)
    def _():
        tri = tri_ref[...]

        def prefix_sums(x, t):
            c3 = jnp.dot(jnp.concatenate(_split3(x), axis=0), t, preferred_element_type=F32)
            return c3[0:FOX_HEADS] + c3[FOX_HEADS:2 * FOX_HEADS] + c3[2 * FOX_HEADS:3 * FOX_HEADS]

        carry = jnp.zeros((FOX_HEADS, 1), F32)
        for b in range(p_len // CUM_BLK):
            c = prefix_sums(lfc_ref[0, :, b * CUM_BLK:(b + 1) * CUM_BLK], tri) + carry
            negc[:, b * CUM_BLK:(b + 1) * CUM_BLK] = -c
            carry = c[:, CUM_BLK - 1:CUM_BLK]
        negn[...] = -(prefix_sums(lfn_ref[0], tri[0:NEW_PAD, 0:NEW_PAD]) + carry)

    rhead = lax.broadcasted_iota(jnp.int32, (rows, 2 * LANES), 0) // t_new
    lhead = lax.broadcasted_iota(jnp.int32, (rows, 2 * LANES), 1) // HEAD_DIM
    qrow = lax.broadcasted_iota(jnp.int32, (rows, NEW_PAD), 0) % t_new
    kcol = lax.broadcasted_iota(jnp.int32, (rows, NEW_PAD), 1)
    lh = lax.broadcasted_iota(jnp.int32, (t_new, 2 * LANES), 1) // HEAD_DIM
    zpad = jnp.zeros((NEW_PAD - t_new, 2 * LANES), F32)
    for g in range(gps):
        qd = gps * pl.program_id(1) + g
        cols = slice(2 * LANES * g, 2 * LANES * (g + 1))
        qq = q_ref[0, :, cols]
        qbd = jnp.where(rhead == lhead, jnp.concatenate([qq] * 4, axis=0), 0.0).astype(BF16)
        kct = kc_ref[0, 4 * g:4 * (g + 1)].reshape(2 * LANES, p_len).astype(BF16)
        vct = vc_ref[0, 4 * g:4 * (g + 1)].reshape(2 * LANES, p_len).astype(BF16)
        s_c = jnp.dot(qbd, kct, preferred_element_type=F32)
        kn = jnp.concatenate([kn_ref[0, :, cols], zpad], axis=0).astype(BF16)
        vn = jnp.concatenate([vn_ref[0, :, cols], zpad], axis=0).astype(BF16)
        s_n = lax.dot_general(qbd, kn, (((1,), (1,)), ((), ())), preferred_element_type=F32)
        bc = jnp.concatenate([jnp.broadcast_to(negc[pl.ds(4 * qd + jh, 1), :], (t_new, p_len))
                              for jh in range(4)], axis=0)
        bn = jnp.concatenate([jnp.broadcast_to(negn[pl.ds(4 * qd + jh, 1), :], (t_new, NEW_PAD))
                              for jh in range(4)], axis=0)
        s_c = s_c + bc
        s_n = jnp.where(kcol <= qrow, s_n + bn, NEG_INF)
        m = jnp.maximum(jnp.max(s_c, axis=-1, keepdims=True), jnp.max(s_n, axis=-1, keepdims=True))
        p_c = jnp.exp(s_c - m)
        p_n = jnp.exp(s_n - m)
        den = jnp.sum(p_c, axis=-1, keepdims=True) + jnp.sum(p_n, axis=-1, keepdims=True)
        o = (lax.dot_general(p_c.astype(BF16), vct, (((1,), (1,)), ((), ())), preferred_element_type=F32)
             + jnp.dot(p_n.astype(BF16), vn, preferred_element_type=F32)) / den
        out = jnp.zeros((t_new, 2 * LANES), F32)
        for jh in range(4):
            out = jnp.where(lh == jh, o[jh * t_new:(jh + 1) * t_new], out)
        o_ref[0, :, cols] = out


def _fox_attn_sample(q, cache_k, cache_v, lfc_t, k_new, v_new, lfn_t):
    b, t_new, _ = q.shape
    p_len = cache_k.shape[3]
    tri = jnp.asarray(np.triu(np.ones((CUM_BLK, CUM_BLK), np.float32)), BF16)
    gps = SAMPLE_GROUPS_PER_STEP
    width = 2 * LANES * gps
    in_specs = [pl.BlockSpec((1, t_new, width), lambda r, d: (r, 0, d)),
                pl.BlockSpec((1, 4 * gps, HEAD_DIM, p_len), lambda r, d: (r, d, 0, 0)),
                pl.BlockSpec((1, 4 * gps, HEAD_DIM, p_len), lambda r, d: (r, d, 0, 0)),
                pl.BlockSpec((1, FOX_HEADS, p_len), lambda r, d: (r, 0, 0)),
                pl.BlockSpec((1, t_new, width), lambda r, d: (r, 0, d)),
                pl.BlockSpec((1, t_new, width), lambda r, d: (r, 0, d)),
                pl.BlockSpec((1, FOX_HEADS, NEW_PAD), lambda r, d: (r, 0, 0)),
                pl.BlockSpec((CUM_BLK, CUM_BLK), lambda r, d: (0, 0))]
    return pl.pallas_call(
        functools.partial(_fox_sample_kernel, t_new=t_new, p_len=p_len, gps=gps),
        grid=(b, D_MODEL // width), in_specs=in_specs,
        out_specs=pl.BlockSpec((1, t_new, width), lambda r, d: (r, 0, d)),
        out_shape=jax.ShapeDtypeStruct((b, t_new, D_MODEL), F32),
        scratch_shapes=[pltpu.VMEM((FOX_HEADS, p_len), F32), pltpu.VMEM((FOX_HEADS, NEW_PAD), F32)],
        compiler_params=_cparams(("arbitrary", "arbitrary")), name="fox_attn_sample")(
            q, cache_k, cache_v, lfc_t, k_new, v_new, lfn_t, tri)


def _out_proj_kernel(r_ref, a_ref, g1_ref, wg_ref, w_ref, y_ref):
    x = r_ref[...]
    h = _rms(x, g1_ref[...]).astype(BF16)
    gate = jnp.dot(h, wg_ref[...], preferred_element_type=F32)
    mixed = (a_ref[...].astype(F32) * _silu(gate)).astype(BF16)
    y_ref[...] = x + jnp.dot(mixed, w_ref[...], preferred_element_type=F32)


def _out_proj(resid, attn, wts, *, tb):
    n = resid.shape[0]
    row = pl.BlockSpec((tb, D_MODEL), lambda i: (i, 0))
    consts = [wts["g1"], wts["w_g1"], wts["w_out1"]]
    return pl.pallas_call(
        _out_proj_kernel, grid=(n // tb,), in_specs=[row, row] + [_const_spec(a.shape) for a in consts],
        out_specs=row, out_shape=jax.ShapeDtypeStruct((n, D_MODEL), F32),
        compiler_params=_cparams(("arbitrary",)), name="out_proj")(resid, attn, *consts)


L0_TB = 512
FOX_TB = 256
ATTN_TQ = 512
ATTN_TK = 512
ATTN_UNROLL = 4
ATTN_HEADS = 2
ATTN_QPB = 8
SWA_CHUNK = 64


def kernel(x_prompt, x_sample, state_pool, cache_swa_k, cache_swa_v, cache_fox_k, cache_fox_v, cache_fox_logf,
           norm0_g, w_in0, w_pool, pool_scale, swa_qn_g, swa_kn_g, swa_sinks, w_out0,
           norm1_g, w_in1, b_forget, fox_qn_g, fox_kn_g, w_out1):
    nb, seq, _ = x_prompt.shape
    db, dseq, _ = x_sample.shape
    past_len = cache_fox_k.shape[1]
    assert nb == 1 and seq % L0_TB == 0 and seq % ATTN_TQ == 0 and dseq % 8 == 0

    seg = np.kron(np.eye(2, dtype=np.float32), np.ones((HEAD_DIM, HEAD_DIM), np.float32))
    mc = FOX_HEADS * HEAD_DIM
    wts = {
        "g0": norm0_g.reshape(1, D_MODEL), "w_in0": w_in0.astype(BF16), "w_pool": w_pool.astype(BF16),
        "pool_scale": pool_scale.reshape(1, C_POOL),
        "qg0": jnp.tile(swa_qn_g, SWA_HEADS).reshape(1, SWA_HEADS * HEAD_DIM),
        "kg0": jnp.tile(swa_kn_g, SWA_KV_HEADS).reshape(1, LANES),
        "seg": jnp.asarray(seg, BF16), "sinks": swa_sinks, "w_out0": w_out0.astype(BF16),
        "g1": norm1_g.reshape(1, D_MODEL),
        "w_out1": w_out1.astype(BF16),
        "w_f": jnp.pad(w_in1[:, 4 * mc:], ((0, 0), (0, LANES - FOX_HEADS))).astype(BF16),
        "b_f": jnp.pad(b_forget.astype(F32), (0, LANES - FOX_HEADS)).reshape(1, LANES),
        "w_ft": w_in1[:, 4 * mc:].T.astype(BF16),
        "b_fc": jnp.broadcast_to(b_forget.astype(F32).reshape(FOX_HEADS, 1), (FOX_HEADS, LANES)),
        "qg1": jnp.tile(fox_qn_g, FOX_HEADS).reshape(1, mc), "kg1": jnp.tile(fox_kn_g, FOX_HEADS).reshape(1, mc),
        "w_k1": w_in1[:, mc:2 * mc].astype(BF16), "w_g1": w_in1[:, 3 * mc:4 * mc].astype(BF16),
        "w_q1t": w_in1[:, 0:mc].T.astype(BF16), "w_v1t": w_in1[:, 2 * mc:3 * mc].T.astype(BF16),
        "gq1c": jnp.broadcast_to(fox_qn_g.astype(F32).reshape(HEAD_DIM, 1), (HEAD_DIM, LANES)),
    }

    xp = x_prompt.reshape(seq, D_MODEL)
    y0p, pool_p, swk_p, swv_p = _layer0(xp, None, wts, ns=1, ts=L0_TB, chunk=SWA_CHUNK, base_pos=0)
    fk_p, fvt_p, flt_p, qa, ka, va, cum_end = _fox_proj_prompt(y0p, wts, tb=FOX_TB)
    fl_p = jnp.transpose(flt_p)
    fv_p = jnp.transpose(fvt_p.reshape(FOX_HEADS, HEAD_DIM, seq), (2, 0, 1))
    per = ATTN_TK // FOX_TB
    cum_end = cum_end[per - 1::per, :, 0].reshape(-1)
    qk_bound = (1.02 * LOG2E * HEAD_DIM * ATTN_SCALE * jnp.max(jnp.abs(fox_qn_g)) * jnp.max(jnp.abs(fox_kn_g))
                ).astype(F32).reshape(1)
    attn_p = _fox_attn_prompt(cum_end, qk_bound, qa, ka, va, tq=ATTN_TQ, tk=ATTN_TK, unroll=ATTN_UNROLL,
                              nh=ATTN_HEADS, qpb=ATTN_QPB)
    yp = _out_proj(y0p, attn_p, wts, tb=L0_TB)

    xs = x_sample.reshape(db * dseq, D_MODEL)
    prefix = (jnp.pad(state_pool, ((0, 0), (POOL_HIST - POOL_PAD, 0), (0, 0))),
              cache_swa_k.reshape(db, WINDOW, LANES), cache_swa_v.reshape(db, WINDOW, LANES))
    y0s, pool_s, swk_s, swv_s = _layer0(xs, prefix, wts, ns=db, ts=dseq, chunk=dseq, base_pos=past_len)
    fk_s, fv_s, fl_s, q_s = _fox_proj_sample(y0s, wts, tb=db * dseq)
    attn_s = _fox_attn_sample(
        q_s.reshape(db, dseq, D_MODEL),
        jnp.transpose(cache_fox_k, (0, 2, 3, 1)), jnp.transpose(cache_fox_v, (0, 2, 3, 1)),
        jnp.transpose(cache_fox_logf, (0, 2, 1)),
        fk_s.reshape(db, dseq, D_MODEL), fv_s.reshape(db, dseq, D_MODEL),
        jnp.pad(jnp.transpose(fl_s.reshape(db, dseq, FOX_HEADS), (0, 2, 1)), ((0, 0), (0, 0), (0, NEW_PAD - dseq))))
    ys = _out_proj(y0s, attn_s.reshape(db * dseq, D_MODEL), wts, tb=db * dseq)

    return (yp.reshape(1, seq, D_MODEL), ys.reshape(db, dseq, D_MODEL),
            pool_p[:, POOL_HIST - POOL_PAD:], pool_s[:, POOL_HIST - POOL_PAD:],
            swk_p.reshape(1, WINDOW, SWA_KV_HEADS, HEAD_DIM), swv_p.reshape(1, WINDOW, SWA_KV_HEADS, HEAD_DIM),
            swk_s.reshape(db, WINDOW, SWA_KV_HEADS, HEAD_DIM), swv_s.reshape(db, WINDOW, SWA_KV_HEADS, HEAD_DIM),
            fk_p.reshape(1, seq, FOX_HEADS, HEAD_DIM), fv_p.reshape(1, seq, FOX_HEADS, HEAD_DIM),
            fl_p.reshape(1, seq, FOX_HEADS),
            fk_s.reshape(db, dseq, FOX_HEADS, HEAD_DIM), fv_s.reshape(db, dseq, FOX_HEADS, HEAD_DIM),
            fl_s.reshape(db, dseq, FOX_HEADS))
```

```python
import functools

import numpy as np
import jax
import jax.numpy as jnp
from jax import lax
from jax.experimental import pallas as pl
from jax.experimental.pallas import tpu as pltpu

F32 = jnp.float32
BF16 = jnp.bfloat16

D_MODEL = 1024
HEAD_DIM = 64
ATTN_SCALE = HEAD_DIM ** -0.5
POOL_WINDOWS = (2, 4, 8, 16)
C_POOL = 512
POOL_PAD = 15
POOL_HIST = POOL_PAD + 1
SWA_HEADS = 8
SWA_KV_HEADS = 2
SWA_REP = 4
WINDOW = 128
FOX_HEADS = 16
NORM_EPS = 1e-6
Q_OFF = C_POOL
K_OFF = Q_OFF + SWA_HEADS * HEAD_DIM
V_OFF = K_OFF + SWA_KV_HEADS * HEAD_DIM
GATE_OFF = V_OFF + SWA_KV_HEADS * HEAD_DIM
AB_IN = GATE_OFF + D_MODEL
NEG_INF = -1e30
LOG2E = 1.4426950408889634
EXP2_ZERO = -152.0

LANES = 128
SWA_KEYS = 256
VMEM_LIMIT = 56 * 1024 * 1024


def _cparams(sem):
    return pltpu.CompilerParams(dimension_semantics=sem, vmem_limit_bytes=VMEM_LIMIT)


def _rms(x, g):
    ms = jnp.mean(x * x, axis=-1, keepdims=True)
    return x * lax.rsqrt(ms + NORM_EPS) * g


def _split3(x):
    hi = x.astype(BF16)
    r = x - hi.astype(F32)
    mid = r.astype(BF16)
    lo = (r - mid.astype(F32)).astype(BF16)
    return hi, mid, lo


def _head_rms_tile(x, seg, g):
    ss = jnp.dot((x * x).astype(BF16), seg, preferred_element_type=F32)
    return x * lax.rsqrt(ss * (1.0 / HEAD_DIM) + NORM_EPS) * g


def _silu(g):
    return g / (1.0 + jnp.exp(-g))


def _l0_kernel(*refs, ns, ts, chunk, has_cache, base_pos, nblk):
    if has_cache:
        (x_ref, pp_ref, kp_ref, vp_ref, g0_ref, win_ref, wpool_ref, pscale_ref, qg_ref, kg_ref, seg_ref,
         bias_ref, sink_ref, wout_ref, y_ref, ps_ref, ks_ref, vs_ref, uext, kext, vext, mix) = refs
    else:
        (x_ref, g0_ref, win_ref, wpool_ref, pscale_ref, qg_ref, kg_ref, seg_ref,
         bias_ref, sink_ref, wout_ref, y_ref, ps_ref, ks_ref, vs_ref, uext, kext, vext, mix) = refs
    padk = SWA_KEYS - WINDOW - chunk
    hist = padk + WINDOW
    i = pl.program_id(0)

    x = x_ref[...]
    h = _rms(x, g0_ref[...]).astype(BF16)
    proj = jnp.dot(h, win_ref[...], preferred_element_type=F32)
    u = proj[:, 0:Q_OFF]
    gate = proj[:, GATE_OFF:AB_IN]
    seg = seg_ref[...]
    qn = [_head_rms_tile(proj[:, Q_OFF + LANES * j:Q_OFF + LANES * (j + 1)], seg,
                         qg_ref[:, LANES * j:LANES * (j + 1)]) * (ATTN_SCALE * LOG2E)
          for j in range(SWA_HEADS // 2)]
    kn = _head_rms_tile(proj[:, K_OFF:V_OFF], seg, kg_ref[...])
    v = proj[:, V_OFF:GATE_OFF]

    if has_cache:
        for s in range(ns):
            uext[s, 0:POOL_HIST, :] = pp_ref[s]
            kext[s, 0:padk, :] = jnp.zeros((padk, LANES), F32)
            vext[s, 0:padk, :] = jnp.zeros((padk, LANES), F32)
            kext[s, padk:hist, :] = kp_ref[s]
            vext[s, padk:hist, :] = vp_ref[s]
    else:
        @pl.when(i == 0)
        def _():
            uext[0, 0:POOL_HIST, :] = jnp.zeros((POOL_HIST, C_POOL), F32)
            kext[0, 0:hist, :] = jnp.zeros((hist, LANES), F32)
            vext[0, 0:hist, :] = jnp.zeros((hist, LANES), F32)

        @pl.when(i > 0)
        def _():
            uext[0, 0:POOL_HIST, :] = uext[0, ts:ts + POOL_HIST, :]
            kext[0, 0:hist, :] = kext[0, ts:ts + hist, :]
            vext[0, 0:hist, :] = vext[0, ts:ts + hist, :]

    for s in range(ns):
        uext[s, POOL_HIST:POOL_HIST + ts, :] = u[s * ts:(s + 1) * ts]
        kext[s, hist:hist + ts, :] = kn[s * ts:(s + 1) * ts]
        vext[s, hist:hist + ts, :] = v[s * ts:(s + 1) * ts]

    pos = base_pos + i * ts + lax.broadcasted_iota(jnp.int32, (ts, LANES), 0)
    for s in range(ns):
        for g, w in enumerate(POOL_WINDOWS):
            cols = slice(LANES * g, LANES * (g + 1))
            acc = uext[s, POOL_HIST:POOL_HIST + ts, cols]
            cur = acc
            for j in range(1, w):
                acc = acc + uext[s, POOL_HIST - j:POOL_HIST - j + ts, cols]
            cnt = jnp.minimum(pos + 1, w).astype(F32)
            mix[s * ts:(s + 1) * ts, cols] = acc / cnt - cur
    for g in range(4):
        cols = slice(LANES * g, LANES * (g + 1))
        d = mix[:, cols].astype(BF16)
        mix[:, cols] = jnp.dot(d, wpool_ref[g], preferred_element_type=F32) * pscale_ref[:, cols]

    rows4 = SWA_REP * chunk
    lo_c = lax.broadcasted_iota(jnp.int32, (chunk, LANES), 1) < HEAD_DIM
    lo_k = lax.broadcasted_iota(jnp.int32, (SWA_KEYS, LANES), 1) < HEAD_DIM
    kj = lax.broadcasted_iota(jnp.int32, (SWA_KEYS, rows4), 0)
    bias = [bias_ref[g] for g in range(SWA_KV_HEADS)]
    sinkc = [sink_ref[g][0:1, :] * LOG2E for g in range(SWA_KV_HEADS)]
    nch = ts // chunk
    units = [(s, c, g) for s in range(ns) for c in range(nch) for g in range(SWA_KV_HEADS)]

    def swa_scores(s, c, g):
        r0 = c * chunk
        rows = slice(s * ts + r0, s * ts + r0 + chunk)
        kwin = kext[s, r0:r0 + SWA_KEYS, :]
        vwin = vext[s, r0:r0 + SWA_KEYS, :]
        krl = pltpu.roll(kwin, HEAD_DIM, 1)
        vrl = pltpu.roll(vwin, HEAD_DIM, 1)
        if g == 0:
            kd = jnp.where(lo_k, kwin, krl).astype(BF16)
            vd = jnp.where(lo_k, vwin, vrl).astype(BF16)
        else:
            kd = jnp.where(lo_k, krl, kwin).astype(BF16)
            vd = jnp.where(lo_k, vrl, vwin).astype(BF16)
        q0 = qn[2 * g][rows]
        q1 = qn[2 * g + 1][rows]
        qs = jnp.concatenate([jnp.where(lo_c, q0, 0.0), jnp.where(lo_c, 0.0, q0),
                              jnp.where(lo_c, q1, 0.0), jnp.where(lo_c, 0.0, q1)], axis=0).astype(BF16)
        return lax.dot_general(kd, qs, (((1,), (1,)), ((), ())), preferred_element_type=F32), vd

    def swa_finish(s, c, g, sc, vd):
        rows = slice(s * ts + c * chunk, s * ts + (c + 1) * chunk)
        sc = sc + bias[g]
        if not has_cache and c * chunk < WINDOW:
            sc = jnp.where(kj >= padk + jnp.maximum(0, WINDOW - (i * nch + c) * chunk), sc, NEG_INF)
        m = jnp.maximum(jnp.max(sc, axis=0, keepdims=True), sinkc[g])
        p = jnp.exp2(sc - m)
        den = jnp.sum(p, axis=0, keepdims=True) + jnp.exp2(sinkc[g] - m)
        wgt = (p / den).astype(BF16)
        o = lax.dot_general(wgt, vd, (((0,), (0,)), ((), ())), preferred_element_type=F32)
        for jj in range(2):
            t = 2 * g + jj
            mix[rows, C_POOL + LANES * t:C_POOL + LANES * (t + 1)] = jnp.where(
                lo_c, o[(2 * jj) * chunk:(2 * jj + 1) * chunk], o[(2 * jj + 1) * chunk:(2 * jj + 2) * chunk])

    pending = swa_scores(*units[0])
    for n, unit in enumerate(units):
        nxt = swa_scores(*units[n + 1]) if n + 1 < len(units) else None
        swa_finish(*unit, *pending)
        pending = nxt

    mixed = (mix[...] * _silu(gate)).astype(BF16)
    y_ref[...] = x + jnp.dot(mixed, wout_ref[...], preferred_element_type=F32)

    @pl.when(i == nblk - 1)
    def _():
        for s in range(ns):
            ps_ref[s] = uext[s, ts:ts + POOL_HIST, :]
            ks_ref[s] = kext[s, padk + ts:padk + ts + WINDOW, :]
            vs_ref[s] = vext[s, padk + ts:padk + ts + WINDOW, :]


def _const_spec(shape):
    nd = len(shape)
    return pl.BlockSpec(shape, lambda i, _nd=nd: (0,) * _nd, pipeline_mode=pl.Buffered(1))


def _layer0(x2d, prefix, wts, *, ns, ts, chunk, base_pos):
    n = x2d.shape[0]
    tb = ns * ts
    nblk = n // tb
    has_cache = prefix is not None
    padk = SWA_KEYS - WINDOW - chunk
    rows4 = SWA_REP * chunk
    slope = np.repeat(2.0 ** (-(np.arange(SWA_HEADS) + 1.0)), chunk).reshape(SWA_KV_HEADS, 1, rows4)
    kj = np.arange(SWA_KEYS).reshape(1, SWA_KEYS, 1)
    rel = (np.arange(rows4) % chunk).reshape(1, 1, rows4) + WINDOW - (kj - padk)
    bias = jnp.asarray(np.where(kj >= padk, -(slope * LOG2E) * np.abs(rel), NEG_INF).astype(np.float32))
    sink = jnp.broadcast_to(jnp.repeat(wts["sinks"].astype(F32), chunk).reshape(SWA_KV_HEADS, 1, rows4),
                            (SWA_KV_HEADS, 8, rows4))
    consts = [wts["g0"], wts["w_in0"], wts["w_pool"], wts["pool_scale"], wts["qg0"], wts["kg0"], wts["seg"],
              bias, sink, wts["w_out0"]]
    in_specs = [pl.BlockSpec((tb, D_MODEL), lambda i: (i, 0))]
    args = [x2d]
    if has_cache:
        for a in prefix:
            in_specs.append(_const_spec(a.shape))
            args.append(a)
    for a in consts:
        in_specs.append(_const_spec(a.shape))
        args.append(a)
    out_shape = [jax.ShapeDtypeStruct((n, D_MODEL), F32),
                 jax.ShapeDtypeStruct((ns, POOL_HIST, C_POOL), F32),
                 jax.ShapeDtypeStruct((ns, WINDOW, LANES), F32),
                 jax.ShapeDtypeStruct((ns, WINDOW, LANES), F32)]
    out_specs = [pl.BlockSpec((tb, D_MODEL), lambda i: (i, 0)),
                 pl.BlockSpec((ns, POOL_HIST, C_POOL), lambda i: (0, 0, 0)),
                 pl.BlockSpec((ns, WINDOW, LANES), lambda i: (0, 0, 0)),
                 pl.BlockSpec((ns, WINDOW, LANES), lambda i: (0, 0, 0))]
    scratch = [pltpu.VMEM((ns, POOL_HIST + ts, C_POOL), F32),
               pltpu.VMEM((ns, padk + WINDOW + ts, LANES), F32),
               pltpu.VMEM((ns, padk + WINDOW + ts, LANES), F32),
               pltpu.VMEM((tb, D_MODEL), F32)]
    kern = functools.partial(_l0_kernel, ns=ns, ts=ts, chunk=chunk, has_cache=has_cache,
                             base_pos=base_pos, nblk=nblk)
    return pl.pallas_call(
        kern, grid=(nblk,), in_specs=in_specs, out_specs=out_specs, out_shape=out_shape,
        scratch_shapes=scratch, compiler_params=_cparams(("arbitrary",)),
        name="layer0_prompt" if not has_cache else "layer0_sample")(*args)


def _log_sigmoid(z):
    return jnp.minimum(z, 0.0) - jnp.log1p(jnp.exp(-jnp.abs(z)))


def _fox_proj_sample_kernel(x_ref, g1_ref, wqt_ref, wk_ref, wvt_ref, wf_ref, bf_ref, qg_ref, kg_ref, seg_ref,
                            k_ref, v_ref, lf_ref, q_ref):
    x = x_ref[...]
    h = _rms(x, g1_ref[...]).astype(BF16)
    z = jnp.dot(h, wf_ref[...], preferred_element_type=F32) + bf_ref[...]
    lf_ref[...] = _log_sigmoid(z)[:, 0:FOX_HEADS]
    seg = seg_ref[...]
    nt = (((1,), (1,)), ((), ()))
    q = lax.dot_general(h, wqt_ref[...], nt, preferred_element_type=F32)
    k = jnp.dot(h, wk_ref[...], preferred_element_type=F32)
    v_ref[...] = lax.dot_general(h, wvt_ref[...], nt, preferred_element_type=F32)
    for j in range(FOX_HEADS // 2):
        cols = slice(LANES * j, LANES * (j + 1))
        q_ref[:, cols] = _head_rms_tile(q[:, cols], seg, qg_ref[:, cols]) * ATTN_SCALE
        k_ref[:, cols] = _head_rms_tile(k[:, cols], seg, kg_ref[:, cols])


def _fox_proj_sample(x2d, wts, *, tb):
    n = x2d.shape[0]
    consts = [wts["g1"], wts["w_q1t"], wts["w_k1"], wts["w_v1t"], wts["w_f"], wts["b_f"], wts["qg1"], wts["kg1"],
              wts["seg"]]
    row_spec = pl.BlockSpec((tb, D_MODEL), lambda i: (i, 0))
    in_specs = [row_spec] + [_const_spec(a.shape) for a in consts]
    row = jax.ShapeDtypeStruct((n, D_MODEL), F32)
    out_shape = [row, row, jax.ShapeDtypeStruct((n, FOX_HEADS), F32), row]
    out_specs = [row_spec, row_spec, pl.BlockSpec((tb, FOX_HEADS), lambda i: (i, 0)), row_spec]
    return pl.pallas_call(
        _fox_proj_sample_kernel, grid=(n // tb,), in_specs=in_specs, out_specs=out_specs, out_shape=out_shape,
        compiler_params=_cparams(("arbitrary",)), name="fox_proj_sample")(x2d, *consts)


def _fox_proj_prompt_kernel(x_ref, g1_ref, wk_ref, wft_ref, bfc_ref, wqt_ref, wvt_ref, gq_ref, kg_ref,
                            seg_ref, tri_ref, e_ref,
                            k_ref, vt_ref, lft_ref, qa_ref, ka_ref, va_ref, ce_ref, carry, *, tb):
    i = pl.program_id(0)
    x = x_ref[...]
    hf = _rms(x, g1_ref[...])
    h = hf.astype(BF16)
    ht = hf.T.astype(BF16)
    bcol = jnp.concatenate([bfc_ref[...]] * (tb // LANES), axis=1)
    logft = _log_sigmoid(jnp.dot(wft_ref[...], ht, preferred_element_type=F32) + bcol)
    lft_ref[...] = logft
    k = jnp.dot(h, wk_ref[...], preferred_element_type=F32)

    @pl.when(i == 0)
    def _():
        carry[...] = jnp.zeros(carry.shape, F32)

    c3 = jnp.dot(jnp.concatenate(_split3(logft), axis=0), tri_ref[...], preferred_element_type=F32)

    qt = jnp.dot(wqt_ref[...], ht, preferred_element_type=F32)
    vt = jnp.dot(wvt_ref[...], ht, preferred_element_type=F32)
    vt_ref[...] = vt
    srow = lax.broadcasted_iota(jnp.int32, (HEAD_DIM, tb), 0)
    q_aug = jnp.where(srow < 3, 1.0, 0.0)
    v_aug = jnp.where(srow == 0, 1.0, 0.0)
    gq = jnp.concatenate([gq_ref[...]] * (tb // LANES), axis=1) * (ATTN_SCALE * LOG2E)
    for hd in range(FOX_HEADS):
        rows = slice(HEAD_DIM * hd, HEAD_DIM * (hd + 1))
        qh = qt[rows]
        ss = jnp.sum(qh * qh, axis=0, keepdims=True)
        qn = qh * lax.rsqrt(ss * (1.0 / HEAD_DIM) + NORM_EPS) * gq
        qa_ref[hd] = jnp.concatenate([qn, q_aug], axis=0).astype(BF16)
        va_ref[hd] = jnp.concatenate([vt[rows], v_aug], axis=0).astype(BF16)

    cumt = c3[0:FOX_HEADS] + c3[FOX_HEADS:2 * FOX_HEADS] + c3[2 * FOX_HEADS:3 * FOX_HEADS] + carry[:, 0:1]
    end = jnp.broadcast_to(cumt[:, tb - 1:tb], (FOX_HEADS, LANES))
    carry[...] = end
    ce_ref[0] = end
    parts = [p.astype(F32) for p in _split3(-LOG2E * cumt)]
    n3 = jnp.concatenate(parts + [jnp.zeros((LANES - 3 * FOX_HEADS, tb), F32)], axis=0).T.astype(BF16)
    biasall = jnp.dot(n3, e_ref[...], preferred_element_type=F32)
    lo_l = lax.broadcasted_iota(jnp.int32, (tb, LANES), 1) < HEAD_DIM
    seg = seg_ref[...]
    for j in range(FOX_HEADS // 2):
        cols = slice(LANES * j, LANES * (j + 1))
        kt = _head_rms_tile(k[:, cols], seg, kg_ref[:, cols])
        k_ref[:, cols] = kt
        bt = biasall[:, cols]
        ka_ref[2 * j] = jnp.where(lo_l, kt, bt).astype(BF16)
        ka_ref[2 * j + 1] = pltpu.roll(jnp.where(lo_l, bt, kt), HEAD_DIM, 1).astype(BF16)


def _fox_proj_prompt(x2d, wts, *, tb):
    n = x2d.shape[0]
    tri = jnp.asarray(np.triu(np.ones((tb, tb), np.float32)), BF16)
    e = np.zeros((LANES, FOX_HEADS // 2 * LANES), np.float32)
    for part in range(3):
        for hd in range(FOX_HEADS):
            e[part * FOX_HEADS + hd, (hd // 2) * LANES + (HEAD_DIM if hd % 2 == 0 else 0) + part] = 1.0
    consts = [wts["g1"], wts["w_k1"], wts["w_ft"], wts["b_fc"], wts["w_q1t"], wts["w_v1t"],
              wts["gq1c"], wts["kg1"], wts["seg"], tri, jnp.asarray(e, BF16)]
    row_spec = pl.BlockSpec((tb, D_MODEL), lambda i: (i, 0))
    in_specs = [row_spec] + [_const_spec(a.shape) for a in consts]
    row = jax.ShapeDtypeStruct((n, D_MODEL), F32)
    out_shape = [row, jax.ShapeDtypeStruct((D_MODEL, n), F32), jax.ShapeDtypeStruct((FOX_HEADS, n), F32),
                 jax.ShapeDtypeStruct((FOX_HEADS, LANES, n), BF16), jax.ShapeDtypeStruct((FOX_HEADS, n, LANES), BF16),
                 jax.ShapeDtypeStruct((FOX_HEADS, LANES, n), BF16),
                 jax.ShapeDtypeStruct((n // tb, FOX_HEADS, LANES), F32)]
    tspec = pl.BlockSpec((FOX_HEADS, LANES, tb), lambda i: (0, 0, i))
    out_specs = [row_spec, pl.BlockSpec((D_MODEL, tb), lambda i: (0, i)),
                 pl.BlockSpec((FOX_HEADS, tb), lambda i: (0, i)),
                 tspec, pl.BlockSpec((FOX_HEADS, tb, LANES), lambda i: (0, i, 0)), tspec,
                 pl.BlockSpec((1, FOX_HEADS, LANES), lambda i: (i, 0, 0))]
    return pl.pallas_call(
        functools.partial(_fox_proj_prompt_kernel, tb=tb), grid=(n // tb,), in_specs=in_specs,
        out_specs=out_specs, out_shape=out_shape, scratch_shapes=[pltpu.VMEM((FOX_HEADS, LANES), F32)],
        compiler_params=_cparams(("arbitrary",)), name="fox_proj_prompt")(x2d, *consts)


def _fox_attn_kernel(ce_ref, qk_ref, q_ref, k_ref, v_ref, o_ref, m_scr, al_scr, acc_scr, s_scr,
                     *, tq, qpb, **kw):
    def blk(qb):
        return _fox_attn_block(pl.program_id(0), pl.program_id(1) * qpb + qb, pl.multiple_of(qb * tq, tq),
                               ce_ref, qk_ref, q_ref, k_ref, v_ref, o_ref, m_scr, al_scr, acc_scr, s_scr,
                               tq=tq, **kw)

    blk(0)[0](None)

    def body(qb, carry):
        blk(qb)[0](blk(qb - 1)[1])
        return carry

    lax.fori_loop(1, qpb, body, 0)
    blk(qpb - 1)[1]()


def _fox_attn_block(pair, qi, qoff, ce_ref, qk_ref, q_ref, k_ref, v_ref, o_ref, m_scr, al_scr, acc_scr, s_scr,
                    *, tq, tk, unroll, nkb, nh):
    n_u = qi

    last = jnp.maximum(n_u - 1, 0)
    slack = 2.0 * qk_ref[0]

    def dead(j):
        jj = jnp.minimum(j, last)
        ok = j < n_u - 1
        for hh in range(nh):
            hd = nh * pair + hh
            gap = LOG2E * (ce_ref[last * FOX_HEADS + hd] - ce_ref[jj * FOX_HEADS + hd])
            ok = ok & (slack + gap < EXP2_ZERO)
        return ok

    lo = jnp.int32(0)
    hi = last
    for _ in range(max(1, (nkb - 1).bit_length())):
        mid = (lo + hi) >> 1
        d = dead(mid)
        lo = jnp.where(d, mid + 1, lo)
        hi = jnp.where(d, hi, mid)
    first = lo

    def score_matmuls(kb):
        off = pl.multiple_of(kb * tk, tk)
        return [jnp.dot(k_ref[hh, pl.ds(off, tk), :], q_ref[hh, :, pl.ds(qoff, tq)],
                        preferred_element_type=F32)
                for hh in range(nh)]

    def score_finish(sts):
        for hh in range(nh):
            st = sts[hh]
            m_old = m_scr[hh, 0:1, :]
            m_new = jnp.maximum(m_old, jnp.max(st, axis=0, keepdims=True))
            al_scr[hh, 0:1, :] = jnp.exp2(m_old - m_new)
            m_scr[hh, 0:1, :] = m_new
            s_scr[hh] = st

    def accumulate(kb):
        off = pl.multiple_of(kb * tk, tk)
        for hh in range(nh):
            p = jnp.exp2(s_scr[hh] - m_scr[hh, 0:1, :]).astype(BF16)
            pv = jnp.dot(v_ref[hh, :, pl.ds(off, tk)], p, preferred_element_type=F32)
            acc_scr[hh] = al_scr[hh, 0:1, :] * acc_scr[hh] + pv

    def advance(kb):
        sts = score_matmuls(kb + 1)
        accumulate(kb)
        score_finish(sts)

    def tail():
        half = tq // 2
        off = pl.multiple_of(n_u * tk, tk)
        sts = [(jnp.dot(k_ref[hh, pl.ds(off, half), :], q_ref[hh, :, pl.ds(qoff, half)],
                        preferred_element_type=F32),
                jnp.dot(k_ref[hh, pl.ds(off, tk), :], q_ref[hh, :, pl.ds(pl.multiple_of(qoff + half, half), half)],
                        preferred_element_type=F32))
               for hh in range(nh)]
        accumulate(last)
        tri = jnp.where(lax.broadcasted_iota(jnp.int32, (half, half), 0)
                        <= lax.broadcasted_iota(jnp.int32, (half, half), 1), 0.0, NEG_INF)
        for hh in range(nh):
            st_a, st_b = sts[hh]
            halves = ((st_a + tri, 0), (jnp.concatenate([st_b[0:half], st_b[half:tk] + tri], axis=0), half))
            for st, c0 in halves:
                nk = st.shape[0]
                m_old = m_scr[hh, 0:1, c0:c0 + half]
                m_new = jnp.maximum(m_old, jnp.max(st, axis=0, keepdims=True))
                p = jnp.exp2(st - m_new).astype(BF16)
                pv = jnp.dot(v_ref[hh, :, pl.ds(off, nk)], p, preferred_element_type=F32)
                acc_scr[hh, :, c0:c0 + half] = jnp.exp2(m_old - m_new) * acc_scr[hh, :, c0:c0 + half] + pv
        for t in range(nh // 2):
            pair_rows = []
            for hh in (2 * t, 2 * t + 1):
                a = acc_scr[hh]
                pair_rows.append(a[0:HEAD_DIM] / a[HEAD_DIM:HEAD_DIM + 1, :])
            o_ref[pl.ds(qoff, tq), LANES * t:LANES * (t + 1)] = jnp.concatenate(
                pair_rows, axis=0).T.astype(o_ref.dtype)

    def head(prev_tail):
        def start():
            if prev_tail is not None:
                prev_tail()
            m_scr[...] = jnp.full(m_scr.shape, NEG_INF, F32)
            acc_scr[...] = jnp.zeros(acc_scr.shape, F32)

        def below_diagonal():
            n_adv = n_u - 1 - first
            rem = lax.rem(n_adv, unroll)
            for k in range(unroll):
                @pl.when(rem == k)
                def _():
                    start()
                    score_finish(score_matmuls(first))
                    for u in range(k):
                        advance(first + u)

            def body(t, carry):
                for u in range(unroll):
                    advance(first + rem + unroll * t + u)
                return carry

            lax.fori_loop(0, n_adv // unroll, body, 0)

        if prev_tail is None:
            @pl.when(qi == 0)
            def _():
                start()
                m_scr[...] = jnp.zeros(m_scr.shape, F32)
                al_scr[...] = jnp.ones(al_scr.shape, F32)
                s_scr[...] = jnp.full(s_scr.shape, NEG_INF, F32)

            pl.when(qi > 0)(below_diagonal)
        else:
            below_diagonal()

    return head, tail


def _fox_attn_prompt(cum_end, qk_bound, qa, ka, va, *, tq, tk, unroll, nh, qpb):
    n = ka.shape[1]
    assert tq == tk and n % (qpb * tq) == 0
    resident = dict(pipeline_mode=pl.Buffered(1)) if nh > 2 else {}
    grid_spec = pltpu.PrefetchScalarGridSpec(
        num_scalar_prefetch=2, grid=(FOX_HEADS // nh, n // (qpb * tq)),
        in_specs=[pl.BlockSpec((nh, LANES, qpb * tq), lambda p, i, ce, qk: (p, 0, i)),
                  pl.BlockSpec((nh, n, LANES), lambda p, i, ce, qk: (p, 0, 0), **resident),
                  pl.BlockSpec((nh, LANES, n), lambda p, i, ce, qk: (p, 0, 0), **resident)],
        out_specs=pl.BlockSpec((qpb * tq, HEAD_DIM * nh), lambda p, i, ce, qk: (i, p)),
        scratch_shapes=[pltpu.VMEM((nh, 8, tq), F32), pltpu.VMEM((nh, 8, tq), F32),
                        pltpu.VMEM((nh, LANES, tq), F32), pltpu.VMEM((nh, tk, tq), F32)])
    return pl.pallas_call(
        functools.partial(_fox_attn_kernel, tq=tq, tk=tk, unroll=unroll, nkb=n // tk, nh=nh, qpb=qpb),
        grid_spec=grid_spec,
        out_shape=jax.ShapeDtypeStruct((n, D_MODEL), BF16),
        compiler_params=_cparams(("arbitrary", "arbitrary")), name="fox_attn_prompt")(
            cum_end, qk_bound, qa, ka, va)


CUM_BLK = 512
NEW_PAD = 128
SAMPLE_GROUPS_PER_STEP = 2


def _fox_sample_kernel(q_ref, kc_ref, vc_ref, lfc_ref, kn_ref, vn_ref, lfn_ref, tri_ref, o_ref, negc, negn,
                       *, t_new, p_len, gps):
    rows = 4 * t_new

    @pl.when(pl.program_id(1) == 0)
    def _():
        tri = tri_ref[...]

        def prefix_sums(x, t):
            c3 = jnp.dot(jnp.concatenate(_split3(x), axis=0), t, preferred_element_type=F32)
            return c3[0:FOX_HEADS] + c3[FOX_HEADS:2 * FOX_HEADS] + c3[2 * FOX_HEADS:3 * FOX_HEADS]

        carry = jnp.zeros((FOX_HEADS, 1), F32)
        for b in range(p_len // CUM_BLK):
            c = prefix_sums(lfc_ref[0, :, b * CUM_BLK:(b + 1) * CUM_BLK], tri) + carry
            negc[:, b * CUM_BLK:(b + 1) * CUM_BLK] = -c
            carry = c[:, CUM_BLK - 1:CUM_BLK]
        negn[...] = -(prefix_sums(lfn_ref[0], tri[0:NEW_PAD, 0:NEW_PAD]) + carry)

    rhead = lax.broadcasted_iota(jnp.int32, (rows, 2 * LANES), 0) // t_new
    lhead = lax.broadcasted_iota(jnp.int32, (rows, 2 * LANES), 1) // HEAD_DIM
    qrow = lax.broadcasted_iota(jnp.int32, (rows, NEW_PAD), 0) % t_new
    kcol = lax.broadcasted_iota(jnp.int32, (rows, NEW_PAD), 1)
    lh = lax.broadcasted_iota(jnp.int32, (t_new, 2 * LANES), 1) // HEAD_DIM
    zpad = jnp.zeros((NEW_PAD - t_new, 2 * LANES), F32)
    for g in range(gps):
        qd = gps * pl.program_id(1) + g
        cols = slice(2 * LANES * g, 2 * LANES * (g + 1))
        qq = q_ref[0, :, cols]
        qbd = jnp.where(rhead == lhead, jnp.concatenate([qq] * 4, axis=0), 0.0).astype(BF16)
        kct = kc_ref[0, 4 * g:4 * (g + 1)].reshape(2 * LANES, p_len).astype(BF16)
        vct = vc_ref[0, 4 * g:4 * (g + 1)].reshape(2 * LANES, p_len).astype(BF16)
        s_c = jnp.dot(qbd, kct, preferred_element_type=F32)
        kn = jnp.concatenate([kn_ref[0, :, cols], zpad], axis=0).astype(BF16)
        vn = jnp.concatenate([vn_ref[0, :, cols], zpad], axis=0).astype(BF16)
        s_n = lax.dot_general(qbd, kn, (((1,), (1,)), ((), ())), preferred_element_type=F32)
        bc = jnp.concatenate([jnp.broadcast_to(negc[pl.ds(4 * qd + jh, 1), :], (t_new, p_len))
                              for jh in range(4)], axis=0)
        bn = jnp.concatenate([jnp.broadcast_to(negn[pl.ds(4 * qd + jh, 1), :], (t_new, NEW_PAD))
                              for jh in range(4)], axis=0)
        s_c = s_c + bc
        s_n = jnp.where(kcol <= qrow, s_n + bn, NEG_INF)
        m = jnp.maximum(jnp.max(s_c, axis=-1, keepdims=True), jnp.max(s_n, axis=-1, keepdims=True))
        p_c = jnp.exp(s_c - m)
        p_n = jnp.exp(s_n - m)
        den = jnp.sum(p_c, axis=-1, keepdims=True) + jnp.sum(p_n, axis=-1, keepdims=True)
        o = (lax.dot_general(p_c.astype(BF16), vct, (((1,), (1,)), ((), ())), preferred_element_type=F32)
             + jnp.dot(p_n.astype(BF16), vn, preferred_element_type=F32)) / den
        out = jnp.zeros((t_new, 2 * LANES), F32)
        for jh in range(4):
            out = jnp.where(lh == jh, o[jh * t_new:(jh + 1) * t_new], out)
        o_ref[0, :, cols] = out


def _fox_attn_sample(q, cache_k, cache_v, lfc_t, k_new, v_new, lfn_t):
    b, t_new, _ = q.shape
    p_len = cache_k.shape[3]
    tri = jnp.asarray(np.triu(np.ones((CUM_BLK, CUM_BLK), np.float32)), BF16)
    gps = SAMPLE_GROUPS_PER_STEP
    width = 2 * LANES * gps
    in_specs = [pl.BlockSpec((1, t_new, width), lambda r, d: (r, 0, d)),
                pl.BlockSpec((1, 4 * gps, HEAD_DIM, p_len), lambda r, d: (r, d, 0, 0)),
                pl.BlockSpec((1, 4 * gps, HEAD_DIM, p_len), lambda r, d: (r, d, 0, 0)),
                pl.BlockSpec((1, FOX_HEADS, p_len), lambda r, d: (r, 0, 0)),
                pl.BlockSpec((1, t_new, width), lambda r, d: (r, 0, d)),
                pl.BlockSpec((1, t_new, width), lambda r, d: (r, 0, d)),
                pl.BlockSpec((1, FOX_HEADS, NEW_PAD), lambda r, d: (r, 0, 0)),
                pl.BlockSpec((CUM_BLK, CUM_BLK), lambda r, d: (0, 0))]
    return pl.pallas_call(
        functools.partial(_fox_sample_kernel, t_new=t_new, p_len=p_len, gps=gps),
        grid=(b, D_MODEL // width), in_specs=in_specs,
        out_specs=pl.BlockSpec((1, t_new, width), lambda r, d: (r, 0, d)),
        out_shape=jax.ShapeDtypeStruct((b, t_new, D_MODEL), F32),
        scratch_shapes=[pltpu.VMEM((FOX_HEADS, p_len), F32), pltpu.VMEM((FOX_HEADS, NEW_PAD), F32)],
        compiler_params=_cparams(("arbitrary", "arbitrary")), name="fox_attn_sample")(
            q, cache_k, cache_v, lfc_t, k_new, v_new, lfn_t, tri)


def _out_proj_kernel(r_ref, a_ref, g1_ref, wg_ref, w_ref, y_ref):
    x = r_ref[...]
    h = _rms(x, g1_ref[...]).astype(BF16)
    gate = jnp.dot(h, wg_ref[...], preferred_element_type=F32)
    mixed = (a_ref[...].astype(F32) * _silu(gate)).astype(BF16)
    y_ref[...] = x + jnp.dot(mixed, w_ref[...], preferred_element_type=F32)


def _out_proj(resid, attn, wts, *, tb):
    n = resid.shape[0]
    row = pl.BlockSpec((tb, D_MODEL), lambda i: (i, 0))
    consts = [wts["g1"], wts["w_g1"], wts["w_out1"]]
    return pl.pallas_call(
        _out_proj_kernel, grid=(n // tb,), in_specs=[row, row] + [_const_spec(a.shape) for a in consts],
        out_specs=row, out_shape=jax.ShapeDtypeStruct((n, D_MODEL), F32),
        compiler_params=_cparams(("arbitrary",)), name="out_proj")(resid, attn, *consts)


L0_TB = 512
OUT_TB = 1024
FOX_TB = 256
ATTN_TQ = 512
ATTN_TK = 512
ATTN_UNROLL = 4
ATTN_HEADS = 2
ATTN_QPB = 8
SWA_CHUNK = 64


def kernel(x_prompt, x_sample, state_pool, cache_swa_k, cache_swa_v, cache_fox_k, cache_fox_v, cache_fox_logf,
           norm0_g, w_in0, w_pool, pool_scale, swa_qn_g, swa_kn_g, swa_sinks, w_out0,
           norm1_g, w_in1, b_forget, fox_qn_g, fox_kn_g, w_out1):
    nb, seq, _ = x_prompt.shape
    db, dseq, _ = x_sample.shape
    past_len = cache_fox_k.shape[1]
    assert nb == 1 and seq % L0_TB == 0 and seq % OUT_TB == 0 and seq % ATTN_TQ == 0 and dseq % 8 == 0

    seg = np.kron(np.eye(2, dtype=np.float32), np.ones((HEAD_DIM, HEAD_DIM), np.float32))
    mc = FOX_HEADS * HEAD_DIM
    wts = {
        "g0": norm0_g.reshape(1, D_MODEL), "w_in0": w_in0.astype(BF16), "w_pool": w_pool.astype(BF16),
        "pool_scale": pool_scale.reshape(1, C_POOL),
        "qg0": jnp.tile(swa_qn_g, SWA_HEADS).reshape(1, SWA_HEADS * HEAD_DIM),
        "kg0": jnp.tile(swa_kn_g, SWA_KV_HEADS).reshape(1, LANES),
        "seg": jnp.asarray(seg, BF16), "sinks": swa_sinks, "w_out0": w_out0.astype(BF16),
        "g1": norm1_g.reshape(1, D_MODEL),
        "w_out1": w_out1.astype(BF16),
        "w_f": jnp.pad(w_in1[:, 4 * mc:], ((0, 0), (0, LANES - FOX_HEADS))).astype(BF16),
        "b_f": jnp.pad(b_forget.astype(F32), (0, LANES - FOX_HEADS)).reshape(1, LANES),
        "w_ft": w_in1[:, 4 * mc:].T.astype(BF16),
        "b_fc": jnp.broadcast_to(b_forget.astype(F32).reshape(FOX_HEADS, 1), (FOX_HEADS, LANES)),
        "qg1": jnp.tile(fox_qn_g, FOX_HEADS).reshape(1, mc), "kg1": jnp.tile(fox_kn_g, FOX_HEADS).reshape(1, mc),
        "w_k1": w_in1[:, mc:2 * mc].astype(BF16), "w_g1": w_in1[:, 3 * mc:4 * mc].astype(BF16),
        "w_q1t": w_in1[:, 0:mc].T.astype(BF16), "w_v1t": w_in1[:, 2 * mc:3 * mc].T.astype(BF16),
        "gq1c": jnp.broadcast_to(fox_qn_g.astype(F32).reshape(HEAD_DIM, 1), (HEAD_DIM, LANES)),
    }

    xp = x_prompt.reshape(seq, D_MODEL)
    y0p, pool_p, swk_p, swv_p = _layer0(xp, None, wts, ns=1, ts=L0_TB, chunk=SWA_CHUNK, base_pos=0)
    fk_p, fvt_p, flt_p, qa, ka, va, cum_end = _fox_proj_prompt(y0p, wts, tb=FOX_TB)
    fl_p = jnp.transpose(flt_p)
    fv_p = jnp.transpose(fvt_p.reshape(FOX_HEADS, HEAD_DIM, seq), (2, 0, 1))
    per = ATTN_TK // FOX_TB
    cum_end = cum_end[per - 1::per, :, 0].reshape(-1)
    qk_bound = (1.02 * LOG2E * HEAD_DIM * ATTN_SCALE * jnp.max(jnp.abs(fox_qn_g)) * jnp.max(jnp.abs(fox_kn_g))
                ).astype(F32).reshape(1)
    attn_p = _fox_attn_prompt(cum_end, qk_bound, qa, ka, va, tq=ATTN_TQ, tk=ATTN_TK, unroll=ATTN_UNROLL,
                              nh=ATTN_HEADS, qpb=ATTN_QPB)
    yp = _out_proj(y0p, attn_p, wts, tb=OUT_TB)

    xs = x_sample.reshape(db * dseq, D_MODEL)
    prefix = (jnp.pad(state_pool, ((0, 0), (POOL_HIST - POOL_PAD, 0), (0, 0))),
              cache_swa_k.reshape(db, WINDOW, LANES), cache_swa_v.reshape(db, WINDOW, LANES))
    y0s, pool_s, swk_s, swv_s = _layer0(xs, prefix, wts, ns=db, ts=dseq, chunk=dseq, base_pos=past_len)
    fk_s, fv_s, fl_s, q_s = _fox_proj_sample(y0s, wts, tb=db * dseq)
    attn_s = _fox_attn_sample(
        q_s.reshape(db, dseq, D_MODEL),
        jnp.transpose(cache_fox_k, (0, 2, 3, 1)), jnp.transpose(cache_fox_v, (0, 2, 3, 1)),
        jnp.transpose(cache_fox_logf, (0, 2, 1)),
        fk_s.reshape(db, dseq, D_MODEL), fv_s.reshape(db, dseq, D_MODEL),
        jnp.pad(jnp.transpose(fl_s.reshape(db, dseq, FOX_HEADS), (0, 2, 1)), ((0, 0), (0, 0), (0, NEW_PAD - dseq))))
    ys = _out_proj(y0s, attn_s.reshape(db * dseq, D_MODEL), wts, tb=db * dseq)

    return (yp.reshape(1, seq, D_MODEL), ys.reshape(db, dseq, D_MODEL),
            pool_p[:, POOL_HIST - POOL_PAD:], pool_s[:, POOL_HIST - POOL_PAD:],
            swk_p.reshape(1, WINDOW, SWA_KV_HEADS, HEAD_DIM), swv_p.reshape(1, WINDOW, SWA_KV_HEADS, HEAD_DIM),
            swk_s.reshape(db, WINDOW, SWA_KV_HEADS, HEAD_DIM), swv_s.reshape(db, WINDOW, SWA_KV_HEADS, HEAD_DIM),
            fk_p.reshape(1, seq, FOX_HEADS, HEAD_DIM), fv_p.reshape(1, seq, FOX_HEADS, HEAD_DIM),
            fl_p.reshape(1, seq, FOX_HEADS),
            fk_s.reshape(db, dseq, FOX_HEADS, HEAD_DIM), fv_s.reshape(db, dseq, FOX_HEADS, HEAD_DIM),
            fl_s.reshape(db, dseq, FOX_HEADS))
```

```python
import functools

import numpy as np
import jax
import jax.numpy as jnp
from jax import lax
from jax.experimental import pallas as pl
from jax.experimental.pallas import tpu as pltpu

F32 = jnp.float32
BF16 = jnp.bfloat16

D_MODEL = 1024
HEAD_DIM = 64
ATTN_SCALE = HEAD_DIM ** -0.5
POOL_WINDOWS = (2, 4, 8, 16)
C_POOL = 512
POOL_PAD = 15
POOL_HIST = POOL_PAD + 1
SWA_HEADS = 8
SWA_KV_HEADS = 2
SWA_REP = 4
WINDOW = 128
FOX_HEADS = 16
NORM_EPS = 1e-6
Q_OFF = C_POOL
K_OFF = Q_OFF + SWA_HEADS * HEAD_DIM
V_OFF = K_OFF + SWA_KV_HEADS * HEAD_DIM
GATE_OFF = V_OFF + SWA_KV_HEADS * HEAD_DIM
AB_IN = GATE_OFF + D_MODEL
NEG_INF = -1e30
LOG2E = 1.4426950408889634
EXP2_ZERO = -152.0

LANES = 128
SWA_KEYS = 256
VMEM_LIMIT = 56 * 1024 * 1024


def _cparams(sem):
    return pltpu.CompilerParams(dimension_semantics=sem, vmem_limit_bytes=VMEM_LIMIT)


def _rms(x, g):
    ms = jnp.mean(x * x, axis=-1, keepdims=True)
    return x * lax.rsqrt(ms + NORM_EPS) * g


def _split3(x):
    hi = x.astype(BF16)
    r = x - hi.astype(F32)
    mid = r.astype(BF16)
    lo = (r - mid.astype(F32)).astype(BF16)
    return hi, mid, lo


def _head_rms_tile(x, seg, g):
    ss = jnp.dot((x * x).astype(BF16), seg, preferred_element_type=F32)
    return x * lax.rsqrt(ss * (1.0 / HEAD_DIM) + NORM_EPS) * g


def _silu(g):
    return g / (1.0 + jnp.exp(-g))


def _l0_kernel(*refs, ns, ts, chunk, has_cache, base_pos, nblk):
    if has_cache:
        (x_ref, pp_ref, kp_ref, vp_ref, g0_ref, win_ref, wpool_ref, pscale_ref, qg_ref, kg_ref, seg_ref,
         bias_ref, sink_ref, wout_ref, y_ref, ps_ref, ks_ref, vs_ref, uext, kext, vext, mix) = refs
    else:
        (x_ref, g0_ref, win_ref, wpool_ref, pscale_ref, qg_ref, kg_ref, seg_ref,
         bias_ref, sink_ref, wout_ref, y_ref, ps_ref, ks_ref, vs_ref, uext, kext, vext, mix) = refs
    padk = SWA_KEYS - WINDOW - chunk
    hist = padk + WINDOW
    i = pl.program_id(0)

    x = x_ref[...]
    h = _rms(x, g0_ref[...]).astype(BF16)
    proj = jnp.dot(h, win_ref[...], preferred_element_type=F32)
    u = proj[:, 0:Q_OFF]
    gate = proj[:, GATE_OFF:AB_IN]
    seg = seg_ref[...]
    qn = [_head_rms_tile(proj[:, Q_OFF + LANES * j:Q_OFF + LANES * (j + 1)], seg,
                         qg_ref[:, LANES * j:LANES * (j + 1)]) * (ATTN_SCALE * LOG2E)
          for j in range(SWA_HEADS // 2)]
    kn = _head_rms_tile(proj[:, K_OFF:V_OFF], seg, kg_ref[...])
    v = proj[:, V_OFF:GATE_OFF]

    if has_cache:
        for s in range(ns):
            uext[s, 0:POOL_HIST, :] = pp_ref[s]
            kext[s, 0:padk, :] = jnp.zeros((padk, LANES), F32)
            vext[s, 0:padk, :] = jnp.zeros((padk, LANES), F32)
            kext[s, padk:hist, :] = kp_ref[s]
            vext[s, padk:hist, :] = vp_ref[s]
    else:
        @pl.when(i == 0)
        def _():
            uext[0, 0:POOL_HIST, :] = jnp.zeros((POOL_HIST, C_POOL), F32)
            kext[0, 0:hist, :] = jnp.zeros((hist, LANES), F32)
            vext[0, 0:hist, :] = jnp.zeros((hist, LANES), F32)

        @pl.when(i > 0)
        def _():
            uext[0, 0:POOL_HIST, :] = uext[0, ts:ts + POOL_HIST, :]
            kext[0, 0:hist, :] = kext[0, ts:ts + hist, :]
            vext[0, 0:hist, :] = vext[0, ts:ts + hist, :]

    for s in range(ns):
        uext[s, POOL_HIST:POOL_HIST + ts, :] = u[s * ts:(s + 1) * ts]
        kext[s, hist:hist + ts, :] = kn[s * ts:(s + 1) * ts]
        vext[s, hist:hist + ts, :] = v[s * ts:(s + 1) * ts]

    pos = base_pos + i * ts + lax.broadcasted_iota(jnp.int32, (ts, LANES), 0)
    for s in range(ns):
        for g, w in enumerate(POOL_WINDOWS):
            cols = slice(LANES * g, LANES * (g + 1))
            acc = uext[s, POOL_HIST:POOL_HIST + ts, cols]
            cur = acc
            for j in range(1, w):
                acc = acc + uext[s, POOL_HIST - j:POOL_HIST - j + ts, cols]
            cnt = jnp.minimum(pos + 1, w).astype(F32)
            mix[s * ts:(s + 1) * ts, cols] = acc / cnt - cur
    for g in range(4):
        cols = slice(LANES * g, LANES * (g + 1))
        d = mix[:, cols].astype(BF16)
        mix[:, cols] = jnp.dot(d, wpool_ref[g], preferred_element_type=F32) * pscale_ref[:, cols]

    rows4 = SWA_REP * chunk
    lo_c = lax.broadcasted_iota(jnp.int32, (chunk, LANES), 1) < HEAD_DIM
    lo_k = lax.broadcasted_iota(jnp.int32, (SWA_KEYS, LANES), 1) < HEAD_DIM
    kj = lax.broadcasted_iota(jnp.int32, (SWA_KEYS, rows4), 0)
    bias = [bias_ref[g] for g in range(SWA_KV_HEADS)]
    sinkc = [sink_ref[g][0:1, :] * LOG2E for g in range(SWA_KV_HEADS)]
    nch = ts // chunk
    units = [(s, c, g) for s in range(ns) for c in range(nch) for g in range(SWA_KV_HEADS)]

    def swa_scores(s, c, g):
        r0 = c * chunk
        rows = slice(s * ts + r0, s * ts + r0 + chunk)
        kwin = kext[s, r0:r0 + SWA_KEYS, :]
        vwin = vext[s, r0:r0 + SWA_KEYS, :]
        krl = pltpu.roll(kwin, HEAD_DIM, 1)
        vrl = pltpu.roll(vwin, HEAD_DIM, 1)
        if g == 0:
            kd = jnp.where(lo_k, kwin, krl).astype(BF16)
            vd = jnp.where(lo_k, vwin, vrl).astype(BF16)
        else:
            kd = jnp.where(lo_k, krl, kwin).astype(BF16)
            vd = jnp.where(lo_k, vrl, vwin).astype(BF16)
        q0 = qn[2 * g][rows]
        q1 = qn[2 * g + 1][rows]
        qs = jnp.concatenate([jnp.where(lo_c, q0, 0.0), jnp.where(lo_c, 0.0, q0),
                              jnp.where(lo_c, q1, 0.0), jnp.where(lo_c, 0.0, q1)], axis=0).astype(BF16)
        return lax.dot_general(kd, qs, (((1,), (1,)), ((), ())), preferred_element_type=F32), vd

    def swa_finish(s, c, g, sc, vd):
        rows = slice(s * ts + c * chunk, s * ts + (c + 1) * chunk)
        sc = sc + bias[g]
        if not has_cache and c * chunk < WINDOW:
            sc = jnp.where(kj >= padk + jnp.maximum(0, WINDOW - (i * nch + c) * chunk), sc, NEG_INF)
        m = jnp.maximum(jnp.max(sc, axis=0, keepdims=True), sinkc[g])
        p = jnp.exp2(sc - m)
        den = jnp.sum(p, axis=0, keepdims=True) + jnp.exp2(sinkc[g] - m)
        wgt = (p / den).astype(BF16)
        o = lax.dot_general(wgt, vd, (((0,), (0,)), ((), ())), preferred_element_type=F32)
        for jj in range(2):
            t = 2 * g + jj
            mix[rows, C_POOL + LANES * t:C_POOL + LANES * (t + 1)] = jnp.where(
                lo_c, o[(2 * jj) * chunk:(2 * jj + 1) * chunk], o[(2 * jj + 1) * chunk:(2 * jj + 2) * chunk])

    pending = swa_scores(*units[0])
    for n, unit in enumerate(units):
        nxt = swa_scores(*units[n + 1]) if n + 1 < len(units) else None
        swa_finish(*unit, *pending)
        pending = nxt

    mixed = (mix[...] * _silu(gate)).astype(BF16)
    y_ref[...] = x + jnp.dot(mixed, wout_ref[...], preferred_element_type=F32)

    @pl.when(i == nblk - 1)
    def _():
        for s in range(ns):
            ps_ref[s] = uext[s, ts:ts + POOL_HIST, :]
            ks_ref[s] = kext[s, padk + ts:padk + ts + WINDOW, :]
            vs_ref[s] = vext[s, padk + ts:padk + ts + WINDOW, :]


def _const_spec(shape):
    nd = len(shape)
    return pl.BlockSpec(shape, lambda i, _nd=nd: (0,) * _nd, pipeline_mode=pl.Buffered(1))


def _layer0(x2d, prefix, wts, *, ns, ts, chunk, base_pos):
    n = x2d.shape[0]
    tb = ns * ts
    nblk = n // tb
    has_cache = prefix is not None
    padk = SWA_KEYS - WINDOW - chunk
    rows4 = SWA_REP * chunk
    slope = np.repeat(2.0 ** (-(np.arange(SWA_HEADS) + 1.0)), chunk).reshape(SWA_KV_HEADS, 1, rows4)
    kj = np.arange(SWA_KEYS).reshape(1, SWA_KEYS, 1)
    rel = (np.arange(rows4) % chunk).reshape(1, 1, rows4) + WINDOW - (kj - padk)
    bias = jnp.asarray(np.where(kj >= padk, -(slope * LOG2E) * np.abs(rel), NEG_INF).astype(np.float32))
    sink = jnp.broadcast_to(jnp.repeat(wts["sinks"].astype(F32), chunk).reshape(SWA_KV_HEADS, 1, rows4),
                            (SWA_KV_HEADS, 8, rows4))
    consts = [wts["g0"], wts["w_in0"], wts["w_pool"], wts["pool_scale"], wts["qg0"], wts["kg0"], wts["seg"],
              bias, sink, wts["w_out0"]]
    in_specs = [pl.BlockSpec((tb, D_MODEL), lambda i: (i, 0))]
    args = [x2d]
    if has_cache:
        for a in prefix:
            in_specs.append(_const_spec(a.shape))
            args.append(a)
    for a in consts:
        in_specs.append(_const_spec(a.shape))
        args.append(a)
    out_shape = [jax.ShapeDtypeStruct((n, D_MODEL), F32),
                 jax.ShapeDtypeStruct((ns, POOL_HIST, C_POOL), F32),
                 jax.ShapeDtypeStruct((ns, WINDOW, LANES), F32),
                 jax.ShapeDtypeStruct((ns, WINDOW, LANES), F32)]
    out_specs = [pl.BlockSpec((tb, D_MODEL), lambda i: (i, 0)),
                 pl.BlockSpec((ns, POOL_HIST, C_POOL), lambda i: (0, 0, 0)),
                 pl.BlockSpec((ns, WINDOW, LANES), lambda i: (0, 0, 0)),
                 pl.BlockSpec((ns, WINDOW, LANES), lambda i: (0, 0, 0))]
    scratch = [pltpu.VMEM((ns, POOL_HIST + ts, C_POOL), F32),
               pltpu.VMEM((ns, padk + WINDOW + ts, LANES), F32),
               pltpu.VMEM((ns, padk + WINDOW + ts, LANES), F32),
               pltpu.VMEM((tb, D_MODEL), F32)]
    kern = functools.partial(_l0_kernel, ns=ns, ts=ts, chunk=chunk, has_cache=has_cache,
                             base_pos=base_pos, nblk=nblk)
    return pl.pallas_call(
        kern, grid=(nblk,), in_specs=in_specs, out_specs=out_specs, out_shape=out_shape,
        scratch_shapes=scratch, compiler_params=_cparams(("arbitrary",)),
        name="layer0_prompt" if not has_cache else "layer0_sample")(*args)


def _log_sigmoid(z):
    return jnp.minimum(z, 0.0) - jnp.log1p(jnp.exp(-jnp.abs(z)))


def _fox_proj_sample_kernel(x_ref, g1_ref, wqt_ref, wk_ref, wvt_ref, wf_ref, bf_ref, qg_ref, kg_ref, seg_ref,
                            k_ref, v_ref, lf_ref, q_ref):
    x = x_ref[...]
    h = _rms(x, g1_ref[...]).astype(BF16)
    z = jnp.dot(h, wf_ref[...], preferred_element_type=F32) + bf_ref[...]
    lf_ref[...] = _log_sigmoid(z)[:, 0:FOX_HEADS]
    seg = seg_ref[...]
    nt = (((1,), (1,)), ((), ()))
    q = lax.dot_general(h, wqt_ref[...], nt, preferred_element_type=F32)
    k = jnp.dot(h, wk_ref[...], preferred_element_type=F32)
    v_ref[...] = lax.dot_general(h, wvt_ref[...], nt, preferred_element_type=F32)
    for j in range(FOX_HEADS // 2):
        cols = slice(LANES * j, LANES * (j + 1))
        q_ref[:, cols] = _head_rms_tile(q[:, cols], seg, qg_ref[:, cols]) * ATTN_SCALE
        k_ref[:, cols] = _head_rms_tile(k[:, cols], seg, kg_ref[:, cols])


def _fox_proj_sample(x2d, wts, *, tb):
    n = x2d.shape[0]
    consts = [wts["g1"], wts["w_q1t"], wts["w_k1"], wts["w_v1t"], wts["w_f"], wts["b_f"], wts["qg1"], wts["kg1"],
              wts["seg"]]
    row_spec = pl.BlockSpec((tb, D_MODEL), lambda i: (i, 0))
    in_specs = [row_spec] + [_const_spec(a.shape) for a in consts]
    row = jax.ShapeDtypeStruct((n, D_MODEL), F32)
    out_shape = [row, row, jax.ShapeDtypeStruct((n, FOX_HEADS), F32), row]
    out_specs = [row_spec, row_spec, pl.BlockSpec((tb, FOX_HEADS), lambda i: (i, 0)), row_spec]
    return pl.pallas_call(
        _fox_proj_sample_kernel, grid=(n // tb,), in_specs=in_specs, out_specs=out_specs, out_shape=out_shape,
        compiler_params=_cparams(("arbitrary",)), name="fox_proj_sample")(x2d, *consts)


def _fox_proj_prompt_kernel(x_ref, g1_ref, wk_ref, wft_ref, bfc_ref, wqt_ref, wvt_ref, gq_ref, kg_ref,
                            seg_ref, tri_ref, e_ref,
                            k_ref, vt_ref, lft_ref, qa_ref, ka_ref, va_ref, ce_ref, carry, *, tb):
    i = pl.program_id(0)
    x = x_ref[...]
    hf = _rms(x, g1_ref[...])
    h = hf.astype(BF16)
    ht = hf.T.astype(BF16)
    bcol = jnp.concatenate([bfc_ref[...]] * (tb // LANES), axis=1)
    logft = _log_sigmoid(jnp.dot(wft_ref[...], ht, preferred_element_type=F32) + bcol)
    lft_ref[...] = logft
    k = jnp.dot(h, wk_ref[...], preferred_element_type=F32)

    @pl.when(i == 0)
    def _():
        carry[...] = jnp.zeros(carry.shape, F32)

    c3 = jnp.dot(jnp.concatenate(_split3(logft), axis=0), tri_ref[...], preferred_element_type=F32)

    qt = jnp.dot(wqt_ref[...], ht, preferred_element_type=F32)
    vt = jnp.dot(wvt_ref[...], ht, preferred_element_type=F32)
    vt_ref[...] = vt
    srow = lax.broadcasted_iota(jnp.int32, (HEAD_DIM, tb), 0)
    q_aug = jnp.where(srow < 3, 1.0, 0.0)
    v_aug = jnp.where(srow == 0, 1.0, 0.0)
    gq = jnp.concatenate([gq_ref[...]] * (tb // LANES), axis=1) * (ATTN_SCALE * LOG2E)
    for hd in range(FOX_HEADS):
        rows = slice(HEAD_DIM * hd, HEAD_DIM * (hd + 1))
        qh = qt[rows]
        ss = jnp.sum(qh * qh, axis=0, keepdims=True)
        qn = qh * lax.rsqrt(ss * (1.0 / HEAD_DIM) + NORM_EPS) * gq
        qa_ref[hd] = jnp.concatenate([qn, q_aug], axis=0).astype(BF16)
        va_ref[hd] = jnp.concatenate([vt[rows], v_aug], axis=0).astype(BF16)

    cumt = c3[0:FOX_HEADS] + c3[FOX_HEADS:2 * FOX_HEADS] + c3[2 * FOX_HEADS:3 * FOX_HEADS] + carry[:, 0:1]
    end = jnp.broadcast_to(cumt[:, tb - 1:tb], (FOX_HEADS, LANES))
    carry[...] = end
    ce_ref[0] = end
    parts = [p.astype(F32) for p in _split3(-LOG2E * cumt)]
    n3 = jnp.concatenate(parts + [jnp.zeros((LANES - 3 * FOX_HEADS, tb), F32)], axis=0).T.astype(BF16)
    biasall = jnp.dot(n3, e_ref[...], preferred_element_type=F32)
    lo_l = lax.broadcasted_iota(jnp.int32, (tb, LANES), 1) < HEAD_DIM
    seg = seg_ref[...]
    for j in range(FOX_HEADS // 2):
        cols = slice(LANES * j, LANES * (j + 1))
        kt = _head_rms_tile(k[:, cols], seg, kg_ref[:, cols])
        k_ref[:, cols] = kt
        bt = biasall[:, cols]
        ka_ref[2 * j] = jnp.where(lo_l, kt, bt).astype(BF16)
        ka_ref[2 * j + 1] = pltpu.roll(jnp.where(lo_l, bt, kt), HEAD_DIM, 1).astype(BF16)


def _fox_proj_prompt(x2d, wts, *, tb):
    n = x2d.shape[0]
    tri = jnp.asarray(np.triu(np.ones((tb, tb), np.float32)), BF16)
    e = np.zeros((LANES, FOX_HEADS // 2 * LANES), np.float32)
    for part in range(3):
        for hd in range(FOX_HEADS):
            e[part * FOX_HEADS + hd, (hd // 2) * LANES + (HEAD_DIM if hd % 2 == 0 else 0) + part] = 1.0
    consts = [wts["g1"], wts["w_k1"], wts["w_ft"], wts["b_fc"], wts["w_q1t"], wts["w_v1t"],
              wts["gq1c"], wts["kg1"], wts["seg"], tri, jnp.asarray(e, BF16)]
    row_spec = pl.BlockSpec((tb, D_MODEL), lambda i: (i, 0))
    in_specs = [row_spec] + [_const_spec(a.shape) for a in consts]
    row = jax.ShapeDtypeStruct((n, D_MODEL), F32)
    out_shape = [row, jax.ShapeDtypeStruct((D_MODEL, n), F32), jax.ShapeDtypeStruct((FOX_HEADS, n), F32),
                 jax.ShapeDtypeStruct((FOX_HEADS, LANES, n), BF16), jax.ShapeDtypeStruct((FOX_HEADS, n, LANES), BF16),
                 jax.ShapeDtypeStruct((FOX_HEADS, LANES, n), BF16),
                 jax.ShapeDtypeStruct((n // tb, FOX_HEADS, LANES), F32)]
    tspec = pl.BlockSpec((FOX_HEADS, LANES, tb), lambda i: (0, 0, i))
    out_specs = [row_spec, pl.BlockSpec((D_MODEL, tb), lambda i: (0, i)),
                 pl.BlockSpec((FOX_HEADS, tb), lambda i: (0, i)),
                 tspec, pl.BlockSpec((FOX_HEADS, tb, LANES), lambda i: (0, i, 0)), tspec,
                 pl.BlockSpec((1, FOX_HEADS, LANES), lambda i: (i, 0, 0))]
    return pl.pallas_call(
        functools.partial(_fox_proj_prompt_kernel, tb=tb), grid=(n // tb,), in_specs=in_specs,
        out_specs=out_specs, out_shape=out_shape, scratch_shapes=[pltpu.VMEM((FOX_HEADS, LANES), F32)],
        compiler_params=_cparams(("arbitrary",)), name="fox_proj_prompt")(x2d, *consts)


def _fox_attn_kernel(ce_ref, qk_ref, q_ref, k_ref, v_ref, o_ref, m_scr, al_scr, acc_scr, s_scr,
                     *, tq, qpb, **kw):
    def blk(qb):
        return _fox_attn_block(pl.program_id(0), pl.program_id(1) * qpb + qb, pl.multiple_of(qb * tq, tq),
                               ce_ref, qk_ref, q_ref, k_ref, v_ref, o_ref, m_scr, al_scr, acc_scr, s_scr,
                               tq=tq, **kw)

    blk(0)[0](None)

    def body(qb, carry):
        blk(qb)[0](blk(qb - 1)[1])
        return carry

    lax.fori_loop(1, qpb, body, 0)
    blk(qpb - 1)[1]()


def _fox_attn_block(pair, qi, qoff, ce_ref, qk_ref, q_ref, k_ref, v_ref, o_ref, m_scr, al_scr, acc_scr, s_scr,
                    *, tq, tk, unroll, nkb, nh):
    n_u = qi

    last = jnp.maximum(n_u - 1, 0)
    slack = 2.0 * qk_ref[0]

    def dead(j):
        jj = jnp.minimum(j, last)
        ok = j < n_u - 1
        for hh in range(nh):
            hd = nh * pair + hh
            gap = LOG2E * (ce_ref[last * FOX_HEADS + hd] - ce_ref[jj * FOX_HEADS + hd])
            ok = ok & (slack + gap < EXP2_ZERO)
        return ok

    lo = jnp.int32(0)
    hi = last
    for _ in range(max(1, (nkb - 1).bit_length())):
        mid = (lo + hi) >> 1
        d = dead(mid)
        lo = jnp.where(d, mid + 1, lo)
        hi = jnp.where(d, hi, mid)
    first = lo

    def score_matmuls(kb):
        off = pl.multiple_of(kb * tk, tk)
        return [jnp.dot(k_ref[hh, pl.ds(off, tk), :], q_ref[hh, :, pl.ds(qoff, tq)],
                        preferred_element_type=F32)
                for hh in range(nh)]

    def score_finish(sts):
        for hh in range(nh):
            st = sts[hh]
            m_old = m_scr[hh, 0:1, :]
            m_new = jnp.maximum(m_old, jnp.max(st, axis=0, keepdims=True))
            al_scr[hh, 0:1, :] = jnp.exp2(m_old - m_new)
            m_scr[hh, 0:1, :] = m_new
            s_scr[hh] = st

    def accumulate(kb):
        off = pl.multiple_of(kb * tk, tk)
        for hh in range(nh):
            p = jnp.exp2(s_scr[hh] - m_scr[hh, 0:1, :]).astype(BF16)
            pv = jnp.dot(v_ref[hh, :, pl.ds(off, tk)], p, preferred_element_type=F32)
            acc_scr[hh] = al_scr[hh, 0:1, :] * acc_scr[hh] + pv

    def advance(kb):
        sts = score_matmuls(kb + 1)
        accumulate(kb)
        score_finish(sts)

    def tail():
        half = tq // 2
        off = pl.multiple_of(n_u * tk, tk)
        sts = [(jnp.dot(k_ref[hh, pl.ds(off, half), :], q_ref[hh, :, pl.ds(qoff, half)],
                        preferred_element_type=F32),
                jnp.dot(k_ref[hh, pl.ds(off, tk), :], q_ref[hh, :, pl.ds(pl.multiple_of(qoff + half, half), half)],
                        preferred_element_type=F32))
               for hh in range(nh)]
        accumulate(last)
        tri = jnp.where(lax.broadcasted_iota(jnp.int32, (half, half), 0)
                        <= lax.broadcasted_iota(jnp.int32, (half, half), 1), 0.0, NEG_INF)
        for hh in range(nh):
            st_a, st_b = sts[hh]
            halves = ((st_a + tri, 0), (jnp.concatenate([st_b[0:half], st_b[half:tk] + tri], axis=0), half))
            for st, c0 in halves:
                nk = st.shape[0]
                m_old = m_scr[hh, 0:1, c0:c0 + half]
                m_new = jnp.maximum(m_old, jnp.max(st, axis=0, keepdims=True))
                p = jnp.exp2(st - m_new).astype(BF16)
                pv = jnp.dot(v_ref[hh, :, pl.ds(off, nk)], p, preferred_element_type=F32)
                acc_scr[hh, :, c0:c0 + half] = jnp.exp2(m_old - m_new) * acc_scr[hh, :, c0:c0 + half] + pv
        for t in range(nh // 2):
            pair_rows = []
            for hh in (2 * t, 2 * t + 1):
                a = acc_scr[hh]
                pair_rows.append(a[0:HEAD_DIM] / a[HEAD_DIM:HEAD_DIM + 1, :])
            o_ref[pl.ds(qoff, tq), LANES * t:LANES * (t + 1)] = jnp.concatenate(
                pair_rows, axis=0).T.astype(o_ref.dtype)

    def head(prev_tail):
        def start():
            if prev_tail is not None:
                prev_tail()
            m_scr[...] = jnp.full(m_scr.shape, NEG_INF, F32)
            acc_scr[...] = jnp.zeros(acc_scr.shape, F32)

        def below_diagonal():
            n_adv = n_u - 1 - first
            rem = lax.rem(n_adv, unroll)
            for k in range(unroll):
                @pl.when(rem == k)
                def _():
                    start()
                    score_finish(score_matmuls(first))
                    for u in range(k):
                        advance(first + u)

            def body(t, carry):
                for u in range(unroll):
                    advance(first + rem + unroll * t + u)
                return carry

            lax.fori_loop(0, n_adv // unroll, body, 0)

        if prev_tail is None:
            @pl.when(qi == 0)
            def _():
                start()
                m_scr[...] = jnp.zeros(m_scr.shape, F32)
                al_scr[...] = jnp.ones(al_scr.shape, F32)
                s_scr[...] = jnp.full(s_scr.shape, NEG_INF, F32)

            pl.when(qi > 0)(below_diagonal)
        else:
            below_diagonal()

    return head, tail


def _fox_attn_prompt(cum_end, qk_bound, qa, ka, va, *, tq, tk, unroll, nh, qpb):
    n = ka.shape[1]
    assert tq == tk and n % (qpb * tq) == 0
    resident = dict(pipeline_mode=pl.Buffered(1)) if nh > 2 else {}
    grid_spec = pltpu.PrefetchScalarGridSpec(
        num_scalar_prefetch=2, grid=(FOX_HEADS // nh, n // (qpb * tq)),
        in_specs=[pl.BlockSpec((nh, LANES, qpb * tq), lambda p, i, ce, qk: (p, 0, i)),
                  pl.BlockSpec((nh, n, LANES), lambda p, i, ce, qk: (p, 0, 0), **resident),
                  pl.BlockSpec((nh, LANES, n), lambda p, i, ce, qk: (p, 0, 0), **resident)],
        out_specs=pl.BlockSpec((qpb * tq, HEAD_DIM * nh), lambda p, i, ce, qk: (i, p)),
        scratch_shapes=[pltpu.VMEM((nh, 8, tq), F32), pltpu.VMEM((nh, 8, tq), F32),
                        pltpu.VMEM((nh, LANES, tq), F32), pltpu.VMEM((nh, tk, tq), F32)])
    return pl.pallas_call(
        functools.partial(_fox_attn_kernel, tq=tq, tk=tk, unroll=unroll, nkb=n // tk, nh=nh, qpb=qpb),
        grid_spec=grid_spec,
        out_shape=jax.ShapeDtypeStruct((n, D_MODEL), BF16),
        compiler_params=_cparams(("arbitrary", "arbitrary")), name="fox_attn_prompt")(
            cum_end, qk_bound, qa, ka, va)


CUM_BLK = 512
NEW_PAD = 128
SAMPLE_GROUPS_PER_STEP = 2


def _fox_sample_kernel(q_ref, kc_ref, vc_ref, lfc_ref, kn_ref, vn_ref, lfn_ref, tri_ref, o_ref, negc, negn,
                       *, t_new, p_len, gps):
    rows = 4 * t_new

    @pl.when(pl.program_id(1) == 0)
    def _():
        tri = tri_ref[...]

        def prefix_sums(x, t):
            c3 = jnp.dot(jnp.concatenate(_split3(x), axis=0), t, preferred_element_type=F32)
            return c3[0:FOX_HEADS] + c3[FOX_HEADS:2 * FOX_HEADS] + c3[2 * FOX_HEADS:3 * FOX_HEADS]

        carry = jnp.zeros((FOX_HEADS, 1), F32)
        for b in range(p_len // CUM_BLK):
            c = prefix_sums(lfc_ref[0, :, b * CUM_BLK:(b + 1) * CUM_BLK], tri) + carry
            negc[:, b * CUM_BLK:(b + 1) * CUM_BLK] = -c
            carry = c[:, CUM_BLK - 1:CUM_BLK]
        negn[...] = -(prefix_sums(lfn_ref[0], tri[0:NEW_PAD, 0:NEW_PAD]) + carry)

    rhead = lax.broadcasted_iota(jnp.int32, (rows, 2 * LANES), 0) // t_new
    lhead = lax.broadcasted_iota(jnp.int32, (rows, 2 * LANES), 1) // HEAD_DIM
    qrow = lax.broadcasted_iota(jnp.int32, (rows, NEW_PAD), 0) % t_new
    kcol = lax.broadcasted_iota(jnp.int32, (rows, NEW_PAD), 1)
    lh = lax.broadcasted_iota(jnp.int32, (t_new, 2 * LANES), 1) // HEAD_DIM
    zpad = jnp.zeros((NEW_PAD - t_new, 2 * LANES), F32)
    for g in range(gps):
        qd = gps * pl.program_id(1) + g
        cols = slice(2 * LANES * g, 2 * LANES * (g + 1))
        qq = q_ref[0, :, cols]
        qbd = jnp.where(rhead == lhead, jnp.concatenate([qq] * 4, axis=0), 0.0).astype(BF16)
        kct = kc_ref[0, 4 * g:4 * (g + 1)].reshape(2 * LANES, p_len).astype(BF16)
        vct = vc_ref[0, 4 * g:4 * (g + 1)].reshape(2 * LANES, p_len).astype(BF16)
        s_c = jnp.dot(qbd, kct, preferred_element_type=F32)
        kn = jnp.concatenate([kn_ref[0, :, cols], zpad], axis=0).astype(BF16)
        vn = jnp.concatenate([vn_ref[0, :, cols], zpad], axis=0).astype(BF16)
        s_n = lax.dot_general(qbd, kn, (((1,), (1,)), ((), ())), preferred_element_type=F32)
        bc = jnp.concatenate([jnp.broadcast_to(negc[pl.ds(4 * qd + jh, 1), :], (t_new, p_len))
                              for jh in range(4)], axis=0)
        bn = jnp.concatenate([jnp.broadcast_to(negn[pl.ds(4 * qd + jh, 1), :], (t_new, NEW_PAD))
                              for jh in range(4)], axis=0)
        s_c = s_c + bc
        s_n = jnp.where(kcol <= qrow, s_n + bn, NEG_INF)
        m = jnp.maximum(jnp.max(s_c, axis=-1, keepdims=True), jnp.max(s_n, axis=-1, keepdims=True))
        p_c = jnp.exp(s_c - m)
        p_n = jnp.exp(s_n - m)
        den = jnp.sum(p_c, axis=-1, keepdims=True) + jnp.sum(p_n, axis=-1, keepdims=True)
        o = (lax.dot_general(p_c.astype(BF16), vct, (((1,), (1,)), ((), ())), preferred_element_type=F32)
             + jnp.dot(p_n.astype(BF16), vn, preferred_element_type=F32)) / den
        out = jnp.zeros((t_new, 2 * LANES), F32)
        for jh in range(4):
            out = jnp.where(lh == jh, o[jh * t_new:(jh + 1) * t_new], out)
        o_ref[0, :, cols] = out


def _fox_attn_sample(q, cache_k, cache_v, lfc_t, k_new, v_new, lfn_t):
    b, t_new, _ = q.shape
    p_len = cache_k.shape[3]
    tri = jnp.asarray(np.triu(np.ones((CUM_BLK, CUM_BLK), np.float32)), BF16)
    gps = SAMPLE_GROUPS_PER_STEP
    width = 2 * LANES * gps
    in_specs = [pl.BlockSpec((1, t_new, width), lambda r, d: (r, 0, d)),
                pl.BlockSpec((1, 4 * gps, HEAD_DIM, p_len), lambda r, d: (r, d, 0, 0)),
                pl.BlockSpec((1, 4 * gps, HEAD_DIM, p_len), lambda r, d: (r, d, 0, 0)),
                pl.BlockSpec((1, FOX_HEADS, p_len), lambda r, d: (r, 0, 0)),
                pl.BlockSpec((1, t_new, width), lambda r, d: (r, 0, d)),
                pl.BlockSpec((1, t_new, width), lambda r, d: (r, 0, d)),
                pl.BlockSpec((1, FOX_HEADS, NEW_PAD), lambda r, d: (r, 0, 0)),
                pl.BlockSpec((CUM_BLK, CUM_BLK), lambda r, d: (0, 0))]
    return pl.pallas_call(
        functools.partial(_fox_sample_kernel, t_new=t_new, p_len=p_len, gps=gps),
        grid=(b, D_MODEL // width), in_specs=in_specs,
        out_specs=pl.BlockSpec((1, t_new, width), lambda r, d: (r, 0, d)),
        out_shape=jax.ShapeDtypeStruct((b, t_new, D_MODEL), F32),
        scratch_shapes=[pltpu.VMEM((FOX_HEADS, p_len), F32), pltpu.VMEM((FOX_HEADS, NEW_PAD), F32)],
        compiler_params=_cparams(("arbitrary", "arbitrary")), name="fox_attn_sample")(
            q, cache_k, cache_v, lfc_t, k_new, v_new, lfn_t, tri)


def _out_proj_kernel(r_ref, a_ref, g1_ref, wg_ref, w_ref, y_ref):
    x = r_ref[...]
    h = _rms(x, g1_ref[...]).astype(BF16)
    gate = jnp.dot(h, wg_ref[...], preferred_element_type=F32)
    mixed = (a_ref[...].astype(F32) * _silu(gate)).astype(BF16)
    y_ref[...] = x + jnp.dot(mixed, w_ref[...], preferred_element_type=F32)


def _out_proj(resid, attn, wts, *, tb):
    n = resid.shape[0]
    row = pl.BlockSpec((tb, D_MODEL), lambda i: (i, 0))
    consts = [wts["g1"], wts["w_g1"], wts["w_out1"]]
    return pl.pallas_call(
        _out_proj_kernel, grid=(n // tb,), in_specs=[row, row] + [_const_spec(a.shape) for a in consts],
        out_specs=row, out_shape=jax.ShapeDtypeStruct((n, D_MODEL), F32),
        compiler_params=_cparams(("arbitrary",)), name="out_proj")(resid, attn, *consts)


L0_TB = 512
OUT_TB = 1024
FOX_TB = 256
ATTN_TQ = 512
ATTN_TK = 512
ATTN_UNROLL = 6
ATTN_HEADS = 2
ATTN_QPB = 8
SWA_CHUNK = 64


def kernel(x_prompt, x_sample, state_pool, cache_swa_k, cache_swa_v, cache_fox_k, cache_fox_v, cache_fox_logf,
           norm0_g, w_in0, w_pool, pool_scale, swa_qn_g, swa_kn_g, swa_sinks, w_out0,
           norm1_g, w_in1, b_forget, fox_qn_g, fox_kn_g, w_out1):
    nb, seq, _ = x_prompt.shape
    db, dseq, _ = x_sample.shape
    past_len = cache_fox_k.shape[1]
    assert nb == 1 and seq % L0_TB == 0 and seq % OUT_TB == 0 and seq % ATTN_TQ == 0 and dseq % 8 == 0

    seg = np.kron(np.eye(2, dtype=np.float32), np.ones((HEAD_DIM, HEAD_DIM), np.float32))
    mc = FOX_HEADS * HEAD_DIM
    wts = {
        "g0": norm0_g.reshape(1, D_MODEL), "w_in0": w_in0.astype(BF16), "w_pool": w_pool.astype(BF16),
        "pool_scale": pool_scale.reshape(1, C_POOL),
        "qg0": jnp.tile(swa_qn_g, SWA_HEADS).reshape(1, SWA_HEADS * HEAD_DIM),
        "kg0": jnp.tile(swa_kn_g, SWA_KV_HEADS).reshape(1, LANES),
        "seg": jnp.asarray(seg, BF16), "sinks": swa_sinks, "w_out0": w_out0.astype(BF16),
        "g1": norm1_g.reshape(1, D_MODEL),
        "w_out1": w_out1.astype(BF16),
        "w_f": jnp.pad(w_in1[:, 4 * mc:], ((0, 0), (0, LANES - FOX_HEADS))).astype(BF16),
        "b_f": jnp.pad(b_forget.astype(F32), (0, LANES - FOX_HEADS)).reshape(1, LANES),
        "w_ft": w_in1[:, 4 * mc:].T.astype(BF16),
        "b_fc": jnp.broadcast_to(b_forget.astype(F32).reshape(FOX_HEADS, 1), (FOX_HEADS, LANES)),
        "qg1": jnp.tile(fox_qn_g, FOX_HEADS).reshape(1, mc), "kg1": jnp.tile(fox_kn_g, FOX_HEADS).reshape(1, mc),
        "w_k1": w_in1[:, mc:2 * mc].astype(BF16), "w_g1": w_in1[:, 3 * mc:4 * mc].astype(BF16),
        "w_q1t": w_in1[:, 0:mc].T.astype(BF16), "w_v1t": w_in1[:, 2 * mc:3 * mc].T.astype(BF16),
        "gq1c": jnp.broadcast_to(fox_qn_g.astype(F32).reshape(HEAD_DIM, 1), (HEAD_DIM, LANES)),
    }

    xp = x_prompt.reshape(seq, D_MODEL)
    y0p, pool_p, swk_p, swv_p = _layer0(xp, None, wts, ns=1, ts=L0_TB, chunk=SWA_CHUNK, base_pos=0)
    fk_p, fvt_p, flt_p, qa, ka, va, cum_end = _fox_proj_prompt(y0p, wts, tb=FOX_TB)
    fl_p = jnp.transpose(flt_p)
    fv_p = jnp.transpose(fvt_p.reshape(FOX_HEADS, HEAD_DIM, seq), (2, 0, 1))
    per = ATTN_TK // FOX_TB
    cum_end = cum_end[per - 1::per, :, 0].reshape(-1)
    qk_bound = (1.02 * LOG2E * HEAD_DIM * ATTN_SCALE * jnp.max(jnp.abs(fox_qn_g)) * jnp.max(jnp.abs(fox_kn_g))
                ).astype(F32).reshape(1)
    attn_p = _fox_attn_prompt(cum_end, qk_bound, qa, ka, va, tq=ATTN_TQ, tk=ATTN_TK, unroll=ATTN_UNROLL,
                              nh=ATTN_HEADS, qpb=ATTN_QPB)
    yp = _out_proj(y0p, attn_p, wts, tb=OUT_TB)

    xs = x_sample.reshape(db * dseq, D_MODEL)
    prefix = (jnp.pad(state_pool, ((0, 0), (POOL_HIST - POOL_PAD, 0), (0, 0))),
              cache_swa_k.reshape(db, WINDOW, LANES), cache_swa_v.reshape(db, WINDOW, LANES))
    y0s, pool_s, swk_s, swv_s = _layer0(xs, prefix, wts, ns=db, ts=dseq, chunk=dseq, base_pos=past_len)
    fk_s, fv_s, fl_s, q_s = _fox_proj_sample(y0s, wts, tb=db * dseq)
    attn_s = _fox_attn_sample(
        q_s.reshape(db, dseq, D_MODEL),
        jnp.transpose(cache_fox_k, (0, 2, 3, 1)), jnp.transpose(cache_fox_v, (0, 2, 3, 1)),
        jnp.transpose(cache_fox_logf, (0, 2, 1)),
        fk_s.reshape(db, dseq, D_MODEL), fv_s.reshape(db, dseq, D_MODEL),
        jnp.pad(jnp.transpose(fl_s.reshape(db, dseq, FOX_HEADS), (0, 2, 1)), ((0, 0), (0, 0), (0, NEW_PAD - dseq))))
    ys = _out_proj(y0s, attn_s.reshape(db * dseq, D_MODEL), wts, tb=db * dseq)

    return (yp.reshape(1, seq, D_MODEL), ys.reshape(db, dseq, D_MODEL),
            pool_p[:, POOL_HIST - POOL_PAD:], pool_s[:, POOL_HIST - POOL_PAD:],
            swk_p.reshape(1, WINDOW, SWA_KV_HEADS, HEAD_DIM), swv_p.reshape(1, WINDOW, SWA_KV_HEADS, HEAD_DIM),
            swk_s.reshape(db, WINDOW, SWA_KV_HEADS, HEAD_DIM), swv_s.reshape(db, WINDOW, SWA_KV_HEADS, HEAD_DIM),
            fk_p.reshape(1, seq, FOX_HEADS, HEAD_DIM), fv_p.reshape(1, seq, FOX_HEADS, HEAD_DIM),
            fl_p.reshape(1, seq, FOX_HEADS),
            fk_s.reshape(db, dseq, FOX_HEADS, HEAD_DIM), fv_s.reshape(db, dseq, FOX_HEADS, HEAD_DIM),
            fl_s.reshape(db, dseq, FOX_HEADS))
```

```python
import functools

import numpy as np
import jax
import jax.numpy as jnp
from jax import lax
from jax.experimental import pallas as pl
from jax.experimental.pallas import tpu as pltpu

F32 = jnp.float32
BF16 = jnp.bfloat16

D_MODEL = 1024
HEAD_DIM = 64
ATTN_SCALE = HEAD_DIM ** -0.5
POOL_WINDOWS = (2, 4, 8, 16)
C_POOL = 512
POOL_PAD = 15
POOL_HIST = POOL_PAD + 1
SWA_HEADS = 8
SWA_KV_HEADS = 2
SWA_REP = 4
WINDOW = 128
FOX_HEADS = 16
NORM_EPS = 1e-6
Q_OFF = C_POOL
K_OFF = Q_OFF + SWA_HEADS * HEAD_DIM
V_OFF = K_OFF + SWA_KV_HEADS * HEAD_DIM
GATE_OFF = V_OFF + SWA_KV_HEADS * HEAD_DIM
AB_IN = GATE_OFF + D_MODEL
NEG_INF = -1e30
LOG2E = 1.4426950408889634
EXP2_ZERO = -152.0

LANES = 128
SWA_KEYS = 256
VMEM_LIMIT = 56 * 1024 * 1024


def _cparams(sem):
    return pltpu.CompilerParams(dimension_semantics=sem, vmem_limit_bytes=VMEM_LIMIT)


def _rms(x, g):
    ms = jnp.mean(x * x, axis=-1, keepdims=True)
    return x * lax.rsqrt(ms + NORM_EPS) * g


def _split3(x):
    hi = x.astype(BF16)
    r = x - hi.astype(F32)
    mid = r.astype(BF16)
    lo = (r - mid.astype(F32)).astype(BF16)
    return hi, mid, lo


def _head_rms_tile(x, seg, g):
    ss = jnp.dot((x * x).astype(BF16), seg, preferred_element_type=F32)
    return x * lax.rsqrt(ss * (1.0 / HEAD_DIM) + NORM_EPS) * g


def _silu(g):
    return g / (1.0 + jnp.exp(-g))


def _l0_kernel(*refs, ns, ts, chunk, has_cache, base_pos, nblk):
    if has_cache:
        (x_ref, pp_ref, kp_ref, vp_ref, g0_ref, win_ref, wpool_ref, pscale_ref, qg_ref, kg_ref, seg_ref,
         bias_ref, sink_ref, wout_ref, y_ref, ps_ref, ks_ref, vs_ref, uext, kext, vext, mix) = refs
    else:
        (x_ref, g0_ref, win_ref, wpool_ref, pscale_ref, qg_ref, kg_ref, seg_ref,
         bias_ref, sink_ref, wout_ref, y_ref, ps_ref, ks_ref, vs_ref, uext, kext, vext, mix) = refs
    padk = SWA_KEYS - WINDOW - chunk
    hist = padk + WINDOW
    i = pl.program_id(0)

    x = x_ref[...]
    h = _rms(x, g0_ref[...]).astype(BF16)
    proj = jnp.dot(h, win_ref[...], preferred_element_type=F32)
    u = proj[:, 0:Q_OFF]
    gate = proj[:, GATE_OFF:AB_IN]
    seg = seg_ref[...]
    qn = [_head_rms_tile(proj[:, Q_OFF + LANES * j:Q_OFF + LANES * (j + 1)], seg,
                         qg_ref[:, LANES * j:LANES * (j + 1)]) * (ATTN_SCALE * LOG2E)
          for j in range(SWA_HEADS // 2)]
    kn = _head_rms_tile(proj[:, K_OFF:V_OFF], seg, kg_ref[...])
    v = proj[:, V_OFF:GATE_OFF]

    if has_cache:
        for s in range(ns):
            uext[s, 0:POOL_HIST, :] = pp_ref[s]
            kext[s, 0:padk, :] = jnp.zeros((padk, LANES), F32)
            vext[s, 0:padk, :] = jnp.zeros((padk, LANES), F32)
            kext[s, padk:hist, :] = kp_ref[s]
            vext[s, padk:hist, :] = vp_ref[s]
    else:
        @pl.when(i == 0)
        def _():
            uext[0, 0:POOL_HIST, :] = jnp.zeros((POOL_HIST, C_POOL), F32)
            kext[0, 0:hist, :] = jnp.zeros((hist, LANES), F32)
            vext[0, 0:hist, :] = jnp.zeros((hist, LANES), F32)

        @pl.when(i > 0)
        def _():
            uext[0, 0:POOL_HIST, :] = uext[0, ts:ts + POOL_HIST, :]
            kext[0, 0:hist, :] = kext[0, ts:ts + hist, :]
            vext[0, 0:hist, :] = vext[0, ts:ts + hist, :]

    for s in range(ns):
        uext[s, POOL_HIST:POOL_HIST + ts, :] = u[s * ts:(s + 1) * ts]
        kext[s, hist:hist + ts, :] = kn[s * ts:(s + 1) * ts]
        vext[s, hist:hist + ts, :] = v[s * ts:(s + 1) * ts]

    pos = base_pos + i * ts + lax.broadcasted_iota(jnp.int32, (ts, LANES), 0)
    for s in range(ns):
        for g, w in enumerate(POOL_WINDOWS):
            cols = slice(LANES * g, LANES * (g + 1))
            acc = uext[s, POOL_HIST:POOL_HIST + ts, cols]
            cur = acc
            for j in range(1, w):
                acc = acc + uext[s, POOL_HIST - j:POOL_HIST - j + ts, cols]
            cnt = jnp.minimum(pos + 1, w).astype(F32)
            mix[s * ts:(s + 1) * ts, cols] = acc / cnt - cur
    for g in range(4):
        cols = slice(LANES * g, LANES * (g + 1))
        d = mix[:, cols].astype(BF16)
        mix[:, cols] = jnp.dot(d, wpool_ref[g], preferred_element_type=F32) * pscale_ref[:, cols]

    rows4 = SWA_REP * chunk
    lo_c = lax.broadcasted_iota(jnp.int32, (chunk, LANES), 1) < HEAD_DIM
    lo_k = lax.broadcasted_iota(jnp.int32, (SWA_KEYS, LANES), 1) < HEAD_DIM
    kj = lax.broadcasted_iota(jnp.int32, (SWA_KEYS, rows4), 0)
    bias = [bias_ref[g] for g in range(SWA_KV_HEADS)]
    sinkc = [sink_ref[g][0:1, :] * LOG2E for g in range(SWA_KV_HEADS)]
    nch = ts // chunk
    units = [(s, c, g) for s in range(ns) for c in range(nch) for g in range(SWA_KV_HEADS)]

    def swa_scores(s, c, g):
        r0 = c * chunk
        rows = slice(s * ts + r0, s * ts + r0 + chunk)
        kwin = kext[s, r0:r0 + SWA_KEYS, :]
        vwin = vext[s, r0:r0 + SWA_KEYS, :]
        krl = pltpu.roll(kwin, HEAD_DIM, 1)
        vrl = pltpu.roll(vwin, HEAD_DIM, 1)
        if g == 0:
            kd = jnp.where(lo_k, kwin, krl).astype(BF16)
            vd = jnp.where(lo_k, vwin, vrl).astype(BF16)
        else:
            kd = jnp.where(lo_k, krl, kwin).astype(BF16)
            vd = jnp.where(lo_k, vrl, vwin).astype(BF16)
        q0 = qn[2 * g][rows]
        q1 = qn[2 * g + 1][rows]
        qs = jnp.concatenate([jnp.where(lo_c, q0, 0.0), jnp.where(lo_c, 0.0, q0),
                              jnp.where(lo_c, q1, 0.0), jnp.where(lo_c, 0.0, q1)], axis=0).astype(BF16)
        return lax.dot_general(kd, qs, (((1,), (1,)), ((), ())), preferred_element_type=F32), vd

    def swa_finish(s, c, g, sc, vd):
        rows = slice(s * ts + c * chunk, s * ts + (c + 1) * chunk)
        sc = sc + bias[g]
        if not has_cache and c * chunk < WINDOW:
            sc = jnp.where(kj >= padk + jnp.maximum(0, WINDOW - (i * nch + c) * chunk), sc, NEG_INF)
        m = jnp.maximum(jnp.max(sc, axis=0, keepdims=True), sinkc[g])
        p = jnp.exp2(sc - m)
        den = jnp.sum(p, axis=0, keepdims=True) + jnp.exp2(sinkc[g] - m)
        wgt = (p / den).astype(BF16)
        o = lax.dot_general(wgt, vd, (((0,), (0,)), ((), ())), preferred_element_type=F32)
        for jj in range(2):
            t = 2 * g + jj
            mix[rows, C_POOL + LANES * t:C_POOL + LANES * (t + 1)] = jnp.where(
                lo_c, o[(2 * jj) * chunk:(2 * jj + 1) * chunk], o[(2 * jj + 1) * chunk:(2 * jj + 2) * chunk])

    pending = swa_scores(*units[0])
    for n, unit in enumerate(units):
        nxt = swa_scores(*units[n + 1]) if n + 1 < len(units) else None
        swa_finish(*unit, *pending)
        pending = nxt

    mixed = (mix[...] * _silu(gate)).astype(BF16)
    y_ref[...] = x + jnp.dot(mixed, wout_ref[...], preferred_element_type=F32)

    @pl.when(i == nblk - 1)
    def _():
        for s in range(ns):
            ps_ref[s] = uext[s, ts:ts + POOL_HIST, :]
            ks_ref[s] = kext[s, padk + ts:padk + ts + WINDOW, :]
            vs_ref[s] = vext[s, padk + ts:padk + ts + WINDOW, :]


def _const_spec(shape):
    nd = len(shape)
    return pl.BlockSpec(shape, lambda i, _nd=nd: (0,) * _nd, pipeline_mode=pl.Buffered(1))


def _layer0(x2d, prefix, wts, *, ns, ts, chunk, base_pos):
    n = x2d.shape[0]
    tb = ns * ts
    nblk = n // tb
    has_cache = prefix is not None
    padk = SWA_KEYS - WINDOW - chunk
    rows4 = SWA_REP * chunk
    slope = np.repeat(2.0 ** (-(np.arange(SWA_HEADS) + 1.0)), chunk).reshape(SWA_KV_HEADS, 1, rows4)
    kj = np.arange(SWA_KEYS).reshape(1, SWA_KEYS, 1)
    rel = (np.arange(rows4) % chunk).reshape(1, 1, rows4) + WINDOW - (kj - padk)
    bias = jnp.asarray(np.where(kj >= padk, -(slope * LOG2E) * np.abs(rel), NEG_INF).astype(np.float32))
    sink = jnp.broadcast_to(jnp.repeat(wts["sinks"].astype(F32), chunk).reshape(SWA_KV_HEADS, 1, rows4),
                            (SWA_KV_HEADS, 8, rows4))
    consts = [wts["g0"], wts["w_in0"], wts["w_pool"], wts["pool_scale"], wts["qg0"], wts["kg0"], wts["seg"],
              bias, sink, wts["w_out0"]]
    in_specs = [pl.BlockSpec((tb, D_MODEL), lambda i: (i, 0))]
    args = [x2d]
    if has_cache:
        for a in prefix:
            in_specs.append(_const_spec(a.shape))
            args.append(a)
    for a in consts:
        in_specs.append(_const_spec(a.shape))
        args.append(a)
    out_shape = [jax.ShapeDtypeStruct((n, D_MODEL), F32),
                 jax.ShapeDtypeStruct((ns, POOL_HIST, C_POOL), F32),
                 jax.ShapeDtypeStruct((ns, WINDOW, LANES), F32),
                 jax.ShapeDtypeStruct((ns, WINDOW, LANES), F32)]
    out_specs = [pl.BlockSpec((tb, D_MODEL), lambda i: (i, 0)),
                 pl.BlockSpec((ns, POOL_HIST, C_POOL), lambda i: (0, 0, 0)),
                 pl.BlockSpec((ns, WINDOW, LANES), lambda i: (0, 0, 0)),
                 pl.BlockSpec((ns, WINDOW, LANES), lambda i: (0, 0, 0))]
    scratch = [pltpu.VMEM((ns, POOL_HIST + ts, C_POOL), F32),
               pltpu.VMEM((ns, padk + WINDOW + ts, LANES), F32),
               pltpu.VMEM((ns, padk + WINDOW + ts, LANES), F32),
               pltpu.VMEM((tb, D_MODEL), F32)]
    kern = functools.partial(_l0_kernel, ns=ns, ts=ts, chunk=chunk, has_cache=has_cache,
                             base_pos=base_pos, nblk=nblk)
    return pl.pallas_call(
        kern, grid=(nblk,), in_specs=in_specs, out_specs=out_specs, out_shape=out_shape,
        scratch_shapes=scratch, compiler_params=_cparams(("arbitrary",)),
        name="layer0_prompt" if not has_cache else "layer0_sample")(*args)


def _log_sigmoid(z):
    return jnp.minimum(z, 0.0) - jnp.log1p(jnp.exp(-jnp.abs(z)))


def _fox_proj_sample_kernel(x_ref, g1_ref, wqt_ref, wk_ref, wvt_ref, wf_ref, bf_ref, qg_ref, kg_ref, seg_ref,
                            k_ref, v_ref, lf_ref, q_ref):
    x = x_ref[...]
    h = _rms(x, g1_ref[...]).astype(BF16)
    z = jnp.dot(h, wf_ref[...], preferred_element_type=F32) + bf_ref[...]
    lf_ref[...] = _log_sigmoid(z)[:, 0:FOX_HEADS]
    seg = seg_ref[...]
    nt = (((1,), (1,)), ((), ()))
    q = lax.dot_general(h, wqt_ref[...], nt, preferred_element_type=F32)
    k = jnp.dot(h, wk_ref[...], preferred_element_type=F32)
    v_ref[...] = lax.dot_general(h, wvt_ref[...], nt, preferred_element_type=F32)
    for j in range(FOX_HEADS // 2):
        cols = slice(LANES * j, LANES * (j + 1))
        q_ref[:, cols] = _head_rms_tile(q[:, cols], seg, qg_ref[:, cols]) * ATTN_SCALE
        k_ref[:, cols] = _head_rms_tile(k[:, cols], seg, kg_ref[:, cols])


def _fox_proj_sample(x2d, wts, *, tb):
    n = x2d.shape[0]
    consts = [wts["g1"], wts["w_q1t"], wts["w_k1"], wts["w_v1t"], wts["w_f"], wts["b_f"], wts["qg1"], wts["kg1"],
              wts["seg"]]
    row_spec = pl.BlockSpec((tb, D_MODEL), lambda i: (i, 0))
    in_specs = [row_spec] + [_const_spec(a.shape) for a in consts]
    row = jax.ShapeDtypeStruct((n, D_MODEL), F32)
    out_shape = [row, row, jax.ShapeDtypeStruct((n, FOX_HEADS), F32), row]
    out_specs = [row_spec, row_spec, pl.BlockSpec((tb, FOX_HEADS), lambda i: (i, 0)), row_spec]
    return pl.pallas_call(
        _fox_proj_sample_kernel, grid=(n // tb,), in_specs=in_specs, out_specs=out_specs, out_shape=out_shape,
        compiler_params=_cparams(("arbitrary",)), name="fox_proj_sample")(x2d, *consts)


def _fox_proj_prompt_kernel(x_ref, g1_ref, wk_ref, wft_ref, bfc_ref, wqt_ref, wvt_ref, gq_ref, kg_ref,
                            seg_ref, tri_ref, e_ref,
                            k_ref, vt_ref, lft_ref, qa_ref, ka_ref, va_ref, ce_ref, carry, *, tb):
    i = pl.program_id(0)
    x = x_ref[...]
    hf = _rms(x, g1_ref[...])
    h = hf.astype(BF16)
    ht = hf.T.astype(BF16)
    bcol = jnp.concatenate([bfc_ref[...]] * (tb // LANES), axis=1)
    logft = _log_sigmoid(jnp.dot(wft_ref[...], ht, preferred_element_type=F32) + bcol)
    lft_ref[...] = logft
    k = jnp.dot(h, wk_ref[...], preferred_element_type=F32)

    @pl.when(i == 0)
    def _():
        carry[...] = jnp.zeros(carry.shape, F32)

    c3 = jnp.dot(jnp.concatenate(_split3(logft), axis=0), tri_ref[...], preferred_element_type=F32)

    qt = jnp.dot(wqt_ref[...], ht, preferred_element_type=F32)
    vt = jnp.dot(wvt_ref[...], ht, preferred_element_type=F32)
    vt_ref[...] = vt
    srow = lax.broadcasted_iota(jnp.int32, (HEAD_DIM, tb), 0)
    q_aug = jnp.where(srow < 3, 1.0, 0.0)
    v_aug = jnp.where(srow == 0, 1.0, 0.0)
    gq = jnp.concatenate([gq_ref[...]] * (tb // LANES), axis=1) * (ATTN_SCALE * LOG2E)
    for hd in range(FOX_HEADS):
        rows = slice(HEAD_DIM * hd, HEAD_DIM * (hd + 1))
        qh = qt[rows]
        ss = jnp.sum(qh * qh, axis=0, keepdims=True)
        qn = qh * lax.rsqrt(ss * (1.0 / HEAD_DIM) + NORM_EPS) * gq
        qa_ref[hd] = jnp.concatenate([qn, q_aug], axis=0).astype(BF16)
        va_ref[hd] = jnp.concatenate([vt[rows], v_aug], axis=0).astype(BF16)

    cumt = c3[0:FOX_HEADS] + c3[FOX_HEADS:2 * FOX_HEADS] + c3[2 * FOX_HEADS:3 * FOX_HEADS] + carry[:, 0:1]
    end = jnp.broadcast_to(cumt[:, tb - 1:tb], (FOX_HEADS, LANES))
    carry[...] = end
    ce_ref[0] = end
    parts = [p.astype(F32) for p in _split3(-LOG2E * cumt)]
    n3 = jnp.concatenate(parts + [jnp.zeros((LANES - 3 * FOX_HEADS, tb), F32)], axis=0).T.astype(BF16)
    biasall = jnp.dot(n3, e_ref[...], preferred_element_type=F32)
    lo_l = lax.broadcasted_iota(jnp.int32, (tb, LANES), 1) < HEAD_DIM
    seg = seg_ref[...]
    for j in range(FOX_HEADS // 2):
        cols = slice(LANES * j, LANES * (j + 1))
        kt = _head_rms_tile(k[:, cols], seg, kg_ref[:, cols])
        k_ref[:, cols] = kt
        bt = biasall[:, cols]
        ka_ref[2 * j] = jnp.where(lo_l, kt, bt).astype(BF16)
        ka_ref[2 * j + 1] = pltpu.roll(jnp.where(lo_l, bt, kt), HEAD_DIM, 1).astype(BF16)


def _fox_proj_prompt(x2d, wts, *, tb):
    n = x2d.shape[0]
    tri = jnp.asarray(np.triu(np.ones((tb, tb), np.float32)), BF16)
    e = np.zeros((LANES, FOX_HEADS // 2 * LANES), np.float32)
    for part in range(3):
        for hd in range(FOX_HEADS):
            e[part * FOX_HEADS + hd, (hd // 2) * LANES + (HEAD_DIM if hd % 2 == 0 else 0) + part] = 1.0
    consts = [wts["g1"], wts["w_k1"], wts["w_ft"], wts["b_fc"], wts["w_q1t"], wts["w_v1t"],
              wts["gq1c"], wts["kg1"], wts["seg"], tri, jnp.asarray(e, BF16)]
    row_spec = pl.BlockSpec((tb, D_MODEL), lambda i: (i, 0))
    in_specs = [row_spec] + [_const_spec(a.shape) for a in consts]
    row = jax.ShapeDtypeStruct((n, D_MODEL), F32)
    out_shape = [row, jax.ShapeDtypeStruct((D_MODEL, n), F32), jax.ShapeDtypeStruct((FOX_HEADS, n), F32),
                 jax.ShapeDtypeStruct((FOX_HEADS, LANES, n), BF16), jax.ShapeDtypeStruct((FOX_HEADS, n, LANES), BF16),
                 jax.ShapeDtypeStruct((FOX_HEADS, LANES, n), BF16),
                 jax.ShapeDtypeStruct((n // tb, FOX_HEADS, LANES), F32)]
    tspec = pl.BlockSpec((FOX_HEADS, LANES, tb), lambda i: (0, 0, i))
    out_specs = [row_spec, pl.BlockSpec((D_MODEL, tb), lambda i: (0, i)),
                 pl.BlockSpec((FOX_HEADS, tb), lambda i: (0, i)),
                 tspec, pl.BlockSpec((FOX_HEADS, tb, LANES), lambda i: (0, i, 0)), tspec,
                 pl.BlockSpec((1, FOX_HEADS, LANES), lambda i: (i, 0, 0))]
    return pl.pallas_call(
        functools.partial(_fox_proj_prompt_kernel, tb=tb), grid=(n // tb,), in_specs=in_specs,
        out_specs=out_specs, out_shape=out_shape, scratch_shapes=[pltpu.VMEM((FOX_HEADS, LANES), F32)],
        compiler_params=_cparams(("arbitrary",)), name="fox_proj_prompt")(x2d, *consts)


def _fox_attn_kernel(ce_ref, qk_ref, q_ref, k_ref, v_ref, o_ref, m_scr, al_scr, acc_scr, s_scr,
                     *, tq, qpb, **kw):
    def blk(qb):
        return _fox_attn_block(pl.program_id(0), pl.program_id(1) * qpb + qb, pl.multiple_of(qb * tq, tq),
                               ce_ref, qk_ref, q_ref, k_ref, v_ref, o_ref, m_scr, al_scr, acc_scr, s_scr,
                               tq=tq, **kw)

    blk(0)[0](None)

    def body(qb, carry):
        blk(qb)[0](blk(qb - 1)[1])
        return carry

    lax.fori_loop(1, qpb, body, 0)
    blk(qpb - 1)[1]()


def _fox_attn_block(pair, qi, qoff, ce_ref, qk_ref, q_ref, k_ref, v_ref, o_ref, m_scr, al_scr, acc_scr, s_scr,
                    *, tq, tk, unroll, nkb, nh):
    n_u = qi

    last = jnp.maximum(n_u - 1, 0)
    slack = 2.0 * qk_ref[0]

    def dead(j):
        jj = jnp.minimum(j, last)
        ok = j < n_u - 1
        for hh in range(nh):
            hd = nh * pair + hh
            gap = LOG2E * (ce_ref[last * FOX_HEADS + hd] - ce_ref[jj * FOX_HEADS + hd])
            ok = ok & (slack + gap < EXP2_ZERO)
        return ok

    lo = jnp.int32(0)
    hi = last
    for _ in range(max(1, (nkb - 1).bit_length())):
        mid = (lo + hi) >> 1
        d = dead(mid)
        lo = jnp.where(d, mid + 1, lo)
        hi = jnp.where(d, hi, mid)
    first = lo

    def score_matmuls(kb):
        off = pl.multiple_of(kb * tk, tk)
        return [jnp.dot(k_ref[hh, pl.ds(off, tk), :], q_ref[hh, :, pl.ds(qoff, tq)],
                        preferred_element_type=F32)
                for hh in range(nh)]

    def score_finish(sts):
        for hh in range(nh):
            st = sts[hh]
            m_old = m_scr[hh, 0:1, :]
            m_new = jnp.maximum(m_old, jnp.max(st, axis=0, keepdims=True))
            al_scr[hh, 0:1, :] = jnp.exp2(m_old - m_new)
            m_scr[hh, 0:1, :] = m_new
            s_scr[hh] = st

    def accumulate(kb):
        off = pl.multiple_of(kb * tk, tk)
        for hh in range(nh):
            p = jnp.exp2(s_scr[hh] - m_scr[hh, 0:1, :]).astype(BF16)
            pv = jnp.dot(v_ref[hh, :, pl.ds(off, tk)], p, preferred_element_type=F32)
            acc_scr[hh] = al_scr[hh, 0:1, :] * acc_scr[hh] + pv

    def advance(kb):
        sts = score_matmuls(kb + 1)
        accumulate(kb)
        score_finish(sts)

    def tail():
        half = tq // 2
        off = pl.multiple_of(n_u * tk, tk)
        sts = [(jnp.dot(k_ref[hh, pl.ds(off, half), :], q_ref[hh, :, pl.ds(qoff, half)],
                        preferred_element_type=F32),
                jnp.dot(k_ref[hh, pl.ds(off, tk), :], q_ref[hh, :, pl.ds(pl.multiple_of(qoff + half, half), half)],
                        preferred_element_type=F32))
               for hh in range(nh)]
        accumulate(last)
        tri = jnp.where(lax.broadcasted_iota(jnp.int32, (half, half), 0)
                        <= lax.broadcasted_iota(jnp.int32, (half, half), 1), 0.0, NEG_INF)
        for hh in range(nh):
            st_a, st_b = sts[hh]
            halves = ((st_a + tri, 0), (jnp.concatenate([st_b[0:half], st_b[half:tk] + tri], axis=0), half))
            for st, c0 in halves:
                nk = st.shape[0]
                m_old = m_scr[hh, 0:1, c0:c0 + half]
                m_new = jnp.maximum(m_old, jnp.max(st, axis=0, keepdims=True))
                p = jnp.exp2(st - m_new).astype(BF16)
                pv = jnp.dot(v_ref[hh, :, pl.ds(off, nk)], p, preferred_element_type=F32)
                acc_scr[hh, :, c0:c0 + half] = jnp.exp2(m_old - m_new) * acc_scr[hh, :, c0:c0 + half] + pv
        for t in range(nh // 2):
            pair_rows = []
            for hh in (2 * t, 2 * t + 1):
                a = acc_scr[hh]
                pair_rows.append(a[0:HEAD_DIM] / a[HEAD_DIM:HEAD_DIM + 1, :])
            o_ref[pl.ds(qoff, tq), LANES * t:LANES * (t + 1)] = jnp.concatenate(
                pair_rows, axis=0).T.astype(o_ref.dtype)

    def head(prev_tail):
        def start():
            if prev_tail is not None:
                prev_tail()
            m_scr[...] = jnp.full(m_scr.shape, NEG_INF, F32)
            acc_scr[...] = jnp.zeros(acc_scr.shape, F32)

        def below_diagonal():
            n_adv = n_u - 1 - first
            rem = lax.rem(n_adv, unroll)
            for k in range(unroll):
                @pl.when(rem == k)
                def _():
                    start()
                    score_finish(score_matmuls(first))
                    for u in range(k):
                        advance(first + u)

            def body(t, carry):
                for u in range(unroll):
                    advance(first + rem + unroll * t + u)
                return carry

            lax.fori_loop(0, n_adv // unroll, body, 0)

        if prev_tail is None:
            @pl.when(qi == 0)
            def _():
                start()
                m_scr[...] = jnp.zeros(m_scr.shape, F32)
                al_scr[...] = jnp.ones(al_scr.shape, F32)
                s_scr[...] = jnp.full(s_scr.shape, NEG_INF, F32)

            pl.when(qi > 0)(below_diagonal)
        else:
            below_diagonal()

    return head, tail


def _fox_attn_prompt(cum_end, qk_bound, qa, ka, va, *, tq, tk, unroll, nh, qpb):
    n = ka.shape[1]
    assert tq == tk and n % (qpb * tq) == 0
    resident = dict(pipeline_mode=pl.Buffered(1)) if nh > 2 else {}
    grid_spec = pltpu.PrefetchScalarGridSpec(
        num_scalar_prefetch=2, grid=(FOX_HEADS // nh, n // (qpb * tq)),
        in_specs=[pl.BlockSpec((nh, LANES, qpb * tq), lambda p, i, ce, qk: (p, 0, i)),
                  pl.BlockSpec((nh, n, LANES), lambda p, i, ce, qk: (p, 0, 0), **resident),
                  pl.BlockSpec((nh, LANES, n), lambda p, i, ce, qk: (p, 0, 0), **resident)],
        out_specs=pl.BlockSpec((qpb * tq, HEAD_DIM * nh), lambda p, i, ce, qk: (i, p)),
        scratch_shapes=[pltpu.VMEM((nh, 8, tq), F32), pltpu.VMEM((nh, 8, tq), F32),
                        pltpu.VMEM((nh, LANES, tq), F32), pltpu.VMEM((nh, tk, tq), F32)])
    return pl.pallas_call(
        functools.partial(_fox_attn_kernel, tq=tq, tk=tk, unroll=unroll, nkb=n // tk, nh=nh, qpb=qpb),
        grid_spec=grid_spec,
        out_shape=jax.ShapeDtypeStruct((n, D_MODEL), BF16),
        compiler_params=_cparams(("arbitrary", "arbitrary")), name="fox_attn_prompt")(
            cum_end, qk_bound, qa, ka, va)


CUM_BLK = 512
NEW_PAD = 128
SAMPLE_GROUPS_PER_STEP = 2


def _fox_sample_kernel(q_ref, kc_ref, vc_ref, lfc_ref, kn_ref, vn_ref, lfn_ref, tri_ref, o_ref, negc, negn,
                       *, t_new, p_len, gps):
    rows = 4 * t_new

    @pl.when(pl.program_id(1) == 0)
    def _():
        tri = tri_ref[...]

        def prefix_sums(x, t):
            c3 = jnp.dot(jnp.concatenate(_split3(x), axis=0), t, preferred_element_type=F32)
            return c3[0:FOX_HEADS] + c3[FOX_HEADS:2 * FOX_HEADS] + c3[2 * FOX_HEADS:3 * FOX_HEADS]

        carry = jnp.zeros((FOX_HEADS, 1), F32)
        for b in range(p_len // CUM_BLK):
            c = prefix_sums(lfc_ref[0, :, b * CUM_BLK:(b + 1) * CUM_BLK], tri) + carry
            negc[:, b * CUM_BLK:(b + 1) * CUM_BLK] = -c
            carry = c[:, CUM_BLK - 1:CUM_BLK]
        negn[...] = -(prefix_sums(lfn_ref[0], tri[0:NEW_PAD, 0:NEW_PAD]) + carry)

    rhead = lax.broadcasted_iota(jnp.int32, (rows, 2 * LANES), 0) // t_new
    lhead = lax.broadcasted_iota(jnp.int32, (rows, 2 * LANES), 1) // HEAD_DIM
    qrow = lax.broadcasted_iota(jnp.int32, (rows, NEW_PAD), 0) % t_new
    kcol = lax.broadcasted_iota(jnp.int32, (rows, NEW_PAD), 1)
    lh = lax.broadcasted_iota(jnp.int32, (t_new, 2 * LANES), 1) // HEAD_DIM
    zpad = jnp.zeros((NEW_PAD - t_new, 2 * LANES), F32)
    for g in range(gps):
        qd = gps * pl.program_id(1) + g
        cols = slice(2 * LANES * g, 2 * LANES * (g + 1))
        qq = q_ref[0, :, cols]
        qbd = jnp.where(rhead == lhead, jnp.concatenate([qq] * 4, axis=0), 0.0).astype(BF16)
        kct = kc_ref[0, 4 * g:4 * (g + 1)].reshape(2 * LANES, p_len).astype(BF16)
        vct = vc_ref[0, 4 * g:4 * (g + 1)].reshape(2 * LANES, p_len).astype(BF16)
        s_c = jnp.dot(qbd, kct, preferred_element_type=F32)
        kn = jnp.concatenate([kn_ref[0, :, cols], zpad], axis=0).astype(BF16)
        vn = jnp.concatenate([vn_ref[0, :, cols], zpad], axis=0).astype(BF16)
        s_n = lax.dot_general(qbd, kn, (((1,), (1,)), ((), ())), preferred_element_type=F32)
        bc = jnp.concatenate([jnp.broadcast_to(negc[pl.ds(4 * qd + jh, 1), :], (t_new, p_len))
                              for jh in range(4)], axis=0)
        bn = jnp.concatenate([jnp.broadcast_to(negn[pl.ds(4 * qd + jh, 1), :], (t_new, NEW_PAD))
                              for jh in range(4)], axis=0)
        s_c = s_c + bc
        s_n = jnp.where(kcol <= qrow, s_n + bn, NEG_INF)
        m = jnp.maximum(jnp.max(s_c, axis=-1, keepdims=True), jnp.max(s_n, axis=-1, keepdims=True))
        p_c = jnp.exp(s_c - m)
        p_n = jnp.exp(s_n - m)
        den = jnp.sum(p_c, axis=-1, keepdims=True) + jnp.sum(p_n, axis=-1, keepdims=True)
        o = (lax.dot_general(p_c.astype(BF16), vct, (((1,), (1,)), ((), ())), preferred_element_type=F32)
             + jnp.dot(p_n.astype(BF16), vn, preferred_element_type=F32)) / den
        out = jnp.zeros((t_new, 2 * LANES), F32)
        for jh in range(4):
            out = jnp.where(lh == jh, o[jh * t_new:(jh + 1) * t_new], out)
        o_ref[0, :, cols] = out


def _fox_attn_sample(q, cache_k, cache_v, lfc_t, k_new, v_new, lfn_t):
    b, t_new, _ = q.shape
    p_len = cache_k.shape[3]
    tri = jnp.asarray(np.triu(np.ones((CUM_BLK, CUM_BLK), np.float32)), BF16)
    gps = SAMPLE_GROUPS_PER_STEP
    width = 2 * LANES * gps
    in_specs = [pl.BlockSpec((1, t_new, width), lambda r, d: (r, 0, d)),
                pl.BlockSpec((1, 4 * gps, HEAD_DIM, p_len), lambda r, d: (r, d, 0, 0)),
                pl.BlockSpec((1, 4 * gps, HEAD_DIM, p_len), lambda r, d: (r, d, 0, 0)),
                pl.BlockSpec((1, FOX_HEADS, p_len), lambda r, d: (r, 0, 0)),
                pl.BlockSpec((1, t_new, width), lambda r, d: (r, 0, d)),
                pl.BlockSpec((1, t_new, width), lambda r, d: (r, 0, d)),
                pl.BlockSpec((1, FOX_HEADS, NEW_PAD), lambda r, d: (r, 0, 0)),
                pl.BlockSpec((CUM_BLK, CUM_BLK), lambda r, d: (0, 0))]
    return pl.pallas_call(
        functools.partial(_fox_sample_kernel, t_new=t_new, p_len=p_len, gps=gps),
        grid=(b, D_MODEL // width), in_specs=in_specs,
        out_specs=pl.BlockSpec((1, t_new, width), lambda r, d: (r, 0, d)),
        out_shape=jax.ShapeDtypeStruct((b, t_new, D_MODEL), F32),
        scratch_shapes=[pltpu.VMEM((FOX_HEADS, p_len), F32), pltpu.VMEM((FOX_HEADS, NEW_PAD), F32)],
        compiler_params=_cparams(("arbitrary", "arbitrary")), name="fox_attn_sample")(
            q, cache_k, cache_v, lfc_t, k_new, v_new, lfn_t, tri)


def _out_proj_kernel(r_ref, a_ref, g1_ref, wg_ref, w_ref, y_ref):
    x = r_ref[...]
    h = _rms(x, g1_ref[...]).astype(BF16)
    gate = jnp.dot(h, wg_ref[...], preferred_element_type=F32)
    mixed = (a_ref[...].astype(F32) * _silu(gate)).astype(BF16)
    y_ref[...] = x + jnp.dot(mixed, w_ref[...], preferred_element_type=F32)


def _out_proj(resid, attn, wts, *, tb):
    n = resid.shape[0]
    row = pl.BlockSpec((tb, D_MODEL), lambda i: (i, 0))
    consts = [wts["g1"], wts["w_g1"], wts["w_out1"]]
    return pl.pallas_call(
        _out_proj_kernel, grid=(n // tb,), in_specs=[row, row] + [_const_spec(a.shape) for a in consts],
        out_specs=row, out_shape=jax.ShapeDtypeStruct((n, D_MODEL), F32),
        compiler_params=_cparams(("arbitrary",)), name="out_proj")(resid, attn, *consts)


L0_TB = 512
OUT_TB = 1024
FOX_TB = 256
ATTN_TQ = 512
ATTN_TK = 512
ATTN_UNROLL = 4
ATTN_HEADS = 2
ATTN_QPB = 16
SWA_CHUNK = 64


def kernel(x_prompt, x_sample, state_pool, cache_swa_k, cache_swa_v, cache_fox_k, cache_fox_v, cache_fox_logf,
           norm0_g, w_in0, w_pool, pool_scale, swa_qn_g, swa_kn_g, swa_sinks, w_out0,
           norm1_g, w_in1, b_forget, fox_qn_g, fox_kn_g, w_out1):
    nb, seq, _ = x_prompt.shape
    db, dseq, _ = x_sample.shape
    past_len = cache_fox_k.shape[1]
    assert nb == 1 and seq % L0_TB == 0 and seq % OUT_TB == 0 and seq % ATTN_TQ == 0 and dseq % 8 == 0

    seg = np.kron(np.eye(2, dtype=np.float32), np.ones((HEAD_DIM, HEAD_DIM), np.float32))
    mc = FOX_HEADS * HEAD_DIM
    wts = {
        "g0": norm0_g.reshape(1, D_MODEL), "w_in0": w_in0.astype(BF16), "w_pool": w_pool.astype(BF16),
        "pool_scale": pool_scale.reshape(1, C_POOL),
        "qg0": jnp.tile(swa_qn_g, SWA_HEADS).reshape(1, SWA_HEADS * HEAD_DIM),
        "kg0": jnp.tile(swa_kn_g, SWA_KV_HEADS).reshape(1, LANES),
        "seg": jnp.asarray(seg, BF16), "sinks": swa_sinks, "w_out0": w_out0.astype(BF16),
        "g1": norm1_g.reshape(1, D_MODEL),
        "w_out1": w_out1.astype(BF16),
        "w_f": jnp.pad(w_in1[:, 4 * mc:], ((0, 0), (0, LANES - FOX_HEADS))).astype(BF16),
        "b_f": jnp.pad(b_forget.astype(F32), (0, LANES - FOX_HEADS)).reshape(1, LANES),
        "w_ft": w_in1[:, 4 * mc:].T.astype(BF16),
        "b_fc": jnp.broadcast_to(b_forget.astype(F32).reshape(FOX_HEADS, 1), (FOX_HEADS, LANES)),
        "qg1": jnp.tile(fox_qn_g, FOX_HEADS).reshape(1, mc), "kg1": jnp.tile(fox_kn_g, FOX_HEADS).reshape(1, mc),
        "w_k1": w_in1[:, mc:2 * mc].astype(BF16), "w_g1": w_in1[:, 3 * mc:4 * mc].astype(BF16),
        "w_q1t": w_in1[:, 0:mc].T.astype(BF16), "w_v1t": w_in1[:, 2 * mc:3 * mc].T.astype(BF16),
        "gq1c": jnp.broadcast_to(fox_qn_g.astype(F32).reshape(HEAD_DIM, 1), (HEAD_DIM, LANES)),
    }

    xp = x_prompt.reshape(seq, D_MODEL)
    y0p, pool_p, swk_p, swv_p = _layer0(xp, None, wts, ns=1, ts=L0_TB, chunk=SWA_CHUNK, base_pos=0)
    fk_p, fvt_p, flt_p, qa, ka, va, cum_end = _fox_proj_prompt(y0p, wts, tb=FOX_TB)
    fl_p = jnp.transpose(flt_p)
    fv_p = jnp.transpose(fvt_p.reshape(FOX_HEADS, HEAD_DIM, seq), (2, 0, 1))
    per = ATTN_TK // FOX_TB
    cum_end = cum_end[per - 1::per, :, 0].reshape(-1)
    qk_bound = (1.02 * LOG2E * HEAD_DIM * ATTN_SCALE * jnp.max(jnp.abs(fox_qn_g)) * jnp.max(jnp.abs(fox_kn_g))
                ).astype(F32).reshape(1)
    attn_p = _fox_attn_prompt(cum_end, qk_bound, qa, ka, va, tq=ATTN_TQ, tk=ATTN_TK, unroll=ATTN_UNROLL,
                              nh=ATTN_HEADS, qpb=ATTN_QPB)
    yp = _out_proj(y0p, attn_p, wts, tb=OUT_TB)

    xs = x_sample.reshape(db * dseq, D_MODEL)
    prefix = (jnp.pad(state_pool, ((0, 0), (POOL_HIST - POOL_PAD, 0), (0, 0))),
              cache_swa_k.reshape(db, WINDOW, LANES), cache_swa_v.reshape(db, WINDOW, LANES))
    y0s, pool_s, swk_s, swv_s = _layer0(xs, prefix, wts, ns=db, ts=dseq, chunk=dseq, base_pos=past_len)
    fk_s, fv_s, fl_s, q_s = _fox_proj_sample(y0s, wts, tb=db * dseq)
    attn_s = _fox_attn_sample(
        q_s.reshape(db, dseq, D_MODEL),
        jnp.transpose(cache_fox_k, (0, 2, 3, 1)), jnp.transpose(cache_fox_v, (0, 2, 3, 1)),
        jnp.transpose(cache_fox_logf, (0, 2, 1)),
        fk_s.reshape(db, dseq, D_MODEL), fv_s.reshape(db, dseq, D_MODEL),
        jnp.pad(jnp.transpose(fl_s.reshape(db, dseq, FOX_HEADS), (0, 2, 1)), ((0, 0), (0, 0), (0, NEW_PAD - dseq))))
    ys = _out_proj(y0s, attn_s.reshape(db * dseq, D_MODEL), wts, tb=db * dseq)

    return (yp.reshape(1, seq, D_MODEL), ys.reshape(db, dseq, D_MODEL),
            pool_p[:, POOL_HIST - POOL_PAD:], pool_s[:, POOL_HIST - POOL_PAD:],
            swk_p.reshape(1, WINDOW, SWA_KV_HEADS, HEAD_DIM), swv_p.reshape(1, WINDOW, SWA_KV_HEADS, HEAD_DIM),
            swk_s.reshape(db, WINDOW, SWA_KV_HEADS, HEAD_DIM), swv_s.reshape(db, WINDOW, SWA_KV_HEADS, HEAD_DIM),
            fk_p.reshape(1, seq, FOX_HEADS, HEAD_DIM), fv_p.reshape(1, seq, FOX_HEADS, HEAD_DIM),
            fl_p.reshape(1, seq, FOX_HEADS),
            fk_s.reshape(db, dseq, FOX_HEADS, HEAD_DIM), fv_s.reshape(db, dseq, FOX_HEADS, HEAD_DIM),
            fl_s.reshape(db, dseq, FOX_HEADS))
```
